```python
import jax, jax.numpy as jnp
from jax import lax
import numpy as np

D_MODEL = 4096
BATCH = 1
SEQ = 8192
DEPTH = 2

N_META = 16
ROPE_THETA = 10000.0
EPS = 1e-6
Q_BLOCK = 128
N_BRANCH = 4
BRANCH_WIDTH = D_MODEL // N_BRANCH
CONV_WIDTH = BRANCH_WIDTH
CONV_K = 3
MLA_NOPE = 128
MLA_ROPE = 64
MLA_V = 128
MLA_HEADS = BRANCH_WIDTH // MLA_V
MLA_Q_RANK = 1536
MLA_KV_RANK = 512
DSA_HEAD_DIM = 128
DSA_HEADS = BRANCH_WIDTH // DSA_HEAD_DIM
IDX_HEADS = 16
IDX_DIM = 64
IDX_TOPK_MAX = 256
POOL_WINDOWS = (2, 4, 8, 16)
POOL_GROUP = BRANCH_WIDTH // len(POOL_WINDOWS)
POOL_WIDTH = POOL_GROUP * len(POOL_WINDOWS)
D_FF = ((8 * D_MODEL + 3 * 256 - 1) // (3 * 256)) * 256
SPLIT_SIZES = ((CONV_WIDTH, CONV_WIDTH, CONV_WIDTH)
               + (MLA_Q_RANK, MLA_KV_RANK, MLA_ROPE)
               + (DSA_HEADS * DSA_HEAD_DIM, DSA_HEADS * DSA_HEAD_DIM, DSA_HEADS * DSA_HEAD_DIM)
               + (IDX_HEADS * IDX_DIM, IDX_DIM, IDX_HEADS)
               + (POOL_WIDTH,))
N_IN = sum(SPLIT_SIZES)

kernel_name = "hybrid_gated_conv_mla_dsa_pool_block"


def rms_norm(x, g):
    xf = x.astype(jnp.float32)
    y = xf * lax.rsqrt(jnp.mean(xf * xf, axis=-1, keepdims=True) + EPS)
    return (y * g.astype(jnp.float32)).astype(x.dtype)


def rope_tables(length, dim):
    inv = 1.0 / jnp.power(ROPE_THETA, jnp.arange(0, dim, 2, dtype=jnp.float32) / dim)
    ang = jnp.arange(length, dtype=jnp.float32)[:, None] * inv[None, :]
    return jnp.cos(ang), jnp.sin(ang)


def apply_rope(x, cos, sin):
    xf = x.astype(jnp.float32)
    x1, x2 = jnp.split(xf, 2, axis=-1)
    c = cos[None, :, None, :]
    s = sin[None, :, None, :]
    return jnp.concatenate([x1 * c - x2 * s, x1 * s + x2 * c], axis=-1).astype(x.dtype)


def to_query_blocks(a, n_blocks):
    pad = n_blocks * Q_BLOCK - a.shape[1]
    a = jnp.pad(a, [(0, 0), (0, pad)] + [(0, 0)] * (a.ndim - 2))
    a = a.reshape((a.shape[0], n_blocks, Q_BLOCK) + a.shape[2:])
    return jnp.moveaxis(a, 1, 0)


def from_query_blocks(o, length):
    o = jnp.moveaxis(o, 0, 1)
    o = o.reshape((o.shape[0], -1) + o.shape[3:])
    return o[:, :length]


def short_conv(z, w):
    return lax.conv_general_dilated(
        z, w[:, None, :].astype(z.dtype), window_strides=(1,),
        padding=[(CONV_K - 1, 0)], dimension_numbers=('NWC', 'WIO', 'NWC'),
        feature_group_count=z.shape[-1])


def causal_block_attention(q, k, v, scale):
    length = q.shape[1]
    n_blocks = -(-length // Q_BLOCK)
    qpos = jnp.arange(n_blocks * Q_BLOCK, dtype=jnp.int32).reshape(n_blocks, Q_BLOCK)
    kpos = jnp.arange(length, dtype=jnp.int32)

    def one_block(args):
        qb, pos = args
        s = jnp.einsum('bqhd,bkhd->bhqk', qb, k).astype(jnp.float32) * scale
        s = jnp.where((kpos[None, :] <= pos[:, None])[None, None], s, -jnp.inf)
        p = jax.nn.softmax(s, axis=-1).astype(v.dtype)
        return jnp.einsum('bhqk,bkhd->bqhd', p, v)

    o = lax.map(one_block, (to_query_blocks(q, n_blocks), qpos))
    return from_query_blocks(o, length)


def dsa_attention(q, k, v, qi, ki, wi, topk):
    length = q.shape[1]
    n_blocks = -(-length // Q_BLOCK)
    qpos = jnp.arange(n_blocks * Q_BLOCK, dtype=jnp.int32).reshape(n_blocks, Q_BLOCK)
    kpos = jnp.arange(length, dtype=jnp.int32)
    scale = DSA_HEAD_DIM ** -0.5
    gather = jax.vmap(lambda a, i: a[i])

    def one_block(args):
        qb, qib, wib, pos = args
        causal = kpos[None, :] <= pos[:, None]
        dots = jnp.einsum('bqgd,bkd->bqgk', qib, ki).astype(jnp.float32)
        score = jnp.einsum('bqgk,bqg->bqk', jax.nn.relu(dots), wib.astype(jnp.float32))
        score = jnp.where(causal[None], score, -jnp.inf)
        _, idx = lax.top_k(score, topk)
        valid = idx <= pos[None, :, None]
        kg = gather(k, idx)
        vg = gather(v, idx)
        s = jnp.einsum('bqhd,bqkhd->bhqk', qb, kg).astype(jnp.float32) * scale
        s = jnp.where(valid[:, None], s, -jnp.inf)
        p = jax.nn.softmax(s, axis=-1).astype(v.dtype)
        return jnp.einsum('bhqk,bqkhd->bqhd', p, vg)

    o = lax.map(one_block, (to_query_blocks(q, n_blocks), to_query_blocks(qi, n_blocks),
                            to_query_blocks(wi, n_blocks), qpos))
    return from_query_blocks(o, length)


def multiscale_pool(u, w_pool, scale):
    b, length, _ = u.shape
    n_groups = len(POOL_WINDOWS)
    ug = u.reshape(b, length, n_groups, POOL_GROUP).astype(jnp.float32)
    csum = jnp.pad(jnp.cumsum(ug, axis=1), ((0, 0), (1, 0), (0, 0), (0, 0)))
    t = jnp.arange(length, dtype=jnp.int32)[:, None]
    win = jnp.array(POOL_WINDOWS, dtype=jnp.int32)[None, :]
    lo = jnp.maximum(t + 1 - win, 0)
    window_sum = csum[:, 1:] - csum[:, lo, jnp.arange(n_groups)[None, :]]
    count = jnp.minimum(t + 1, win).astype(jnp.float32)
    pooled = (window_sum / count[None, :, :, None] - ug).astype(u.dtype)
    mixed = jnp.einsum('blgc,gcd->blgd', pooled, w_pool)
    return mixed.reshape(b, length, POOL_WIDTH) * scale


def setup_inputs(seed: int = 0) -> dict:
    key = jax.random.key(seed)
    ks = jax.random.split(key, 24)
    f32 = jnp.float32

    def nrm(k, shape, scale):
        return jax.random.normal(k, shape, f32) * scale

    def gain(k, shape):
        return 1.0 + 0.05 * jax.random.normal(k, shape, f32)

    L_ = DEPTH
    return {
        "x": nrm(ks[0], (BATCH, SEQ, D_MODEL), 1.0),
        "meta_tokens": nrm(ks[1], (N_META, D_MODEL), 1.0),
        "norm_mix_pre": gain(ks[2], (L_, D_MODEL)),
        "norm_mix_post": gain(ks[3], (L_, D_MODEL)),
        "norm_ffn_pre": gain(ks[4], (L_, D_MODEL)),
        "norm_ffn_post": gain(ks[5], (L_, D_MODEL)),
        "w_in": nrm(ks[6], (L_, D_MODEL, N_IN), D_MODEL ** -0.5),
        "conv_w": nrm(ks[7], (L_, CONV_K, CONV_WIDTH), CONV_K ** -0.5),
        "mla_q_norm": gain(ks[8], (L_, MLA_Q_RANK)),
        "mla_w_uq": nrm(ks[9], (L_, MLA_Q_RANK, MLA_HEADS * (MLA_NOPE + MLA_ROPE)), MLA_Q_RANK ** -0.5),
        "mla_kv_norm": gain(ks[10], (L_, MLA_KV_RANK)),
        "mla_w_ukv": nrm(ks[11], (L_, MLA_KV_RANK, MLA_HEADS * (MLA_NOPE + MLA_V)), MLA_KV_RANK ** -0.5),
        "pool_w": nrm(ks[12], (L_, len(POOL_WINDOWS), POOL_GROUP, POOL_GROUP), POOL_GROUP ** -0.5),
        "pool_scale": gain(ks[13], (L_, POOL_WIDTH)),
        "w_branch": nrm(ks[14], (L_, N_BRANCH, BRANCH_WIDTH, D_MODEL), BRANCH_WIDTH ** -0.5),
        "w_gate": nrm(ks[15], (L_, N_BRANCH, D_MODEL, D_MODEL), D_MODEL ** -0.5),
        "b_gate": nrm(ks[16], (L_, N_BRANCH, D_MODEL), 0.01),
        "w_out": nrm(ks[17], (L_, D_MODEL, D_MODEL), D_MODEL ** -0.5),
        "ffn_w_gate": nrm(ks[18], (L_, D_MODEL, D_FF), D_MODEL ** -0.5),
        "ffn_w_up": nrm(ks[19], (L_, D_MODEL, D_FF), D_MODEL ** -0.5),
        "ffn_w_down": nrm(ks[20], (L_, D_FF, D_MODEL), D_FF ** -0.5),
    }


def reference(x, meta_tokens, norm_mix_pre, norm_mix_post, norm_ffn_pre, norm_ffn_post,
              w_in, conv_w, mla_q_norm, mla_w_uq, mla_kv_norm, mla_w_ukv, pool_w, pool_scale,
              w_branch, w_gate, b_gate, w_out, ffn_w_gate, ffn_w_up, ffn_w_down):
    b = x.shape[0]
    meta = jnp.broadcast_to(meta_tokens[None].astype(x.dtype), (b, N_META, D_MODEL))
    h = jnp.concatenate([meta, x], axis=1)
    length = h.shape[1]
    topk = min(IDX_TOPK_MAX, length // 4)
    cos_m, sin_m = rope_tables(length, MLA_ROPE)
    cos_h, sin_h = rope_tables(length, DSA_HEAD_DIM)
    cos_i, sin_i = rope_tables(length, IDX_DIM)
    offsets = np.cumsum(SPLIT_SIZES)[:-1].tolist()
    idx_w_scale = (IDX_HEADS ** -0.5) * (IDX_DIM ** -0.5)

    for l in range(DEPTH):
        xn = rms_norm(h, norm_mix_pre[l])
        proj = xn @ w_in[l]
        (c_b, c_c, c_u, c_q, c_kv, k_r, d_q, d_k, d_v,
         i_q, i_k, i_w, p_u) = jnp.split(proj, offsets, axis=-1)

        y_conv = c_b * short_conv(c_c * c_u, conv_w[l])

        q = (rms_norm(c_q, mla_q_norm[l]) @ mla_w_uq[l]).reshape(b, length, MLA_HEADS, MLA_NOPE + MLA_ROPE)
        q_nope, q_pe = jnp.split(q, [MLA_NOPE], axis=-1)
        kv = (rms_norm(c_kv, mla_kv_norm[l]) @ mla_w_ukv[l]).reshape(b, length, MLA_HEADS, MLA_NOPE + MLA_V)
        k_nope, v_mla = jnp.split(kv, [MLA_NOPE], axis=-1)
        k_pe = apply_rope(k_r.reshape(b, length, 1, MLA_ROPE), cos_m, sin_m)
        q_mla = jnp.concatenate([q_nope, apply_rope(q_pe, cos_m, sin_m)], axis=-1)
        k_mla = jnp.concatenate([k_nope, jnp.broadcast_to(k_pe, (b, length, MLA_HEADS, MLA_ROPE))], axis=-1)
        y_mla = causal_block_attention(q_mla, k_mla, v_mla, (MLA_NOPE + MLA_ROPE) ** -0.5)
        y_mla = y_mla.reshape(b, length, MLA_HEADS * MLA_V)

        dq = apply_rope(d_q.reshape(b, length, DSA_HEADS, DSA_HEAD_DIM), cos_h, sin_h)
        dk = apply_rope(d_k.reshape(b, length, DSA_HEADS, DSA_HEAD_DIM), cos_h, sin_h)
        dv = d_v.reshape(b, length, DSA_HEADS, DSA_HEAD_DIM)
        iq = apply_rope(i_q.reshape(b, length, IDX_HEADS, IDX_DIM), cos_i, sin_i)
        ik = apply_rope(i_k.reshape(b, length, 1, IDX_DIM), cos_i, sin_i)[:, :, 0]
        y_dsa = dsa_attention(dq, dk, dv, iq, ik, i_w * idx_w_scale, topk)
        y_dsa = y_dsa.reshape(b, length, DSA_HEADS * DSA_HEAD_DIM)

        y_pool = multiscale_pool(p_u, pool_w[l], pool_scale[l])

        merged = jnp.zeros_like(h)
        for bi, y_b in enumerate((y_conv, y_mla, y_dsa, y_pool)):
            gate = jax.nn.sigmoid(xn @ w_gate[l, bi] + b_gate[l, bi])
            merged = merged + gate * (y_b @ w_branch[l, bi])
        h = h + rms_norm(merged @ w_out[l], norm_mix_post[l])

        xn = rms_norm(h, norm_ffn_pre[l])
        f = (jax.nn.silu(xn @ ffn_w_gate[l]) * (xn @ ffn_w_up[l])) @ ffn_w_down[l]
        h = h + rms_norm(f, norm_ffn_post[l])

    return h[:, N_META:]
```

```python
import functools

import jax
import jax.numpy as jnp
import numpy as np
from jax import lax
from jax.experimental import pallas as pl
from jax.experimental.pallas import tpu as pltpu

F32 = jnp.float32
BF16 = jnp.bfloat16

D_MODEL = 4096
N_META = 16
ROPE_THETA = 10000.0
EPS = 1e-6
BRANCH = 1024
CONV_K = 3
MLA_NOPE, MLA_ROPE, MLA_V, MLA_HEADS = 128, 64, 128, 8
MLA_Q_RANK, MLA_KV_RANK = 1536, 512
DSA_DIM, DSA_HEADS = 128, 8
IDX_HEADS, IDX_DIM, IDX_TOPK_MAX = 16, 64, 256
POOL_WINDOWS = (2, 4, 8, 16)
POOL_GROUP = 256
D_FF = 11008
D_FF_PAD = 11264

LANES = 128
HALO = 16
ROW_TILE = 768
Q_TILE = 256
K_TILE = 768
IDX_K_TILE = 256
COUNT_CHUNK = 64
NORM_TILE = 192
NEG = -1e30
VMEM_LIMIT = 56 * 1024 * 1024

MAIN_COLS = 10240
SMALL_COLS = 256


def _cp(*sem):
    return pltpu.CompilerParams(dimension_semantics=sem, vmem_limit_bytes=VMEM_LIMIT)


def _sigmoid(x):
    return 1.0 / (1.0 + jnp.exp(-x))


def _rms(x, g):
    return x * lax.rsqrt(jnp.mean(x * x, axis=-1, keepdims=True) + EPS) * g


def _mm_kernel(a_ref, b_ref, o_ref):
    o_ref[...] = jnp.dot(a_ref[...], b_ref[...], preferred_element_type=F32).astype(o_ref.dtype)


def _matmul(a, b, out_dtype, tm, tn, name):
    m, k = a.shape
    n = b.shape[1]
    return pl.pallas_call(
        _mm_kernel,
        grid=(n // tn, m // tm),
        in_specs=[pl.BlockSpec((tm, k), lambda j, i: (i, 0)),
                  pl.BlockSpec((k, tn), lambda j, i: (0, j))],
        out_specs=pl.BlockSpec((tm, tn), lambda j, i: (i, j)),
        out_shape=jax.ShapeDtypeStruct((m, n), out_dtype),
        compiler_params=_cp("parallel", "parallel"),
        name=name,
    )(a, b)


def _swiglu_kernel(a_ref, wg_ref, wu_ref, o_ref):
    a = a_ref[...]
    g = jnp.dot(a, wg_ref[...], preferred_element_type=F32)
    u = jnp.dot(a, wu_ref[...], preferred_element_type=F32)
    o_ref[...] = (g * _sigmoid(g) * u).astype(o_ref.dtype)


def _swiglu(a, wg, wu, tm, tn):
    m, k = a.shape
    n = wg.shape[1]
    return pl.pallas_call(
        _swiglu_kernel,
        grid=(n // tn, m // tm),
        in_specs=[pl.BlockSpec((tm, k), lambda j, i: (i, 0)),
                  pl.BlockSpec((k, tn), lambda j, i: (0, j)),
                  pl.BlockSpec((k, tn), lambda j, i: (0, j))],
        out_specs=pl.BlockSpec((tm, tn), lambda j, i: (i, j)),
        out_shape=jax.ShapeDtypeStruct((m, n), BF16),
        compiler_params=_cp("parallel", "parallel"),
        name="ffn_swiglu",
    )(a, wg, wu)


def _gate_kernel(xn_ref, y0_ref, y1_ref, y2_ref, y3_ref, wg_ref, wb_ref, bg_ref, o_ref):
    xn = xn_ref[...]
    acc = None
    for b, y_ref in enumerate((y0_ref, y1_ref, y2_ref, y3_ref)):
        gate = _sigmoid(jnp.dot(xn, wg_ref[b], preferred_element_type=F32) + bg_ref[b])
        val = jnp.dot(y_ref[...], wb_ref[b], preferred_element_type=F32)
        acc = gate * val if acc is None else acc + gate * val
    o_ref[...] = acc.astype(o_ref.dtype)


def _gate_merge(xn, ys, wg, wb, bg, tm, tn):
    m = xn.shape[0]
    y_spec = pl.BlockSpec((tm, BRANCH), lambda j, i: (i, 0))
    return pl.pallas_call(
        _gate_kernel,
        grid=(D_MODEL // tn, m // tm),
        in_specs=[pl.BlockSpec((tm, D_MODEL), lambda j, i: (i, 0)),
                  y_spec, y_spec, y_spec, y_spec,
                  pl.BlockSpec((4, D_MODEL, tn), lambda j, i: (0, 0, j)),
                  pl.BlockSpec((4, BRANCH, tn), lambda j, i: (0, 0, j)),
                  pl.BlockSpec((4, 1, tn), lambda j, i: (0, 0, j))],
        out_specs=pl.BlockSpec((tm, tn), lambda j, i: (i, j)),
        out_shape=jax.ShapeDtypeStruct((m, D_MODEL), BF16),
        compiler_params=_cp("parallel", "parallel"),
        name="gate_merge",
    )(xn, *ys, wg, wb, bg)


def _norm_kernel(h_ref, g_ref, xn_ref):
    xn_ref[...] = _rms(h_ref[...], g_ref[...]).astype(xn_ref.dtype)


def _prenorm(h, g):
    m = h.shape[0]
    return pl.pallas_call(
        _norm_kernel,
        grid=(m // NORM_TILE,),
        in_specs=[pl.BlockSpec((NORM_TILE, D_MODEL), lambda i: (i, 0)),
                  pl.BlockSpec((1, D_MODEL), lambda i: (0, 0))],
        out_specs=pl.BlockSpec((NORM_TILE, D_MODEL), lambda i: (i, 0)),
        out_shape=jax.ShapeDtypeStruct((m, D_MODEL), BF16),
        compiler_params=_cp("parallel"),
        name="prenorm",
    )(h, g)


def _resid_norm_kernel(h_ref, o_ref, gpost_ref, gnext_ref, hn_ref, xn_ref):
    hn = h_ref[...] + _rms(o_ref[...], gpost_ref[...])
    hn_ref[...] = hn
    xn_ref[...] = _rms(hn, gnext_ref[...]).astype(xn_ref.dtype)


def _resid_norm(h, o, g_post, g_next):
    m = h.shape[0]
    row = pl.BlockSpec((NORM_TILE, D_MODEL), lambda i: (i, 0))
    gain = pl.BlockSpec((1, D_MODEL), lambda i: (0, 0))
    return pl.pallas_call(
        _resid_norm_kernel,
        grid=(m // NORM_TILE,),
        in_specs=[row, row, gain, gain],
        out_specs=[row, row],
        out_shape=[jax.ShapeDtypeStruct((m, D_MODEL), F32),
                   jax.ShapeDtypeStruct((m, D_MODEL), BF16)],
        compiler_params=_cp("parallel"),
        name="resid_norm",
    )(h, o, g_post, g_next)


def _rope_half128(x, cos, sin_signed):
    return x * cos + pltpu.roll(x, 64, axis=1) * sin_signed


def _rope_half64(x, cos, sin_signed):
    lane = lax.broadcasted_iota(jnp.int32, x.shape, 1)
    partner = jnp.where((lane & 32) == 0, pltpu.roll(x, 96, axis=1), pltpu.roll(x, 32, axis=1))
    return x * cos + partner * sin_signed


def _conv_kernel(cb_ref, cc_ref, cu_ref, pc_ref, pu_ref, w_ref, y_ref, z_ref):
    tr = cb_ref.shape[0]
    i = pl.program_id(0)
    z = cc_ref[...].astype(F32) * cu_ref[...].astype(F32)
    zp = pc_ref[...].astype(F32) * pu_ref[...].astype(F32)
    z_ref[0:HALO, :] = jnp.where(i > 0, zp, 0.0)
    z_ref[HALO:HALO + tr, :] = z
    w = w_ref[...]
    conv = (w[2:3, :] * z
            + w[1:2, :] * z_ref[HALO - 1:HALO - 1 + tr, :]
            + w[0:1, :] * z_ref[HALO - 2:HALO - 2 + tr, :])
    y_ref[...] = (cb_ref[...].astype(F32) * conv).astype(y_ref.dtype)


def _conv_branch(proj, conv_w, tr):
    m = proj.shape[0]
    cw = 256
    nb = BRANCH // cw

    def halo_row(i):
        return jnp.maximum(i * (tr // HALO) - 1, 0)

    return pl.pallas_call(
        _conv_kernel,
        grid=(m // tr, nb),
        in_specs=[pl.BlockSpec((tr, cw), lambda i, c: (i, c)),
                  pl.BlockSpec((tr, cw), lambda i, c: (i, nb + c)),
                  pl.BlockSpec((tr, cw), lambda i, c: (i, 2 * nb + c)),
                  pl.BlockSpec((HALO, cw), lambda i, c: (halo_row(i), nb + c)),
                  pl.BlockSpec((HALO, cw), lambda i, c: (halo_row(i), 2 * nb + c)),
                  pl.BlockSpec((CONV_K, cw), lambda i, c: (0, c))],
        out_specs=pl.BlockSpec((tr, cw), lambda i, c: (i, c)),
        out_shape=jax.ShapeDtypeStruct((m, BRANCH), BF16),
        scratch_shapes=[pltpu.VMEM((HALO + tr, cw), F32)],
        compiler_params=_cp("parallel", "parallel"),
        name="conv_branch",
    )(proj, proj, proj, proj, proj, conv_w)


def _pool_kernel(u_ref, pu_ref, w_ref, s_ref, y_ref, x_ref):
    tr = u_ref.shape[0]
    i = pl.program_id(0)
    g = pl.program_id(1)
    win = jnp.left_shift(jnp.int32(2), g)
    x = u_ref[...].astype(F32)
    x_ref[0:HALO, :] = jnp.where(i > 0, pu_ref[...].astype(F32), 0.0)
    x_ref[HALO:HALO + tr, :] = x
    total = x
    for j in range(1, max(POOL_WINDOWS)):
        keep = jnp.where(j < win, 1.0, 0.0).astype(F32)
        total = total + keep * x_ref[HALO - j:HALO - j + tr, :]
    t = i * tr + lax.broadcasted_iota(jnp.int32, x.shape, 0)
    count = jnp.minimum(t + 1, win).astype(F32)
    pooled = (total / count - x).astype(BF16)
    mixed = jnp.dot(pooled, w_ref[0], preferred_element_type=F32)
    y_ref[...] = (mixed * s_ref[...]).astype(y_ref.dtype)


def _pool_branch(proj, pool_w, pool_scale, tr):
    m = proj.shape[0]
    base = (MAIN_COLS - BRANCH) // POOL_GROUP
    ng = len(POOL_WINDOWS)

    def halo_row(i):
        return jnp.maximum(i * (tr // HALO) - 1, 0)

    return pl.pallas_call(
        _pool_kernel,
        grid=(m // tr, ng),
        in_specs=[pl.BlockSpec((tr, POOL_GROUP), lambda i, g: (i, base + g)),
                  pl.BlockSpec((HALO, POOL_GROUP), lambda i, g: (halo_row(i), base + g)),
                  pl.BlockSpec((1, POOL_GROUP, POOL_GROUP), lambda i, g: (g, 0, 0)),
                  pl.BlockSpec((1, POOL_GROUP), lambda i, g: (0, g))],
        out_specs=pl.BlockSpec((tr, POOL_GROUP), lambda i, g: (i, g)),
        out_shape=jax.ShapeDtypeStruct((m, BRANCH), BF16),
        scratch_shapes=[pltpu.VMEM((HALO + tr, POOL_GROUP), F32)],
        compiler_params=_cp("parallel", "parallel"),
        name="pool_branch",
    )(proj, proj, pool_w, pool_scale)


def _mla_prep_kernel(cq_ref, ckv_ref, sm_ref, gq_ref, gkv_ref, wuq_ref, wukv_ref,
                     cos_ref, sin_ref, q_ref, kn_ref, kpe_ref, vt_ref):
    cos = cos_ref[...]
    sin = sin_ref[...]
    scale = (MLA_NOPE + MLA_ROPE) ** -0.5
    cqn = _rms(cq_ref[...].astype(F32), gq_ref[...]).astype(BF16)
    q = jnp.dot(cqn, wuq_ref[...], preferred_element_type=F32)
    for h in range(MLA_HEADS):
        lo = 2 * h * LANES
        q_ref[:, lo:lo + LANES] = (q[:, lo:lo + LANES] * scale).astype(q_ref.dtype)
        pe = _rope_half64(q[:, lo + LANES:lo + 2 * LANES], cos, sin)
        q_ref[:, lo + LANES:lo + 2 * LANES] = (pe * scale).astype(q_ref.dtype)
    ckvn = _rms(ckv_ref[...].astype(F32), gkv_ref[...]).astype(BF16)
    kv = jnp.dot(ckvn, wukv_ref[...], preferred_element_type=F32)
    kn_ref[...] = kv[:, :BRANCH].astype(kn_ref.dtype)
    vt_ref[0] = kv[:, BRANCH:].T.astype(vt_ref.dtype)
    kpe_ref[...] = _rope_half64(sm_ref[:, 0:LANES], cos, sin).astype(kpe_ref.dtype)


def _mla_prep(proj, small, gq, gkv, wuq, wukv, cos64, sin64, tr):
    m = proj.shape[0]
    qw = 2 * LANES * MLA_HEADS
    return pl.pallas_call(
        _mla_prep_kernel,
        grid=(m // tr,),
        in_specs=[pl.BlockSpec((tr, MLA_Q_RANK), lambda i: (i, 3 * BRANCH // MLA_Q_RANK)),
                  pl.BlockSpec((tr, MLA_KV_RANK), lambda i: (i, (3 * BRANCH + MLA_Q_RANK) // MLA_KV_RANK)),
                  pl.BlockSpec((tr, SMALL_COLS), lambda i: (i, 0)),
                  pl.BlockSpec((1, MLA_Q_RANK), lambda i: (0, 0)),
                  pl.BlockSpec((1, MLA_KV_RANK), lambda i: (0, 0)),
                  pl.BlockSpec((MLA_Q_RANK, qw), lambda i: (0, 0)),
                  pl.BlockSpec((MLA_KV_RANK, 2 * BRANCH), lambda i: (0, 0)),
                  pl.BlockSpec((tr, LANES), lambda i: (i, 0)),
                  pl.BlockSpec((tr, LANES), lambda i: (i, 0))],
        out_specs=[pl.BlockSpec((tr, qw), lambda i: (i, 0)),
                   pl.BlockSpec((tr, BRANCH), lambda i: (i, 0)),
                   pl.BlockSpec((tr, LANES), lambda i: (i, 0)),
                   pl.BlockSpec((1, BRANCH, tr), lambda i: (i, 0, 0))],
        out_shape=[jax.ShapeDtypeStruct((m, qw), BF16),
                   jax.ShapeDtypeStruct((m, BRANCH), BF16),
                   jax.ShapeDtypeStruct((m, LANES), BF16),
                   jax.ShapeDtypeStruct((m // tr, BRANCH, tr), BF16)],
        compiler_params=_cp("parallel"),
        name="mla_prep",
    )(proj, proj, small, gq, gkv, wuq, wukv, cos64, sin64)


def _dsa_prep_kernel(dq_ref, dk_ref, dv_ref, iq_ref, sm_ref, c128_ref, s128_ref, c64_ref, s64_ref,
                     q_ref, k_ref, vt_ref, iqr_ref, ik_ref, wt_ref):
    c128, s128 = c128_ref[...], s128_ref[...]
    c64, s64 = c64_ref[...], s64_ref[...]
    scale = DSA_DIM ** -0.5
    for h in range(BRANCH // LANES):
        sl = slice(h * LANES, (h + 1) * LANES)
        q_ref[:, sl] = (_rope_half128(dq_ref[:, sl].astype(F32), c128, s128) * scale).astype(q_ref.dtype)
        k_ref[:, sl] = _rope_half128(dk_ref[:, sl].astype(F32), c128, s128).astype(k_ref.dtype)
        iqr_ref[:, sl] = _rope_half64(iq_ref[:, sl].astype(F32), c64, s64).astype(iqr_ref.dtype)
    vt_ref[0] = dv_ref[...].astype(F32).T.astype(vt_ref.dtype)
    tail = sm_ref[:, LANES:2 * LANES]
    lane = lax.broadcasted_iota(jnp.int32, tail.shape, 1)
    ik_lo = jnp.where(lane < IDX_DIM, _rope_half64(tail, c64, s64), 0.0)
    ik_ref[:, 0:LANES] = ik_lo.astype(ik_ref.dtype)
    ik_ref[:, LANES:2 * LANES] = pltpu.roll(ik_lo, IDX_DIM, axis=1).astype(ik_ref.dtype)
    idx_w_scale = (IDX_HEADS ** -0.5) * (IDX_DIM ** -0.5)
    wt_ref[...] = (tail * idx_w_scale).T


def _dsa_prep(proj, small, c128, s128, c64, s64, tr):
    m = proj.shape[0]
    col = lambda c: pl.BlockSpec((tr, BRANCH), lambda i: (i, c))
    tab = pl.BlockSpec((tr, LANES), lambda i: (i, 0))
    return pl.pallas_call(
        _dsa_prep_kernel,
        grid=(m // tr,),
        in_specs=[col(5), col(6), col(7), col(8),
                  pl.BlockSpec((tr, SMALL_COLS), lambda i: (i, 0)),
                  tab, tab, tab, tab],
        out_specs=[pl.BlockSpec((tr, BRANCH), lambda i: (i, 0)),
                   pl.BlockSpec((tr, BRANCH), lambda i: (i, 0)),
                   pl.BlockSpec((1, BRANCH, tr), lambda i: (i, 0, 0)),
                   pl.BlockSpec((tr, BRANCH), lambda i: (i, 0)),
                   pl.BlockSpec((tr, 2 * LANES), lambda i: (i, 0)),
                   pl.BlockSpec((LANES, tr), lambda i: (0, i))],
        out_shape=[jax.ShapeDtypeStruct((m, BRANCH), BF16),
                   jax.ShapeDtypeStruct((m, BRANCH), BF16),
                   jax.ShapeDtypeStruct((m // tr, BRANCH, tr), BF16),
                   jax.ShapeDtypeStruct((m, BRANCH), BF16),
                   jax.ShapeDtypeStruct((m, 2 * LANES), BF16),
                   jax.ShapeDtypeStruct((LANES, m), F32)],
        compiler_params=_cp("parallel"),
        name="dsa_prep",
    )(proj, proj, proj, proj, small, c128, s128, c64, s64)


def _float_to_ordered_int(s):
    b = lax.bitcast_convert_type(s, jnp.int32)
    return b ^ ((b >> 31) & jnp.int32(0x7FFFFFFF))


def _indexer_kernel(iq_ref, ik_ref, wt_ref, bias_ref, key_ref, cut_ref, *, topk):
    tq = iq_ref.shape[0]
    total_rows = ik_ref.shape[0]
    tk = IDX_K_TILE
    i = pl.program_id(0)
    n_tiles = (i * tq + tq) // tk
    n_rows = n_tiles * tk
    qpos = i * tq + lax.broadcasted_iota(jnp.int32, (1, tq), 1)
    int_min = jnp.int32(-2 ** 31)

    def score_tile(kt, carry):
        start = pl.multiple_of(kt * tk, tk)
        ik_lo = ik_ref[pl.ds(start, tk), 0:LANES]
        ik_hi = ik_ref[pl.ds(start, tk), LANES:2 * LANES]
        acc = jnp.zeros((tk, tq), F32)
        for j in range(IDX_HEADS // 2):
            qpair = iq_ref[:, j * LANES:(j + 1) * LANES]
            for half, ik in enumerate((ik_lo, ik_hi)):
                g = 2 * j + half
                dots = lax.dot_general(ik, qpair, (((1,), (1,)), ((), ())),
                                       preferred_element_type=F32)
                acc = acc + jnp.maximum(dots, 0.0) * wt_ref[IDX_DIM + g:IDX_DIM + g + 1, :]
        acc = acc + 0.0
        kpos = start + lax.broadcasted_iota(jnp.int32, (tk, 1), 0)
        key_ref[pl.ds(start, tk), :] = jnp.where(kpos <= qpos, _float_to_ordered_int(acc), int_min)
        return carry

    lax.fori_loop(0, n_tiles, score_tile, 0)

    n_chunks = n_rows // COUNT_CHUNK

    def count(pred):
        def body(c, acc):
            start = pl.multiple_of(c * COUNT_CHUNK, COUNT_CHUNK)
            blk = key_ref[pl.ds(start, COUNT_CHUNK), :]
            pos = start + lax.broadcasted_iota(jnp.int32, (COUNT_CHUNK, 1), 0)
            ind = jnp.where(pred(blk, pos), 1, 0).astype(jnp.int32)
            for r in range(COUNT_CHUNK // 8):
                acc = acc + ind[8 * r:8 * r + 8, :]
            return acc
        acc = lax.fori_loop(0, n_chunks, body, jnp.zeros((8, tq), jnp.int32))
        return jnp.sum(acc, axis=0, keepdims=True)

    def bit_step(b, tau):
        cand = tau + jnp.left_shift(jnp.int32(1), 31 - b)
        cnt = count(lambda blk, pos: blk >= cand)
        return jnp.where(cnt >= topk, cand, tau)

    tau = lax.fori_loop(0, 32, bit_step, jnp.full((1, tq), int_min, jnp.int32))

    n_gt = count(lambda blk, pos: blk > tau)
    n_eq = count(lambda blk, pos: blk == tau)
    need = topk - n_gt
    cut_bits = 14
    cut_ref[...] = jnp.full((1, tq), 1 << cut_bits, jnp.int32)

    @pl.when(jnp.max(n_eq - need) > 0)
    def _():
        def cut_step(b, cut):
            cand = cut + jnp.left_shift(jnp.int32(1), cut_bits - 1 - b)
            cnt = count(lambda blk, pos: (blk == tau) & (pos < cand))
            return jnp.where(cnt <= need, cand, cut)
        cut_ref[...] = lax.fori_loop(0, cut_bits, cut_step, jnp.zeros((1, tq), jnp.int32))

    cut = cut_ref[...]

    def write_sel(c, carry):
        start = pl.multiple_of(c * COUNT_CHUNK, COUNT_CHUNK)
        blk = key_ref[pl.ds(start, COUNT_CHUNK), :]
        pos = start + lax.broadcasted_iota(jnp.int32, (COUNT_CHUNK, 1), 0)
        sel = ((blk > tau) | ((blk == tau) & (pos < cut))) & (pos <= qpos)
        bias_ref[pl.ds(start, COUNT_CHUNK), :] = jnp.where(sel, 0.0, NEG).astype(bias_ref.dtype)
        return carry

    lax.fori_loop(0, n_chunks, write_sel, 0)

    def write_neg(c, carry):
        start = pl.multiple_of(c * COUNT_CHUNK, COUNT_CHUNK)
        bias_ref[pl.ds(start, COUNT_CHUNK), :] = jnp.full((COUNT_CHUNK, tq), NEG, bias_ref.dtype)
        return carry

    lax.fori_loop(n_chunks, total_rows // COUNT_CHUNK, write_neg, 0)


def _indexer(iq_r, ik_ab, wt, topk):
    m = iq_r.shape[0]
    return pl.pallas_call(
        functools.partial(_indexer_kernel, topk=topk),
        grid=(m // Q_TILE,),
        in_specs=[pl.BlockSpec((Q_TILE, BRANCH), lambda i: (i, 0)),
                  pl.BlockSpec((m, 2 * LANES), lambda i: (0, 0)),
                  pl.BlockSpec((LANES, Q_TILE), lambda i: (0, i))],
        out_specs=pl.BlockSpec((m, Q_TILE), lambda i: (0, i)),
        out_shape=jax.ShapeDtypeStruct((m, m), BF16),
        scratch_shapes=[pltpu.VMEM((m, Q_TILE), jnp.int32),
                        pltpu.VMEM((1, Q_TILE), jnp.int32)],
        compiler_params=_cp("parallel"),
        name="dsa_indexer",
    )(iq_r, ik_ab, wt)


def _flash_kernel(*refs, q_axis, has_kpe, has_bias):
    refs = list(refs)
    q_ref, k_ref = refs[0], refs[1]
    pos = 2
    kpe_ref = bias_ref = None
    if has_kpe:
        kpe_ref = refs[pos]
        pos += 1
    vt_ref = refs[pos]
    pos += 1
    if has_bias:
        bias_ref = refs[pos]
        pos += 1
    o_ref = refs[pos]

    tq = q_ref.shape[0]
    dv = vt_ref.shape[1]
    tk = vt_ref.shape[2]
    i = pl.program_id(q_axis)
    n_tiles = (i * tq + tq + tk - 1) // tk
    qpos = i * tq + lax.broadcasted_iota(jnp.int32, (1, tq), 1)
    q = q_ref[...]

    def body(kt, carry):
        m_run, l_run, acc = carry
        start = pl.multiple_of(kt * tk, tk)
        k = k_ref[pl.ds(start, tk), :]
        if has_kpe:
            k = jnp.concatenate([k, kpe_ref[pl.ds(start, tk), :]], axis=1)
        s = lax.dot_general(k, q, (((1,), (1,)), ((), ())), preferred_element_type=F32)
        if has_bias:
            s = s + bias_ref[pl.ds(start, tk), :].astype(F32)
        kpos = start + lax.broadcasted_iota(jnp.int32, (tk, 1), 0)
        s = jnp.where(kpos <= qpos, s, NEG)
        m_new = jnp.maximum(m_run, jnp.max(s, axis=0, keepdims=True))
        alpha = jnp.exp(m_run - m_new)
        p = jnp.exp(s - m_new)
        l_new = alpha * l_run + jnp.sum(p, axis=0, keepdims=True)
        pv = jnp.dot(vt_ref[kt], p.astype(BF16), preferred_element_type=F32)
        return m_new, l_new, alpha * acc + pv

    init = (jnp.full((1, tq), NEG, F32), jnp.zeros((1, tq), F32), jnp.zeros((dv, tq), F32))
    _, l_fin, acc = lax.fori_loop(0, n_tiles, body, init)
    o_ref[...] = (acc / l_fin).T.astype(o_ref.dtype)


def _mla_attention(q, kn, kpe, vt):
    m = q.shape[0]
    n_kt, _, tk = vt.shape
    return pl.pallas_call(
        functools.partial(_flash_kernel, q_axis=1, has_kpe=True, has_bias=False),
        grid=(MLA_HEADS, m // Q_TILE),
        in_specs=[pl.BlockSpec((Q_TILE, 2 * LANES), lambda h, i: (i, h)),
                  pl.BlockSpec((m, LANES), lambda h, i: (0, h)),
                  pl.BlockSpec((m, LANES), lambda h, i: (0, 0)),
                  pl.BlockSpec((n_kt, MLA_V, tk), lambda h, i: (0, h, 0))],
        out_specs=pl.BlockSpec((Q_TILE, MLA_V), lambda h, i: (i, h)),
        out_shape=jax.ShapeDtypeStruct((m, BRANCH), BF16),
        compiler_params=_cp("parallel", "parallel"),
        name="mla_attention",
    )(q, kn, kpe, vt)


def _dsa_attention(q, k, vt, bias):
    m = q.shape[0]
    n_kt, _, tk = vt.shape
    return pl.pallas_call(
        functools.partial(_flash_kernel, q_axis=0, has_kpe=False, has_bias=True),
        grid=(m // Q_TILE, DSA_HEADS),
        in_specs=[pl.BlockSpec((Q_TILE, DSA_DIM), lambda i, h: (i, h)),
                  pl.BlockSpec((m, DSA_DIM), lambda i, h: (0, h)),
                  pl.BlockSpec((n_kt, DSA_DIM, tk), lambda i, h: (0, h, 0)),
                  pl.BlockSpec((m, Q_TILE), lambda i, h: (0, i))],
        out_specs=pl.BlockSpec((Q_TILE, DSA_DIM), lambda i, h: (i, h)),
        out_shape=jax.ShapeDtypeStruct((m, BRANCH), BF16),
        compiler_params=_cp("parallel", "parallel"),
        name="dsa_attention",
    )(q, k, vt, bias)


def _rope_tables(rows, dim):
    inv = 1.0 / jnp.power(ROPE_THETA, jnp.arange(0, dim, 2, dtype=F32) / dim)
    ang = jnp.arange(rows, dtype=F32)[:, None] * inv[None, :]
    cos, sin = jnp.cos(ang), jnp.sin(ang)
    reps = LANES // dim
    return (jnp.tile(jnp.concatenate([cos, cos], axis=1), (1, reps)),
            jnp.tile(jnp.concatenate([-sin, sin], axis=1), (1, reps)))


def _split_w_in(w):
    o_kr = 3 * BRANCH + MLA_Q_RANK + MLA_KV_RANK
    o_dq = o_kr + MLA_ROPE
    o_ik = o_dq + 4 * BRANCH
    o_iw = o_ik + IDX_DIM
    o_pu = o_iw + IDX_HEADS
    main = jnp.concatenate([w[:, :o_kr], w[:, o_dq:o_ik], w[:, o_pu:]], axis=1)
    zeros = lambda n: jnp.zeros((w.shape[0], n), w.dtype)
    small = jnp.concatenate([w[:, o_kr:o_dq], zeros(LANES - MLA_ROPE),
                             w[:, o_ik:o_iw], w[:, o_iw:o_pu],
                             zeros(LANES - IDX_DIM - IDX_HEADS)], axis=1)
    return main.astype(BF16), small.astype(BF16)


def _layout_w_uq(w):
    w3 = w.reshape(MLA_Q_RANK, MLA_HEADS, MLA_NOPE + MLA_ROPE)
    w3 = jnp.pad(w3, ((0, 0), (0, 0), (0, 2 * LANES - MLA_NOPE - MLA_ROPE)))
    return w3.reshape(MLA_Q_RANK, MLA_HEADS * 2 * LANES).astype(BF16)


def _layout_w_ukv(w):
    w4 = w.reshape(MLA_KV_RANK, MLA_HEADS, 2, MLA_NOPE)
    return w4.transpose(0, 2, 1, 3).reshape(MLA_KV_RANK, 2 * BRANCH).astype(BF16)


def _forward(x, meta_tokens, norm_mix_pre, norm_mix_post, norm_ffn_pre, norm_ffn_post,
             w_in, conv_w, mla_q_norm, mla_w_uq, mla_kv_norm, mla_w_ukv, pool_w, pool_scale,
             w_branch, w_gate, b_gate, w_out, ffn_w_gate, ffn_w_up, ffn_w_down):
    assert x.shape[0] == 1 and x.shape[2] == D_MODEL
    depth = w_in.shape[0]
    seq = x.shape[1]
    length = N_META + seq
    topk = min(IDX_TOPK_MAX, length // 4)
    rows = -(-length // ROW_TILE) * ROW_TILE
    assert rows % Q_TILE == 0 and rows % K_TILE == 0 and rows % NORM_TILE == 0
    assert rows < (1 << 14)

    h = jnp.concatenate([meta_tokens.astype(F32), x[0],
                         jnp.zeros((rows - length, D_MODEL), F32)], axis=0)
    c64, s64 = _rope_tables(rows, 64)
    c128, s128 = _rope_tables(rows, 128)
    ff_pad = D_FF_PAD - D_FF

    xn = _prenorm(h, norm_mix_pre[0][None])
    for l in range(depth):
        w_main, w_small = _split_w_in(w_in[l])
        proj = _matmul(xn, w_main, BF16, ROW_TILE, 1024, "in_proj")
        small = _matmul(xn, w_small, F32, ROW_TILE, SMALL_COLS, "in_proj_small")

        y_conv = _conv_branch(proj, conv_w[l], ROW_TILE)
        y_pool = _pool_branch(proj, pool_w[l].astype(BF16), pool_scale[l][None], ROW_TILE)

        q_m, kn_m, kpe_m, vt_m = _mla_prep(
            proj, small, mla_q_norm[l][None], mla_kv_norm[l][None],
            _layout_w_uq(mla_w_uq[l]), _layout_w_ukv(mla_w_ukv[l]), c64, s64, K_TILE)
        y_mla = _mla_attention(q_m, kn_m, kpe_m, vt_m)

        q_d, k_d, vt_d, iq_r, ik_ab, wt = _dsa_prep(proj, small, c128, s128, c64, s64, K_TILE)
        bias = _indexer(iq_r, ik_ab, wt, topk)
        y_dsa = _dsa_attention(q_d, k_d, vt_d, bias)

        merged = _gate_merge(xn, (y_conv, y_mla, y_dsa, y_pool), w_gate[l].astype(BF16),
                             w_branch[l].astype(BF16), b_gate[l][:, None, :], 384, 256)
        mix = _matmul(merged, w_out[l].astype(BF16), F32, ROW_TILE, 1024, "out_proj")
        h, xn = _resid_norm(h, mix, norm_mix_post[l][None], norm_ffn_pre[l][None])

        wg = jnp.pad(ffn_w_gate[l], ((0, 0), (0, ff_pad))).astype(BF16)
        wu = jnp.pad(ffn_w_up[l], ((0, 0), (0, ff_pad))).astype(BF16)
        wd = jnp.pad(ffn_w_down[l], ((0, ff_pad), (0, 0))).astype(BF16)
        act = _swiglu(xn, wg, wu, ROW_TILE, 512)
        f = _matmul(act, wd, F32, 384, 512, "ffn_down")
        g_next = norm_mix_pre[l + 1] if l + 1 < depth else norm_mix_pre[l]
        h, xn = _resid_norm(h, f, norm_ffn_post[l][None], g_next[None])

    return h[N_META:length][None]


def kernel(x, meta_tokens, norm_mix_pre, norm_mix_post, norm_ffn_pre, norm_ffn_post, w_in, conv_w, mla_q_norm, mla_w_uq, mla_kv_norm, mla_w_ukv, pool_w, pool_scale, w_branch, w_gate, b_gate, w_out, ffn_w_gate, ffn_w_up, ffn_w_down):
    return _forward(x, meta_tokens, norm_mix_pre, norm_mix_post, norm_ffn_pre, norm_ffn_post,
                    w_in, conv_w, mla_q_norm, mla_w_uq, mla_kv_norm, mla_w_ukv, pool_w, pool_scale,
                    w_branch, w_gate, b_gate, w_out, ffn_w_gate, ffn_w_up, ffn_w_down)
```

```python
import functools
import math

import jax
import jax.numpy as jnp
from jax import lax
from jax.experimental import pallas as pl
from jax.experimental.pallas import tpu as pltpu

F32 = jnp.float32
BF16 = jnp.bfloat16

D_MODEL = 4096
N_META = 16
ROPE_THETA = 10000.0
EPS = 1e-6
N_BRANCH = 4
BRANCH = 1024
CONV_K = 3
MLA_NOPE, MLA_ROPE, MLA_V, MLA_HEADS = 128, 64, 128, 8
MLA_Q_RANK, MLA_KV_RANK = 1536, 512
DSA_DIM, DSA_HEADS = 128, 8
IDX_HEADS, IDX_DIM, IDX_TOPK_MAX = 16, 64, 256
POOL_WINDOWS = (2, 4, 8, 16)
POOL_GROUP = 256
D_FF = 11008

LANES = 128
HALO = 16
ROW_TILE = 768
Q_TILE = 256
K_TILE = 768
V_ROWS = 144
HEADS_PER_STEP = 2
IDX_K_TILE = 256
COUNT_CHUNK = 256
COUNT_ACCS = 4
WRITE_CHUNK = 64
NORM_TILE = 192
NEG = -1e30
LOG2E = math.log2(math.e)
VMEM_LIMIT = 58 * 1024 * 1024

HALF_COLS = 5120
SMALL_COLS = 256


def _cp(*sem):
    return pltpu.CompilerParams(dimension_semantics=sem, vmem_limit_bytes=VMEM_LIMIT)


def _sigmoid(x):
    return 1.0 / (1.0 + jnp.exp(-x))


def _rms(x, g):
    return x * lax.rsqrt(jnp.mean(x * x, axis=-1, keepdims=True) + EPS) * g


def _cast_weights_once(w_refs, wbf_refs):
    @pl.when(pl.program_id(1) == 0)
    def _():
        for w_ref, wbf_ref in zip(w_refs, wbf_refs):
            wbf_ref[...] = w_ref[...].astype(wbf_ref.dtype)


def _proj_kernel(a_ref, w_ref, o_ref, wbf_ref):
    _cast_weights_once((w_ref,), (wbf_ref,))
    o_ref[...] = jnp.dot(a_ref[...], wbf_ref[...], preferred_element_type=F32).astype(o_ref.dtype)


def _gate_kernel(a_ref, w_ref, b_ref, o_ref, wbf_ref):
    _cast_weights_once((w_ref,), (wbf_ref,))
    z = jnp.dot(a_ref[...], wbf_ref[...], preferred_element_type=F32) + b_ref[...]
    o_ref[...] = _sigmoid(z).astype(o_ref.dtype)


def _stacked_proj(a, w, lead_of, col_of, n_tiles, tm, tn, out_dtype, name, bias=None,
                  single_buffer_weight=False):
    m, k = a.shape
    assert w.shape[1] == k
    w_kwargs = {"pipeline_mode": pl.Buffered(1)} if single_buffer_weight else {}
    in_specs = [pl.BlockSpec((tm, k), lambda j, i: (i, 0)),
                pl.BlockSpec((None, k, tn), lambda j, i: (lead_of(j), 0, col_of(j)), **w_kwargs)]
    args = [a, w]
    body = _proj_kernel
    if bias is not None:
        in_specs.append(pl.BlockSpec((None, 1, tn), lambda j, i: (lead_of(j), 0, col_of(j))))
        args.append(bias)
        body = _gate_kernel
    return pl.pallas_call(
        body,
        grid=(n_tiles, m // tm),
        in_specs=in_specs,
        out_specs=pl.BlockSpec((tm, tn), lambda j, i: (i, j)),
        out_shape=jax.ShapeDtypeStruct((m, n_tiles * tn), out_dtype),
        scratch_shapes=[pltpu.VMEM((k, tn), BF16)],
        compiler_params=_cp("parallel", "arbitrary"),
        name=name,
    )(*args)


def _swiglu_kernel(a_ref, wg_ref, wu_ref, o_ref, wgbf_ref, wubf_ref):
    _cast_weights_once((wg_ref, wu_ref), (wgbf_ref, wubf_ref))
    a = a_ref[...]
    g = jnp.dot(a, wgbf_ref[...], preferred_element_type=F32)
    u = jnp.dot(a, wubf_ref[...], preferred_element_type=F32)
    o_ref[...] = (g * _sigmoid(g) * u).astype(o_ref.dtype)


def _swiglu(a, wg, wu, layer, tm, tn):
    m, k = a.shape
    n = wg.shape[2]
    w_spec = pl.BlockSpec((None, k, tn), lambda j, i: (layer, 0, j))
    return pl.pallas_call(
        _swiglu_kernel,
        grid=(n // tn, m // tm),
        in_specs=[pl.BlockSpec((tm, k), lambda j, i: (i, 0)), w_spec, w_spec],
        out_specs=pl.BlockSpec((tm, tn), lambda j, i: (i, j)),
        out_shape=jax.ShapeDtypeStruct((m, n), BF16),
        scratch_shapes=[pltpu.VMEM((k, tn), BF16), pltpu.VMEM((k, tn), BF16)],
        compiler_params=_cp("parallel", "arbitrary"),
        name="ffn_swiglu",
    )(a, wg, wu)


def _merge_kernel(y0_ref, y1_ref, y2_ref, y3_ref, g0_ref, g1_ref, g2_ref, g3_ref, w_ref,
                  o_ref, wbf_ref):
    _cast_weights_once((w_ref,), (wbf_ref,))
    acc = None
    for b, (y_ref, g_ref) in enumerate(((y0_ref, g0_ref), (y1_ref, g1_ref),
                                        (y2_ref, g2_ref), (y3_ref, g3_ref))):
        val = g_ref[...].astype(F32) * jnp.dot(y_ref[...], wbf_ref[b], preferred_element_type=F32)
        acc = val if acc is None else acc + val
    o_ref[...] = acc.astype(o_ref.dtype)


def _gated_merge(ys, gates, w_branch, layer, tm, tn):
    m = ys[0].shape[0]
    nj = D_MODEL // tn
    y_spec = pl.BlockSpec((tm, BRANCH), lambda j, i: (i, 0))
    g_specs = [pl.BlockSpec((tm, tn), functools.partial(lambda j, i, b: (i, b * nj + j), b=b))
               for b in range(N_BRANCH)]
    return pl.pallas_call(
        _merge_kernel,
        grid=(nj, m // tm),
        in_specs=[y_spec] * N_BRANCH + g_specs
                 + [pl.BlockSpec((None, N_BRANCH, BRANCH, tn), lambda j, i: (layer, 0, 0, j))],
        out_specs=pl.BlockSpec((tm, tn), lambda j, i: (i, j)),
        out_shape=jax.ShapeDtypeStruct((m, D_MODEL), BF16),
        scratch_shapes=[pltpu.VMEM((N_BRANCH, BRANCH, tn), BF16)],
        compiler_params=_cp("parallel", "arbitrary"),
        name="gated_merge",
    )(*ys, gates, gates, gates, gates, w_branch)


def _norm_kernel(h_ref, g_ref, xn_ref):
    xn_ref[...] = _rms(h_ref[...], g_ref[...]).astype(xn_ref.dtype)


def _prenorm(h, g):
    m = h.shape[0]
    return pl.pallas_call(
        _norm_kernel,
        grid=(m // NORM_TILE,),
        in_specs=[pl.BlockSpec((NORM_TILE, D_MODEL), lambda i: (i, 0)),
                  pl.BlockSpec((1, D_MODEL), lambda i: (0, 0))],
        out_specs=pl.BlockSpec((NORM_TILE, D_MODEL), lambda i: (i, 0)),
        out_shape=jax.ShapeDtypeStruct((m, D_MODEL), BF16),
        compiler_params=_cp("parallel"),
        name="prenorm",
    )(h, g)


def _resid_norm_kernel(h_ref, o_ref, gpost_ref, gnext_ref, hn_ref, xn_ref):
    hn = h_ref[...] + _rms(o_ref[...], gpost_ref[...])
    hn_ref[...] = hn
    xn_ref[...] = _rms(hn, gnext_ref[...]).astype(xn_ref.dtype)


def _resid_kernel(h_ref, o_ref, gpost_ref, hn_ref):
    hn_ref[...] = h_ref[...] + _rms(o_ref[...], gpost_ref[...])


def _resid_norm(h, o, g_post, g_next):
    m = h.shape[0]
    row = pl.BlockSpec((NORM_TILE, D_MODEL), lambda i: (i, 0))
    gain = pl.BlockSpec((1, D_MODEL), lambda i: (0, 0))
    if g_next is None:
        return pl.pallas_call(
            _resid_kernel,
            grid=(m // NORM_TILE,),
            in_specs=[row, row, gain],
            out_specs=row,
            out_shape=jax.ShapeDtypeStruct((m, D_MODEL), F32),
            compiler_params=_cp("parallel"),
            name="resid_last",
        )(h, o, g_post), None
    return pl.pallas_call(
        _resid_norm_kernel,
        grid=(m // NORM_TILE,),
        in_specs=[row, row, gain, gain],
        out_specs=[row, row],
        out_shape=[jax.ShapeDtypeStruct((m, D_MODEL), F32),
                   jax.ShapeDtypeStruct((m, D_MODEL), BF16)],
        compiler_params=_cp("parallel"),
        name="resid_norm",
    )(h, o, g_post, g_next)


def _rope_half128(x, cos, sin_signed):
    return x * cos + pltpu.roll(x, 64, axis=1) * sin_signed


def _rope_half64(x, cos, sin_signed):
    lane = lax.broadcasted_iota(jnp.int32, x.shape, 1)
    partner = jnp.where((lane & 32) == 0, pltpu.roll(x, 96, axis=1), pltpu.roll(x, 32, axis=1))
    return x * cos + partner * sin_signed


def _conv_kernel(cb_ref, cc_ref, cu_ref, pc_ref, pu_ref, w_ref, y_ref, z_ref):
    tr = cb_ref.shape[0]
    i = pl.program_id(0)
    z = cc_ref[...].astype(F32) * cu_ref[...].astype(F32)
    zp = pc_ref[...].astype(F32) * pu_ref[...].astype(F32)
    z_ref[0:HALO, :] = jnp.where(i > 0, zp, 0.0)
    z_ref[HALO:HALO + tr, :] = z
    w = w_ref[...]
    conv = (w[2:3, :] * z
            + w[1:2, :] * z_ref[HALO - 1:HALO - 1 + tr, :]
            + w[0:1, :] * z_ref[HALO - 2:HALO - 2 + tr, :])
    y_ref[...] = (cb_ref[...].astype(F32) * conv).astype(y_ref.dtype)


def _conv_branch(proj, conv_w, layer, tr):
    m = proj.shape[0]
    cw = 256
    nb = BRANCH // cw

    def halo_row(i):
        return jnp.maximum(i * (tr // HALO) - 1, 0)

    return pl.pallas_call(
        _conv_kernel,
        grid=(m // tr, nb),
        in_specs=[pl.BlockSpec((tr, cw), lambda i, c: (i, c)),
                  pl.BlockSpec((tr, cw), lambda i, c: (i, nb + c)),
                  pl.BlockSpec((tr, cw), lambda i, c: (i, 2 * nb + c)),
                  pl.BlockSpec((HALO, cw), lambda i, c: (halo_row(i), nb + c)),
                  pl.BlockSpec((HALO, cw), lambda i, c: (halo_row(i), 2 * nb + c)),
                  pl.BlockSpec((None, CONV_K, cw), lambda i, c: (layer, 0, c))],
        out_specs=pl.BlockSpec((tr, cw), lambda i, c: (i, c)),
        out_shape=jax.ShapeDtypeStruct((m, BRANCH), BF16),
        scratch_shapes=[pltpu.VMEM((HALO + tr, cw), F32)],
        compiler_params=_cp("parallel", "parallel"),
        name="conv_branch",
    )(proj, proj, proj, proj, proj, conv_w)


def _pool_kernel(u_ref, pu_ref, w_ref, s_ref, y_ref, x_ref):
    tr = u_ref.shape[0]
    i = pl.program_id(0)
    g = pl.program_id(1)
    win = jnp.left_shift(jnp.int32(2), g)
    x = u_ref[...].astype(F32)
    x_ref[0:HALO, :] = jnp.where(i > 0, pu_ref[...].astype(F32), 0.0)
    x_ref[HALO:HALO + tr, :] = x
    total = x
    for j in range(1, max(POOL_WINDOWS)):
        keep = jnp.where(j < win, 1.0, 0.0).astype(F32)
        total = total + keep * x_ref[HALO - j:HALO - j + tr, :]
    t = i * tr + lax.broadcasted_iota(jnp.int32, x.shape, 0)
    count = jnp.minimum(t + 1, win).astype(F32)
    pooled = (total / count - x).astype(BF16)
    mixed = jnp.dot(pooled, w_ref[...].astype(BF16), preferred_element_type=F32)
    y_ref[...] = (mixed * s_ref[...]).astype(y_ref.dtype)


def _pool_branch(proj, pool_w, pool_scale, layer, tr):
    m = proj.shape[0]
    base = (HALF_COLS - BRANCH) // POOL_GROUP
    ng = len(POOL_WINDOWS)

    def halo_row(i):
        return jnp.maximum(i * (tr // HALO) - 1, 0)

    return pl.pallas_call(
        _pool_kernel,
        grid=(m // tr, ng),
        in_specs=[pl.BlockSpec((tr, POOL_GROUP), lambda i, g: (i, base + g)),
                  pl.BlockSpec((HALO, POOL_GROUP), lambda i, g: (halo_row(i), base + g)),
                  pl.BlockSpec((None, None, POOL_GROUP, POOL_GROUP), lambda i, g: (layer, g, 0, 0)),
                  pl.BlockSpec((None, 1, POOL_GROUP), lambda i, g: (layer, 0, g))],
        out_specs=pl.BlockSpec((tr, POOL_GROUP), lambda i, g: (i, g)),
        out_shape=jax.ShapeDtypeStruct((m, BRANCH), BF16),
        scratch_shapes=[pltpu.VMEM((HALO + tr, POOL_GROUP), F32)],
        compiler_params=_cp("parallel", "parallel"),
        name="pool_branch",
    )(proj, proj, pool_w, pool_scale)


def _store_value_t(vt_ref, v, heads):
    tr = v.shape[0]
    vt = v.T.astype(vt_ref.dtype)
    ones = jnp.ones((V_ROWS - LANES, tr), vt_ref.dtype)
    for h in range(heads):
        vt_ref[0, h * V_ROWS:h * V_ROWS + LANES, :] = vt[h * LANES:(h + 1) * LANES, :]
        vt_ref[0, h * V_ROWS + LANES:(h + 1) * V_ROWS, :] = ones


def _mla_prep_kernel(cq_ref, ckv_ref, sm_ref, gq_ref, gkv_ref, wuq_ref, wukv_ref,
                     cos_ref, sin_ref, qt_ref, kn_ref, kpe_ref, vt_ref):
    cos = cos_ref[...]
    sin = sin_ref[...]
    scale = (MLA_NOPE + MLA_ROPE) ** -0.5 * LOG2E
    cqn = _rms(cq_ref[...].astype(F32), gq_ref[...]).astype(BF16)
    q = jnp.dot(cqn, wuq_ref[...], preferred_element_type=F32)
    for h in range(MLA_HEADS):
        lo = 2 * h * LANES
        qt_ref[lo:lo + LANES, :] = (q[:, lo:lo + LANES] * scale).T.astype(qt_ref.dtype)
        pe = _rope_half64(q[:, lo + LANES:lo + 2 * LANES], cos, sin)
        qt_ref[lo + LANES:lo + 2 * LANES, :] = (pe * scale).T.astype(qt_ref.dtype)
    ckvn = _rms(ckv_ref[...].astype(F32), gkv_ref[...]).astype(BF16)
    kv = jnp.dot(ckvn, wukv_ref[...], preferred_element_type=F32)
    kn_ref[...] = kv[:, :BRANCH].astype(kn_ref.dtype)
    _store_value_t(vt_ref, kv[:, BRANCH:], MLA_HEADS)
    kpe_ref[...] = _rope_half64(sm_ref[:, 0:LANES], cos, sin).astype(kpe_ref.dtype)


def _mla_prep(proj, small, gq, gkv, wuq, wukv, cos64, sin64, layer, tr):
    m = proj.shape[0]
    qw = 2 * LANES * MLA_HEADS
    return pl.pallas_call(
        _mla_prep_kernel,
        grid=(m // tr,),
        in_specs=[pl.BlockSpec((tr, MLA_Q_RANK), lambda i: (i, 3 * BRANCH // MLA_Q_RANK)),
                  pl.BlockSpec((tr, MLA_KV_RANK), lambda i: (i, (3 * BRANCH + MLA_Q_RANK) // MLA_KV_RANK)),
                  pl.BlockSpec((tr, SMALL_COLS), lambda i: (i, 0)),
                  pl.BlockSpec((None, 1, MLA_Q_RANK), lambda i: (layer, 0, 0)),
                  pl.BlockSpec((None, 1, MLA_KV_RANK), lambda i: (layer, 0, 0)),
                  pl.BlockSpec((MLA_Q_RANK, qw), lambda i: (0, 0)),
                  pl.BlockSpec((MLA_KV_RANK, 2 * BRANCH), lambda i: (0, 0)),
                  pl.BlockSpec((tr, LANES), lambda i: (i, 0)),
                  pl.BlockSpec((tr, LANES), lambda i: (i, 0))],
        out_specs=[pl.BlockSpec((qw, tr), lambda i: (0, i)),
                   pl.BlockSpec((tr, BRANCH), lambda i: (i, 0)),
                   pl.BlockSpec((tr, LANES), lambda i: (i, 0)),
                   pl.BlockSpec((1, MLA_HEADS * V_ROWS, tr), lambda i: (i, 0, 0))],
        out_shape=[jax.ShapeDtypeStruct((qw, m), BF16),
                   jax.ShapeDtypeStruct((m, BRANCH), BF16),
                   jax.ShapeDtypeStruct((m, LANES), BF16),
                   jax.ShapeDtypeStruct((m // tr, MLA_HEADS * V_ROWS, tr), BF16)],
        compiler_params=_cp("parallel"),
        name="mla_prep",
    )(proj, proj, small, gq, gkv, wuq, wukv, cos64, sin64)


def _dsa_prep_kernel(dq_ref, dk_ref, dv_ref, iq_ref, sm_ref, c128_ref, s128_ref, c64_ref, s64_ref,
                     qt_ref, k_ref, vt_ref, iqr_ref, ik_ref, wt_ref):
    c128, s128 = c128_ref[...], s128_ref[...]
    c64, s64 = c64_ref[...], s64_ref[...]
    scale = DSA_DIM ** -0.5 * LOG2E
    for h in range(BRANCH // LANES):
        sl = slice(h * LANES, (h + 1) * LANES)
        qt_ref[sl, :] = (_rope_half128(dq_ref[:, sl].astype(F32), c128, s128) * scale).T.astype(qt_ref.dtype)
        k_ref[:, sl] = _rope_half128(dk_ref[:, sl].astype(F32), c128, s128).astype(k_ref.dtype)
        iqr_ref[:, sl] = _rope_half64(iq_ref[:, sl].astype(F32), c64, s64).astype(iqr_ref.dtype)
    _store_value_t(vt_ref, dv_ref[...].astype(F32), DSA_HEADS)
    tail = sm_ref[:, LANES:2 * LANES]
    lane = lax.broadcasted_iota(jnp.int32, tail.shape, 1)
    ik_lo = jnp.where(lane < IDX_DIM, _rope_half64(tail, c64, s64), 0.0)
    ik_ref[:, 0:LANES] = ik_lo.astype(ik_ref.dtype)
    ik_ref[:, LANES:2 * LANES] = pltpu.roll(ik_lo, IDX_DIM, axis=1).astype(ik_ref.dtype)
    idx_w_scale = (IDX_HEADS ** -0.5) * (IDX_DIM ** -0.5)
    wt_ref[...] = (tail * idx_w_scale).T


def _dsa_prep(proj_b, small, c128, s128, c64, s64, tr):
    m = proj_b.shape[0]
    col = lambda c: pl.BlockSpec((tr, BRANCH), lambda i: (i, c))
    tab = pl.BlockSpec((tr, LANES), lambda i: (i, 0))
    return pl.pallas_call(
        _dsa_prep_kernel,
        grid=(m // tr,),
        in_specs=[col(0), col(1), col(2), col(3),
                  pl.BlockSpec((tr, SMALL_COLS), lambda i: (i, 0)),
                  tab, tab, tab, tab],
        out_specs=[pl.BlockSpec((BRANCH, tr), lambda i: (0, i)),
                   pl.BlockSpec((tr, BRANCH), lambda i: (i, 0)),
                   pl.BlockSpec((1, DSA_HEADS * V_ROWS, tr), lambda i: (i, 0, 0)),
                   pl.BlockSpec((tr, BRANCH), lambda i: (i, 0)),
                   pl.BlockSpec((tr, 2 * LANES), lambda i: (i, 0)),
                   pl.BlockSpec((LANES, tr), lambda i: (0, i))],
        out_shape=[jax.ShapeDtypeStruct((BRANCH, m), BF16),
                   jax.ShapeDtypeStruct((m, BRANCH), BF16),
                   jax.ShapeDtypeStruct((m // tr, DSA_HEADS * V_ROWS, tr), BF16),
                   jax.ShapeDtypeStruct((m, BRANCH), BF16),
                   jax.ShapeDtypeStruct((m, 2 * LANES), BF16),
                   jax.ShapeDtypeStruct((LANES, m), F32)],
        compiler_params=_cp("parallel"),
        name="dsa_prep",
    )(proj_b, proj_b, proj_b, proj_b, small, c128, s128, c64, s64)


def _float_to_ordered_int(s):
    b = lax.bitcast_convert_type(s, jnp.int32)
    return b ^ ((b >> 31) & jnp.int32(0x7FFFFFFF))


def _indexer_kernel(iq_ref, ik_ref, wt_ref, bias_ref, key_ref, cut_ref, *, topk):
    tq = iq_ref.shape[0]
    total_rows = ik_ref.shape[0]
    tk = IDX_K_TILE
    i = pl.program_id(0)
    n_tiles = (i * tq + tq) // tk
    n_rows = n_tiles * tk
    qpos = i * tq + lax.broadcasted_iota(jnp.int32, (1, tq), 1)
    int_min = jnp.int32(-2 ** 31)

    def score_tile(kt, carry):
        start = pl.multiple_of(kt * tk, tk)
        ik_lo = ik_ref[pl.ds(start, tk), 0:LANES]
        ik_hi = ik_ref[pl.ds(start, tk), LANES:2 * LANES]
        acc = jnp.zeros((tk, tq), F32)
        for j in range(IDX_HEADS // 2):
            qpair = iq_ref[:, j * LANES:(j + 1) * LANES]
            for half, ik in enumerate((ik_lo, ik_hi)):
                g = 2 * j + half
                dots = lax.dot_general(ik, qpair, (((1,), (1,)), ((), ())),
                                       preferred_element_type=F32)
                acc = acc + jnp.maximum(dots, 0.0) * wt_ref[IDX_DIM + g:IDX_DIM + g + 1, :]
        acc = acc + 0.0
        kpos = start + lax.broadcasted_iota(jnp.int32, (tk, 1), 0)
        key_ref[pl.ds(start, tk), :] = jnp.where(kpos <= qpos, _float_to_ordered_int(acc), int_min)
        return carry

    lax.fori_loop(0, n_tiles, score_tile, 0)

    n_chunks = n_rows // COUNT_CHUNK

    def count(pred, with_pos=False):
        def body(c, accs):
            start = pl.multiple_of(c * COUNT_CHUNK, COUNT_CHUNK)
            accs = list(accs)
            chunk = key_ref[pl.ds(start, COUNT_CHUNK), :]
            for r in range(COUNT_CHUNK // 8):
                blk = chunk[8 * r:8 * r + 8, :]
                if with_pos:
                    pos = start + 8 * r + lax.broadcasted_iota(jnp.int32, (8, 1), 0)
                    hit = pred(blk, pos)
                else:
                    hit = pred(blk)
                accs[r % COUNT_ACCS] = accs[r % COUNT_ACCS] + jnp.where(hit, 1, 0).astype(jnp.int32)
            return tuple(accs)
        zero = jnp.zeros((8, tq), jnp.int32)
        accs = lax.fori_loop(0, n_chunks, body, (zero,) * COUNT_ACCS)
        return jnp.sum(functools.reduce(lambda a, b: a + b, accs), axis=0, keepdims=True)

    def bit_step(b, tau):
        cand = tau + jnp.left_shift(jnp.int32(1), 31 - b)
        cnt = count(lambda blk: blk >= cand)
        return jnp.where(cnt >= topk, cand, tau)

    tau = lax.fori_loop(0, 32, bit_step, jnp.full((1, tq), int_min, jnp.int32))

    n_gt = count(lambda blk: blk > tau)
    n_eq = count(lambda blk: blk == tau)
    need = topk - n_gt
    cut_bits = 14
    cut_ref[...] = jnp.full((1, tq), 1 << cut_bits, jnp.int32)

    @pl.when(jnp.max(n_eq - need) > 0)
    def _():
        def cut_step(b, cut):
            cand = cut + jnp.left_shift(jnp.int32(1), cut_bits - 1 - b)
            cnt = count(lambda blk, pos: (blk == tau) & (pos < cand), with_pos=True)
            return jnp.where(cnt <= need, cand, cut)
        cut_ref[...] = lax.fori_loop(0, cut_bits, cut_step, jnp.zeros((1, tq), jnp.int32))

    cut = cut_ref[...]

    def write_sel(c, carry):
        start = pl.multiple_of(c * WRITE_CHUNK, WRITE_CHUNK)
        blk = key_ref[pl.ds(start, WRITE_CHUNK), :]
        pos = start + lax.broadcasted_iota(jnp.int32, (WRITE_CHUNK, 1), 0)
        sel = ((blk > tau) | ((blk == tau) & (pos < cut))) & (pos <= qpos)
        bias_ref[pl.ds(start, WRITE_CHUNK), :] = jnp.where(sel, 0.0, NEG).astype(bias_ref.dtype)
        return carry

    lax.fori_loop(0, n_rows // WRITE_CHUNK, write_sel, 0)

    def write_neg(c, carry):
        start = pl.multiple_of(c * WRITE_CHUNK, WRITE_CHUNK)
        bias_ref[pl.ds(start, WRITE_CHUNK), :] = jnp.full((WRITE_CHUNK, tq), NEG, bias_ref.dtype)
        return carry

    lax.fori_loop(n_rows // WRITE_CHUNK, total_rows // WRITE_CHUNK, write_neg, 0)


def _indexer(iq_r, ik_ab, wt, topk):
    m = iq_r.shape[0]
    return pl.pallas_call(
        functools.partial(_indexer_kernel, topk=topk),
        grid=(m // Q_TILE,),
        in_specs=[pl.BlockSpec((Q_TILE, BRANCH), lambda i: (i, 0)),
                  pl.BlockSpec((m, 2 * LANES), lambda i: (0, 0)),
                  pl.BlockSpec((LANES, Q_TILE), lambda i: (0, i))],
        out_specs=pl.BlockSpec((m, Q_TILE), lambda i: (0, i)),
        out_shape=jax.ShapeDtypeStruct((m, m), F32),
        scratch_shapes=[pltpu.VMEM((m, Q_TILE), jnp.int32),
                        pltpu.VMEM((1, Q_TILE), jnp.int32)],
        compiler_params=_cp("parallel"),
        name="dsa_indexer",
    )(iq_r, ik_ab, wt)


def _flash_kernel(*refs, q_axis, has_kpe, has_bias):
    refs = list(refs)
    q_ref, k_ref = refs[0], refs[1]
    pos = 2
    kpe_ref = bias_ref = None
    if has_kpe:
        kpe_ref = refs[pos]
        pos += 1
    vt_ref = refs[pos]
    pos += 1
    if has_bias:
        bias_ref = refs[pos]
        pos += 1
    o_ref, sa_ref, sb_ref = refs[pos], refs[pos + 1], refs[pos + 2]

    tq = q_ref.shape[1]
    heads = vt_ref.shape[1] // V_ROWS
    tk = vt_ref.shape[2]
    dq = q_ref.shape[0] // heads
    assert tk % tq == 0
    i = pl.program_id(q_axis)
    n_tiles = (i * tq + tq + tk - 1) // tk
    qpos = i * tq + lax.broadcasted_iota(jnp.int32, (1, tq), 1)

    def compute_scores(kt, s_ref):
        start = pl.multiple_of(kt * tk, tk)
        kpe = kpe_ref[pl.ds(start, tk), :] if has_kpe else None
        for h in range(heads):
            k = k_ref[pl.ds(start, tk), h * LANES:(h + 1) * LANES]
            if has_kpe:
                k = jnp.concatenate([k, kpe], axis=1)
            s_ref[h] = jnp.dot(k, q_ref[h * dq:(h + 1) * dq, :], preferred_element_type=F32)

    def consume_scores(kt, s_ref, carry, causal_mask):
        start = pl.multiple_of(kt * tk, tk)
        bias = bias_ref[pl.ds(start, tk), :] if has_bias else None
        new = []
        for h in range(heads):
            m_run, acc = carry[h]
            s = s_ref[h]
            if has_bias:
                s = s + bias
            if causal_mask:
                kpos = start + lax.broadcasted_iota(jnp.int32, (tk, 1), 0)
                s = jnp.where(kpos <= qpos, s, NEG)
            m_new = jnp.maximum(m_run, jnp.max(s, axis=0, keepdims=True))
            alpha = jnp.exp2(m_run - m_new)
            p = jnp.exp2(s - m_new).astype(BF16)
            pv = jnp.dot(vt_ref[kt, h * V_ROWS:(h + 1) * V_ROWS, :], p, preferred_element_type=F32)
            new.append((m_new, alpha * acc + pv))
        return tuple(new)

    def double_step(u, carry):
        kt = 2 * u
        compute_scores(kt + 1, sb_ref)
        carry = consume_scores(kt, sa_ref, carry, False)
        compute_scores(kt + 2, sa_ref)
        return consume_scores(kt + 1, sb_ref, carry, False)

    mask_last = not has_bias
    last = n_tiles - 1

    def odd_tail(carry):
        compute_scores(last, sb_ref)
        carry = consume_scores(last - 1, sa_ref, carry, False)
        return consume_scores(last, sb_ref, carry, mask_last)

    def even_tail(carry):
        return consume_scores(last, sa_ref, carry, mask_last)

    init = tuple((jnp.full((1, tq), NEG, F32), jnp.zeros((V_ROWS, tq), F32)) for _ in range(heads))
    compute_scores(0, sa_ref)
    carry = lax.fori_loop(0, last // 2, double_step, init)
    carry = lax.cond(last % 2 == 1, odd_tail, even_tail, carry)
    for h in range(heads):
        acc = carry[h][1]
        out = acc[0:LANES, :] / acc[LANES:LANES + 1, :]
        o_ref[:, h * LANES:(h + 1) * LANES] = out.T.astype(o_ref.dtype)


def _score_scratch(heads, tk):
    return [pltpu.VMEM((heads, tk, Q_TILE), F32), pltpu.VMEM((heads, tk, Q_TILE), F32)]


def _mla_attention(q, kn, kpe, vt):
    m = kn.shape[0]
    n_kt, _, tk = vt.shape
    hp = HEADS_PER_STEP
    return pl.pallas_call(
        functools.partial(_flash_kernel, q_axis=1, has_kpe=True, has_bias=False),
        grid=(MLA_HEADS // hp, m // Q_TILE),
        in_specs=[pl.BlockSpec((hp * 2 * LANES, Q_TILE), lambda h, i: (h, i)),
                  pl.BlockSpec((m, hp * LANES), lambda h, i: (0, h)),
                  pl.BlockSpec((m, LANES), lambda h, i: (0, 0)),
                  pl.BlockSpec((n_kt, hp * V_ROWS, tk), lambda h, i: (0, h, 0))],
        out_specs=pl.BlockSpec((Q_TILE, hp * MLA_V), lambda h, i: (i, h)),
        out_shape=jax.ShapeDtypeStruct((m, BRANCH), BF16),
        scratch_shapes=_score_scratch(hp, tk),
        compiler_params=_cp("parallel", "parallel"),
        name="mla_attention",
    )(q, kn, kpe, vt)


def _dsa_attention(q, k, vt, bias):
    m = k.shape[0]
    n_kt, _, tk = vt.shape
    hp = HEADS_PER_STEP
    return pl.pallas_call(
        functools.partial(_flash_kernel, q_axis=0, has_kpe=False, has_bias=True),
        grid=(m // Q_TILE, DSA_HEADS // hp),
        in_specs=[pl.BlockSpec((hp * DSA_DIM, Q_TILE), lambda i, h: (h, i)),
                  pl.BlockSpec((m, hp * DSA_DIM), lambda i, h: (0, h)),
                  pl.BlockSpec((n_kt, hp * V_ROWS, tk), lambda i, h: (0, h, 0)),
                  pl.BlockSpec((m, Q_TILE), lambda i, h: (0, i))],
        out_specs=pl.BlockSpec((Q_TILE, hp * DSA_DIM), lambda i, h: (i, h)),
        out_shape=jax.ShapeDtypeStruct((m, BRANCH), BF16),
        scratch_shapes=_score_scratch(hp, tk),
        compiler_params=_cp("parallel", "parallel"),
        name="dsa_attention",
    )(q, k, vt, bias)


def _rope_tables(rows, dim):
    inv = 1.0 / jnp.power(ROPE_THETA, jnp.arange(0, dim, 2, dtype=F32) / dim)
    ang = jnp.arange(rows, dtype=F32)[:, None] * inv[None, :]
    cos, sin = jnp.cos(ang), jnp.sin(ang)
    reps = LANES // dim
    return (jnp.tile(jnp.concatenate([cos, cos], axis=1), (1, reps)),
            jnp.tile(jnp.concatenate([-sin, sin], axis=1), (1, reps)))


def _regroup_w_in(w):
    o_kr = 3 * BRANCH + MLA_Q_RANK + MLA_KV_RANK
    o_dq = o_kr + MLA_ROPE
    o_ik = o_dq + 4 * BRANCH
    o_iw = o_ik + IDX_DIM
    o_pu = o_iw + IDX_HEADS
    assert o_kr == HALF_COLS
    second = jnp.concatenate([w[:, :, o_dq:o_ik], w[:, :, o_pu:]], axis=2)
    zeros = lambda n: jnp.zeros(w.shape[:2] + (n,), w.dtype)
    small = jnp.concatenate([w[:, :, o_kr:o_dq], zeros(LANES - MLA_ROPE),
                             w[:, :, o_ik:o_iw], w[:, :, o_iw:o_pu],
                             zeros(LANES - IDX_DIM - IDX_HEADS)], axis=2)
    return second, small


def _layout_w_uq(w):
    w3 = w.reshape(MLA_Q_RANK, MLA_HEADS, MLA_NOPE + MLA_ROPE)
    w3 = jnp.pad(w3, ((0, 0), (0, 0), (0, 2 * LANES - MLA_NOPE - MLA_ROPE)))
    return w3.reshape(MLA_Q_RANK, MLA_HEADS * 2 * LANES).astype(BF16)


def _layout_w_ukv(w):
    w4 = w.reshape(MLA_KV_RANK, MLA_HEADS, 2, MLA_NOPE)
    return w4.transpose(0, 2, 1, 3).reshape(MLA_KV_RANK, 2 * BRANCH).astype(BF16)


def _forward(x, meta_tokens, norm_mix_pre, norm_mix_post, norm_ffn_pre, norm_ffn_post,
             w_in, conv_w, mla_q_norm, mla_w_uq, mla_kv_norm, mla_w_ukv, pool_w, pool_scale,
             w_branch, w_gate, b_gate, w_out, ffn_w_gate, ffn_w_up, ffn_w_down):
    assert x.shape[0] == 1 and x.shape[2] == D_MODEL
    depth = w_in.shape[0]
    seq = x.shape[1]
    length = N_META + seq
    topk = min(IDX_TOPK_MAX, length // 4)
    rows = -(-length // ROW_TILE) * ROW_TILE
    assert rows % Q_TILE == 0 and rows % K_TILE == 0 and rows % NORM_TILE == 0
    assert rows < (1 << 14)

    h = jnp.concatenate([meta_tokens.astype(F32), x[0],
                         jnp.zeros((rows - length, D_MODEL), F32)], axis=0)
    c64, s64 = _rope_tables(rows, 64)
    c128, s128 = _rope_tables(rows, 128)
    w_in_second, w_in_small = _regroup_w_in(w_in)
    w_gate_flat = w_gate.reshape(depth * N_BRANCH, D_MODEL, D_MODEL)
    b_gate_flat = b_gate.reshape(depth * N_BRANCH, 1, D_MODEL)
    tn = 512
    gate_tiles = D_MODEL // tn

    xn = _prenorm(h, norm_mix_pre[0][None])
    for l in range(depth):
        at_layer = lambda j, l=l: l
        tile = lambda j: j
        proj_a = _stacked_proj(xn, w_in, at_layer, tile, HALF_COLS // tn, ROW_TILE, tn, BF16, "in_proj_a")
        proj_b = _stacked_proj(xn, w_in_second, at_layer, tile, HALF_COLS // tn, ROW_TILE, tn, BF16,
                               "in_proj_b")
        small = _stacked_proj(xn, w_in_small, at_layer, tile, 1, ROW_TILE, SMALL_COLS, F32,
                              "in_proj_small")
        gates = _stacked_proj(xn, w_gate_flat,
                              lambda j, l=l: l * N_BRANCH + j // gate_tiles, lambda j: j % gate_tiles,
                              N_BRANCH * gate_tiles, ROW_TILE, tn, BF16, "gates", bias=b_gate_flat)

        y_conv = _conv_branch(proj_a, conv_w, l, ROW_TILE)
        y_pool = _pool_branch(proj_b, pool_w, pool_scale[:, None, :], l, ROW_TILE)

        q_m, kn_m, kpe_m, vt_m = _mla_prep(
            proj_a, small, mla_q_norm[:, None, :], mla_kv_norm[:, None, :],
            _layout_w_uq(mla_w_uq[l]), _layout_w_ukv(mla_w_ukv[l]), c64, s64, l, K_TILE)
        y_mla = _mla_attention(q_m, kn_m, kpe_m, vt_m)

        q_d, k_d, vt_d, iq_r, ik_ab, wt = _dsa_prep(proj_b, small, c128, s128, c64, s64, K_TILE)
        bias = _indexer(iq_r, ik_ab, wt, topk)
        y_dsa = _dsa_attention(q_d, k_d, vt_d, bias)

        merged = _gated_merge((y_conv, y_mla, y_dsa, y_pool), gates, w_branch, l, ROW_TILE, tn)
        mix = _stacked_proj(merged, w_out, at_layer, tile, D_MODEL // tn, ROW_TILE, tn, F32, "out_proj")
        h, xn = _resid_norm(h, mix, norm_mix_post[l][None], norm_ffn_pre[l][None])

        act = _swiglu(xn, ffn_w_gate, ffn_w_up, l, ROW_TILE, 256)
        f = _stacked_proj(act, ffn_w_down, at_layer, tile, D_MODEL // tn, 384, tn, F32, "ffn_down",
                          single_buffer_weight=True)
        g_next = norm_mix_pre[l + 1][None] if l + 1 < depth else None
        h, xn = _resid_norm(h, f, norm_ffn_post[l][None], g_next)

    return h[N_META:length][None]


def kernel(x, meta_tokens, norm_mix_pre, norm_mix_post, norm_ffn_pre, norm_ffn_post, w_in, conv_w, mla_q_norm, mla_w_uq, mla_kv_norm, mla_w_ukv, pool_w, pool_scale, w_branch, w_gate, b_gate, w_out, ffn_w_gate, ffn_w_up, ffn_w_down):
    return _forward(x, meta_tokens, norm_mix_pre, norm_mix_post, norm_ffn_pre, norm_ffn_post,
                    w_in, conv_w, mla_q_norm, mla_w_uq, mla_kv_norm, mla_w_ukv, pool_w, pool_scale,
                    w_branch, w_gate, b_gate, w_out, ffn_w_gate, ffn_w_up, ffn_w_down)
```

```python
import functools
import math

import jax
import jax.numpy as jnp
from jax import lax
from jax.experimental import pallas as pl
from jax.experimental.pallas import tpu as pltpu

F32 = jnp.float32
BF16 = jnp.bfloat16

D_MODEL = 4096
N_META = 16
ROPE_THETA = 10000.0
EPS = 1e-6
N_BRANCH = 4
BRANCH = 1024
CONV_K = 3
MLA_NOPE, MLA_ROPE, MLA_V, MLA_HEADS = 128, 64, 128, 8
MLA_Q_RANK, MLA_KV_RANK = 1536, 512
DSA_DIM, DSA_HEADS = 128, 8
IDX_HEADS, IDX_DIM, IDX_TOPK_MAX = 16, 64, 256
POOL_WINDOWS = (2, 4, 8, 16)
POOL_GROUP = 256
D_FF = 11008

LANES = 128
HALO = 16
ROW_TILE = 768
Q_TILE = 256
K_TILE = 768
V_ROWS = 144
HEADS_PER_STEP = 4
IDX_K_TILE = 256
COUNT_CHUNK = 256
COUNT_ACCS = 4
WRITE_CHUNK = 64
NORM_TILE = 192
NEG = -1e30
LOG2E = math.log2(math.e)
VMEM_LIMIT = 58 * 1024 * 1024

HALF_COLS = 5120
SMALL_COLS = 256


def _cp(*sem):
    return pltpu.CompilerParams(dimension_semantics=sem, vmem_limit_bytes=VMEM_LIMIT)


def _sigmoid(x):
    return 1.0 / (1.0 + jnp.exp(-x))


def _rms(x, g):
    return x * lax.rsqrt(jnp.mean(x * x, axis=-1, keepdims=True) + EPS) * g


def _cast_weights_once(w_refs, wbf_refs):
    @pl.when(pl.program_id(1) == 0)
    def _():
        for w_ref, wbf_ref in zip(w_refs, wbf_refs):
            wbf_ref[...] = w_ref[...].astype(wbf_ref.dtype)


def _proj_kernel(a_ref, w_ref, o_ref, wbf_ref):
    _cast_weights_once((w_ref,), (wbf_ref,))
    o_ref[...] = jnp.dot(a_ref[...], wbf_ref[...], preferred_element_type=F32).astype(o_ref.dtype)


def _gate_kernel(a_ref, w_ref, b_ref, o_ref, wbf_ref):
    _cast_weights_once((w_ref,), (wbf_ref,))
    z = jnp.dot(a_ref[...], wbf_ref[...], preferred_element_type=F32) + b_ref[...]
    o_ref[...] = _sigmoid(z).astype(o_ref.dtype)


def _stacked_proj(a, w, lead_of, col_of, n_tiles, tm, tn, out_dtype, name, bias=None,
                  single_buffer_weight=False):
    m, k = a.shape
    assert w.shape[1] == k
    w_kwargs = {"pipeline_mode": pl.Buffered(1)} if single_buffer_weight else {}
    in_specs = [pl.BlockSpec((tm, k), lambda j, i: (i, 0)),
                pl.BlockSpec((None, k, tn), lambda j, i: (lead_of(j), 0, col_of(j)), **w_kwargs)]
    args = [a, w]
    body = _proj_kernel
    if bias is not None:
        in_specs.append(pl.BlockSpec((None, 1, tn), lambda j, i: (lead_of(j), 0, col_of(j))))
        args.append(bias)
        body = _gate_kernel
    return pl.pallas_call(
        body,
        grid=(n_tiles, m // tm),
        in_specs=in_specs,
        out_specs=pl.BlockSpec((tm, tn), lambda j, i: (i, j)),
        out_shape=jax.ShapeDtypeStruct((m, n_tiles * tn), out_dtype),
        scratch_shapes=[pltpu.VMEM((k, tn), BF16)],
        compiler_params=_cp("parallel", "arbitrary"),
        name=name,
    )(*args)


def _wide_proj_kernel(*refs, has_bias):
    if has_bias:
        a_ref, w_ref, b_ref, o_ref, wbf_ref = refs
    else:
        a_ref, w_ref, o_ref, wbf_ref = refs
        b_ref = None
    half = w_ref.shape[1]
    i = pl.program_id(1)

    @pl.when(i == 0)
    def _():
        wbf_ref[:, 0:half] = w_ref[...].astype(wbf_ref.dtype)

    @pl.when(i == 1)
    def _():
        wbf_ref[:, half:2 * half] = w_ref[...].astype(wbf_ref.dtype)

    @pl.when(i >= 1)
    def _():
        z = jnp.dot(a_ref[...], wbf_ref[...], preferred_element_type=F32)
        if has_bias:
            z = _sigmoid(z + b_ref[...])
        o_ref[...] = z.astype(o_ref.dtype)


def _wide_proj(a, w, lead_of, col_of, n_tiles, tm, half, out_dtype, name, bias=None):
    m, k = a.shape
    assert w.shape[1] == k
    row = lambda i: jnp.maximum(i - 1, 0)
    in_specs = [pl.BlockSpec((tm, k), lambda j, i: (row(i), 0)),
                pl.BlockSpec((None, k, half),
                             lambda j, i: (lead_of(j), 0, 2 * col_of(j) + jnp.minimum(i, 1)))]
    args = [a, w]
    if bias is not None:
        in_specs.append(pl.BlockSpec((None, 1, 2 * half), lambda j, i: (lead_of(j), 0, col_of(j))))
        args.append(bias)
    return pl.pallas_call(
        functools.partial(_wide_proj_kernel, has_bias=bias is not None),
        grid=(n_tiles, m // tm + 1),
        in_specs=in_specs,
        out_specs=pl.BlockSpec((tm, 2 * half), lambda j, i: (row(i), j)),
        out_shape=jax.ShapeDtypeStruct((m, n_tiles * 2 * half), out_dtype),
        scratch_shapes=[pltpu.VMEM((k, 2 * half), BF16)],
        compiler_params=_cp("parallel", "arbitrary"),
        name=name,
    )(*args)


def _swiglu_kernel(a_ref, wg_ref, wu_ref, o_ref, wgbf_ref, wubf_ref):
    _cast_weights_once((wg_ref, wu_ref), (wgbf_ref, wubf_ref))
    a = a_ref[...]
    g = jnp.dot(a, wgbf_ref[...], preferred_element_type=F32)
    u = jnp.dot(a, wubf_ref[...], preferred_element_type=F32)
    o_ref[...] = (g * _sigmoid(g) * u).astype(o_ref.dtype)


def _swiglu(a, wg, wu, layer, tm, tn):
    m, k = a.shape
    n = wg.shape[2]
    w_spec = pl.BlockSpec((None, k, tn), lambda j, i: (layer, 0, j))
    return pl.pallas_call(
        _swiglu_kernel,
        grid=(n // tn, m // tm),
        in_specs=[pl.BlockSpec((tm, k), lambda j, i: (i, 0)), w_spec, w_spec],
        out_specs=pl.BlockSpec((tm, tn), lambda j, i: (i, j)),
        out_shape=jax.ShapeDtypeStruct((m, n), BF16),
        scratch_shapes=[pltpu.VMEM((k, tn), BF16), pltpu.VMEM((k, tn), BF16)],
        compiler_params=_cp("parallel", "arbitrary"),
        name="ffn_swiglu",
    )(a, wg, wu)


def _merge_kernel(y0_ref, y1_ref, y2_ref, y3_ref, g0_ref, g1_ref, g2_ref, g3_ref, w_ref,
                  o_ref, wbf_ref):
    _cast_weights_once((w_ref,), (wbf_ref,))
    acc = None
    for b, (y_ref, g_ref) in enumerate(((y0_ref, g0_ref), (y1_ref, g1_ref),
                                        (y2_ref, g2_ref), (y3_ref, g3_ref))):
        val = g_ref[...].astype(F32) * jnp.dot(y_ref[...], wbf_ref[b], preferred_element_type=F32)
        acc = val if acc is None else acc + val
    o_ref[...] = acc.astype(o_ref.dtype)


def _gated_merge(ys, gates, w_branch, layer, tm, tn):
    m = ys[0].shape[0]
    nj = D_MODEL // tn
    y_spec = pl.BlockSpec((tm, BRANCH), lambda j, i: (i, 0))
    g_specs = [pl.BlockSpec((tm, tn), functools.partial(lambda j, i, b: (i, b * nj + j), b=b))
               for b in range(N_BRANCH)]
    return pl.pallas_call(
        _merge_kernel,
        grid=(nj, m // tm),
        in_specs=[y_spec] * N_BRANCH + g_specs
                 + [pl.BlockSpec((None, N_BRANCH, BRANCH, tn), lambda j, i: (layer, 0, 0, j))],
        out_specs=pl.BlockSpec((tm, tn), lambda j, i: (i, j)),
        out_shape=jax.ShapeDtypeStruct((m, D_MODEL), BF16),
        scratch_shapes=[pltpu.VMEM((N_BRANCH, BRANCH, tn), BF16)],
        compiler_params=_cp("parallel", "arbitrary"),
        name="gated_merge",
    )(*ys, gates, gates, gates, gates, w_branch)


def _norm_kernel(h_ref, g_ref, xn_ref):
    xn_ref[...] = _rms(h_ref[...], g_ref[...]).astype(xn_ref.dtype)


def _prenorm(h, g):
    m = h.shape[0]
    return pl.pallas_call(
        _norm_kernel,
        grid=(m // NORM_TILE,),
        in_specs=[pl.BlockSpec((NORM_TILE, D_MODEL), lambda i: (i, 0)),
                  pl.BlockSpec((1, D_MODEL), lambda i: (0, 0))],
        out_specs=pl.BlockSpec((NORM_TILE, D_MODEL), lambda i: (i, 0)),
        out_shape=jax.ShapeDtypeStruct((m, D_MODEL), BF16),
        compiler_params=_cp("parallel"),
        name="prenorm",
    )(h, g)


def _resid_norm_kernel(h_ref, o_ref, gpost_ref, gnext_ref, hn_ref, xn_ref):
    hn = h_ref[...] + _rms(o_ref[...], gpost_ref[...])
    hn_ref[...] = hn
    xn_ref[...] = _rms(hn, gnext_ref[...]).astype(xn_ref.dtype)


def _resid_kernel(h_ref, o_ref, gpost_ref, hn_ref):
    hn_ref[...] = h_ref[...] + _rms(o_ref[...], gpost_ref[...])


def _resid_norm(h, o, g_post, g_next):
    m = h.shape[0]
    row = pl.BlockSpec((NORM_TILE, D_MODEL), lambda i: (i, 0))
    gain = pl.BlockSpec((1, D_MODEL), lambda i: (0, 0))
    if g_next is None:
        return pl.pallas_call(
            _resid_kernel,
            grid=(m // NORM_TILE,),
            in_specs=[row, row, gain],
            out_specs=row,
            out_shape=jax.ShapeDtypeStruct((m, D_MODEL), F32),
            compiler_params=_cp("parallel"),
            name="resid_last",
        )(h, o, g_post), None
    return pl.pallas_call(
        _resid_norm_kernel,
        grid=(m // NORM_TILE,),
        in_specs=[row, row, gain, gain],
        out_specs=[row, row],
        out_shape=[jax.ShapeDtypeStruct((m, D_MODEL), F32),
                   jax.ShapeDtypeStruct((m, D_MODEL), BF16)],
        compiler_params=_cp("parallel"),
        name="resid_norm",
    )(h, o, g_post, g_next)


def _rope_half128(x, cos, sin_signed):
    return x * cos + pltpu.roll(x, 64, axis=1) * sin_signed


def _rope_half64(x, cos, sin_signed):
    lane = lax.broadcasted_iota(jnp.int32, x.shape, 1)
    partner = jnp.where((lane & 32) == 0, pltpu.roll(x, 96, axis=1), pltpu.roll(x, 32, axis=1))
    return x * cos + partner * sin_signed


def _conv_kernel(cb_ref, cc_ref, cu_ref, pc_ref, pu_ref, w_ref, y_ref, z_ref):
    tr = cb_ref.shape[0]
    i = pl.program_id(0)
    z = cc_ref[...].astype(F32) * cu_ref[...].astype(F32)
    zp = pc_ref[...].astype(F32) * pu_ref[...].astype(F32)
    z_ref[0:HALO, :] = jnp.where(i > 0, zp, 0.0)
    z_ref[HALO:HALO + tr, :] = z
    w = w_ref[...]
    conv = (w[2:3, :] * z
            + w[1:2, :] * z_ref[HALO - 1:HALO - 1 + tr, :]
            + w[0:1, :] * z_ref[HALO - 2:HALO - 2 + tr, :])
    y_ref[...] = (cb_ref[...].astype(F32) * conv).astype(y_ref.dtype)


def _conv_branch(proj, conv_w, layer, tr):
    m = proj.shape[0]
    cw = 256
    nb = BRANCH // cw

    def halo_row(i):
        return jnp.maximum(i * (tr // HALO) - 1, 0)

    return pl.pallas_call(
        _conv_kernel,
        grid=(m // tr, nb),
        in_specs=[pl.BlockSpec((tr, cw), lambda i, c: (i, c)),
                  pl.BlockSpec((tr, cw), lambda i, c: (i, nb + c)),
                  pl.BlockSpec((tr, cw), lambda i, c: (i, 2 * nb + c)),
                  pl.BlockSpec((HALO, cw), lambda i, c: (halo_row(i), nb + c)),
                  pl.BlockSpec((HALO, cw), lambda i, c: (halo_row(i), 2 * nb + c)),
                  pl.BlockSpec((None, CONV_K, cw), lambda i, c: (layer, 0, c))],
        out_specs=pl.BlockSpec((tr, cw), lambda i, c: (i, c)),
        out_shape=jax.ShapeDtypeStruct((m, BRANCH), BF16),
        scratch_shapes=[pltpu.VMEM((HALO + tr, cw), F32)],
        compiler_params=_cp("parallel", "parallel"),
        name="conv_branch",
    )(proj, proj, proj, proj, proj, conv_w)


def _pool_kernel(u_ref, pu_ref, w_ref, s_ref, y_ref, x_ref):
    tr = u_ref.shape[0]
    i = pl.program_id(0)
    g = pl.program_id(1)
    win = jnp.left_shift(jnp.int32(2), g)
    x = u_ref[...].astype(F32)
    x_ref[0:HALO, :] = jnp.where(i > 0, pu_ref[...].astype(F32), 0.0)
    x_ref[HALO:HALO + tr, :] = x
    total = x
    for j in range(1, max(POOL_WINDOWS)):
        keep = jnp.where(j < win, 1.0, 0.0).astype(F32)
        total = total + keep * x_ref[HALO - j:HALO - j + tr, :]
    t = i * tr + lax.broadcasted_iota(jnp.int32, x.shape, 0)
    count = jnp.minimum(t + 1, win).astype(F32)
    pooled = (total / count - x).astype(BF16)
    mixed = jnp.dot(pooled, w_ref[...].astype(BF16), preferred_element_type=F32)
    y_ref[...] = (mixed * s_ref[...]).astype(y_ref.dtype)


def _pool_branch(proj, pool_w, pool_scale, layer, tr):
    m = proj.shape[0]
    base = (HALF_COLS - BRANCH) // POOL_GROUP
    ng = len(POOL_WINDOWS)

    def halo_row(i):
        return jnp.maximum(i * (tr // HALO) - 1, 0)

    return pl.pallas_call(
        _pool_kernel,
        grid=(m // tr, ng),
        in_specs=[pl.BlockSpec((tr, POOL_GROUP), lambda i, g: (i, base + g)),
                  pl.BlockSpec((HALO, POOL_GROUP), lambda i, g: (halo_row(i), base + g)),
                  pl.BlockSpec((None, None, POOL_GROUP, POOL_GROUP), lambda i, g: (layer, g, 0, 0)),
                  pl.BlockSpec((None, 1, POOL_GROUP), lambda i, g: (layer, 0, g))],
        out_specs=pl.BlockSpec((tr, POOL_GROUP), lambda i, g: (i, g)),
        out_shape=jax.ShapeDtypeStruct((m, BRANCH), BF16),
        scratch_shapes=[pltpu.VMEM((HALO + tr, POOL_GROUP), F32)],
        compiler_params=_cp("parallel", "parallel"),
        name="pool_branch",
    )(proj, proj, pool_w, pool_scale)


def _store_value_t(vt_ref, v, heads):
    tr = v.shape[0]
    vt = v.T.astype(vt_ref.dtype)
    ones = jnp.ones((V_ROWS - LANES, tr), vt_ref.dtype)
    for h in range(heads):
        vt_ref[0, h * V_ROWS:h * V_ROWS + LANES, :] = vt[h * LANES:(h + 1) * LANES, :]
        vt_ref[0, h * V_ROWS + LANES:(h + 1) * V_ROWS, :] = ones


def _mla_prep_kernel(cq_ref, ckv_ref, sm_ref, gq_ref, gkv_ref, wuq_ref, wukv_ref,
                     cos_ref, sin_ref, qt_ref, kn_ref, kpe_ref, vt_ref):
    cos = cos_ref[...]
    sin = sin_ref[...]
    scale = (MLA_NOPE + MLA_ROPE) ** -0.5 * LOG2E
    cqn = _rms(cq_ref[...].astype(F32), gq_ref[...]).astype(BF16)
    q = jnp.dot(cqn, wuq_ref[...], preferred_element_type=F32)
    for h in range(MLA_HEADS):
        lo = 2 * h * LANES
        qt_ref[lo:lo + LANES, :] = (q[:, lo:lo + LANES] * scale).T.astype(qt_ref.dtype)
        pe = _rope_half64(q[:, lo + LANES:lo + 2 * LANES], cos, sin)
        qt_ref[lo + LANES:lo + 2 * LANES, :] = (pe * scale).T.astype(qt_ref.dtype)
    ckvn = _rms(ckv_ref[...].astype(F32), gkv_ref[...]).astype(BF16)
    kv = jnp.dot(ckvn, wukv_ref[...], preferred_element_type=F32)
    kn_ref[...] = kv[:, :BRANCH].astype(kn_ref.dtype)
    _store_value_t(vt_ref, kv[:, BRANCH:], MLA_HEADS)
    kpe_ref[...] = _rope_half64(sm_ref[:, 0:LANES], cos, sin).astype(kpe_ref.dtype)


def _mla_prep(proj, small, gq, gkv, wuq, wukv, cos64, sin64, layer, tr):
    m = proj.shape[0]
    qw = 2 * LANES * MLA_HEADS
    return pl.pallas_call(
        _mla_prep_kernel,
        grid=(m // tr,),
        in_specs=[pl.BlockSpec((tr, MLA_Q_RANK), lambda i: (i, 3 * BRANCH // MLA_Q_RANK)),
                  pl.BlockSpec((tr, MLA_KV_RANK), lambda i: (i, (3 * BRANCH + MLA_Q_RANK) // MLA_KV_RANK)),
                  pl.BlockSpec((tr, SMALL_COLS), lambda i: (i, 0)),
                  pl.BlockSpec((None, 1, MLA_Q_RANK), lambda i: (layer, 0, 0)),
                  pl.BlockSpec((None, 1, MLA_KV_RANK), lambda i: (layer, 0, 0)),
                  pl.BlockSpec((MLA_Q_RANK, qw), lambda i: (0, 0)),
                  pl.BlockSpec((MLA_KV_RANK, 2 * BRANCH), lambda i: (0, 0)),
                  pl.BlockSpec((tr, LANES), lambda i: (i, 0)),
                  pl.BlockSpec((tr, LANES), lambda i: (i, 0))],
        out_specs=[pl.BlockSpec((qw, tr), lambda i: (0, i)),
                   pl.BlockSpec((tr, BRANCH), lambda i: (i, 0)),
                   pl.BlockSpec((tr, LANES), lambda i: (i, 0)),
                   pl.BlockSpec((1, MLA_HEADS * V_ROWS, tr), lambda i: (i, 0, 0))],
        out_shape=[jax.ShapeDtypeStruct((qw, m), BF16),
                   jax.ShapeDtypeStruct((m, BRANCH), BF16),
                   jax.ShapeDtypeStruct((m, LANES), BF16),
                   jax.ShapeDtypeStruct((m // tr, MLA_HEADS * V_ROWS, tr), BF16)],
        compiler_params=_cp("parallel"),
        name="mla_prep",
    )(proj, proj, small, gq, gkv, wuq, wukv, cos64, sin64)


def _dsa_prep_kernel(dq_ref, dk_ref, dv_ref, iq_ref, sm_ref, c128_ref, s128_ref, c64_ref, s64_ref,
                     qt_ref, k_ref, vt_ref, iqr_ref, ik_ref, wt_ref):
    c128, s128 = c128_ref[...], s128_ref[...]
    c64, s64 = c64_ref[...], s64_ref[...]
    scale = DSA_DIM ** -0.5 * LOG2E
    for h in range(BRANCH // LANES):
        sl = slice(h * LANES, (h + 1) * LANES)
        qt_ref[sl, :] = (_rope_half128(dq_ref[:, sl].astype(F32), c128, s128) * scale).T.astype(qt_ref.dtype)
        k_ref[:, sl] = _rope_half128(dk_ref[:, sl].astype(F32), c128, s128).astype(k_ref.dtype)
        iqr_ref[:, sl] = _rope_half64(iq_ref[:, sl].astype(F32), c64, s64).astype(iqr_ref.dtype)
    _store_value_t(vt_ref, dv_ref[...].astype(F32), DSA_HEADS)
    tail = sm_ref[:, LANES:2 * LANES]
    lane = lax.broadcasted_iota(jnp.int32, tail.shape, 1)
    ik_lo = jnp.where(lane < IDX_DIM, _rope_half64(tail, c64, s64), 0.0)
    ik_ref[:, 0:LANES] = ik_lo.astype(ik_ref.dtype)
    ik_ref[:, LANES:2 * LANES] = pltpu.roll(ik_lo, IDX_DIM, axis=1).astype(ik_ref.dtype)
    idx_w_scale = (IDX_HEADS ** -0.5) * (IDX_DIM ** -0.5)
    wt_ref[...] = (tail * idx_w_scale).T


def _dsa_prep(proj_b, small, c128, s128, c64, s64, tr):
    m = proj_b.shape[0]
    col = lambda c: pl.BlockSpec((tr, BRANCH), lambda i: (i, c))
    tab = pl.BlockSpec((tr, LANES), lambda i: (i, 0))
    return pl.pallas_call(
        _dsa_prep_kernel,
        grid=(m // tr,),
        in_specs=[col(0), col(1), col(2), col(3),
                  pl.BlockSpec((tr, SMALL_COLS), lambda i: (i, 0)),
                  tab, tab, tab, tab],
        out_specs=[pl.BlockSpec((BRANCH, tr), lambda i: (0, i)),
                   pl.BlockSpec((tr, BRANCH), lambda i: (i, 0)),
                   pl.BlockSpec((1, DSA_HEADS * V_ROWS, tr), lambda i: (i, 0, 0)),
                   pl.BlockSpec((tr, BRANCH), lambda i: (i, 0)),
                   pl.BlockSpec((tr, 2 * LANES), lambda i: (i, 0)),
                   pl.BlockSpec((LANES, tr), lambda i: (0, i))],
        out_shape=[jax.ShapeDtypeStruct((BRANCH, m), BF16),
                   jax.ShapeDtypeStruct((m, BRANCH), BF16),
                   jax.ShapeDtypeStruct((m // tr, DSA_HEADS * V_ROWS, tr), BF16),
                   jax.ShapeDtypeStruct((m, BRANCH), BF16),
                   jax.ShapeDtypeStruct((m, 2 * LANES), BF16),
                   jax.ShapeDtypeStruct((LANES, m), F32)],
        compiler_params=_cp("parallel"),
        name="dsa_prep",
    )(proj_b, proj_b, proj_b, proj_b, small, c128, s128, c64, s64)


def _float_to_ordered_int(s):
    b = lax.bitcast_convert_type(s, jnp.int32)
    return b ^ ((b >> 31) & jnp.int32(0x7FFFFFFF))


def _indexer_kernel(iq_ref, ik_ref, wt_ref, bias_ref, key_ref, cut_ref, slot_ref, *, topk):
    tq = iq_ref.shape[0]
    total_rows = ik_ref.shape[0]
    tk = IDX_K_TILE
    i = pl.program_id(0)
    n_tiles = (i * tq + tq) // tk
    n_rows = n_tiles * tk
    qpos = i * tq + lax.broadcasted_iota(jnp.int32, (1, tq), 1)
    int_min = jnp.int32(-2 ** 31)

    def score_tile(kt, carry):
        start = pl.multiple_of(kt * tk, tk)
        ik_lo = ik_ref[pl.ds(start, tk), 0:LANES]
        ik_hi = ik_ref[pl.ds(start, tk), LANES:2 * LANES]
        acc = jnp.zeros((tk, tq), F32)
        for j in range(IDX_HEADS // 2):
            qpair = iq_ref[:, j * LANES:(j + 1) * LANES]
            for half, ik in enumerate((ik_lo, ik_hi)):
                g = 2 * j + half
                dots = lax.dot_general(ik, qpair, (((1,), (1,)), ((), ())),
                                       preferred_element_type=F32)
                acc = acc + jnp.maximum(dots, 0.0) * wt_ref[IDX_DIM + g:IDX_DIM + g + 1, :]
        acc = acc + 0.0
        kpos = start + lax.broadcasted_iota(jnp.int32, (tk, 1), 0)
        key_ref[pl.ds(start, tk), :] = jnp.where(kpos <= qpos, _float_to_ordered_int(acc), int_min)
        return carry

    lax.fori_loop(0, n_tiles, score_tile, 0)

    n_chunks = n_rows // COUNT_CHUNK

    def count(pred, with_pos=False):
        def body(c, accs):
            start = pl.multiple_of(c * COUNT_CHUNK, COUNT_CHUNK)
            accs = list(accs)
            chunk = key_ref[pl.ds(start, COUNT_CHUNK), :]
            for r in range(COUNT_CHUNK // 8):
                blk = chunk[8 * r:8 * r + 8, :]
                if with_pos:
                    pos = start + 8 * r + lax.broadcasted_iota(jnp.int32, (8, 1), 0)
                    hit = pred(blk, pos)
                else:
                    hit = pred(blk)
                accs[r % COUNT_ACCS] = accs[r % COUNT_ACCS] + jnp.where(hit, 1, 0).astype(jnp.int32)
            return tuple(accs)
        zero = jnp.zeros((8, tq), jnp.int32)
        accs = lax.fori_loop(0, n_chunks, body, (zero,) * COUNT_ACCS)
        return jnp.sum(functools.reduce(lambda a, b: a + b, accs), axis=0, keepdims=True)

    assert topk <= COUNT_CHUNK
    slot_ref[...] = key_ref[0:COUNT_CHUNK, :]

    def fold_slots(c, carry):
        start = pl.multiple_of(c * COUNT_CHUNK, COUNT_CHUNK)
        slot_ref[...] = jnp.maximum(slot_ref[...], key_ref[pl.ds(start, COUNT_CHUNK), :])
        return carry

    lax.fori_loop(1, n_chunks, fold_slots, 0)
    slots = slot_ref[...]
    hi = jnp.max(slots, axis=0, keepdims=True)
    lo = jnp.min(slots, axis=0, keepdims=True)
    n_bits = 32 - jnp.min(lax.clz(lo ^ hi))
    low_mask = jnp.where(n_bits >= 32, jnp.int32(-1),
                         jnp.left_shift(jnp.int32(1), jnp.minimum(n_bits, 31)) - 1)
    tau0 = ((lo ^ int_min) & ~low_mask) ^ int_min

    def bit_step(b, tau):
        cand = tau + jnp.left_shift(jnp.int32(1), 31 - b)
        cnt = count(lambda blk: blk >= cand)
        return jnp.where(cnt >= topk, cand, tau)

    tau = lax.fori_loop(32 - n_bits, 32, bit_step, tau0)

    n_gt = count(lambda blk: blk > tau)
    n_eq = count(lambda blk: blk == tau)
    need = topk - n_gt
    cut_bits = 14
    cut_ref[...] = jnp.full((1, tq), 1 << cut_bits, jnp.int32)

    @pl.when(jnp.max(n_eq - need) > 0)
    def _():
        def cut_step(b, cut):
            cand = cut + jnp.left_shift(jnp.int32(1), cut_bits - 1 - b)
            cnt = count(lambda blk, pos: (blk == tau) & (pos < cand), with_pos=True)
            return jnp.where(cnt <= need, cand, cut)
        cut_ref[...] = lax.fori_loop(0, cut_bits, cut_step, jnp.zeros((1, tq), jnp.int32))

    cut = cut_ref[...]

    def write_sel(c, carry):
        start = pl.multiple_of(c * WRITE_CHUNK, WRITE_CHUNK)
        blk = key_ref[pl.ds(start, WRITE_CHUNK), :]
        pos = start + lax.broadcasted_iota(jnp.int32, (WRITE_CHUNK, 1), 0)
        sel = ((blk > tau) | ((blk == tau) & (pos < cut))) & (pos <= qpos)
        bias_ref[pl.ds(start, WRITE_CHUNK), :] = jnp.where(sel, 0.0, NEG).astype(bias_ref.dtype)
        return carry

    lax.fori_loop(0, n_rows // WRITE_CHUNK, write_sel, 0)

    def write_neg(c, carry):
        start = pl.multiple_of(c * WRITE_CHUNK, WRITE_CHUNK)
        bias_ref[pl.ds(start, WRITE_CHUNK), :] = jnp.full((WRITE_CHUNK, tq), NEG, bias_ref.dtype)
        return carry

    lax.fori_loop(n_rows // WRITE_CHUNK, total_rows // WRITE_CHUNK, write_neg, 0)


def _indexer(iq_r, ik_ab, wt, topk):
    m = iq_r.shape[0]
    return pl.pallas_call(
        functools.partial(_indexer_kernel, topk=topk),
        grid=(m // Q_TILE,),
        in_specs=[pl.BlockSpec((Q_TILE, BRANCH), lambda i: (i, 0)),
                  pl.BlockSpec((m, 2 * LANES), lambda i: (0, 0)),
                  pl.BlockSpec((LANES, Q_TILE), lambda i: (0, i))],
        out_specs=pl.BlockSpec((m, Q_TILE), lambda i: (0, i)),
        out_shape=jax.ShapeDtypeStruct((m, m), BF16),
        scratch_shapes=[pltpu.VMEM((m, Q_TILE), jnp.int32),
                        pltpu.VMEM((1, Q_TILE), jnp.int32),
                        pltpu.VMEM((COUNT_CHUNK, Q_TILE), jnp.int32)],
        compiler_params=_cp("parallel"),
        name="dsa_indexer",
    )(iq_r, ik_ab, wt)


def _flash_kernel(*refs, q_axis, has_kpe, has_bias):
    refs = list(refs)
    q_ref, k_ref = refs[0], refs[1]
    pos = 2
    kpe_ref = bias_ref = None
    if has_kpe:
        kpe_ref = refs[pos]
        pos += 1
    vt_ref = refs[pos]
    pos += 1
    if has_bias:
        bias_ref = refs[pos]
        pos += 1
    o_ref, sa_ref, sb_ref = refs[pos], refs[pos + 1], refs[pos + 2]

    tq = q_ref.shape[1]
    heads = vt_ref.shape[1] // V_ROWS
    tk = vt_ref.shape[2]
    dq = q_ref.shape[0] // heads
    assert tk % tq == 0
    i = pl.program_id(q_axis)
    n_tiles = (i * tq + tq + tk - 1) // tk
    qpos = i * tq + lax.broadcasted_iota(jnp.int32, (1, tq), 1)

    def compute_scores(kt, s_ref):
        start = pl.multiple_of(kt * tk, tk)
        kpe = kpe_ref[pl.ds(start, tk), :] if has_kpe else None
        for h in range(heads):
            k = k_ref[pl.ds(start, tk), h * LANES:(h + 1) * LANES]
            if has_kpe:
                k = jnp.concatenate([k, kpe], axis=1)
            s_ref[h] = jnp.dot(k, q_ref[h * dq:(h + 1) * dq, :], preferred_element_type=F32)

    def consume_scores(kt, s_ref, carry, causal_mask):
        start = pl.multiple_of(kt * tk, tk)
        bias = bias_ref[pl.ds(start, tk), :].astype(F32) if has_bias else None
        new = []
        for h in range(heads):
            m_run, acc = carry[h]
            s = s_ref[h]
            if has_bias:
                s = s + bias
            if causal_mask:
                kpos = start + lax.broadcasted_iota(jnp.int32, (tk, 1), 0)
                s = jnp.where(kpos <= qpos, s, NEG)
            m_new = jnp.maximum(m_run, jnp.max(s, axis=0, keepdims=True))
            alpha = jnp.exp2(m_run - m_new)
            p = jnp.exp2(s - m_new).astype(BF16)
            pv = jnp.dot(vt_ref[kt, h * V_ROWS:(h + 1) * V_ROWS, :], p, preferred_element_type=F32)
            new.append((m_new, alpha * acc + pv))
        return tuple(new)

    def double_step(u, carry):
        kt = 2 * u
        compute_scores(kt + 1, sb_ref)
        carry = consume_scores(kt, sa_ref, carry, False)
        compute_scores(kt + 2, sa_ref)
        return consume_scores(kt + 1, sb_ref, carry, False)

    mask_last = not has_bias
    last = n_tiles - 1

    def odd_tail(carry):
        compute_scores(last, sb_ref)
        carry = consume_scores(last - 1, sa_ref, carry, False)
        return consume_scores(last, sb_ref, carry, mask_last)

    def even_tail(carry):
        return consume_scores(last, sa_ref, carry, mask_last)

    init = tuple((jnp.full((1, tq), NEG, F32), jnp.zeros((V_ROWS, tq), F32)) for _ in range(heads))
    compute_scores(0, sa_ref)
    carry = lax.fori_loop(0, last // 2, double_step, init)
    carry = lax.cond(last % 2 == 1, odd_tail, even_tail, carry)
    for h in range(heads):
        acc = carry[h][1]
        out = acc[0:LANES, :] / acc[LANES:LANES + 1, :]
        o_ref[:, h * LANES:(h + 1) * LANES] = out.T.astype(o_ref.dtype)


def _score_scratch(heads, tk):
    return [pltpu.VMEM((heads, tk, Q_TILE), F32), pltpu.VMEM((heads, tk, Q_TILE), F32)]


def _mla_attention(q, kn, kpe, vt):
    m = kn.shape[0]
    n_kt, _, tk = vt.shape
    hp = HEADS_PER_STEP
    return pl.pallas_call(
        functools.partial(_flash_kernel, q_axis=1, has_kpe=True, has_bias=False),
        grid=(MLA_HEADS // hp, m // Q_TILE),
        in_specs=[pl.BlockSpec((hp * 2 * LANES, Q_TILE), lambda h, i: (h, i)),
                  pl.BlockSpec((m, hp * LANES), lambda h, i: (0, h)),
                  pl.BlockSpec((m, LANES), lambda h, i: (0, 0)),
                  pl.BlockSpec((n_kt, hp * V_ROWS, tk), lambda h, i: (0, h, 0))],
        out_specs=pl.BlockSpec((Q_TILE, hp * MLA_V), lambda h, i: (i, h)),
        out_shape=jax.ShapeDtypeStruct((m, BRANCH), BF16),
        scratch_shapes=_score_scratch(hp, tk),
        compiler_params=_cp("parallel", "parallel"),
        name="mla_attention",
    )(q, kn, kpe, vt)


def _dsa_attention(q, k, vt, bias):
    m = k.shape[0]
    n_kt, _, tk = vt.shape
    hp = HEADS_PER_STEP
    return pl.pallas_call(
        functools.partial(_flash_kernel, q_axis=1, has_kpe=False, has_bias=True),
        grid=(DSA_HEADS // hp, m // Q_TILE),
        in_specs=[pl.BlockSpec((hp * DSA_DIM, Q_TILE), lambda h, i: (h, i)),
                  pl.BlockSpec((m, hp * DSA_DIM), lambda h, i: (0, h)),
                  pl.BlockSpec((n_kt, hp * V_ROWS, tk), lambda h, i: (0, h, 0)),
                  pl.BlockSpec((m, Q_TILE), lambda h, i: (0, i))],
        out_specs=pl.BlockSpec((Q_TILE, hp * DSA_DIM), lambda h, i: (i, h)),
        out_shape=jax.ShapeDtypeStruct((m, BRANCH), BF16),
        scratch_shapes=_score_scratch(hp, tk),
        compiler_params=_cp("parallel", "parallel"),
        name="dsa_attention",
    )(q, k, vt, bias)


def _rope_tables(rows, dim):
    inv = 1.0 / jnp.power(ROPE_THETA, jnp.arange(0, dim, 2, dtype=F32) / dim)
    ang = jnp.arange(rows, dtype=F32)[:, None] * inv[None, :]
    cos, sin = jnp.cos(ang), jnp.sin(ang)
    reps = LANES // dim
    return (jnp.tile(jnp.concatenate([cos, cos], axis=1), (1, reps)),
            jnp.tile(jnp.concatenate([-sin, sin], axis=1), (1, reps)))


def _regroup_w_in(w):
    o_kr = 3 * BRANCH + MLA_Q_RANK + MLA_KV_RANK
    o_dq = o_kr + MLA_ROPE
    o_ik = o_dq + 4 * BRANCH
    o_iw = o_ik + IDX_DIM
    o_pu = o_iw + IDX_HEADS
    assert o_kr == HALF_COLS
    second = jnp.concatenate([w[:, :, o_dq:o_ik], w[:, :, o_pu:]], axis=2)
    zeros = lambda n: jnp.zeros(w.shape[:2] + (n,), w.dtype)
    small = jnp.concatenate([w[:, :, o_kr:o_dq], zeros(LANES - MLA_ROPE),
                             w[:, :, o_ik:o_iw], w[:, :, o_iw:o_pu],
                             zeros(LANES - IDX_DIM - IDX_HEADS)], axis=2)
    return second, small


def _layout_w_uq(w):
    w3 = w.reshape(MLA_Q_RANK, MLA_HEADS, MLA_NOPE + MLA_ROPE)
    w3 = jnp.pad(w3, ((0, 0), (0, 0), (0, 2 * LANES - MLA_NOPE - MLA_ROPE)))
    return w3.reshape(MLA_Q_RANK, MLA_HEADS * 2 * LANES).astype(BF16)


def _layout_w_ukv(w):
    w4 = w.reshape(MLA_KV_RANK, MLA_HEADS, 2, MLA_NOPE)
    return w4.transpose(0, 2, 1, 3).reshape(MLA_KV_RANK, 2 * BRANCH).astype(BF16)


def _forward(x, meta_tokens, norm_mix_pre, norm_mix_post, norm_ffn_pre, norm_ffn_post,
             w_in, conv_w, mla_q_norm, mla_w_uq, mla_kv_norm, mla_w_ukv, pool_w, pool_scale,
             w_branch, w_gate, b_gate, w_out, ffn_w_gate, ffn_w_up, ffn_w_down):
    assert x.shape[0] == 1 and x.shape[2] == D_MODEL
    depth = w_in.shape[0]
    seq = x.shape[1]
    length = N_META + seq
    topk = min(IDX_TOPK_MAX, length // 4)
    rows = -(-length // ROW_TILE) * ROW_TILE
    assert rows % Q_TILE == 0 and rows % K_TILE == 0 and rows % NORM_TILE == 0
    assert rows < (1 << 14)

    h = jnp.concatenate([meta_tokens.astype(F32), x[0],
                         jnp.zeros((rows - length, D_MODEL), F32)], axis=0)
    c64, s64 = _rope_tables(rows, 64)
    c128, s128 = _rope_tables(rows, 128)
    w_in_second, w_in_small = _regroup_w_in(w_in)
    w_gate_flat = w_gate.reshape(depth * N_BRANCH, D_MODEL, D_MODEL)
    b_gate_flat = b_gate.reshape(depth * N_BRANCH, 1, D_MODEL)
    tn = 512
    wide = 2 * tn
    gate_tiles = D_MODEL // wide

    xn = _prenorm(h, norm_mix_pre[0][None])
    for l in range(depth):
        at_layer = lambda j, l=l: l
        tile = lambda j: j
        proj_a = _wide_proj(xn, w_in, at_layer, tile, HALF_COLS // wide, ROW_TILE, tn, BF16, "in_proj_a")
        proj_b = _wide_proj(xn, w_in_second, at_layer, tile, HALF_COLS // wide, ROW_TILE, tn, BF16,
                            "in_proj_b")
        small = _stacked_proj(xn, w_in_small, at_layer, tile, 1, ROW_TILE, SMALL_COLS, F32,
                              "in_proj_small")
        gates = _wide_proj(xn, w_gate_flat,
                           lambda j, l=l: l * N_BRANCH + j // gate_tiles, lambda j: j % gate_tiles,
                           N_BRANCH * gate_tiles, ROW_TILE, tn, BF16, "gates", bias=b_gate_flat)

        y_conv = _conv_branch(proj_a, conv_w, l, ROW_TILE)
        y_pool = _pool_branch(proj_b, pool_w, pool_scale[:, None, :], l, ROW_TILE)

        q_m, kn_m, kpe_m, vt_m = _mla_prep(
            proj_a, small, mla_q_norm[:, None, :], mla_kv_norm[:, None, :],
            _layout_w_uq(mla_w_uq[l]), _layout_w_ukv(mla_w_ukv[l]), c64, s64, l, K_TILE)
        y_mla = _mla_attention(q_m, kn_m, kpe_m, vt_m)

        q_d, k_d, vt_d, iq_r, ik_ab, wt = _dsa_prep(proj_b, small, c128, s128, c64, s64, K_TILE)
        bias = _indexer(iq_r, ik_ab, wt, topk)
        y_dsa = _dsa_attention(q_d, k_d, vt_d, bias)

        merged = _gated_merge((y_conv, y_mla, y_dsa, y_pool), gates, w_branch, l, ROW_TILE, tn)
        mix = _wide_proj(merged, w_out, at_layer, tile, D_MODEL // wide, ROW_TILE, tn, F32, "out_proj")
        h, xn = _resid_norm(h, mix, norm_mix_post[l][None], norm_ffn_pre[l][None])

        act = _swiglu(xn, ffn_w_gate, ffn_w_up, l, ROW_TILE, 256)
        f = _stacked_proj(act, ffn_w_down, at_layer, tile, D_MODEL // tn, 384, tn, F32, "ffn_down",
                          single_buffer_weight=True)
        g_next = norm_mix_pre[l + 1][None] if l + 1 < depth else None
        h, xn = _resid_norm(h, f, norm_ffn_post[l][None], g_next)

    return h[N_META:length][None]


def kernel(x, meta_tokens, norm_mix_pre, norm_mix_post, norm_ffn_pre, norm_ffn_post, w_in, conv_w, mla_q_norm, mla_w_uq, mla_kv_norm, mla_w_ukv, pool_w, pool_scale, w_branch, w_gate, b_gate, w_out, ffn_w_gate, ffn_w_up, ffn_w_down):
    return _forward(x, meta_tokens, norm_mix_pre, norm_mix_post, norm_ffn_pre, norm_ffn_post,
                    w_in, conv_w, mla_q_norm, mla_w_uq, mla_kv_norm, mla_w_ukv, pool_w, pool_scale,
                    w_branch, w_gate, b_gate, w_out, ffn_w_gate, ffn_w_up, ffn_w_down)
```

```python
import functools
import math

import jax
import jax.numpy as jnp
from jax import lax
from jax.experimental import pallas as pl
from jax.experimental.pallas import tpu as pltpu

F32 = jnp.float32
BF16 = jnp.bfloat16

D_MODEL = 4096
N_META = 16
ROPE_THETA = 10000.0
EPS = 1e-6
N_BRANCH = 4
BRANCH = 1024
CONV_K = 3
MLA_NOPE, MLA_ROPE, MLA_V, MLA_HEADS = 128, 64, 128, 8
MLA_Q_RANK, MLA_KV_RANK = 1536, 512
DSA_DIM, DSA_HEADS = 128, 8
IDX_HEADS, IDX_DIM, IDX_TOPK_MAX = 16, 64, 256
POOL_WINDOWS = (2, 4, 8, 16)
POOL_GROUP = 256
D_FF = 11008

LANES = 128
HALO = 16
ROW_TILE = 768
BIG_ROW_TILE = 1056
Q_TILE = 256
K_TILE = 768
V_ROWS = 144
HEADS_PER_STEP = 4
IDX_K_TILE = 256
COUNT_CHUNK = 256
COUNT_ACCS = 4
WRITE_CHUNK = 64
NORM_TILE = 192
NEG = -1e30
LOG2E = math.log2(math.e)
VMEM_LIMIT = 58 * 1024 * 1024

HALF_COLS = 5120
SMALL_COLS = 256


def _cp(*sem):
    return pltpu.CompilerParams(dimension_semantics=sem, vmem_limit_bytes=VMEM_LIMIT)


def _sigmoid(x):
    return 1.0 / (1.0 + jnp.exp(-x))


def _rms(x, g):
    return x * lax.rsqrt(jnp.mean(x * x, axis=-1, keepdims=True) + EPS) * g


def _cast_weights_once(w_refs, wbf_refs):
    @pl.when(pl.program_id(1) == 0)
    def _():
        for w_ref, wbf_ref in zip(w_refs, wbf_refs):
            wbf_ref[...] = w_ref[...].astype(wbf_ref.dtype)


def _proj_kernel(a_ref, w_ref, o_ref, wbf_ref):
    _cast_weights_once((w_ref,), (wbf_ref,))
    o_ref[...] = jnp.dot(a_ref[...], wbf_ref[...], preferred_element_type=F32).astype(o_ref.dtype)


def _gate_kernel(a_ref, w_ref, b_ref, o_ref, wbf_ref):
    _cast_weights_once((w_ref,), (wbf_ref,))
    z = jnp.dot(a_ref[...], wbf_ref[...], preferred_element_type=F32) + b_ref[...]
    o_ref[...] = _sigmoid(z).astype(o_ref.dtype)


def _stacked_proj(a, w, lead_of, col_of, n_tiles, tm, tn, out_dtype, name, bias=None,
                  single_buffer_weight=False):
    m, k = a.shape
    assert w.shape[1] == k
    w_kwargs = {"pipeline_mode": pl.Buffered(1)} if single_buffer_weight else {}
    in_specs = [pl.BlockSpec((tm, k), lambda j, i: (i, 0)),
                pl.BlockSpec((None, k, tn), lambda j, i: (lead_of(j), 0, col_of(j)), **w_kwargs)]
    args = [a, w]
    body = _proj_kernel
    if bias is not None:
        in_specs.append(pl.BlockSpec((None, 1, tn), lambda j, i: (lead_of(j), 0, col_of(j))))
        args.append(bias)
        body = _gate_kernel
    return pl.pallas_call(
        body,
        grid=(n_tiles, m // tm),
        in_specs=in_specs,
        out_specs=pl.BlockSpec((tm, tn), lambda j, i: (i, j)),
        out_shape=jax.ShapeDtypeStruct((m, n_tiles * tn), out_dtype),
        scratch_shapes=[pltpu.VMEM((k, tn), BF16)],
        compiler_params=_cp("parallel", "arbitrary"),
        name=name,
    )(*args)


def _wide_proj_kernel(*refs, has_bias):
    if has_bias:
        a_ref, w_ref, b_ref, o_ref, wbf_ref = refs
    else:
        a_ref, w_ref, o_ref, wbf_ref = refs
        b_ref = None
    half = w_ref.shape[1]
    i = pl.program_id(1)

    @pl.when(i == 0)
    def _():
        wbf_ref[:, 0:half] = w_ref[...].astype(wbf_ref.dtype)

    @pl.when(i == 1)
    def _():
        wbf_ref[:, half:2 * half] = w_ref[...].astype(wbf_ref.dtype)

    @pl.when(i >= 1)
    def _():
        z = jnp.dot(a_ref[...], wbf_ref[...], preferred_element_type=F32)
        if has_bias:
            z = _sigmoid(z + b_ref[...])
        o_ref[...] = z.astype(o_ref.dtype)


def _wide_proj(a, w, lead_of, col_of, n_tiles, tm, half, out_dtype, name, bias=None):
    m, k = a.shape
    assert w.shape[1] == k
    row = lambda i: jnp.maximum(i - 1, 0)
    in_specs = [pl.BlockSpec((tm, k), lambda j, i: (row(i), 0)),
                pl.BlockSpec((None, k, half),
                             lambda j, i: (lead_of(j), 0, 2 * col_of(j) + jnp.minimum(i, 1)))]
    args = [a, w]
    if bias is not None:
        in_specs.append(pl.BlockSpec((None, 1, 2 * half), lambda j, i: (lead_of(j), 0, col_of(j))))
        args.append(bias)
    return pl.pallas_call(
        functools.partial(_wide_proj_kernel, has_bias=bias is not None),
        grid=(n_tiles, m // tm + 1),
        in_specs=in_specs,
        out_specs=pl.BlockSpec((tm, 2 * half), lambda j, i: (row(i), j)),
        out_shape=jax.ShapeDtypeStruct((m, n_tiles * 2 * half), out_dtype),
        scratch_shapes=[pltpu.VMEM((k, 2 * half), BF16)],
        compiler_params=_cp("parallel", "arbitrary"),
        name=name,
    )(*args)


def _swiglu_kernel(a_ref, wg_ref, wu_ref, o_ref, wgbf_ref, wubf_ref, *, n_half_blocks):
    half = wg_ref.shape[1]
    j = pl.program_id(0)
    i = pl.program_id(1)

    @pl.when(i == 0)
    def _():
        wgbf_ref[:, 0:half] = wg_ref[...].astype(wgbf_ref.dtype)
        wubf_ref[:, 0:half] = wu_ref[...].astype(wubf_ref.dtype)

    @pl.when(i == 1)
    def _():
        wgbf_ref[:, half:2 * half] = wg_ref[...].astype(wgbf_ref.dtype)
        wubf_ref[:, half:2 * half] = wu_ref[...].astype(wubf_ref.dtype)

    def emit(width):
        a = a_ref[...]
        g = jnp.dot(a, wgbf_ref[:, 0:width], preferred_element_type=F32)
        u = jnp.dot(a, wubf_ref[:, 0:width], preferred_element_type=F32)
        o_ref[:, 0:width] = (g * _sigmoid(g) * u).astype(o_ref.dtype)

    has_right = 2 * j + 1 < n_half_blocks

    @pl.when((i >= 1) & has_right)
    def _():
        emit(2 * half)

    @pl.when((i >= 1) & jnp.logical_not(has_right))
    def _():
        emit(half)


def _swiglu(a, wg, wu, layer, tm, half):
    m, k = a.shape
    n = wg.shape[2]
    n_half_blocks = n // half
    assert n_half_blocks * half == n
    n_tiles = -(-n_half_blocks // 2)
    row = lambda i: jnp.maximum(i - 1, 0)
    w_spec = pl.BlockSpec(
        (None, k, half),
        lambda j, i: (layer, 0, jnp.minimum(2 * j + jnp.minimum(i, 1), n_half_blocks - 1)))
    return pl.pallas_call(
        functools.partial(_swiglu_kernel, n_half_blocks=n_half_blocks),
        grid=(n_tiles, m // tm + 1),
        in_specs=[pl.BlockSpec((tm, k), lambda j, i: (row(i), 0)), w_spec, w_spec],
        out_specs=pl.BlockSpec((tm, 2 * half), lambda j, i: (row(i), j)),
        out_shape=jax.ShapeDtypeStruct((m, n), BF16),
        scratch_shapes=[pltpu.VMEM((k, 2 * half), BF16), pltpu.VMEM((k, 2 * half), BF16)],
        compiler_params=_cp("parallel", "arbitrary"),
        name="ffn_swiglu",
    )(a, wg, wu)


def _merge_kernel(y0_ref, y1_ref, y2_ref, y3_ref, g0_ref, g1_ref, g2_ref, g3_ref, w_ref,
                  o_ref, wbf_ref):
    _cast_weights_once((w_ref,), (wbf_ref,))
    acc = None
    for b, (y_ref, g_ref) in enumerate(((y0_ref, g0_ref), (y1_ref, g1_ref),
                                        (y2_ref, g2_ref), (y3_ref, g3_ref))):
        val = g_ref[...].astype(F32) * jnp.dot(y_ref[...], wbf_ref[b], preferred_element_type=F32)
        acc = val if acc is None else acc + val
    o_ref[...] = acc.astype(o_ref.dtype)


def _gated_merge(ys, gates, w_branch, layer, tm, tn):
    m = ys[0].shape[0]
    nj = D_MODEL // tn
    y_spec = pl.BlockSpec((tm, BRANCH), lambda j, i: (i, 0))
    g_specs = [pl.BlockSpec((tm, tn), functools.partial(lambda j, i, b: (i, b * nj + j), b=b))
               for b in range(N_BRANCH)]
    return pl.pallas_call(
        _merge_kernel,
        grid=(nj, m // tm),
        in_specs=[y_spec] * N_BRANCH + g_specs
                 + [pl.BlockSpec((None, N_BRANCH, BRANCH, tn), lambda j, i: (layer, 0, 0, j))],
        out_specs=pl.BlockSpec((tm, tn), lambda j, i: (i, j)),
        out_shape=jax.ShapeDtypeStruct((m, D_MODEL), BF16),
        scratch_shapes=[pltpu.VMEM((N_BRANCH, BRANCH, tn), BF16)],
        compiler_params=_cp("parallel", "arbitrary"),
        name="gated_merge",
    )(*ys, gates, gates, gates, gates, w_branch)


def _norm_kernel(h_ref, g_ref, xn_ref):
    xn_ref[...] = _rms(h_ref[...], g_ref[...]).astype(xn_ref.dtype)


def _prenorm(h, g):
    m = h.shape[0]
    return pl.pallas_call(
        _norm_kernel,
        grid=(m // NORM_TILE,),
        in_specs=[pl.BlockSpec((NORM_TILE, D_MODEL), lambda i: (i, 0)),
                  pl.BlockSpec((1, D_MODEL), lambda i: (0, 0))],
        out_specs=pl.BlockSpec((NORM_TILE, D_MODEL), lambda i: (i, 0)),
        out_shape=jax.ShapeDtypeStruct((m, D_MODEL), BF16),
        compiler_params=_cp("parallel"),
        name="prenorm",
    )(h, g)


def _resid_norm_kernel(h_ref, o_ref, gpost_ref, gnext_ref, hn_ref, xn_ref):
    hn = h_ref[...] + _rms(o_ref[...], gpost_ref[...])
    hn_ref[...] = hn
    xn_ref[...] = _rms(hn, gnext_ref[...]).astype(xn_ref.dtype)


def _resid_kernel(h_ref, o_ref, gpost_ref, hn_ref):
    hn_ref[...] = h_ref[...] + _rms(o_ref[...], gpost_ref[...])


def _resid_norm(h, o, g_post, g_next):
    m = h.shape[0]
    row = pl.BlockSpec((NORM_TILE, D_MODEL), lambda i: (i, 0))
    gain = pl.BlockSpec((1, D_MODEL), lambda i: (0, 0))
    if g_next is None:
        return pl.pallas_call(
            _resid_kernel,
            grid=(m // NORM_TILE,),
            in_specs=[row, row, gain],
            out_specs=row,
            out_shape=jax.ShapeDtypeStruct((m, D_MODEL), F32),
            compiler_params=_cp("parallel"),
            name="resid_last",
        )(h, o, g_post), None
    return pl.pallas_call(
        _resid_norm_kernel,
        grid=(m // NORM_TILE,),
        in_specs=[row, row, gain, gain],
        out_specs=[row, row],
        out_shape=[jax.ShapeDtypeStruct((m, D_MODEL), F32),
                   jax.ShapeDtypeStruct((m, D_MODEL), BF16)],
        compiler_params=_cp("parallel"),
        name="resid_norm",
    )(h, o, g_post, g_next)


def _rope_half128(x, cos, sin_signed):
    return x * cos + pltpu.roll(x, 64, axis=1) * sin_signed


def _rope_half64(x, cos, sin_signed):
    lane = lax.broadcasted_iota(jnp.int32, x.shape, 1)
    partner = jnp.where((lane & 32) == 0, pltpu.roll(x, 96, axis=1), pltpu.roll(x, 32, axis=1))
    return x * cos + partner * sin_signed


def _conv_kernel(cb_ref, cc_ref, cu_ref, pc_ref, pu_ref, w_ref, y_ref, z_ref):
    tr = cb_ref.shape[0]
    i = pl.program_id(0)
    z = cc_ref[...].astype(F32) * cu_ref[...].astype(F32)
    zp = pc_ref[...].astype(F32) * pu_ref[...].astype(F32)
    z_ref[0:HALO, :] = jnp.where(i > 0, zp, 0.0)
    z_ref[HALO:HALO + tr, :] = z
    w = w_ref[...]
    conv = (w[2:3, :] * z
            + w[1:2, :] * z_ref[HALO - 1:HALO - 1 + tr, :]
            + w[0:1, :] * z_ref[HALO - 2:HALO - 2 + tr, :])
    y_ref[...] = (cb_ref[...].astype(F32) * conv).astype(y_ref.dtype)


def _conv_branch(proj, conv_w, layer, tr):
    m = proj.shape[0]
    cw = 256
    nb = BRANCH // cw

    def halo_row(i):
        return jnp.maximum(i * (tr // HALO) - 1, 0)

    return pl.pallas_call(
        _conv_kernel,
        grid=(m // tr, nb),
        in_specs=[pl.BlockSpec((tr, cw), lambda i, c: (i, c)),
                  pl.BlockSpec((tr, cw), lambda i, c: (i, nb + c)),
                  pl.BlockSpec((tr, cw), lambda i, c: (i, 2 * nb + c)),
                  pl.BlockSpec((HALO, cw), lambda i, c: (halo_row(i), nb + c)),
                  pl.BlockSpec((HALO, cw), lambda i, c: (halo_row(i), 2 * nb + c)),
                  pl.BlockSpec((None, CONV_K, cw), lambda i, c: (layer, 0, c))],
        out_specs=pl.BlockSpec((tr, cw), lambda i, c: (i, c)),
        out_shape=jax.ShapeDtypeStruct((m, BRANCH), BF16),
        scratch_shapes=[pltpu.VMEM((HALO + tr, cw), F32)],
        compiler_params=_cp("parallel", "parallel"),
        name="conv_branch",
    )(proj, proj, proj, proj, proj, conv_w)


def _pool_kernel(u_ref, pu_ref, w_ref, s_ref, y_ref, x_ref):
    tr = u_ref.shape[0]
    i = pl.program_id(0)
    g = pl.program_id(1)
    win = jnp.left_shift(jnp.int32(2), g)
    x = u_ref[...].astype(F32)
    x_ref[0:HALO, :] = jnp.where(i > 0, pu_ref[...].astype(F32), 0.0)
    x_ref[HALO:HALO + tr, :] = x
    total = x
    for j in range(1, max(POOL_WINDOWS)):
        keep = jnp.where(j < win, 1.0, 0.0).astype(F32)
        total = total + keep * x_ref[HALO - j:HALO - j + tr, :]
    t = i * tr + lax.broadcasted_iota(jnp.int32, x.shape, 0)
    count = jnp.minimum(t + 1, win).astype(F32)
    pooled = (total / count - x).astype(BF16)
    mixed = jnp.dot(pooled, w_ref[...].astype(BF16), preferred_element_type=F32)
    y_ref[...] = (mixed * s_ref[...]).astype(y_ref.dtype)


def _pool_branch(proj, pool_w, pool_scale, layer, tr):
    m = proj.shape[0]
    base = (HALF_COLS - BRANCH) // POOL_GROUP
    ng = len(POOL_WINDOWS)

    def halo_row(i):
        return jnp.maximum(i * (tr // HALO) - 1, 0)

    return pl.pallas_call(
        _pool_kernel,
        grid=(m // tr, ng),
        in_specs=[pl.BlockSpec((tr, POOL_GROUP), lambda i, g: (i, base + g)),
                  pl.BlockSpec((HALO, POOL_GROUP), lambda i, g: (halo_row(i), base + g)),
                  pl.BlockSpec((None, None, POOL_GROUP, POOL_GROUP), lambda i, g: (layer, g, 0, 0)),
                  pl.BlockSpec((None, 1, POOL_GROUP), lambda i, g: (layer, 0, g))],
        out_specs=pl.BlockSpec((tr, POOL_GROUP), lambda i, g: (i, g)),
        out_shape=jax.ShapeDtypeStruct((m, BRANCH), BF16),
        scratch_shapes=[pltpu.VMEM((HALO + tr, POOL_GROUP), F32)],
        compiler_params=_cp("parallel", "parallel"),
        name="pool_branch",
    )(proj, proj, pool_w, pool_scale)


def _store_value_t(vt_ref, v, heads):
    tr = v.shape[0]
    vt = v.T.astype(vt_ref.dtype)
    ones = jnp.ones((V_ROWS - LANES, tr), vt_ref.dtype)
    for h in range(heads):
        vt_ref[0, h * V_ROWS:h * V_ROWS + LANES, :] = vt[h * LANES:(h + 1) * LANES, :]
        vt_ref[0, h * V_ROWS + LANES:(h + 1) * V_ROWS, :] = ones


def _mla_prep_kernel(cq_ref, ckv_ref, sm_ref, gq_ref, gkv_ref, wuq_ref, wukv_ref,
                     cos_ref, sin_ref, qt_ref, kn_ref, kpe_ref, vt_ref):
    cos = cos_ref[...]
    sin = sin_ref[...]
    scale = (MLA_NOPE + MLA_ROPE) ** -0.5 * LOG2E
    cqn = _rms(cq_ref[...].astype(F32), gq_ref[...]).astype(BF16)
    q = jnp.dot(cqn, wuq_ref[...], preferred_element_type=F32)
    for h in range(MLA_HEADS):
        lo = 2 * h * LANES
        qt_ref[lo:lo + LANES, :] = (q[:, lo:lo + LANES] * scale).T.astype(qt_ref.dtype)
        pe = _rope_half64(q[:, lo + LANES:lo + 2 * LANES], cos, sin)
        qt_ref[lo + LANES:lo + 2 * LANES, :] = (pe * scale).T.astype(qt_ref.dtype)
    ckvn = _rms(ckv_ref[...].astype(F32), gkv_ref[...]).astype(BF16)
    kv = jnp.dot(ckvn, wukv_ref[...], preferred_element_type=F32)
    kn_ref[...] = kv[:, :BRANCH].astype(kn_ref.dtype)
    _store_value_t(vt_ref, kv[:, BRANCH:], MLA_HEADS)
    kpe_ref[...] = _rope_half64(sm_ref[:, 0:LANES], cos, sin).astype(kpe_ref.dtype)


def _mla_prep(proj, small, gq, gkv, wuq, wukv, cos64, sin64, layer, tr):
    m = proj.shape[0]
    qw = 2 * LANES * MLA_HEADS
    return pl.pallas_call(
        _mla_prep_kernel,
        grid=(m // tr,),
        in_specs=[pl.BlockSpec((tr, MLA_Q_RANK), lambda i: (i, 3 * BRANCH // MLA_Q_RANK)),
                  pl.BlockSpec((tr, MLA_KV_RANK), lambda i: (i, (3 * BRANCH + MLA_Q_RANK) // MLA_KV_RANK)),
                  pl.BlockSpec((tr, SMALL_COLS), lambda i: (i, 0)),
                  pl.BlockSpec((None, 1, MLA_Q_RANK), lambda i: (layer, 0, 0)),
                  pl.BlockSpec((None, 1, MLA_KV_RANK), lambda i: (layer, 0, 0)),
                  pl.BlockSpec((MLA_Q_RANK, qw), lambda i: (0, 0)),
                  pl.BlockSpec((MLA_KV_RANK, 2 * BRANCH), lambda i: (0, 0)),
                  pl.BlockSpec((tr, LANES), lambda i: (i, 0)),
                  pl.BlockSpec((tr, LANES), lambda i: (i, 0))],
        out_specs=[pl.BlockSpec((qw, tr), lambda i: (0, i)),
                   pl.BlockSpec((tr, BRANCH), lambda i: (i, 0)),
                   pl.BlockSpec((tr, LANES), lambda i: (i, 0)),
                   pl.BlockSpec((1, MLA_HEADS * V_ROWS, tr), lambda i: (i, 0, 0))],
        out_shape=[jax.ShapeDtypeStruct((qw, m), BF16),
                   jax.ShapeDtypeStruct((m, BRANCH), BF16),
                   jax.ShapeDtypeStruct((m, LANES), BF16),
                   jax.ShapeDtypeStruct((m // tr, MLA_HEADS * V_ROWS, tr), BF16)],
        compiler_params=_cp("parallel"),
        name="mla_prep",
    )(proj, proj, small, gq, gkv, wuq, wukv, cos64, sin64)


def _dsa_prep_kernel(dq_ref, dk_ref, dv_ref, iq_ref, sm_ref, c128_ref, s128_ref, c64_ref, s64_ref,
                     qt_ref, k_ref, vt_ref, iqr_ref, ik_ref, wt_ref):
    c128, s128 = c128_ref[...], s128_ref[...]
    c64, s64 = c64_ref[...], s64_ref[...]
    scale = DSA_DIM ** -0.5 * LOG2E
    for h in range(BRANCH // LANES):
        sl = slice(h * LANES, (h + 1) * LANES)
        qt_ref[sl, :] = (_rope_half128(dq_ref[:, sl].astype(F32), c128, s128) * scale).T.astype(qt_ref.dtype)
        k_ref[:, sl] = _rope_half128(dk_ref[:, sl].astype(F32), c128, s128).astype(k_ref.dtype)
        iqr_ref[:, sl] = _rope_half64(iq_ref[:, sl].astype(F32), c64, s64).astype(iqr_ref.dtype)
    _store_value_t(vt_ref, dv_ref[...].astype(F32), DSA_HEADS)
    tail = sm_ref[:, LANES:2 * LANES]
    lane = lax.broadcasted_iota(jnp.int32, tail.shape, 1)
    ik_lo = jnp.where(lane < IDX_DIM, _rope_half64(tail, c64, s64), 0.0)
    ik_ref[:, 0:LANES] = ik_lo.astype(ik_ref.dtype)
    ik_ref[:, LANES:2 * LANES] = pltpu.roll(ik_lo, IDX_DIM, axis=1).astype(ik_ref.dtype)
    idx_w_scale = (IDX_HEADS ** -0.5) * (IDX_DIM ** -0.5)
    wt_ref[...] = (tail * idx_w_scale).T


def _dsa_prep(proj_b, small, c128, s128, c64, s64, tr):
    m = proj_b.shape[0]
    col = lambda c: pl.BlockSpec((tr, BRANCH), lambda i: (i, c))
    tab = pl.BlockSpec((tr, LANES), lambda i: (i, 0))
    return pl.pallas_call(
        _dsa_prep_kernel,
        grid=(m // tr,),
        in_specs=[col(0), col(1), col(2), col(3),
                  pl.BlockSpec((tr, SMALL_COLS), lambda i: (i, 0)),
                  tab, tab, tab, tab],
        out_specs=[pl.BlockSpec((BRANCH, tr), lambda i: (0, i)),
                   pl.BlockSpec((tr, BRANCH), lambda i: (i, 0)),
                   pl.BlockSpec((1, DSA_HEADS * V_ROWS, tr), lambda i: (i, 0, 0)),
                   pl.BlockSpec((tr, BRANCH), lambda i: (i, 0)),
                   pl.BlockSpec((tr, 2 * LANES), lambda i: (i, 0)),
                   pl.BlockSpec((LANES, tr), lambda i: (0, i))],
        out_shape=[jax.ShapeDtypeStruct((BRANCH, m), BF16),
                   jax.ShapeDtypeStruct((m, BRANCH), BF16),
                   jax.ShapeDtypeStruct((m // tr, DSA_HEADS * V_ROWS, tr), BF16),
                   jax.ShapeDtypeStruct((m, BRANCH), BF16),
                   jax.ShapeDtypeStruct((m, 2 * LANES), BF16),
                   jax.ShapeDtypeStruct((LANES, m), F32)],
        compiler_params=_cp("parallel"),
        name="dsa_prep",
    )(proj_b, proj_b, proj_b, proj_b, small, c128, s128, c64, s64)


def _float_to_ordered_int(s):
    b = lax.bitcast_convert_type(s, jnp.int32)
    return b ^ ((b >> 31) & jnp.int32(0x7FFFFFFF))


def _indexer_kernel(iq_ref, ik_ref, wt_ref, bias_ref, key_ref, cut_ref, slot_ref, *, topk):
    tq = iq_ref.shape[0]
    total_rows = ik_ref.shape[0]
    tk = IDX_K_TILE
    i = pl.program_id(0)
    n_tiles = (i * tq + tq) // tk
    n_rows = n_tiles * tk
    qpos = i * tq + lax.broadcasted_iota(jnp.int32, (1, tq), 1)
    int_min = jnp.int32(-2 ** 31)

    def score_tile(kt, carry):
        start = pl.multiple_of(kt * tk, tk)
        ik_lo = ik_ref[pl.ds(start, tk), 0:LANES]
        ik_hi = ik_ref[pl.ds(start, tk), LANES:2 * LANES]
        acc = jnp.zeros((tk, tq), F32)
        for j in range(IDX_HEADS // 2):
            qpair = iq_ref[:, j * LANES:(j + 1) * LANES]
            for half, ik in enumerate((ik_lo, ik_hi)):
                g = 2 * j + half
                dots = lax.dot_general(ik, qpair, (((1,), (1,)), ((), ())),
                                       preferred_element_type=F32)
                acc = acc + jnp.maximum(dots, 0.0) * wt_ref[IDX_DIM + g:IDX_DIM + g + 1, :]
        acc = acc + 0.0
        kpos = start + lax.broadcasted_iota(jnp.int32, (tk, 1), 0)
        key_ref[pl.ds(start, tk), :] = jnp.where(kpos <= qpos, _float_to_ordered_int(acc), int_min)
        return carry

    lax.fori_loop(0, n_tiles, score_tile, 0)

    n_chunks = n_rows // COUNT_CHUNK

    def count(pred, with_pos=False):
        def body(c, accs):
            start = pl.multiple_of(c * COUNT_CHUNK, COUNT_CHUNK)
            accs = list(accs)
            chunk = key_ref[pl.ds(start, COUNT_CHUNK), :]
            for r in range(COUNT_CHUNK // 8):
                blk = chunk[8 * r:8 * r + 8, :]
                if with_pos:
                    pos = start + 8 * r + lax.broadcasted_iota(jnp.int32, (8, 1), 0)
                    hit = pred(blk, pos)
                else:
                    hit = pred(blk)
                accs[r % COUNT_ACCS] = accs[r % COUNT_ACCS] + jnp.where(hit, 1, 0).astype(jnp.int32)
            return tuple(accs)
        zero = jnp.zeros((8, tq), jnp.int32)
        accs = lax.fori_loop(0, n_chunks, body, (zero,) * COUNT_ACCS)
        return jnp.sum(functools.reduce(lambda a, b: a + b, accs), axis=0, keepdims=True)

    assert topk <= COUNT_CHUNK
    slot_ref[...] = key_ref[0:COUNT_CHUNK, :]

    def fold_slots(c, carry):
        start = pl.multiple_of(c * COUNT_CHUNK, COUNT_CHUNK)
        slot_ref[...] = jnp.maximum(slot_ref[...], key_ref[pl.ds(start, COUNT_CHUNK), :])
        return carry

    lax.fori_loop(1, n_chunks, fold_slots, 0)
    slots = slot_ref[...]
    hi = jnp.max(slots, axis=0, keepdims=True)
    lo = jnp.min(slots, axis=0, keepdims=True)
    n_bits = 32 - jnp.min(lax.clz(hi - lo))

    def bit_step(b, tau):
        cand = tau + jnp.left_shift(jnp.int32(1), n_bits - 1 - b)
        cnt = count(lambda blk: blk >= cand)
        return jnp.where((cnt >= topk) & (cand > tau), cand, tau)

    tau = lax.fori_loop(0, n_bits, bit_step, lo)

    n_gt = count(lambda blk: blk > tau)
    n_eq = count(lambda blk: blk == tau)
    need = topk - n_gt
    cut_bits = 14
    cut_ref[...] = jnp.full((1, tq), 1 << cut_bits, jnp.int32)

    @pl.when(jnp.max(n_eq - need) > 0)
    def _():
        def cut_step(b, cut):
            cand = cut + jnp.left_shift(jnp.int32(1), cut_bits - 1 - b)
            cnt = count(lambda blk, pos: (blk == tau) & (pos < cand), with_pos=True)
            return jnp.where(cnt <= need, cand, cut)
        cut_ref[...] = lax.fori_loop(0, cut_bits, cut_step, jnp.zeros((1, tq), jnp.int32))

    cut = cut_ref[...]

    def write_sel(c, carry):
        start = pl.multiple_of(c * WRITE_CHUNK, WRITE_CHUNK)
        blk = key_ref[pl.ds(start, WRITE_CHUNK), :]
        pos = start + lax.broadcasted_iota(jnp.int32, (WRITE_CHUNK, 1), 0)
        sel = ((blk > tau) | ((blk == tau) & (pos < cut))) & (pos <= qpos)
        bias_ref[pl.ds(start, WRITE_CHUNK), :] = jnp.where(sel, 0.0, NEG).astype(bias_ref.dtype)
        return carry

    lax.fori_loop(0, n_rows // WRITE_CHUNK, write_sel, 0)

    def write_neg(c, carry):
        start = pl.multiple_of(c * WRITE_CHUNK, WRITE_CHUNK)
        bias_ref[pl.ds(start, WRITE_CHUNK), :] = jnp.full((WRITE_CHUNK, tq), NEG, bias_ref.dtype)
        return carry

    lax.fori_loop(n_rows // WRITE_CHUNK, total_rows // WRITE_CHUNK, write_neg, 0)


def _indexer(iq_r, ik_ab, wt, topk):
    m = iq_r.shape[0]
    return pl.pallas_call(
        functools.partial(_indexer_kernel, topk=topk),
        grid=(m // Q_TILE,),
        in_specs=[pl.BlockSpec((Q_TILE, BRANCH), lambda i: (i, 0)),
                  pl.BlockSpec((m, 2 * LANES), lambda i: (0, 0)),
                  pl.BlockSpec((LANES, Q_TILE), lambda i: (0, i))],
        out_specs=pl.BlockSpec((m, Q_TILE), lambda i: (0, i)),
        out_shape=jax.ShapeDtypeStruct((m, m), BF16),
        scratch_shapes=[pltpu.VMEM((m, Q_TILE), jnp.int32),
                        pltpu.VMEM((1, Q_TILE), jnp.int32),
                        pltpu.VMEM((COUNT_CHUNK, Q_TILE), jnp.int32)],
        compiler_params=_cp("parallel"),
        name="dsa_indexer",
    )(iq_r, ik_ab, wt)


def _flash_kernel(*refs, q_axis, has_kpe, has_bias):
    refs = list(refs)
    q_ref, k_ref = refs[0], refs[1]
    pos = 2
    kpe_ref = bias_ref = None
    if has_kpe:
        kpe_ref = refs[pos]
        pos += 1
    vt_ref = refs[pos]
    pos += 1
    if has_bias:
        bias_ref = refs[pos]
        pos += 1
    o_ref, sa_ref, sb_ref = refs[pos], refs[pos + 1], refs[pos + 2]

    tq = q_ref.shape[1]
    heads = vt_ref.shape[1] // V_ROWS
    tk = vt_ref.shape[2]
    dq = q_ref.shape[0] // heads
    assert tk % tq == 0
    i = pl.program_id(q_axis)
    n_tiles = (i * tq + tq + tk - 1) // tk
    qpos = i * tq + lax.broadcasted_iota(jnp.int32, (1, tq), 1)

    def compute_scores(kt, s_ref):
        start = pl.multiple_of(kt * tk, tk)
        kpe = kpe_ref[pl.ds(start, tk), :] if has_kpe else None
        for h in range(heads):
            k = k_ref[pl.ds(start, tk), h * LANES:(h + 1) * LANES]
            if has_kpe:
                k = jnp.concatenate([k, kpe], axis=1)
            s_ref[h] = jnp.dot(k, q_ref[h * dq:(h + 1) * dq, :], preferred_element_type=F32)

    def consume_scores(kt, s_ref, carry, causal_mask):
        start = pl.multiple_of(kt * tk, tk)
        bias = bias_ref[pl.ds(start, tk), :].astype(F32) if has_bias else None
        new = []
        for h in range(heads):
            m_run, acc = carry[h]
            s = s_ref[h]
            if has_bias:
                s = s + bias
            if causal_mask:
                kpos = start + lax.broadcasted_iota(jnp.int32, (tk, 1), 0)
                s = jnp.where(kpos <= qpos, s, NEG)
            m_new = jnp.maximum(m_run, jnp.max(s, axis=0, keepdims=True))
            alpha = jnp.exp2(m_run - m_new)
            p = jnp.exp2(s - m_new).astype(BF16)
            pv = jnp.dot(vt_ref[kt, h * V_ROWS:(h + 1) * V_ROWS, :], p, preferred_element_type=F32)
            new.append((m_new, alpha * acc + pv))
        return tuple(new)

    def double_step(u, carry):
        kt = 2 * u
        compute_scores(kt + 1, sb_ref)
        carry = consume_scores(kt, sa_ref, carry, False)
        compute_scores(kt + 2, sa_ref)
        return consume_scores(kt + 1, sb_ref, carry, False)

    mask_last = not has_bias
    last = n_tiles - 1

    def odd_tail(carry):
        compute_scores(last, sb_ref)
        carry = consume_scores(last - 1, sa_ref, carry, False)
        return consume_scores(last, sb_ref, carry, mask_last)

    def even_tail(carry):
        return consume_scores(last, sa_ref, carry, mask_last)

    init = tuple((jnp.full((1, tq), NEG, F32), jnp.zeros((V_ROWS, tq), F32)) for _ in range(heads))
    compute_scores(0, sa_ref)
    carry = lax.fori_loop(0, last // 2, double_step, init)
    carry = lax.cond(last % 2 == 1, odd_tail, even_tail, carry)
    for h in range(heads):
        acc = carry[h][1]
        out = acc[0:LANES, :] / acc[LANES:LANES + 1, :]
        o_ref[:, h * LANES:(h + 1) * LANES] = out.T.astype(o_ref.dtype)


def _score_scratch(heads, tk):
    return [pltpu.VMEM((heads, tk, Q_TILE), F32), pltpu.VMEM((heads, tk, Q_TILE), F32)]


def _mla_attention(q, kn, kpe, vt):
    m = kn.shape[0]
    n_kt, _, tk = vt.shape
    hp = HEADS_PER_STEP
    return pl.pallas_call(
        functools.partial(_flash_kernel, q_axis=1, has_kpe=True, has_bias=False),
        grid=(MLA_HEADS // hp, m // Q_TILE),
        in_specs=[pl.BlockSpec((hp * 2 * LANES, Q_TILE), lambda h, i: (h, i)),
                  pl.BlockSpec((m, hp * LANES), lambda h, i: (0, h)),
                  pl.BlockSpec((m, LANES), lambda h, i: (0, 0)),
                  pl.BlockSpec((n_kt, hp * V_ROWS, tk), lambda h, i: (0, h, 0))],
        out_specs=pl.BlockSpec((Q_TILE, hp * MLA_V), lambda h, i: (i, h)),
        out_shape=jax.ShapeDtypeStruct((m, BRANCH), BF16),
        scratch_shapes=_score_scratch(hp, tk),
        compiler_params=_cp("parallel", "parallel"),
        name="mla_attention",
    )(q, kn, kpe, vt)


def _dsa_attention(q, k, vt, bias):
    m = k.shape[0]
    n_kt, _, tk = vt.shape
    hp = HEADS_PER_STEP
    return pl.pallas_call(
        functools.partial(_flash_kernel, q_axis=1, has_kpe=False, has_bias=True),
        grid=(DSA_HEADS // hp, m // Q_TILE),
        in_specs=[pl.BlockSpec((hp * DSA_DIM, Q_TILE), lambda h, i: (h, i)),
                  pl.BlockSpec((m, hp * DSA_DIM), lambda h, i: (0, h)),
                  pl.BlockSpec((n_kt, hp * V_ROWS, tk), lambda h, i: (0, h, 0)),
                  pl.BlockSpec((m, Q_TILE), lambda h, i: (0, i))],
        out_specs=pl.BlockSpec((Q_TILE, hp * DSA_DIM), lambda h, i: (i, h)),
        out_shape=jax.ShapeDtypeStruct((m, BRANCH), BF16),
        scratch_shapes=_score_scratch(hp, tk),
        compiler_params=_cp("parallel", "parallel"),
        name="dsa_attention",
    )(q, k, vt, bias)


def _rope_tables(rows, dim):
    inv = 1.0 / jnp.power(ROPE_THETA, jnp.arange(0, dim, 2, dtype=F32) / dim)
    ang = jnp.arange(rows, dtype=F32)[:, None] * inv[None, :]
    cos, sin = jnp.cos(ang), jnp.sin(ang)
    reps = LANES // dim
    return (jnp.tile(jnp.concatenate([cos, cos], axis=1), (1, reps)),
            jnp.tile(jnp.concatenate([-sin, sin], axis=1), (1, reps)))


def _regroup_w_in(w):
    o_kr = 3 * BRANCH + MLA_Q_RANK + MLA_KV_RANK
    o_dq = o_kr + MLA_ROPE
    o_ik = o_dq + 4 * BRANCH
    o_iw = o_ik + IDX_DIM
    o_pu = o_iw + IDX_HEADS
    assert o_kr == HALF_COLS
    second = jnp.concatenate([w[:, :, o_dq:o_ik], w[:, :, o_pu:]], axis=2)
    zeros = lambda n: jnp.zeros(w.shape[:2] + (n,), w.dtype)
    small = jnp.concatenate([w[:, :, o_kr:o_dq], zeros(LANES - MLA_ROPE),
                             w[:, :, o_ik:o_iw], w[:, :, o_iw:o_pu],
                             zeros(LANES - IDX_DIM - IDX_HEADS)], axis=2)
    return second, small


def _layout_w_uq(w):
    w3 = w.reshape(MLA_Q_RANK, MLA_HEADS, MLA_NOPE + MLA_ROPE)
    w3 = jnp.pad(w3, ((0, 0), (0, 0), (0, 2 * LANES - MLA_NOPE - MLA_ROPE)))
    return w3.reshape(MLA_Q_RANK, MLA_HEADS * 2 * LANES).astype(BF16)


def _layout_w_ukv(w):
    w4 = w.reshape(MLA_KV_RANK, MLA_HEADS, 2, MLA_NOPE)
    return w4.transpose(0, 2, 1, 3).reshape(MLA_KV_RANK, 2 * BRANCH).astype(BF16)


def _forward(x, meta_tokens, norm_mix_pre, norm_mix_post, norm_ffn_pre, norm_ffn_post,
             w_in, conv_w, mla_q_norm, mla_w_uq, mla_kv_norm, mla_w_ukv, pool_w, pool_scale,
             w_branch, w_gate, b_gate, w_out, ffn_w_gate, ffn_w_up, ffn_w_down):
    assert x.shape[0] == 1 and x.shape[2] == D_MODEL
    depth = w_in.shape[0]
    seq = x.shape[1]
    length = N_META + seq
    topk = min(IDX_TOPK_MAX, length // 4)
    rows = -(-length // ROW_TILE) * ROW_TILE
    assert rows % Q_TILE == 0 and rows % K_TILE == 0 and rows % NORM_TILE == 0
    assert rows < (1 << 14)

    h = jnp.concatenate([meta_tokens.astype(F32), x[0],
                         jnp.zeros((rows - length, D_MODEL), F32)], axis=0)
    c64, s64 = _rope_tables(rows, 64)
    c128, s128 = _rope_tables(rows, 128)
    w_in_second, w_in_small = _regroup_w_in(w_in)
    w_gate_flat = w_gate.reshape(depth * N_BRANCH, D_MODEL, D_MODEL)
    b_gate_flat = b_gate.reshape(depth * N_BRANCH, 1, D_MODEL)
    tn = 512
    wide = 2 * tn
    gate_tiles = D_MODEL // wide
    big_tm = BIG_ROW_TILE if rows % BIG_ROW_TILE == 0 else ROW_TILE

    xn = _prenorm(h, norm_mix_pre[0][None])
    for l in range(depth):
        at_layer = lambda j, l=l: l
        tile = lambda j: j
        proj_a = _wide_proj(xn, w_in, at_layer, tile, HALF_COLS // wide, big_tm, tn, BF16, "in_proj_a")
        proj_b = _wide_proj(xn, w_in_second, at_layer, tile, HALF_COLS // wide, big_tm, tn, BF16,
                            "in_proj_b")
        small = _stacked_proj(xn, w_in_small, at_layer, tile, 1, ROW_TILE, SMALL_COLS, F32,
                              "in_proj_small")
        gates = _wide_proj(xn, w_gate_flat,
                           lambda j, l=l: l * N_BRANCH + j // gate_tiles, lambda j: j % gate_tiles,
                           N_BRANCH * gate_tiles, big_tm, tn, BF16, "gates", bias=b_gate_flat)

        y_conv = _conv_branch(proj_a, conv_w, l, ROW_TILE)
        y_pool = _pool_branch(proj_b, pool_w, pool_scale[:, None, :], l, ROW_TILE)

        q_m, kn_m, kpe_m, vt_m = _mla_prep(
            proj_a, small, mla_q_norm[:, None, :], mla_kv_norm[:, None, :],
            _layout_w_uq(mla_w_uq[l]), _layout_w_ukv(mla_w_ukv[l]), c64, s64, l, K_TILE)
        y_mla = _mla_attention(q_m, kn_m, kpe_m, vt_m)

        q_d, k_d, vt_d, iq_r, ik_ab, wt = _dsa_prep(proj_b, small, c128, s128, c64, s64, K_TILE)
        bias = _indexer(iq_r, ik_ab, wt, topk)
        y_dsa = _dsa_attention(q_d, k_d, vt_d, bias)

        merged = _gated_merge((y_conv, y_mla, y_dsa, y_pool), gates, w_branch, l, ROW_TILE, tn)
        mix = _wide_proj(merged, w_out, at_layer, tile, D_MODEL // wide, ROW_TILE, tn, F32, "out_proj")
        h, xn = _resid_norm(h, mix, norm_mix_post[l][None], norm_ffn_pre[l][None])

        act = _swiglu(xn, ffn_w_gate, ffn_w_up, l, ROW_TILE, 256)
        f = _stacked_proj(act, ffn_w_down, at_layer, tile, D_MODEL // tn, 384, tn, F32, "ffn_down",
                          single_buffer_weight=True)
        g_next = norm_mix_pre[l + 1][None] if l + 1 < depth else None
        h, xn = _resid_norm(h, f, norm_ffn_post[l][None], g_next)

    return h[N_META:length][None]


def kernel(x, meta_tokens, norm_mix_pre, norm_mix_post, norm_ffn_pre, norm_ffn_post, w_in, conv_w, mla_q_norm, mla_w_uq, mla_kv_norm, mla_w_ukv, pool_w, pool_scale, w_branch, w_gate, b_gate, w_out, ffn_w_gate, ffn_w_up, ffn_w_down):
    return _forward(x, meta_tokens, norm_mix_pre, norm_mix_post, norm_ffn_pre, norm_ffn_post,
                    w_in, conv_w, mla_q_norm, mla_w_uq, mla_kv_norm, mla_w_ukv, pool_w, pool_scale,
                    w_branch, w_gate, b_gate, w_out, ffn_w_gate, ffn_w_up, ffn_w_down)
```

```python
import functools
import math

import jax
import jax.numpy as jnp
from jax import lax
from jax.experimental import pallas as pl
from jax.experimental.pallas import tpu as pltpu

F32 = jnp.float32
BF16 = jnp.bfloat16

D_MODEL = 4096
N_META = 16
ROPE_THETA = 10000.0
EPS = 1e-6
N_BRANCH = 4
BRANCH = 1024
CONV_K = 3
MLA_NOPE, MLA_ROPE, MLA_V, MLA_HEADS = 128, 64, 128, 8
MLA_Q_RANK, MLA_KV_RANK = 1536, 512
DSA_DIM, DSA_HEADS = 128, 8
IDX_HEADS, IDX_DIM, IDX_TOPK_MAX = 16, 64, 256
POOL_WINDOWS = (2, 4, 8, 16)
POOL_GROUP = 256
D_FF = 11008

LANES = 128
HALO = 16
ROW_TILE = 768
BIG_ROW_TILE = 1056
Q_TILE = 256
K_TILE = 768
V_ROWS = 144
HEADS_PER_STEP = 4
IDX_K_TILE = 256
CAST_ROWS = 256
COUNT_CHUNK = 256
COUNT_ACCS = 4
WRITE_CHUNK = 64
NORM_TILE = 192
NEG = -1e30
LOG2E = math.log2(math.e)
VMEM_LIMIT = 58 * 1024 * 1024

HALF_COLS = 5120
SMALL_COLS = 256


def _cp(*sem):
    return pltpu.CompilerParams(dimension_semantics=sem, vmem_limit_bytes=VMEM_LIMIT)


def _sigmoid(x):
    return 1.0 / (1.0 + jnp.exp(-x))


def _rms(x, g):
    return x * lax.rsqrt(jnp.mean(x * x, axis=-1, keepdims=True) + EPS) * g


def _cast_weights_once(w_refs, wbf_refs):
    @pl.when(pl.program_id(1) == 0)
    def _():
        for w_ref, wbf_ref in zip(w_refs, wbf_refs):
            wbf_ref[...] = w_ref[...].astype(wbf_ref.dtype)


def _proj_kernel(a_ref, w_ref, o_ref, wbf_ref):
    _cast_weights_once((w_ref,), (wbf_ref,))
    o_ref[...] = jnp.dot(a_ref[...], wbf_ref[...], preferred_element_type=F32).astype(o_ref.dtype)


def _gate_kernel(a_ref, w_ref, b_ref, o_ref, wbf_ref):
    _cast_weights_once((w_ref,), (wbf_ref,))
    z = jnp.dot(a_ref[...], wbf_ref[...], preferred_element_type=F32) + b_ref[...]
    o_ref[...] = _sigmoid(z).astype(o_ref.dtype)


def _stacked_proj(a, w, lead_of, col_of, n_tiles, tm, tn, out_dtype, name, bias=None,
                  single_buffer_weight=False):
    m, k = a.shape
    assert w.shape[1] == k
    w_kwargs = {"pipeline_mode": pl.Buffered(1)} if single_buffer_weight else {}
    in_specs = [pl.BlockSpec((tm, k), lambda j, i: (i, 0)),
                pl.BlockSpec((None, k, tn), lambda j, i: (lead_of(j), 0, col_of(j)), **w_kwargs)]
    args = [a, w]
    body = _proj_kernel
    if bias is not None:
        in_specs.append(pl.BlockSpec((None, 1, tn), lambda j, i: (lead_of(j), 0, col_of(j))))
        args.append(bias)
        body = _gate_kernel
    return pl.pallas_call(
        body,
        grid=(n_tiles, m // tm),
        in_specs=in_specs,
        out_specs=pl.BlockSpec((tm, tn), lambda j, i: (i, j)),
        out_shape=jax.ShapeDtypeStruct((m, n_tiles * tn), out_dtype),
        scratch_shapes=[pltpu.VMEM((k, tn), BF16)],
        compiler_params=_cp("parallel", "arbitrary"),
        name=name,
    )(*args)


def _cast_shifted(w_ref, next_ref, wbf_ref, dst_lo, lane_shift):
    k, half = w_ref.shape
    back = LANES - lane_shift

    def rows(c, carry):
        r0 = pl.multiple_of(c * CAST_ROWS, CAST_ROWS)
        lane = lax.broadcasted_iota(jnp.int32, (CAST_ROWS, LANES), 1)
        for t in range(half // LANES):
            cur = w_ref[pl.ds(r0, CAST_ROWS), t * LANES:(t + 1) * LANES]
            if (t + 1) * LANES < half:
                nxt = w_ref[pl.ds(r0, CAST_ROWS), (t + 1) * LANES:(t + 2) * LANES]
            else:
                nxt = next_ref[pl.ds(r0, CAST_ROWS), :]
            blk = jnp.where(lane < back, pltpu.roll(cur, back, axis=1), pltpu.roll(nxt, back, axis=1))
            wbf_ref[pl.ds(r0, CAST_ROWS), dst_lo + t * LANES:dst_lo + (t + 1) * LANES] = (
                blk.astype(wbf_ref.dtype))
        return carry

    lax.fori_loop(0, k // CAST_ROWS, rows, 0)


def _wide_proj_kernel(*refs, has_bias, lane_shift):
    refs = list(refs)
    a_ref, w_ref = refs[0], refs[1]
    pos = 2
    next_ref = b_ref = None
    if lane_shift:
        next_ref = refs[pos]
        pos += 1
    if has_bias:
        b_ref = refs[pos]
        pos += 1
    o_ref, wbf_ref = refs[pos], refs[pos + 1]
    half = w_ref.shape[1]
    i = pl.program_id(1)

    def cast_half(dst_lo):
        if lane_shift:
            _cast_shifted(w_ref, next_ref, wbf_ref, dst_lo, lane_shift)
        else:
            wbf_ref[:, dst_lo:dst_lo + half] = w_ref[...].astype(wbf_ref.dtype)

    @pl.when(i == 0)
    def _():
        cast_half(0)

    @pl.when(i == 1)
    def _():
        cast_half(half)

    @pl.when(i >= 1)
    def _():
        z = jnp.dot(a_ref[...], wbf_ref[...], preferred_element_type=F32)
        if has_bias:
            z = _sigmoid(z + b_ref[...])
        o_ref[...] = z.astype(o_ref.dtype)


def _wide_proj(a, w, lead_of, col_of, n_tiles, tm, half, out_dtype, name, bias=None,
               first_half_block=0, lane_shift=0):
    m, k = a.shape
    assert w.shape[1] == k and 0 <= lane_shift < LANES
    row = lambda i: jnp.maximum(i - 1, 0)
    half_block = lambda j, i: first_half_block + 2 * col_of(j) + jnp.minimum(i, 1)
    in_specs = [pl.BlockSpec((tm, k), lambda j, i: (row(i), 0)),
                pl.BlockSpec((None, k, half), lambda j, i: (lead_of(j), 0, half_block(j, i)))]
    args = [a, w]
    if lane_shift:
        lanes_per_half = half // LANES
        in_specs.append(pl.BlockSpec(
            (None, k, LANES), lambda j, i: (lead_of(j), 0, (half_block(j, i) + 1) * lanes_per_half)))
        args.append(w)
    if bias is not None:
        in_specs.append(pl.BlockSpec((None, 1, 2 * half), lambda j, i: (lead_of(j), 0, col_of(j))))
        args.append(bias)
    return pl.pallas_call(
        functools.partial(_wide_proj_kernel, has_bias=bias is not None, lane_shift=lane_shift),
        grid=(n_tiles, m // tm + 1),
        in_specs=in_specs,
        out_specs=pl.BlockSpec((tm, 2 * half), lambda j, i: (row(i), j)),
        out_shape=jax.ShapeDtypeStruct((m, n_tiles * 2 * half), out_dtype),
        scratch_shapes=[pltpu.VMEM((k, 2 * half), BF16)],
        compiler_params=_cp("parallel", "arbitrary"),
        name=name,
    )(*args)


def _swiglu_kernel(a_ref, wg_ref, wu_ref, o_ref, wgbf_ref, wubf_ref, *, n_half_blocks):
    half = wg_ref.shape[1]
    j = pl.program_id(0)
    i = pl.program_id(1)

    @pl.when(i == 0)
    def _():
        wgbf_ref[:, 0:half] = wg_ref[...].astype(wgbf_ref.dtype)
        wubf_ref[:, 0:half] = wu_ref[...].astype(wubf_ref.dtype)

    @pl.when(i == 1)
    def _():
        wgbf_ref[:, half:2 * half] = wg_ref[...].astype(wgbf_ref.dtype)
        wubf_ref[:, half:2 * half] = wu_ref[...].astype(wubf_ref.dtype)

    def emit(width):
        a = a_ref[...]
        g = jnp.dot(a, wgbf_ref[:, 0:width], preferred_element_type=F32)
        u = jnp.dot(a, wubf_ref[:, 0:width], preferred_element_type=F32)
        o_ref[:, 0:width] = (g * _sigmoid(g) * u).astype(o_ref.dtype)

    has_right = 2 * j + 1 < n_half_blocks

    @pl.when((i >= 1) & has_right)
    def _():
        emit(2 * half)

    @pl.when((i >= 1) & jnp.logical_not(has_right))
    def _():
        emit(half)


def _swiglu(a, wg, wu, layer, tm, half):
    m, k = a.shape
    n = wg.shape[2]
    n_half_blocks = n // half
    assert n_half_blocks * half == n
    n_tiles = -(-n_half_blocks // 2)
    row = lambda i: jnp.maximum(i - 1, 0)
    w_spec = pl.BlockSpec(
        (None, k, half),
        lambda j, i: (layer, 0, jnp.minimum(2 * j + jnp.minimum(i, 1), n_half_blocks - 1)))
    return pl.pallas_call(
        functools.partial(_swiglu_kernel, n_half_blocks=n_half_blocks),
        grid=(n_tiles, m // tm + 1),
        in_specs=[pl.BlockSpec((tm, k), lambda j, i: (row(i), 0)), w_spec, w_spec],
        out_specs=pl.BlockSpec((tm, 2 * half), lambda j, i: (row(i), j)),
        out_shape=jax.ShapeDtypeStruct((m, n), BF16),
        scratch_shapes=[pltpu.VMEM((k, 2 * half), BF16), pltpu.VMEM((k, 2 * half), BF16)],
        compiler_params=_cp("parallel", "arbitrary"),
        name="ffn_swiglu",
    )(a, wg, wu)


def _merge_kernel(y0_ref, y1_ref, y2_ref, y3_ref, g0_ref, g1_ref, g2_ref, g3_ref, w_ref,
                  o_ref, wbf_ref):
    _cast_weights_once((w_ref,), (wbf_ref,))
    acc = None
    for b, (y_ref, g_ref) in enumerate(((y0_ref, g0_ref), (y1_ref, g1_ref),
                                        (y2_ref, g2_ref), (y3_ref, g3_ref))):
        val = g_ref[...].astype(F32) * jnp.dot(y_ref[...], wbf_ref[b], preferred_element_type=F32)
        acc = val if acc is None else acc + val
    o_ref[...] = acc.astype(o_ref.dtype)


def _gated_merge(ys, gates, w_branch, layer, tm, tn):
    m = ys[0].shape[0]
    nj = D_MODEL // tn
    y_spec = pl.BlockSpec((tm, BRANCH), lambda j, i: (i, 0))
    g_specs = [pl.BlockSpec((tm, tn), functools.partial(lambda j, i, b: (i, b * nj + j), b=b))
               for b in range(N_BRANCH)]
    return pl.pallas_call(
        _merge_kernel,
        grid=(nj, m // tm),
        in_specs=[y_spec] * N_BRANCH + g_specs
                 + [pl.BlockSpec((None, N_BRANCH, BRANCH, tn), lambda j, i: (layer, 0, 0, j))],
        out_specs=pl.BlockSpec((tm, tn), lambda j, i: (i, j)),
        out_shape=jax.ShapeDtypeStruct((m, D_MODEL), BF16),
        scratch_shapes=[pltpu.VMEM((N_BRANCH, BRANCH, tn), BF16)],
        compiler_params=_cp("parallel", "arbitrary"),
        name="gated_merge",
    )(*ys, gates, gates, gates, gates, w_branch)


def _assemble_kernel(x_ref, prev_ref, meta_ref, g_ref, h_ref, xn_ref, *, length):
    tr = x_ref.shape[0]
    i = pl.program_id(0)
    head = jnp.where(i == 0, meta_ref[...], prev_ref[...])
    tile = jnp.concatenate([head, x_ref[0:tr - N_META, :]], axis=0)
    pos = i * tr + lax.broadcasted_iota(jnp.int32, (tr, 1), 0)
    h = jnp.where(pos < length, tile, 0.0)
    h_ref[...] = h
    xn_ref[...] = _rms(h, g_ref[...]).astype(xn_ref.dtype)


def _assemble_prenorm(x2d, meta, g, rows):
    seq = x2d.shape[0]
    assert N_META == HALO and NORM_TILE % N_META == 0
    last_x = (seq - 1) // NORM_TILE
    last_prev = (seq - 1) // N_META
    per_tile = NORM_TILE // N_META
    row = pl.BlockSpec((NORM_TILE, D_MODEL), lambda i: (i, 0))
    return pl.pallas_call(
        functools.partial(_assemble_kernel, length=N_META + seq),
        grid=(rows // NORM_TILE,),
        in_specs=[pl.BlockSpec((NORM_TILE, D_MODEL), lambda i: (jnp.minimum(i, last_x), 0)),
                  pl.BlockSpec((N_META, D_MODEL),
                               lambda i: (jnp.clip(i * per_tile - 1, 0, last_prev), 0)),
                  pl.BlockSpec((N_META, D_MODEL), lambda i: (0, 0)),
                  pl.BlockSpec((1, D_MODEL), lambda i: (0, 0))],
        out_specs=[row, row],
        out_shape=[jax.ShapeDtypeStruct((rows, D_MODEL), F32),
                   jax.ShapeDtypeStruct((rows, D_MODEL), BF16)],
        compiler_params=_cp("parallel"),
        name="assemble_prenorm",
    )(x2d, x2d, meta, g)


def _resid_norm_kernel(h_ref, o_ref, gpost_ref, gnext_ref, hn_ref, xn_ref):
    hn = h_ref[...] + _rms(o_ref[...], gpost_ref[...])
    hn_ref[...] = hn
    xn_ref[...] = _rms(hn, gnext_ref[...]).astype(xn_ref.dtype)


def _resid_out_kernel(h_ref, o_ref, hnext_ref, onext_ref, gpost_ref, out_ref):
    g = gpost_ref[...]
    cur = h_ref[N_META:, :] + _rms(o_ref[N_META:, :], g)
    nxt = hnext_ref[...] + _rms(onext_ref[...], g)
    out_ref[...] = jnp.concatenate([cur, nxt], axis=0)


def _resid_norm(h, o, g_post, g_next):
    m = h.shape[0]
    row = pl.BlockSpec((NORM_TILE, D_MODEL), lambda i: (i, 0))
    gain = pl.BlockSpec((1, D_MODEL), lambda i: (0, 0))
    return pl.pallas_call(
        _resid_norm_kernel,
        grid=(m // NORM_TILE,),
        in_specs=[row, row, gain, gain],
        out_specs=[row, row],
        out_shape=[jax.ShapeDtypeStruct((m, D_MODEL), F32),
                   jax.ShapeDtypeStruct((m, D_MODEL), BF16)],
        compiler_params=_cp("parallel"),
        name="resid_norm",
    )(h, o, g_post, g_next)


def _resid_out(h, o, g_post, seq):
    m = h.shape[0]
    per_tile = NORM_TILE // N_META
    last_head = m // N_META - 1
    row = pl.BlockSpec((NORM_TILE, D_MODEL), lambda i: (i, 0))
    head = pl.BlockSpec((N_META, D_MODEL), lambda i: (jnp.minimum((i + 1) * per_tile, last_head), 0))
    return pl.pallas_call(
        _resid_out_kernel,
        grid=(-(-seq // NORM_TILE),),
        in_specs=[row, row, head, head, pl.BlockSpec((1, D_MODEL), lambda i: (0, 0))],
        out_specs=row,
        out_shape=jax.ShapeDtypeStruct((seq, D_MODEL), F32),
        compiler_params=_cp("parallel"),
        name="resid_out",
    )(h, o, h, o, g_post)


def _rope_half128(x, cos, sin_signed):
    return x * cos + pltpu.roll(x, 64, axis=1) * sin_signed


def _rope_half64(x, cos, sin_signed):
    lane = lax.broadcasted_iota(jnp.int32, x.shape, 1)
    partner = jnp.where((lane & 32) == 0, pltpu.roll(x, 96, axis=1), pltpu.roll(x, 32, axis=1))
    return x * cos + partner * sin_signed


def _conv_kernel(cb_ref, cc_ref, cu_ref, pc_ref, pu_ref, w_ref, y_ref, z_ref):
    tr = cb_ref.shape[0]
    i = pl.program_id(0)
    z = cc_ref[...].astype(F32) * cu_ref[...].astype(F32)
    zp = pc_ref[...].astype(F32) * pu_ref[...].astype(F32)
    z_ref[0:HALO, :] = jnp.where(i > 0, zp, 0.0)
    z_ref[HALO:HALO + tr, :] = z
    w = w_ref[...]
    conv = (w[2:3, :] * z
            + w[1:2, :] * z_ref[HALO - 1:HALO - 1 + tr, :]
            + w[0:1, :] * z_ref[HALO - 2:HALO - 2 + tr, :])
    y_ref[...] = (cb_ref[...].astype(F32) * conv).astype(y_ref.dtype)


def _conv_branch(proj, conv_w, layer, tr):
    m = proj.shape[0]
    cw = 256
    nb = BRANCH // cw

    def halo_row(i):
        return jnp.maximum(i * (tr // HALO) - 1, 0)

    return pl.pallas_call(
        _conv_kernel,
        grid=(m // tr, nb),
        in_specs=[pl.BlockSpec((tr, cw), lambda i, c: (i, c)),
                  pl.BlockSpec((tr, cw), lambda i, c: (i, nb + c)),
                  pl.BlockSpec((tr, cw), lambda i, c: (i, 2 * nb + c)),
                  pl.BlockSpec((HALO, cw), lambda i, c: (halo_row(i), nb + c)),
                  pl.BlockSpec((HALO, cw), lambda i, c: (halo_row(i), 2 * nb + c)),
                  pl.BlockSpec((None, CONV_K, cw), lambda i, c: (layer, 0, c))],
        out_specs=pl.BlockSpec((tr, cw), lambda i, c: (i, c)),
        out_shape=jax.ShapeDtypeStruct((m, BRANCH), BF16),
        scratch_shapes=[pltpu.VMEM((HALO + tr, cw), F32)],
        compiler_params=_cp("parallel", "parallel"),
        name="conv_branch",
    )(proj, proj, proj, proj, proj, conv_w)


def _pool_kernel(u_ref, pu_ref, w_ref, s_ref, y_ref, x_ref):
    tr = u_ref.shape[0]
    i = pl.program_id(0)
    g = pl.program_id(1)
    win = jnp.left_shift(jnp.int32(2), g)
    x = u_ref[...].astype(F32)
    x_ref[0:HALO, :] = jnp.where(i > 0, pu_ref[...].astype(F32), 0.0)
    x_ref[HALO:HALO + tr, :] = x
    total = x
    for j in range(1, max(POOL_WINDOWS)):
        keep = jnp.where(j < win, 1.0, 0.0).astype(F32)
        total = total + keep * x_ref[HALO - j:HALO - j + tr, :]
    t = i * tr + lax.broadcasted_iota(jnp.int32, x.shape, 0)
    count = jnp.minimum(t + 1, win).astype(F32)
    pooled = (total / count - x).astype(BF16)
    mixed = jnp.dot(pooled, w_ref[...].astype(BF16), preferred_element_type=F32)
    y_ref[...] = (mixed * s_ref[...]).astype(y_ref.dtype)


def _pool_branch(proj, pool_w, pool_scale, layer, tr):
    m = proj.shape[0]
    base = 0
    ng = len(POOL_WINDOWS)

    def halo_row(i):
        return jnp.maximum(i * (tr // HALO) - 1, 0)

    return pl.pallas_call(
        _pool_kernel,
        grid=(m // tr, ng),
        in_specs=[pl.BlockSpec((tr, POOL_GROUP), lambda i, g: (i, base + g)),
                  pl.BlockSpec((HALO, POOL_GROUP), lambda i, g: (halo_row(i), base + g)),
                  pl.BlockSpec((None, None, POOL_GROUP, POOL_GROUP), lambda i, g: (layer, g, 0, 0)),
                  pl.BlockSpec((None, 1, POOL_GROUP), lambda i, g: (layer, 0, g))],
        out_specs=pl.BlockSpec((tr, POOL_GROUP), lambda i, g: (i, g)),
        out_shape=jax.ShapeDtypeStruct((m, BRANCH), BF16),
        scratch_shapes=[pltpu.VMEM((HALO + tr, POOL_GROUP), F32)],
        compiler_params=_cp("parallel", "parallel"),
        name="pool_branch",
    )(proj, proj, pool_w, pool_scale)


def _store_value_t(vt_ref, v, heads):
    tr = v.shape[0]
    vt = v.T.astype(vt_ref.dtype)
    ones = jnp.ones((V_ROWS - LANES, tr), vt_ref.dtype)
    for h in range(heads):
        vt_ref[0, h * V_ROWS:h * V_ROWS + LANES, :] = vt[h * LANES:(h + 1) * LANES, :]
        vt_ref[0, h * V_ROWS + LANES:(h + 1) * V_ROWS, :] = ones


def _mla_prep_kernel(cq_ref, ckv_ref, sm_ref, gq_ref, gkv_ref, wuq_ref, wukv_ref,
                     cos_ref, sin_ref, qt_ref, kn_ref, kpe_ref, vt_ref):
    cos = cos_ref[...]
    sin = sin_ref[...]
    scale = (MLA_NOPE + MLA_ROPE) ** -0.5 * LOG2E
    cqn = _rms(cq_ref[...].astype(F32), gq_ref[...]).astype(BF16)
    q = jnp.dot(cqn, wuq_ref[...], preferred_element_type=F32)
    for h in range(MLA_HEADS):
        lo = 2 * h * LANES
        qt_ref[lo:lo + LANES, :] = (q[:, lo:lo + LANES] * scale).T.astype(qt_ref.dtype)
        pe = _rope_half64(q[:, lo + LANES:lo + 2 * LANES], cos, sin)
        qt_ref[lo + LANES:lo + 2 * LANES, :] = (pe * scale).T.astype(qt_ref.dtype)
    ckvn = _rms(ckv_ref[...].astype(F32), gkv_ref[...]).astype(BF16)
    kv = jnp.dot(ckvn, wukv_ref[...], preferred_element_type=F32)
    kn_ref[...] = kv[:, :BRANCH].astype(kn_ref.dtype)
    _store_value_t(vt_ref, kv[:, BRANCH:], MLA_HEADS)
    kpe_ref[...] = _rope_half64(sm_ref[:, 0:LANES], cos, sin).astype(kpe_ref.dtype)


def _mla_prep(proj, small, gq, gkv, wuq, wukv, cos64, sin64, layer, tr):
    m = proj.shape[0]
    qw = 2 * LANES * MLA_HEADS
    return pl.pallas_call(
        _mla_prep_kernel,
        grid=(m // tr,),
        in_specs=[pl.BlockSpec((tr, MLA_Q_RANK), lambda i: (i, 3 * BRANCH // MLA_Q_RANK)),
                  pl.BlockSpec((tr, MLA_KV_RANK), lambda i: (i, (3 * BRANCH + MLA_Q_RANK) // MLA_KV_RANK)),
                  pl.BlockSpec((tr, SMALL_COLS), lambda i: (i, 0)),
                  pl.BlockSpec((None, 1, MLA_Q_RANK), lambda i: (layer, 0, 0)),
                  pl.BlockSpec((None, 1, MLA_KV_RANK), lambda i: (layer, 0, 0)),
                  pl.BlockSpec((MLA_Q_RANK, qw), lambda i: (0, 0)),
                  pl.BlockSpec((MLA_KV_RANK, 2 * BRANCH), lambda i: (0, 0)),
                  pl.BlockSpec((tr, LANES), lambda i: (i, 0)),
                  pl.BlockSpec((tr, LANES), lambda i: (i, 0))],
        out_specs=[pl.BlockSpec((qw, tr), lambda i: (0, i)),
                   pl.BlockSpec((tr, BRANCH), lambda i: (i, 0)),
                   pl.BlockSpec((tr, LANES), lambda i: (i, 0)),
                   pl.BlockSpec((1, MLA_HEADS * V_ROWS, tr), lambda i: (i, 0, 0))],
        out_shape=[jax.ShapeDtypeStruct((qw, m), BF16),
                   jax.ShapeDtypeStruct((m, BRANCH), BF16),
                   jax.ShapeDtypeStruct((m, LANES), BF16),
                   jax.ShapeDtypeStruct((m // tr, MLA_HEADS * V_ROWS, tr), BF16)],
        compiler_params=_cp("parallel"),
        name="mla_prep",
    )(proj, proj, small, gq, gkv, wuq, wukv, cos64, sin64)


def _dsa_prep_kernel(dq_ref, dk_ref, dv_ref, iq_ref, sm_ref, c128_ref, s128_ref, c64_ref, s64_ref,
                     qt_ref, k_ref, vt_ref, iqr_ref, ik_ref, wt_ref):
    c128, s128 = c128_ref[...], s128_ref[...]
    c64, s64 = c64_ref[...], s64_ref[...]
    scale = DSA_DIM ** -0.5 * LOG2E
    for h in range(BRANCH // LANES):
        sl = slice(h * LANES, (h + 1) * LANES)
        qt_ref[sl, :] = (_rope_half128(dq_ref[:, sl].astype(F32), c128, s128) * scale).T.astype(qt_ref.dtype)
        k_ref[:, sl] = _rope_half128(dk_ref[:, sl].astype(F32), c128, s128).astype(k_ref.dtype)
        iqr_ref[:, sl] = _rope_half64(iq_ref[:, sl].astype(F32), c64, s64).astype(iqr_ref.dtype)
    _store_value_t(vt_ref, dv_ref[...].astype(F32), DSA_HEADS)
    tail = sm_ref[:, LANES:2 * LANES]
    lane = lax.broadcasted_iota(jnp.int32, tail.shape, 1)
    ik_lo = jnp.where(lane < IDX_DIM, _rope_half64(tail, c64, s64), 0.0)
    ik_ref[:, 0:LANES] = ik_lo.astype(ik_ref.dtype)
    ik_ref[:, LANES:2 * LANES] = pltpu.roll(ik_lo, IDX_DIM, axis=1).astype(ik_ref.dtype)
    idx_w_scale = (IDX_HEADS ** -0.5) * (IDX_DIM ** -0.5)
    wt_ref[...] = (tail * idx_w_scale).T


def _dsa_prep(proj_b, small, c128, s128, c64, s64, tr):
    m = proj_b.shape[0]
    col = lambda c: pl.BlockSpec((tr, BRANCH), lambda i: (i, c))
    tab = pl.BlockSpec((tr, LANES), lambda i: (i, 0))
    return pl.pallas_call(
        _dsa_prep_kernel,
        grid=(m // tr,),
        in_specs=[col(0), col(1), col(2), col(3),
                  pl.BlockSpec((tr, SMALL_COLS), lambda i: (i, 0)),
                  tab, tab, tab, tab],
        out_specs=[pl.BlockSpec((BRANCH, tr), lambda i: (0, i)),
                   pl.BlockSpec((tr, BRANCH), lambda i: (i, 0)),
                   pl.BlockSpec((1, DSA_HEADS * V_ROWS, tr), lambda i: (i, 0, 0)),
                   pl.BlockSpec((tr, BRANCH), lambda i: (i, 0)),
                   pl.BlockSpec((tr, 2 * LANES), lambda i: (i, 0)),
                   pl.BlockSpec((LANES, tr), lambda i: (0, i))],
        out_shape=[jax.ShapeDtypeStruct((BRANCH, m), BF16),
                   jax.ShapeDtypeStruct((m, BRANCH), BF16),
                   jax.ShapeDtypeStruct((m // tr, DSA_HEADS * V_ROWS, tr), BF16),
                   jax.ShapeDtypeStruct((m, BRANCH), BF16),
                   jax.ShapeDtypeStruct((m, 2 * LANES), BF16),
                   jax.ShapeDtypeStruct((LANES, m), F32)],
        compiler_params=_cp("parallel"),
        name="dsa_prep",
    )(proj_b, proj_b, proj_b, proj_b, small, c128, s128, c64, s64)


def _float_to_ordered_int(s):
    b = lax.bitcast_convert_type(s, jnp.int32)
    return b ^ ((b >> 31) & jnp.int32(0x7FFFFFFF))


def _indexer_kernel(iq_ref, ik_ref, wt_ref, bias_ref, key_ref, cut_ref, slot_ref, *, topk):
    tq = iq_ref.shape[0]
    total_rows = ik_ref.shape[0]
    tk = IDX_K_TILE
    i = pl.program_id(0)
    n_tiles = (i * tq + tq) // tk
    n_rows = n_tiles * tk
    qpos = i * tq + lax.broadcasted_iota(jnp.int32, (1, tq), 1)
    int_min = jnp.int32(-2 ** 31)

    def score_tile(kt, carry):
        start = pl.multiple_of(kt * tk, tk)
        ik_lo = ik_ref[pl.ds(start, tk), 0:LANES]
        ik_hi = ik_ref[pl.ds(start, tk), LANES:2 * LANES]
        acc = jnp.zeros((tk, tq), F32)
        for j in range(IDX_HEADS // 2):
            qpair = iq_ref[:, j * LANES:(j + 1) * LANES]
            for half, ik in enumerate((ik_lo, ik_hi)):
                g = 2 * j + half
                dots = lax.dot_general(ik, qpair, (((1,), (1,)), ((), ())),
                                       preferred_element_type=F32)
                acc = acc + jnp.maximum(dots, 0.0) * wt_ref[IDX_DIM + g:IDX_DIM + g + 1, :]
        acc = acc + 0.0
        kpos = start + lax.broadcasted_iota(jnp.int32, (tk, 1), 0)
        key_ref[pl.ds(start, tk), :] = jnp.where(kpos <= qpos, _float_to_ordered_int(acc), int_min)
        return carry

    lax.fori_loop(0, n_tiles, score_tile, 0)

    n_chunks = n_rows // COUNT_CHUNK

    def count(pred, with_pos=False):
        def body(c, accs):
            start = pl.multiple_of(c * COUNT_CHUNK, COUNT_CHUNK)
            accs = list(accs)
            chunk = key_ref[pl.ds(start, COUNT_CHUNK), :]
            for r in range(COUNT_CHUNK // 8):
                blk = chunk[8 * r:8 * r + 8, :]
                if with_pos:
                    pos = start + 8 * r + lax.broadcasted_iota(jnp.int32, (8, 1), 0)
                    hit = pred(blk, pos)
                else:
                    hit = pred(blk)
                accs[r % COUNT_ACCS] = accs[r % COUNT_ACCS] + jnp.where(hit, 1, 0).astype(jnp.int32)
            return tuple(accs)
        zero = jnp.zeros((8, tq), jnp.int32)
        accs = lax.fori_loop(0, n_chunks, body, (zero,) * COUNT_ACCS)
        return jnp.sum(functools.reduce(lambda a, b: a + b, accs), axis=0, keepdims=True)

    assert topk <= COUNT_CHUNK
    slot_ref[...] = key_ref[0:COUNT_CHUNK, :]

    def fold_slots(c, carry):
        start = pl.multiple_of(c * COUNT_CHUNK, COUNT_CHUNK)
        slot_ref[...] = jnp.maximum(slot_ref[...], key_ref[pl.ds(start, COUNT_CHUNK), :])
        return carry

    lax.fori_loop(1, n_chunks, fold_slots, 0)
    slots = slot_ref[...]
    hi = jnp.max(slots, axis=0, keepdims=True)
    lo = jnp.min(slots, axis=0, keepdims=True)
    n_bits = 32 - jnp.min(lax.clz(hi - lo))

    def bit_step(b, tau):
        cand = tau + jnp.left_shift(jnp.int32(1), n_bits - 1 - b)
        cnt = count(lambda blk: blk >= cand)
        return jnp.where((cnt >= topk) & (cand > tau), cand, tau)

    tau = lax.fori_loop(0, n_bits, bit_step, lo)

    n_gt = count(lambda blk: blk > tau)
    n_eq = count(lambda blk: blk == tau)
    need = topk - n_gt
    cut_bits = 14
    cut_ref[...] = jnp.full((1, tq), 1 << cut_bits, jnp.int32)

    @pl.when(jnp.max(n_eq - need) > 0)
    def _():
        def cut_step(b, cut):
            cand = cut + jnp.left_shift(jnp.int32(1), cut_bits - 1 - b)
            cnt = count(lambda blk, pos: (blk == tau) & (pos < cand), with_pos=True)
            return jnp.where(cnt <= need, cand, cut)
        cut_ref[...] = lax.fori_loop(0, cut_bits, cut_step, jnp.zeros((1, tq), jnp.int32))

    cut = cut_ref[...]

    def write_sel(c, carry):
        start = pl.multiple_of(c * WRITE_CHUNK, WRITE_CHUNK)
        blk = key_ref[pl.ds(start, WRITE_CHUNK), :]
        pos = start + lax.broadcasted_iota(jnp.int32, (WRITE_CHUNK, 1), 0)
        sel = ((blk > tau) | ((blk == tau) & (pos < cut))) & (pos <= qpos)
        bias_ref[pl.ds(start, WRITE_CHUNK), :] = jnp.where(sel, 0.0, NEG).astype(bias_ref.dtype)
        return carry

    lax.fori_loop(0, n_rows // WRITE_CHUNK, write_sel, 0)

    def write_neg(c, carry):
        start = pl.multiple_of(c * WRITE_CHUNK, WRITE_CHUNK)
        bias_ref[pl.ds(start, WRITE_CHUNK), :] = jnp.full((WRITE_CHUNK, tq), NEG, bias_ref.dtype)
        return carry

    lax.fori_loop(n_rows // WRITE_CHUNK, total_rows // WRITE_CHUNK, write_neg, 0)


def _indexer(iq_r, ik_ab, wt, topk):
    m = iq_r.shape[0]
    return pl.pallas_call(
        functools.partial(_indexer_kernel, topk=topk),
        grid=(m // Q_TILE,),
        in_specs=[pl.BlockSpec((Q_TILE, BRANCH), lambda i: (i, 0)),
                  pl.BlockSpec((m, 2 * LANES), lambda i: (0, 0)),
                  pl.BlockSpec((LANES, Q_TILE), lambda i: (0, i))],
        out_specs=pl.BlockSpec((m, Q_TILE), lambda i: (0, i)),
        out_shape=jax.ShapeDtypeStruct((m, m), BF16),
        scratch_shapes=[pltpu.VMEM((m, Q_TILE), jnp.int32),
                        pltpu.VMEM((1, Q_TILE), jnp.int32),
                        pltpu.VMEM((COUNT_CHUNK, Q_TILE), jnp.int32)],
        compiler_params=_cp("parallel"),
        name="dsa_indexer",
    )(iq_r, ik_ab, wt)


def _flash_kernel(*refs, q_axis, has_kpe, has_bias):
    refs = list(refs)
    q_ref, k_ref = refs[0], refs[1]
    pos = 2
    kpe_ref = bias_ref = None
    if has_kpe:
        kpe_ref = refs[pos]
        pos += 1
    vt_ref = refs[pos]
    pos += 1
    if has_bias:
        bias_ref = refs[pos]
        pos += 1
    o_ref, sa_ref, sb_ref = refs[pos], refs[pos + 1], refs[pos + 2]

    tq = q_ref.shape[1]
    heads = vt_ref.shape[1] // V_ROWS
    tk = vt_ref.shape[2]
    dq = q_ref.shape[0] // heads
    assert tk % tq == 0
    i = pl.program_id(q_axis)
    n_tiles = (i * tq + tq + tk - 1) // tk
    qpos = i * tq + lax.broadcasted_iota(jnp.int32, (1, tq), 1)

    def compute_scores(kt, s_ref):
        start = pl.multiple_of(kt * tk, tk)
        kpe = kpe_ref[pl.ds(start, tk), :] if has_kpe else None
        for h in range(heads):
            k = k_ref[pl.ds(start, tk), h * LANES:(h + 1) * LANES]
            if has_kpe:
                k = jnp.concatenate([k, kpe], axis=1)
            s_ref[h] = jnp.dot(k, q_ref[h * dq:(h + 1) * dq, :], preferred_element_type=F32)

    def consume_scores(kt, s_ref, carry, causal_mask):
        start = pl.multiple_of(kt * tk, tk)
        bias = bias_ref[pl.ds(start, tk), :].astype(F32) if has_bias else None
        new = []
        for h in range(heads):
            m_run, acc = carry[h]
            s = s_ref[h]
            if has_bias:
                s = s + bias
            if causal_mask:
                kpos = start + lax.broadcasted_iota(jnp.int32, (tk, 1), 0)
                s = jnp.where(kpos <= qpos, s, NEG)
            m_new = jnp.maximum(m_run, jnp.max(s, axis=0, keepdims=True))
            alpha = jnp.exp2(m_run - m_new)
            p = jnp.exp2(s - m_new).astype(BF16)
            pv = jnp.dot(vt_ref[kt, h * V_ROWS:(h + 1) * V_ROWS, :], p, preferred_element_type=F32)
            new.append((m_new, alpha * acc + pv))
        return tuple(new)

    def double_step(u, carry):
        kt = 2 * u
        compute_scores(kt + 1, sb_ref)
        carry = consume_scores(kt, sa_ref, carry, False)
        compute_scores(kt + 2, sa_ref)
        return consume_scores(kt + 1, sb_ref, carry, False)

    mask_last = not has_bias
    last = n_tiles - 1

    def odd_tail(carry):
        compute_scores(last, sb_ref)
        carry = consume_scores(last - 1, sa_ref, carry, False)
        return consume_scores(last, sb_ref, carry, mask_last)

    def even_tail(carry):
        return consume_scores(last, sa_ref, carry, mask_last)

    init = tuple((jnp.full((1, tq), NEG, F32), jnp.zeros((V_ROWS, tq), F32)) for _ in range(heads))
    compute_scores(0, sa_ref)
    carry = lax.fori_loop(0, last // 2, double_step, init)
    carry = lax.cond(last % 2 == 1, odd_tail, even_tail, carry)
    for h in range(heads):
        acc = carry[h][1]
        out = acc[0:LANES, :] / acc[LANES:LANES + 1, :]
        o_ref[:, h * LANES:(h + 1) * LANES] = out.T.astype(o_ref.dtype)


def _score_scratch(heads, tk):
    return [pltpu.VMEM((heads, tk, Q_TILE), F32), pltpu.VMEM((heads, tk, Q_TILE), F32)]


def _mla_attention(q, kn, kpe, vt):
    m = kn.shape[0]
    n_kt, _, tk = vt.shape
    hp = HEADS_PER_STEP
    return pl.pallas_call(
        functools.partial(_flash_kernel, q_axis=1, has_kpe=True, has_bias=False),
        grid=(MLA_HEADS // hp, m // Q_TILE),
        in_specs=[pl.BlockSpec((hp * 2 * LANES, Q_TILE), lambda h, i: (h, i)),
                  pl.BlockSpec((m, hp * LANES), lambda h, i: (0, h)),
                  pl.BlockSpec((m, LANES), lambda h, i: (0, 0)),
                  pl.BlockSpec((n_kt, hp * V_ROWS, tk), lambda h, i: (0, h, 0))],
        out_specs=pl.BlockSpec((Q_TILE, hp * MLA_V), lambda h, i: (i, h)),
        out_shape=jax.ShapeDtypeStruct((m, BRANCH), BF16),
        scratch_shapes=_score_scratch(hp, tk),
        compiler_params=_cp("parallel", "parallel"),
        name="mla_attention",
    )(q, kn, kpe, vt)


def _dsa_attention(q, k, vt, bias):
    m = k.shape[0]
    n_kt, _, tk = vt.shape
    hp = HEADS_PER_STEP
    return pl.pallas_call(
        functools.partial(_flash_kernel, q_axis=1, has_kpe=False, has_bias=True),
        grid=(DSA_HEADS // hp, m // Q_TILE),
        in_specs=[pl.BlockSpec((hp * DSA_DIM, Q_TILE), lambda h, i: (h, i)),
                  pl.BlockSpec((m, hp * DSA_DIM), lambda h, i: (0, h)),
                  pl.BlockSpec((n_kt, hp * V_ROWS, tk), lambda h, i: (0, h, 0)),
                  pl.BlockSpec((m, Q_TILE), lambda h, i: (0, i))],
        out_specs=pl.BlockSpec((Q_TILE, hp * DSA_DIM), lambda h, i: (i, h)),
        out_shape=jax.ShapeDtypeStruct((m, BRANCH), BF16),
        scratch_shapes=_score_scratch(hp, tk),
        compiler_params=_cp("parallel", "parallel"),
        name="dsa_attention",
    )(q, k, vt, bias)


def _rope_tables(rows, dim):
    inv = 1.0 / jnp.power(ROPE_THETA, jnp.arange(0, dim, 2, dtype=F32) / dim)
    ang = jnp.arange(rows, dtype=F32)[:, None] * inv[None, :]
    cos, sin = jnp.cos(ang), jnp.sin(ang)
    reps = LANES // dim
    return (jnp.tile(jnp.concatenate([cos, cos], axis=1), (1, reps)),
            jnp.tile(jnp.concatenate([-sin, sin], axis=1), (1, reps)))


def _regroup_w_in(w):
    o_kr = 3 * BRANCH + MLA_Q_RANK + MLA_KV_RANK
    o_dq = o_kr + MLA_ROPE
    o_ik = o_dq + 4 * BRANCH
    o_iw = o_ik + IDX_DIM
    o_pu = o_iw + IDX_HEADS
    assert o_kr == HALF_COLS
    assert o_dq - o_kr == MLA_ROPE and o_ik - o_dq == 4 * BRANCH
    pool = w[:, :, o_pu:]
    zeros = lambda n: jnp.zeros(w.shape[:2] + (n,), w.dtype)
    small = jnp.concatenate([w[:, :, o_kr:o_dq], zeros(LANES - MLA_ROPE),
                             w[:, :, o_ik:o_iw], w[:, :, o_iw:o_pu],
                             zeros(LANES - IDX_DIM - IDX_HEADS)], axis=2)
    return pool, small


def _layout_w_uq(w):
    w3 = w.reshape(MLA_Q_RANK, MLA_HEADS, MLA_NOPE + MLA_ROPE)
    w3 = jnp.pad(w3, ((0, 0), (0, 0), (0, 2 * LANES - MLA_NOPE - MLA_ROPE)))
    return w3.reshape(MLA_Q_RANK, MLA_HEADS * 2 * LANES).astype(BF16)


def _layout_w_ukv(w):
    w4 = w.reshape(MLA_KV_RANK, MLA_HEADS, 2, MLA_NOPE)
    return w4.transpose(0, 2, 1, 3).reshape(MLA_KV_RANK, 2 * BRANCH).astype(BF16)


def _forward(x, meta_tokens, norm_mix_pre, norm_mix_post, norm_ffn_pre, norm_ffn_post,
             w_in, conv_w, mla_q_norm, mla_w_uq, mla_kv_norm, mla_w_ukv, pool_w, pool_scale,
             w_branch, w_gate, b_gate, w_out, ffn_w_gate, ffn_w_up, ffn_w_down):
    assert x.shape[0] == 1 and x.shape[2] == D_MODEL
    depth = w_in.shape[0]
    seq = x.shape[1]
    length = N_META + seq
    topk = min(IDX_TOPK_MAX, length // 4)
    rows = -(-length // ROW_TILE) * ROW_TILE
    assert rows % Q_TILE == 0 and rows % K_TILE == 0 and rows % NORM_TILE == 0
    assert rows < (1 << 14)

    c64, s64 = _rope_tables(rows, 64)
    c128, s128 = _rope_tables(rows, 128)
    w_in_pool, w_in_small = _regroup_w_in(w_in)
    w_gate_flat = w_gate.reshape(depth * N_BRANCH, D_MODEL, D_MODEL)
    b_gate_flat = b_gate.reshape(depth * N_BRANCH, 1, D_MODEL)
    tn = 512
    wide = 2 * tn
    gate_tiles = D_MODEL // wide
    big_tm = BIG_ROW_TILE if rows % BIG_ROW_TILE == 0 else ROW_TILE

    h, xn = _assemble_prenorm(x[0], meta_tokens.astype(F32), norm_mix_pre[0][None], rows)
    for l in range(depth):
        at_layer = lambda j, l=l: l
        tile = lambda j: j
        proj_a = _wide_proj(xn, w_in, at_layer, tile, HALF_COLS // wide, big_tm, tn, BF16, "in_proj_a")
        proj_b = _wide_proj(xn, w_in, at_layer, tile, 4 * BRANCH // wide, big_tm, tn, BF16,
                            "in_proj_b", first_half_block=HALF_COLS // tn, lane_shift=MLA_ROPE)
        proj_p = _wide_proj(xn, w_in_pool, at_layer, tile, BRANCH // wide, big_tm, tn, BF16,
                            "in_proj_pool")
        small = _stacked_proj(xn, w_in_small, at_layer, tile, 1, ROW_TILE, SMALL_COLS, F32,
                              "in_proj_small")
        gates = _wide_proj(xn, w_gate_flat,
                           lambda j, l=l: l * N_BRANCH + j // gate_tiles, lambda j: j % gate_tiles,
                           N_BRANCH * gate_tiles, big_tm, tn, BF16, "gates", bias=b_gate_flat)

        y_conv = _conv_branch(proj_a, conv_w, l, ROW_TILE)
        y_pool = _pool_branch(proj_p, pool_w, pool_scale[:, None, :], l, ROW_TILE)

        q_m, kn_m, kpe_m, vt_m = _mla_prep(
            proj_a, small, mla_q_norm[:, None, :], mla_kv_norm[:, None, :],
            _layout_w_uq(mla_w_uq[l]), _layout_w_ukv(mla_w_ukv[l]), c64, s64, l, K_TILE)
        y_mla = _mla_attention(q_m, kn_m, kpe_m, vt_m)

        q_d, k_d, vt_d, iq_r, ik_ab, wt = _dsa_prep(proj_b, small, c128, s128, c64, s64, K_TILE)
        bias = _indexer(iq_r, ik_ab, wt, topk)
        y_dsa = _dsa_attention(q_d, k_d, vt_d, bias)

        merged = _gated_merge((y_conv, y_mla, y_dsa, y_pool), gates, w_branch, l, ROW_TILE, tn)
        mix = _wide_proj(merged, w_out, at_layer, tile, D_MODEL // wide, ROW_TILE, tn, F32, "out_proj")
        h, xn = _resid_norm(h, mix, norm_mix_post[l][None], norm_ffn_pre[l][None])

        act = _swiglu(xn, ffn_w_gate, ffn_w_up, l, ROW_TILE, 256)
        f = _stacked_proj(act, ffn_w_down, at_layer, tile, D_MODEL // tn, 384, tn, F32, "ffn_down",
                          single_buffer_weight=True)
        if l + 1 < depth:
            h, xn = _resid_norm(h, f, norm_ffn_post[l][None], norm_mix_pre[l + 1][None])
        else:
            out = _resid_out(h, f, norm_ffn_post[l][None], seq)

    return out[None]


def kernel(x, meta_tokens, norm_mix_pre, norm_mix_post, norm_ffn_pre, norm_ffn_post, w_in, conv_w, mla_q_norm, mla_w_uq, mla_kv_norm, mla_w_ukv, pool_w, pool_scale, w_branch, w_gate, b_gate, w_out, ffn_w_gate, ffn_w_up, ffn_w_down):
    return _forward(x, meta_tokens, norm_mix_pre, norm_mix_post, norm_ffn_pre, norm_ffn_post,
                    w_in, conv_w, mla_q_norm, mla_w_uq, mla_kv_norm, mla_w_ukv, pool_w, pool_scale,
                    w_branch, w_gate, b_gate, w_out, ffn_w_gate, ffn_w_up, ffn_w_down)
```

```python
import functools
import math

import jax
import jax.numpy as jnp
from jax import lax
from jax.experimental import pallas as pl
from jax.experimental.pallas import tpu as pltpu

F32 = jnp.float32
BF16 = jnp.bfloat16

D_MODEL = 4096
N_META = 16
ROPE_THETA = 10000.0
EPS = 1e-6
N_BRANCH = 4
BRANCH = 1024
CONV_K = 3
MLA_NOPE, MLA_ROPE, MLA_V, MLA_HEADS = 128, 64, 128, 8
MLA_Q_RANK, MLA_KV_RANK = 1536, 512
DSA_DIM, DSA_HEADS = 128, 8
IDX_HEADS, IDX_DIM, IDX_TOPK_MAX = 16, 64, 256
POOL_WINDOWS = (2, 4, 8, 16)
POOL_GROUP = 256
D_FF = 11008

LANES = 128
HALO = 16
ROW_TILE = 768
Q_TILE = 256
K_TILE = 768
V_ROWS = 144
HEADS_PER_STEP = 4
IDX_K_TILE = 256
CAST_ROWS = 256
COUNT_CHUNK = 256
COUNT_ACCS = 4
WRITE_CHUNK = 64
NORM_TILE = 192
NEG = -1e30
LOG2E = math.log2(math.e)
VMEM_LIMIT = 58 * 1024 * 1024

HALF_COLS = 5120
SMALL_COLS = 256


def _cp(*sem):
    return pltpu.CompilerParams(dimension_semantics=sem, vmem_limit_bytes=VMEM_LIMIT)


def _sigmoid(x):
    return 1.0 / (1.0 + jnp.exp(-x))


def _rms(x, g):
    return x * lax.rsqrt(jnp.mean(x * x, axis=-1, keepdims=True) + EPS) * g


def _cast_weights_once(w_refs, wbf_refs):
    @pl.when(pl.program_id(1) == 0)
    def _():
        for w_ref, wbf_ref in zip(w_refs, wbf_refs):
            wbf_ref[...] = w_ref[...].astype(wbf_ref.dtype)


def _proj_kernel(a_ref, w_ref, o_ref, wbf_ref):
    _cast_weights_once((w_ref,), (wbf_ref,))
    o_ref[...] = jnp.dot(a_ref[...], wbf_ref[...], preferred_element_type=F32).astype(o_ref.dtype)


def _gate_kernel(a_ref, w_ref, b_ref, o_ref, wbf_ref):
    _cast_weights_once((w_ref,), (wbf_ref,))
    z = jnp.dot(a_ref[...], wbf_ref[...], preferred_element_type=F32) + b_ref[...]
    o_ref[...] = _sigmoid(z).astype(o_ref.dtype)


def _stacked_proj(a, w, lead_of, col_of, n_tiles, tm, tn, out_dtype, name, bias=None,
                  single_buffer_weight=False):
    m, k = a.shape
    assert w.shape[-2] == k
    lead = (None,) * (w.ndim - 2)
    w_kwargs = {"pipeline_mode": pl.Buffered(1)} if single_buffer_weight else {}
    in_specs = [pl.BlockSpec((tm, k), lambda j, i: (i, 0)),
                pl.BlockSpec(lead + (k, tn), lambda j, i: (*lead_of(j), 0, col_of(j)), **w_kwargs)]
    args = [a, w]
    body = _proj_kernel
    if bias is not None:
        in_specs.append(pl.BlockSpec(lead + (1, tn), lambda j, i: (*lead_of(j), 0, col_of(j))))
        args.append(bias)
        body = _gate_kernel
    return pl.pallas_call(
        body,
        grid=(n_tiles, m // tm),
        in_specs=in_specs,
        out_specs=pl.BlockSpec((tm, tn), lambda j, i: (i, j)),
        out_shape=jax.ShapeDtypeStruct((m, n_tiles * tn), out_dtype),
        scratch_shapes=[pltpu.VMEM((k, tn), BF16)],
        compiler_params=_cp("parallel", "arbitrary"),
        name=name,
    )(*args)


def _cast_shifted(w_ref, next_ref, wbf_ref, dst_lo, lane_shift):
    k, half = w_ref.shape
    back = LANES - lane_shift

    def rows(c, carry):
        r0 = pl.multiple_of(c * CAST_ROWS, CAST_ROWS)
        lane = lax.broadcasted_iota(jnp.int32, (CAST_ROWS, LANES), 1)
        for t in range(half // LANES):
            cur = w_ref[pl.ds(r0, CAST_ROWS), t * LANES:(t + 1) * LANES]
            if (t + 1) * LANES < half:
                nxt = w_ref[pl.ds(r0, CAST_ROWS), (t + 1) * LANES:(t + 2) * LANES]
            else:
                nxt = next_ref[pl.ds(r0, CAST_ROWS), :]
            blk = jnp.where(lane < back, pltpu.roll(cur, back, axis=1), pltpu.roll(nxt, back, axis=1))
            wbf_ref[pl.ds(r0, CAST_ROWS), dst_lo + t * LANES:dst_lo + (t + 1) * LANES] = (
                blk.astype(wbf_ref.dtype))
        return carry

    lax.fori_loop(0, k // CAST_ROWS, rows, 0)


def _wide_proj_kernel(*refs, has_bias, lane_shift):
    refs = list(refs)
    a_ref, w_ref = refs[0], refs[1]
    pos = 2
    next_ref = b_ref = None
    if lane_shift:
        next_ref = refs[pos]
        pos += 1
    if has_bias:
        b_ref = refs[pos]
        pos += 1
    o_ref, wbf_ref = refs[pos], refs[pos + 1]
    half = w_ref.shape[1]
    i = pl.program_id(1)

    def cast_half(dst_lo):
        if lane_shift:
            _cast_shifted(w_ref, next_ref, wbf_ref, dst_lo, lane_shift)
        else:
            wbf_ref[:, dst_lo:dst_lo + half] = w_ref[...].astype(wbf_ref.dtype)

    def emit(lo, width):
        z = jnp.dot(a_ref[...], wbf_ref[:, lo:lo + width], preferred_element_type=F32)
        if has_bias:
            z = _sigmoid(z + b_ref[:, lo:lo + width])
        o_ref[:, lo:lo + width] = z.astype(o_ref.dtype)

    @pl.when(i == 0)
    def _():
        cast_half(0)
        emit(0, half)

    @pl.when(i == 1)
    def _():
        cast_half(half)
        emit(half, half)

    @pl.when(i >= 2)
    def _():
        emit(0, 2 * half)


def _wide_proj(a, w, lead_of, col_of, n_tiles, tm, half, out_dtype, name, bias=None,
               first_half_block=0, lane_shift=0):
    m, k = a.shape
    assert w.shape[-2] == k and 0 <= lane_shift < LANES
    lead = (None,) * (w.ndim - 2)
    row = lambda i: jnp.maximum(i - 1, 0)

    def weight_window(j, i):
        jw = jnp.where(i >= 2, jnp.minimum(j + 1, n_tiles - 1), j)
        return lead_of(jw), first_half_block + 2 * col_of(jw) + jnp.where(i == 1, 1, 0)

    def w_index(j, i):
        stacked, hb = weight_window(j, i)
        return (*stacked, 0, hb)

    in_specs = [pl.BlockSpec((tm, k), lambda j, i: (row(i), 0)),
                pl.BlockSpec(lead + (k, half), w_index)]
    args = [a, w]
    if lane_shift:
        lanes_per_half = half // LANES

        def next_index(j, i):
            stacked, hb = weight_window(j, i)
            return (*stacked, 0, (hb + 1) * lanes_per_half)

        in_specs.append(pl.BlockSpec(lead + (k, LANES), next_index))
        args.append(w)
    if bias is not None:
        in_specs.append(pl.BlockSpec(lead + (1, 2 * half), lambda j, i: (*lead_of(j), 0, col_of(j))))
        args.append(bias)
    return pl.pallas_call(
        functools.partial(_wide_proj_kernel, has_bias=bias is not None, lane_shift=lane_shift),
        grid=(n_tiles, m // tm + 1),
        in_specs=in_specs,
        out_specs=pl.BlockSpec((tm, 2 * half), lambda j, i: (row(i), j)),
        out_shape=jax.ShapeDtypeStruct((m, n_tiles * 2 * half), out_dtype),
        scratch_shapes=[pltpu.VMEM((k, 2 * half), BF16)],
        compiler_params=_cp("parallel", "arbitrary"),
        name=name,
    )(*args)


def _swiglu_kernel(a_ref, wg_ref, wu_ref, o_ref, wgbf_ref, wubf_ref, *, n_half_blocks):
    half = wg_ref.shape[1]
    j = pl.program_id(0)
    i = pl.program_id(1)

    def cast_half(lo):
        wgbf_ref[:, lo:lo + half] = wg_ref[...].astype(wgbf_ref.dtype)
        wubf_ref[:, lo:lo + half] = wu_ref[...].astype(wubf_ref.dtype)

    def emit(lo, width):
        a = a_ref[...]
        g = jnp.dot(a, wgbf_ref[:, lo:lo + width], preferred_element_type=F32)
        u = jnp.dot(a, wubf_ref[:, lo:lo + width], preferred_element_type=F32)
        o_ref[:, lo:lo + width] = (g * _sigmoid(g) * u).astype(o_ref.dtype)

    has_right = 2 * j + 1 < n_half_blocks

    @pl.when(i == 0)
    def _():
        cast_half(0)
        emit(0, half)

    @pl.when((i == 1) & has_right)
    def _():
        cast_half(half)
        emit(half, half)

    @pl.when((i >= 2) & has_right)
    def _():
        emit(0, 2 * half)

    @pl.when((i >= 2) & jnp.logical_not(has_right))
    def _():
        emit(0, half)


def _swiglu(a, wg, wu, layer, tm, half):
    m, k = a.shape
    n = wg.shape[2]
    n_half_blocks = n // half
    assert n_half_blocks * half == n
    n_tiles = -(-n_half_blocks // 2)
    row = lambda i: jnp.maximum(i - 1, 0)
    def w_index(j, i):
        jw = jnp.where(i >= 2, jnp.minimum(j + 1, n_tiles - 1), j)
        return (layer, 0, jnp.minimum(2 * jw + jnp.where(i == 1, 1, 0), n_half_blocks - 1))

    w_spec = pl.BlockSpec((None, k, half), w_index)
    return pl.pallas_call(
        functools.partial(_swiglu_kernel, n_half_blocks=n_half_blocks),
        grid=(n_tiles, m // tm + 1),
        in_specs=[pl.BlockSpec((tm, k), lambda j, i: (row(i), 0)), w_spec, w_spec],
        out_specs=pl.BlockSpec((tm, 2 * half), lambda j, i: (row(i), j)),
        out_shape=jax.ShapeDtypeStruct((m, n), BF16),
        scratch_shapes=[pltpu.VMEM((k, 2 * half), BF16), pltpu.VMEM((k, 2 * half), BF16)],
        compiler_params=_cp("parallel", "arbitrary"),
        name="ffn_swiglu",
    )(a, wg, wu)


def _merge_kernel(y0_ref, y1_ref, y2_ref, y3_ref, g0_ref, g1_ref, g2_ref, g3_ref, w_ref,
                  o_ref, wbf_ref):
    _cast_weights_once((w_ref,), (wbf_ref,))
    acc = None
    for b, (y_ref, g_ref) in enumerate(((y0_ref, g0_ref), (y1_ref, g1_ref),
                                        (y2_ref, g2_ref), (y3_ref, g3_ref))):
        val = g_ref[...].astype(F32) * jnp.dot(y_ref[...], wbf_ref[b], preferred_element_type=F32)
        acc = val if acc is None else acc + val
    o_ref[...] = acc.astype(o_ref.dtype)


def _gated_merge(ys, gates, w_branch, layer, tm, tn):
    m = ys[0].shape[0]
    nj = D_MODEL // tn
    y_spec = pl.BlockSpec((tm, BRANCH), lambda j, i: (i, 0))
    g_specs = [pl.BlockSpec((tm, tn), functools.partial(lambda j, i, b: (i, b * nj + j), b=b))
               for b in range(N_BRANCH)]
    return pl.pallas_call(
        _merge_kernel,
        grid=(nj, m // tm),
        in_specs=[y_spec] * N_BRANCH + g_specs
                 + [pl.BlockSpec((None, N_BRANCH, BRANCH, tn), lambda j, i: (layer, 0, 0, j))],
        out_specs=pl.BlockSpec((tm, tn), lambda j, i: (i, j)),
        out_shape=jax.ShapeDtypeStruct((m, D_MODEL), BF16),
        scratch_shapes=[pltpu.VMEM((N_BRANCH, BRANCH, tn), BF16)],
        compiler_params=_cp("parallel", "arbitrary"),
        name="gated_merge",
    )(*ys, gates, gates, gates, gates, w_branch)


def _assemble_kernel(x_ref, prev_ref, meta_ref, g_ref, h_ref, xn_ref, *, length):
    tr = x_ref.shape[0]
    i = pl.program_id(0)
    head = jnp.where(i == 0, meta_ref[...], prev_ref[...])
    tile = jnp.concatenate([head, x_ref[0:tr - N_META, :]], axis=0)
    pos = i * tr + lax.broadcasted_iota(jnp.int32, (tr, 1), 0)
    h = jnp.where(pos < length, tile, 0.0)
    h_ref[...] = h
    xn_ref[...] = _rms(h, g_ref[...]).astype(xn_ref.dtype)


def _assemble_prenorm(x2d, meta, g, rows):
    seq = x2d.shape[0]
    assert N_META == HALO and NORM_TILE % N_META == 0
    last_x = (seq - 1) // NORM_TILE
    last_prev = (seq - 1) // N_META
    per_tile = NORM_TILE // N_META
    row = pl.BlockSpec((NORM_TILE, D_MODEL), lambda i: (i, 0))
    return pl.pallas_call(
        functools.partial(_assemble_kernel, length=N_META + seq),
        grid=(rows // NORM_TILE,),
        in_specs=[pl.BlockSpec((NORM_TILE, D_MODEL), lambda i: (jnp.minimum(i, last_x), 0)),
                  pl.BlockSpec((N_META, D_MODEL),
                               lambda i: (jnp.clip(i * per_tile - 1, 0, last_prev), 0)),
                  pl.BlockSpec((N_META, D_MODEL), lambda i: (0, 0)),
                  pl.BlockSpec((1, D_MODEL), lambda i: (0, 0))],
        out_specs=[row, row],
        out_shape=[jax.ShapeDtypeStruct((rows, D_MODEL), F32),
                   jax.ShapeDtypeStruct((rows, D_MODEL), BF16)],
        compiler_params=_cp("parallel"),
        name="assemble_prenorm",
    )(x2d, x2d, meta, g)


def _resid_norm_kernel(h_ref, o_ref, gpost_ref, gnext_ref, hn_ref, xn_ref):
    hn = h_ref[...] + _rms(o_ref[...], gpost_ref[...])
    hn_ref[...] = hn
    xn_ref[...] = _rms(hn, gnext_ref[...]).astype(xn_ref.dtype)


def _resid_out_kernel(h_ref, o_ref, hnext_ref, onext_ref, gpost_ref, out_ref):
    g = gpost_ref[...]
    cur = h_ref[N_META:, :] + _rms(o_ref[N_META:, :], g)
    nxt = hnext_ref[...] + _rms(onext_ref[...], g)
    out_ref[...] = jnp.concatenate([cur, nxt], axis=0)


def _resid_norm(h, o, g_post, g_next):
    m = h.shape[0]
    row = pl.BlockSpec((NORM_TILE, D_MODEL), lambda i: (i, 0))
    gain = pl.BlockSpec((1, D_MODEL), lambda i: (0, 0))
    return pl.pallas_call(
        _resid_norm_kernel,
        grid=(m // NORM_TILE,),
        in_specs=[row, row, gain, gain],
        out_specs=[row, row],
        out_shape=[jax.ShapeDtypeStruct((m, D_MODEL), F32),
                   jax.ShapeDtypeStruct((m, D_MODEL), BF16)],
        compiler_params=_cp("parallel"),
        name="resid_norm",
    )(h, o, g_post, g_next)


def _resid_out(h, o, g_post, seq):
    m = h.shape[0]
    per_tile = NORM_TILE // N_META
    last_head = m // N_META - 1
    row = pl.BlockSpec((NORM_TILE, D_MODEL), lambda i: (i, 0))
    head = pl.BlockSpec((N_META, D_MODEL), lambda i: (jnp.minimum((i + 1) * per_tile, last_head), 0))
    return pl.pallas_call(
        _resid_out_kernel,
        grid=(-(-seq // NORM_TILE),),
        in_specs=[row, row, head, head, pl.BlockSpec((1, D_MODEL), lambda i: (0, 0))],
        out_specs=row,
        out_shape=jax.ShapeDtypeStruct((seq, D_MODEL), F32),
        compiler_params=_cp("parallel"),
        name="resid_out",
    )(h, o, h, o, g_post)


def _rope_half128(x, cos, sin_signed):
    return x * cos + pltpu.roll(x, 64, axis=1) * sin_signed


def _rope_half64(x, cos, sin_signed):
    lane = lax.broadcasted_iota(jnp.int32, x.shape, 1)
    partner = jnp.where((lane & 32) == 0, pltpu.roll(x, 96, axis=1), pltpu.roll(x, 32, axis=1))
    return x * cos + partner * sin_signed


def _conv_kernel(cb_ref, cc_ref, cu_ref, pc_ref, pu_ref, w_ref, y_ref, z_ref):
    tr = cb_ref.shape[0]
    i = pl.program_id(0)
    z = cc_ref[...].astype(F32) * cu_ref[...].astype(F32)
    zp = pc_ref[...].astype(F32) * pu_ref[...].astype(F32)
    z_ref[0:HALO, :] = jnp.where(i > 0, zp, 0.0)
    z_ref[HALO:HALO + tr, :] = z
    w = w_ref[...]
    conv = (w[2:3, :] * z
            + w[1:2, :] * z_ref[HALO - 1:HALO - 1 + tr, :]
            + w[0:1, :] * z_ref[HALO - 2:HALO - 2 + tr, :])
    y_ref[...] = (cb_ref[...].astype(F32) * conv).astype(y_ref.dtype)


def _conv_branch(proj, conv_w, layer, tr):
    m = proj.shape[0]
    cw = 256
    nb = BRANCH // cw

    def halo_row(i):
        return jnp.maximum(i * (tr // HALO) - 1, 0)

    return pl.pallas_call(
        _conv_kernel,
        grid=(m // tr, nb),
        in_specs=[pl.BlockSpec((tr, cw), lambda i, c: (i, c)),
                  pl.BlockSpec((tr, cw), lambda i, c: (i, nb + c)),
                  pl.BlockSpec((tr, cw), lambda i, c: (i, 2 * nb + c)),
                  pl.BlockSpec((HALO, cw), lambda i, c: (halo_row(i), nb + c)),
                  pl.BlockSpec((HALO, cw), lambda i, c: (halo_row(i), 2 * nb + c)),
                  pl.BlockSpec((None, CONV_K, cw), lambda i, c: (layer, 0, c))],
        out_specs=pl.BlockSpec((tr, cw), lambda i, c: (i, c)),
        out_shape=jax.ShapeDtypeStruct((m, BRANCH), BF16),
        scratch_shapes=[pltpu.VMEM((HALO + tr, cw), F32)],
        compiler_params=_cp("parallel", "parallel"),
        name="conv_branch",
    )(proj, proj, proj, proj, proj, conv_w)


def _pool_kernel(u_ref, pu_ref, w_ref, s_ref, y_ref, x_ref):
    tr = u_ref.shape[0]
    i = pl.program_id(0)
    g = pl.program_id(1)
    win = jnp.left_shift(jnp.int32(2), g)
    x = u_ref[...].astype(F32)
    x_ref[0:HALO, :] = jnp.where(i > 0, pu_ref[...].astype(F32), 0.0)
    x_ref[HALO:HALO + tr, :] = x
    total = x
    for j in range(1, max(POOL_WINDOWS)):
        keep = jnp.where(j < win, 1.0, 0.0).astype(F32)
        total = total + keep * x_ref[HALO - j:HALO - j + tr, :]
    t = i * tr + lax.broadcasted_iota(jnp.int32, x.shape, 0)
    count = jnp.minimum(t + 1, win).astype(F32)
    pooled = (total / count - x).astype(BF16)
    mixed = jnp.dot(pooled, w_ref[...].astype(BF16), preferred_element_type=F32)
    y_ref[...] = (mixed * s_ref[...]).astype(y_ref.dtype)


def _pool_branch(proj, pool_w, pool_scale, layer, tr):
    m = proj.shape[0]
    base = 0
    ng = len(POOL_WINDOWS)

    def halo_row(i):
        return jnp.maximum(i * (tr // HALO) - 1, 0)

    return pl.pallas_call(
        _pool_kernel,
        grid=(m // tr, ng),
        in_specs=[pl.BlockSpec((tr, POOL_GROUP), lambda i, g: (i, base + g)),
                  pl.BlockSpec((HALO, POOL_GROUP), lambda i, g: (halo_row(i), base + g)),
                  pl.BlockSpec((None, None, POOL_GROUP, POOL_GROUP), lambda i, g: (layer, g, 0, 0)),
                  pl.BlockSpec((None, 1, POOL_GROUP), lambda i, g: (layer, 0, g))],
        out_specs=pl.BlockSpec((tr, POOL_GROUP), lambda i, g: (i, g)),
        out_shape=jax.ShapeDtypeStruct((m, BRANCH), BF16),
        scratch_shapes=[pltpu.VMEM((HALO + tr, POOL_GROUP), F32)],
        compiler_params=_cp("parallel", "parallel"),
        name="pool_branch",
    )(proj, proj, pool_w, pool_scale)


def _store_value_t(vt_ref, v, heads):
    tr = v.shape[0]
    vt = v.T.astype(vt_ref.dtype)
    ones = jnp.ones((V_ROWS - LANES, tr), vt_ref.dtype)
    for h in range(heads):
        vt_ref[0, h * V_ROWS:h * V_ROWS + LANES, :] = vt[h * LANES:(h + 1) * LANES, :]
        vt_ref[0, h * V_ROWS + LANES:(h + 1) * V_ROWS, :] = ones


def _mla_prep_kernel(cq_ref, ckv_ref, sm_ref, gq_ref, gkv_ref, wuq_ref, wukv_ref,
                     cos_ref, sin_ref, qt_ref, kn_ref, kpe_ref, vt_ref):
    cos = cos_ref[...]
    sin = sin_ref[...]
    scale = (MLA_NOPE + MLA_ROPE) ** -0.5 * LOG2E
    cqn = _rms(cq_ref[...].astype(F32), gq_ref[...]).astype(BF16)
    q = jnp.dot(cqn, wuq_ref[...], preferred_element_type=F32)
    for h in range(MLA_HEADS):
        lo = 2 * h * LANES
        qt_ref[lo:lo + LANES, :] = (q[:, lo:lo + LANES] * scale).T.astype(qt_ref.dtype)
        pe = _rope_half64(q[:, lo + LANES:lo + 2 * LANES], cos, sin)
        qt_ref[lo + LANES:lo + 2 * LANES, :] = (pe * scale).T.astype(qt_ref.dtype)
    ckvn = _rms(ckv_ref[...].astype(F32), gkv_ref[...]).astype(BF16)
    kv = jnp.dot(ckvn, wukv_ref[...], preferred_element_type=F32)
    kn_ref[...] = kv[:, :BRANCH].astype(kn_ref.dtype)
    _store_value_t(vt_ref, kv[:, BRANCH:], MLA_HEADS)
    kpe_ref[...] = _rope_half64(sm_ref[:, 0:LANES], cos, sin).astype(kpe_ref.dtype)


def _mla_prep(proj, small, gq, gkv, wuq, wukv, cos64, sin64, layer, tr):
    m = proj.shape[0]
    qw = 2 * LANES * MLA_HEADS
    return pl.pallas_call(
        _mla_prep_kernel,
        grid=(m // tr,),
        in_specs=[pl.BlockSpec((tr, MLA_Q_RANK), lambda i: (i, 3 * BRANCH // MLA_Q_RANK)),
                  pl.BlockSpec((tr, MLA_KV_RANK), lambda i: (i, (3 * BRANCH + MLA_Q_RANK) // MLA_KV_RANK)),
                  pl.BlockSpec((tr, SMALL_COLS), lambda i: (i, 0)),
                  pl.BlockSpec((None, 1, MLA_Q_RANK), lambda i: (layer, 0, 0)),
                  pl.BlockSpec((None, 1, MLA_KV_RANK), lambda i: (layer, 0, 0)),
                  pl.BlockSpec((MLA_Q_RANK, qw), lambda i: (0, 0)),
                  pl.BlockSpec((MLA_KV_RANK, 2 * BRANCH), lambda i: (0, 0)),
                  pl.BlockSpec((tr, LANES), lambda i: (i, 0)),
                  pl.BlockSpec((tr, LANES), lambda i: (i, 0))],
        out_specs=[pl.BlockSpec((qw, tr), lambda i: (0, i)),
                   pl.BlockSpec((tr, BRANCH), lambda i: (i, 0)),
                   pl.BlockSpec((tr, LANES), lambda i: (i, 0)),
                   pl.BlockSpec((1, MLA_HEADS * V_ROWS, tr), lambda i: (i, 0, 0))],
        out_shape=[jax.ShapeDtypeStruct((qw, m), BF16),
                   jax.ShapeDtypeStruct((m, BRANCH), BF16),
                   jax.ShapeDtypeStruct((m, LANES), BF16),
                   jax.ShapeDtypeStruct((m // tr, MLA_HEADS * V_ROWS, tr), BF16)],
        compiler_params=_cp("parallel"),
        name="mla_prep",
    )(proj, proj, small, gq, gkv, wuq, wukv, cos64, sin64)


def _dsa_prep_kernel(dq_ref, dk_ref, dv_ref, iq_ref, sm_ref, c128_ref, s128_ref, c64_ref, s64_ref,
                     qt_ref, k_ref, vt_ref, iqr_ref, ik_ref, wt_ref):
    c128, s128 = c128_ref[...], s128_ref[...]
    c64, s64 = c64_ref[...], s64_ref[...]
    scale = DSA_DIM ** -0.5 * LOG2E
    for h in range(BRANCH // LANES):
        sl = slice(h * LANES, (h + 1) * LANES)
        qt_ref[sl, :] = (_rope_half128(dq_ref[:, sl].astype(F32), c128, s128) * scale).T.astype(qt_ref.dtype)
        k_ref[:, sl] = _rope_half128(dk_ref[:, sl].astype(F32), c128, s128).astype(k_ref.dtype)
        iqr_ref[:, sl] = _rope_half64(iq_ref[:, sl].astype(F32), c64, s64).astype(iqr_ref.dtype)
    _store_value_t(vt_ref, dv_ref[...].astype(F32), DSA_HEADS)
    tail = sm_ref[:, LANES:2 * LANES]
    lane = lax.broadcasted_iota(jnp.int32, tail.shape, 1)
    ik_lo = jnp.where(lane < IDX_DIM, _rope_half64(tail, c64, s64), 0.0)
    ik_ref[:, 0:LANES] = ik_lo.astype(ik_ref.dtype)
    ik_ref[:, LANES:2 * LANES] = pltpu.roll(ik_lo, IDX_DIM, axis=1).astype(ik_ref.dtype)
    idx_w_scale = (IDX_HEADS ** -0.5) * (IDX_DIM ** -0.5)
    wt_ref[...] = (tail * idx_w_scale).T


def _dsa_prep(proj_b, small, c128, s128, c64, s64, tr):
    m = proj_b.shape[0]
    col = lambda c: pl.BlockSpec((tr, BRANCH), lambda i: (i, c))
    tab = pl.BlockSpec((tr, LANES), lambda i: (i, 0))
    return pl.pallas_call(
        _dsa_prep_kernel,
        grid=(m // tr,),
        in_specs=[col(0), col(1), col(2), col(3),
                  pl.BlockSpec((tr, SMALL_COLS), lambda i: (i, 0)),
                  tab, tab, tab, tab],
        out_specs=[pl.BlockSpec((BRANCH, tr), lambda i: (0, i)),
                   pl.BlockSpec((tr, BRANCH), lambda i: (i, 0)),
                   pl.BlockSpec((1, DSA_HEADS * V_ROWS, tr), lambda i: (i, 0, 0)),
                   pl.BlockSpec((tr, BRANCH), lambda i: (i, 0)),
                   pl.BlockSpec((tr, 2 * LANES), lambda i: (i, 0)),
                   pl.BlockSpec((LANES, tr), lambda i: (0, i))],
        out_shape=[jax.ShapeDtypeStruct((BRANCH, m), BF16),
                   jax.ShapeDtypeStruct((m, BRANCH), BF16),
                   jax.ShapeDtypeStruct((m // tr, DSA_HEADS * V_ROWS, tr), BF16),
                   jax.ShapeDtypeStruct((m, BRANCH), BF16),
                   jax.ShapeDtypeStruct((m, 2 * LANES), BF16),
                   jax.ShapeDtypeStruct((LANES, m), F32)],
        compiler_params=_cp("parallel"),
        name="dsa_prep",
    )(proj_b, proj_b, proj_b, proj_b, small, c128, s128, c64, s64)


def _float_to_ordered_int(s):
    b = lax.bitcast_convert_type(s, jnp.int32)
    return b ^ ((b >> 31) & jnp.int32(0x7FFFFFFF))


def _indexer_kernel(iq_ref, ik_ref, wt_ref, bias_ref, key_ref, cut_ref, slot_ref, *, topk):
    tq = iq_ref.shape[0]
    total_rows = ik_ref.shape[0]
    tk = IDX_K_TILE
    i = pl.program_id(0)
    n_tiles = (i * tq + tq) // tk
    n_rows = n_tiles * tk
    qpos = i * tq + lax.broadcasted_iota(jnp.int32, (1, tq), 1)
    int_min = jnp.int32(-2 ** 31)

    def score_tile(kt, carry):
        start = pl.multiple_of(kt * tk, tk)
        ik_lo = ik_ref[pl.ds(start, tk), 0:LANES]
        ik_hi = ik_ref[pl.ds(start, tk), LANES:2 * LANES]
        acc = jnp.zeros((tk, tq), F32)
        for j in range(IDX_HEADS // 2):
            qpair = iq_ref[:, j * LANES:(j + 1) * LANES]
            for half, ik in enumerate((ik_lo, ik_hi)):
                g = 2 * j + half
                dots = lax.dot_general(ik, qpair, (((1,), (1,)), ((), ())),
                                       preferred_element_type=F32)
                acc = acc + jnp.maximum(dots, 0.0) * wt_ref[IDX_DIM + g:IDX_DIM + g + 1, :]
        acc = acc + 0.0
        kpos = start + lax.broadcasted_iota(jnp.int32, (tk, 1), 0)
        key_ref[pl.ds(start, tk), :] = jnp.where(kpos <= qpos, _float_to_ordered_int(acc), int_min)
        return carry

    lax.fori_loop(0, n_tiles, score_tile, 0)

    n_chunks = n_rows // COUNT_CHUNK

    def count(pred, with_pos=False):
        def body(c, accs):
            start = pl.multiple_of(c * COUNT_CHUNK, COUNT_CHUNK)
            accs = list(accs)
            chunk = key_ref[pl.ds(start, COUNT_CHUNK), :]
            for r in range(COUNT_CHUNK // 8):
                blk = chunk[8 * r:8 * r + 8, :]
                if with_pos:
                    pos = start + 8 * r + lax.broadcasted_iota(jnp.int32, (8, 1), 0)
                    hit = pred(blk, pos)
                else:
                    hit = pred(blk)
                accs[r % COUNT_ACCS] = accs[r % COUNT_ACCS] + jnp.where(hit, 1, 0).astype(jnp.int32)
            return tuple(accs)
        zero = jnp.zeros((8, tq), jnp.int32)
        accs = lax.fori_loop(0, n_chunks, body, (zero,) * COUNT_ACCS)
        return jnp.sum(functools.reduce(lambda a, b: a + b, accs), axis=0, keepdims=True)

    assert topk <= COUNT_CHUNK
    slot_ref[...] = key_ref[0:COUNT_CHUNK, :]

    def fold_slots(c, carry):
        start = pl.multiple_of(c * COUNT_CHUNK, COUNT_CHUNK)
        slot_ref[...] = jnp.maximum(slot_ref[...], key_ref[pl.ds(start, COUNT_CHUNK), :])
        return carry

    lax.fori_loop(1, n_chunks, fold_slots, 0)
    slots = slot_ref[...]
    hi = jnp.max(slots, axis=0, keepdims=True)
    lo = jnp.min(slots, axis=0, keepdims=True)
    n_bits = 32 - jnp.min(lax.clz(hi - lo))

    def bit_step(b, tau):
        cand = tau + jnp.left_shift(jnp.int32(1), n_bits - 1 - b)
        cnt = count(lambda blk: blk >= cand)
        return jnp.where((cnt >= topk) & (cand > tau), cand, tau)

    tau = lax.fori_loop(0, n_bits, bit_step, lo)

    n_gt = count(lambda blk: blk > tau)
    n_eq = count(lambda blk: blk == tau)
    need = topk - n_gt
    cut_bits = 14
    cut_ref[...] = jnp.full((1, tq), 1 << cut_bits, jnp.int32)

    @pl.when(jnp.max(n_eq - need) > 0)
    def _():
        def cut_step(b, cut):
            cand = cut + jnp.left_shift(jnp.int32(1), cut_bits - 1 - b)
            cnt = count(lambda blk, pos: (blk == tau) & (pos < cand), with_pos=True)
            return jnp.where(cnt <= need, cand, cut)
        cut_ref[...] = lax.fori_loop(0, cut_bits, cut_step, jnp.zeros((1, tq), jnp.int32))

    cut = cut_ref[...]

    def write_sel(c, carry):
        start = pl.multiple_of(c * WRITE_CHUNK, WRITE_CHUNK)
        blk = key_ref[pl.ds(start, WRITE_CHUNK), :]
        pos = start + lax.broadcasted_iota(jnp.int32, (WRITE_CHUNK, 1), 0)
        sel = ((blk > tau) | ((blk == tau) & (pos < cut))) & (pos <= qpos)
        bias_ref[pl.ds(start, WRITE_CHUNK), :] = jnp.where(sel, 0.0, NEG).astype(bias_ref.dtype)
        return carry

    lax.fori_loop(0, n_rows // WRITE_CHUNK, write_sel, 0)

    def write_neg(c, carry):
        start = pl.multiple_of(c * WRITE_CHUNK, WRITE_CHUNK)
        bias_ref[pl.ds(start, WRITE_CHUNK), :] = jnp.full((WRITE_CHUNK, tq), NEG, bias_ref.dtype)
        return carry

    lax.fori_loop(n_rows // WRITE_CHUNK, total_rows // WRITE_CHUNK, write_neg, 0)


def _indexer(iq_r, ik_ab, wt, topk):
    m = iq_r.shape[0]
    return pl.pallas_call(
        functools.partial(_indexer_kernel, topk=topk),
        grid=(m // Q_TILE,),
        in_specs=[pl.BlockSpec((Q_TILE, BRANCH), lambda i: (i, 0)),
                  pl.BlockSpec((m, 2 * LANES), lambda i: (0, 0)),
                  pl.BlockSpec((LANES, Q_TILE), lambda i: (0, i))],
        out_specs=pl.BlockSpec((m, Q_TILE), lambda i: (0, i)),
        out_shape=jax.ShapeDtypeStruct((m, m), BF16),
        scratch_shapes=[pltpu.VMEM((m, Q_TILE), jnp.int32),
                        pltpu.VMEM((1, Q_TILE), jnp.int32),
                        pltpu.VMEM((COUNT_CHUNK, Q_TILE), jnp.int32)],
        compiler_params=_cp("parallel"),
        name="dsa_indexer",
    )(iq_r, ik_ab, wt)


def _flash_kernel(*refs, q_axis, has_kpe, has_bias):
    refs = list(refs)
    q_ref, k_ref = refs[0], refs[1]
    pos = 2
    kpe_ref = bias_ref = None
    if has_kpe:
        kpe_ref = refs[pos]
        pos += 1
    vt_ref = refs[pos]
    pos += 1
    if has_bias:
        bias_ref = refs[pos]
        pos += 1
    o_ref, sa_ref, sb_ref = refs[pos], refs[pos + 1], refs[pos + 2]

    tq = q_ref.shape[1]
    heads = vt_ref.shape[1] // V_ROWS
    tk = vt_ref.shape[2]
    dq = q_ref.shape[0] // heads
    assert tk % tq == 0
    i = pl.program_id(q_axis)
    n_tiles = (i * tq + tq + tk - 1) // tk
    qpos = i * tq + lax.broadcasted_iota(jnp.int32, (1, tq), 1)

    def compute_scores(kt, s_ref):
        start = pl.multiple_of(kt * tk, tk)
        kpe = kpe_ref[pl.ds(start, tk), :] if has_kpe else None
        for h in range(heads):
            k = k_ref[pl.ds(start, tk), h * LANES:(h + 1) * LANES]
            if has_kpe:
                k = jnp.concatenate([k, kpe], axis=1)
            s_ref[h] = jnp.dot(k, q_ref[h * dq:(h + 1) * dq, :], preferred_element_type=F32)

    def consume_scores(kt, s_ref, carry, causal_mask):
        start = pl.multiple_of(kt * tk, tk)
        bias = bias_ref[pl.ds(start, tk), :].astype(F32) if has_bias else None
        new = []
        for h in range(heads):
            m_run, acc = carry[h]
            s = s_ref[h]
            if has_bias:
                s = s + bias
            if causal_mask:
                kpos = start + lax.broadcasted_iota(jnp.int32, (tk, 1), 0)
                s = jnp.where(kpos <= qpos, s, NEG)
            m_new = jnp.maximum(m_run, jnp.max(s, axis=0, keepdims=True))
            alpha = jnp.exp2(m_run - m_new)
            p = jnp.exp2(s - m_new).astype(BF16)
            pv = jnp.dot(vt_ref[kt, h * V_ROWS:(h + 1) * V_ROWS, :], p, preferred_element_type=F32)
            new.append((m_new, alpha * acc + pv))
        return tuple(new)

    def double_step(u, carry):
        kt = 2 * u
        compute_scores(kt + 1, sb_ref)
        carry = consume_scores(kt, sa_ref, carry, False)
        compute_scores(kt + 2, sa_ref)
        return consume_scores(kt + 1, sb_ref, carry, False)

    mask_last = not has_bias
    last = n_tiles - 1

    def odd_tail(carry):
        compute_scores(last, sb_ref)
        carry = consume_scores(last - 1, sa_ref, carry, False)
        return consume_scores(last, sb_ref, carry, mask_last)

    def even_tail(carry):
        return consume_scores(last, sa_ref, carry, mask_last)

    init = tuple((jnp.full((1, tq), NEG, F32), jnp.zeros((V_ROWS, tq), F32)) for _ in range(heads))
    compute_scores(0, sa_ref)
    carry = lax.fori_loop(0, last // 2, double_step, init)
    carry = lax.cond(last % 2 == 1, odd_tail, even_tail, carry)
    for h in range(heads):
        acc = carry[h][1]
        out = acc[0:LANES, :] / acc[LANES:LANES + 1, :]
        o_ref[:, h * LANES:(h + 1) * LANES] = out.T.astype(o_ref.dtype)


def _score_scratch(heads, tk):
    return [pltpu.VMEM((heads, tk, Q_TILE), F32), pltpu.VMEM((heads, tk, Q_TILE), F32)]


def _mla_attention(q, kn, kpe, vt):
    m = kn.shape[0]
    n_kt, _, tk = vt.shape
    hp = HEADS_PER_STEP
    return pl.pallas_call(
        functools.partial(_flash_kernel, q_axis=1, has_kpe=True, has_bias=False),
        grid=(MLA_HEADS // hp, m // Q_TILE),
        in_specs=[pl.BlockSpec((hp * 2 * LANES, Q_TILE), lambda h, i: (h, i)),
                  pl.BlockSpec((m, hp * LANES), lambda h, i: (0, h)),
                  pl.BlockSpec((m, LANES), lambda h, i: (0, 0)),
                  pl.BlockSpec((n_kt, hp * V_ROWS, tk), lambda h, i: (0, h, 0))],
        out_specs=pl.BlockSpec((Q_TILE, hp * MLA_V), lambda h, i: (i, h)),
        out_shape=jax.ShapeDtypeStruct((m, BRANCH), BF16),
        scratch_shapes=_score_scratch(hp, tk),
        compiler_params=_cp("parallel", "parallel"),
        name="mla_attention",
    )(q, kn, kpe, vt)


def _dsa_attention(q, k, vt, bias):
    m = k.shape[0]
    n_kt, _, tk = vt.shape
    hp = HEADS_PER_STEP
    return pl.pallas_call(
        functools.partial(_flash_kernel, q_axis=1, has_kpe=False, has_bias=True),
        grid=(DSA_HEADS // hp, m // Q_TILE),
        in_specs=[pl.BlockSpec((hp * DSA_DIM, Q_TILE), lambda h, i: (h, i)),
                  pl.BlockSpec((m, hp * DSA_DIM), lambda h, i: (0, h)),
                  pl.BlockSpec((n_kt, hp * V_ROWS, tk), lambda h, i: (0, h, 0)),
                  pl.BlockSpec((m, Q_TILE), lambda h, i: (0, i))],
        out_specs=pl.BlockSpec((Q_TILE, hp * DSA_DIM), lambda h, i: (i, h)),
        out_shape=jax.ShapeDtypeStruct((m, BRANCH), BF16),
        scratch_shapes=_score_scratch(hp, tk),
        compiler_params=_cp("parallel", "parallel"),
        name="dsa_attention",
    )(q, k, vt, bias)


def _rope_tables(rows, dim):
    inv = 1.0 / jnp.power(ROPE_THETA, jnp.arange(0, dim, 2, dtype=F32) / dim)
    ang = jnp.arange(rows, dtype=F32)[:, None] * inv[None, :]
    cos, sin = jnp.cos(ang), jnp.sin(ang)
    reps = LANES // dim
    return (jnp.tile(jnp.concatenate([cos, cos], axis=1), (1, reps)),
            jnp.tile(jnp.concatenate([-sin, sin], axis=1), (1, reps)))


def _regroup_w_in(w):
    o_kr = 3 * BRANCH + MLA_Q_RANK + MLA_KV_RANK
    o_dq = o_kr + MLA_ROPE
    o_ik = o_dq + 4 * BRANCH
    o_iw = o_ik + IDX_DIM
    o_pu = o_iw + IDX_HEADS
    assert o_kr == HALF_COLS
    assert o_dq - o_kr == MLA_ROPE and o_ik - o_dq == 4 * BRANCH
    pool = w[:, :, o_pu:]
    zeros = lambda n: jnp.zeros(w.shape[:2] + (n,), w.dtype)
    small = jnp.concatenate([w[:, :, o_kr:o_dq], zeros(LANES - MLA_ROPE),
                             w[:, :, o_ik:o_iw], w[:, :, o_iw:o_pu],
                             zeros(LANES - IDX_DIM - IDX_HEADS)], axis=2)
    return pool, small


def _layout_w_uq(w):
    w3 = w.reshape(MLA_Q_RANK, MLA_HEADS, MLA_NOPE + MLA_ROPE)
    w3 = jnp.pad(w3, ((0, 0), (0, 0), (0, 2 * LANES - MLA_NOPE - MLA_ROPE)))
    return w3.reshape(MLA_Q_RANK, MLA_HEADS * 2 * LANES).astype(BF16)


def _layout_w_ukv(w):
    w4 = w.reshape(MLA_KV_RANK, MLA_HEADS, 2, MLA_NOPE)
    return w4.transpose(0, 2, 1, 3).reshape(MLA_KV_RANK, 2 * BRANCH).astype(BF16)


def _forward(x, meta_tokens, norm_mix_pre, norm_mix_post, norm_ffn_pre, norm_ffn_post,
             w_in, conv_w, mla_q_norm, mla_w_uq, mla_kv_norm, mla_w_ukv, pool_w, pool_scale,
             w_branch, w_gate, b_gate, w_out, ffn_w_gate, ffn_w_up, ffn_w_down):
    assert x.shape[0] == 1 and x.shape[2] == D_MODEL
    depth = w_in.shape[0]
    seq = x.shape[1]
    length = N_META + seq
    topk = min(IDX_TOPK_MAX, length // 4)
    rows = -(-length // ROW_TILE) * ROW_TILE
    assert rows % Q_TILE == 0 and rows % K_TILE == 0 and rows % NORM_TILE == 0
    assert rows < (1 << 14)

    c64, s64 = _rope_tables(rows, 64)
    c128, s128 = _rope_tables(rows, 128)
    w_in_pool, w_in_small = _regroup_w_in(w_in)
    tn = 512
    wide = 2 * tn
    gate_tiles = D_MODEL // wide

    h, xn = _assemble_prenorm(x[0], meta_tokens.astype(F32), norm_mix_pre[0][None], rows)
    for l in range(depth):
        at_layer = lambda j, l=l: (l,)
        tile = lambda j: j
        proj_a = _wide_proj(xn, w_in, at_layer, tile, HALF_COLS // wide, ROW_TILE, tn, BF16, "in_proj_a")
        proj_b = _wide_proj(xn, w_in, at_layer, tile, 4 * BRANCH // wide, ROW_TILE, tn, BF16,
                            "in_proj_b", first_half_block=HALF_COLS // tn, lane_shift=MLA_ROPE)
        proj_p = _wide_proj(xn, w_in_pool, at_layer, tile, BRANCH // wide, ROW_TILE, tn, BF16,
                            "in_proj_pool")
        small = _stacked_proj(xn, w_in_small, at_layer, tile, 1, ROW_TILE, SMALL_COLS, F32,
                              "in_proj_small")
        gates = _wide_proj(xn, w_gate,
                           lambda j, l=l: (l, j // gate_tiles), lambda j: j % gate_tiles,
                           N_BRANCH * gate_tiles, ROW_TILE, tn, BF16, "gates",
                           bias=b_gate[:, :, None, :])

        y_conv = _conv_branch(proj_a, conv_w, l, ROW_TILE)
        y_pool = _pool_branch(proj_p, pool_w, pool_scale[:, None, :], l, ROW_TILE)

        q_m, kn_m, kpe_m, vt_m = _mla_prep(
            proj_a, small, mla_q_norm[:, None, :], mla_kv_norm[:, None, :],
            _layout_w_uq(mla_w_uq[l]), _layout_w_ukv(mla_w_ukv[l]), c64, s64, l, K_TILE)
        y_mla = _mla_attention(q_m, kn_m, kpe_m, vt_m)

        q_d, k_d, vt_d, iq_r, ik_ab, wt = _dsa_prep(proj_b, small, c128, s128, c64, s64, K_TILE)
        bias = _indexer(iq_r, ik_ab, wt, topk)
        y_dsa = _dsa_attention(q_d, k_d, vt_d, bias)

        merged = _gated_merge((y_conv, y_mla, y_dsa, y_pool), gates, w_branch, l, ROW_TILE, tn)
        mix = _wide_proj(merged, w_out, at_layer, tile, D_MODEL // wide, ROW_TILE, tn, F32, "out_proj")
        h, xn = _resid_norm(h, mix, norm_mix_post[l][None], norm_ffn_pre[l][None])

        act = _swiglu(xn, ffn_w_gate, ffn_w_up, l, ROW_TILE, 256)
        f = _stacked_proj(act, ffn_w_down, at_layer, tile, D_MODEL // tn, 384, tn, F32, "ffn_down",
                          single_buffer_weight=True)
        if l + 1 < depth:
            h, xn = _resid_norm(h, f, norm_ffn_post[l][None], norm_mix_pre[l + 1][None])
        else:
            out = _resid_out(h, f, norm_ffn_post[l][None], seq)

    return out[None]


def kernel(x, meta_tokens, norm_mix_pre, norm_mix_post, norm_ffn_pre, norm_ffn_post, w_in, conv_w, mla_q_norm, mla_w_uq, mla_kv_norm, mla_w_ukv, pool_w, pool_scale, w_branch, w_gate, b_gate, w_out, ffn_w_gate, ffn_w_up, ffn_w_down):
    return _forward(x, meta_tokens, norm_mix_pre, norm_mix_post, norm_ffn_pre, norm_ffn_post,
                    w_in, conv_w, mla_q_norm, mla_w_uq, mla_kv_norm, mla_w_ukv, pool_w, pool_scale,
                    w_branch, w_gate, b_gate, w_out, ffn_w_gate, ffn_w_up, ffn_w_down)
```

```python
import functools
import math

import jax
import jax.numpy as jnp
from jax import lax
from jax.experimental import pallas as pl
from jax.experimental.pallas import tpu as pltpu

F32 = jnp.float32
BF16 = jnp.bfloat16

D_MODEL = 4096
N_META = 16
ROPE_THETA = 10000.0
EPS = 1e-6
N_BRANCH = 4
BRANCH = 1024
CONV_K = 3
MLA_NOPE, MLA_ROPE, MLA_V, MLA_HEADS = 128, 64, 128, 8
MLA_Q_RANK, MLA_KV_RANK = 1536, 512
DSA_DIM, DSA_HEADS = 128, 8
IDX_HEADS, IDX_DIM, IDX_TOPK_MAX = 16, 64, 256
POOL_WINDOWS = (2, 4, 8, 16)
POOL_GROUP = 256
D_FF = 11008

LANES = 128
HALO = 16
ROW_TILE = 768
Q_TILE = 256
K_TILE = 768
V_ROWS = 144
HEADS_PER_STEP = 4
IDX_K_TILE = 256
NEXT_ROWS = 256
COUNT_CHUNK = 256
COUNT_ACCS = 4
WRITE_CHUNK = 64
NORM_TILE = 192
NEG = -1e30
LOG2E = math.log2(math.e)
VMEM_LIMIT = 58 * 1024 * 1024

HALF_COLS = 5120
SMALL_COLS = 256


def _cp(*sem):
    return pltpu.CompilerParams(dimension_semantics=sem, vmem_limit_bytes=VMEM_LIMIT)


def _sigmoid(x):
    return 1.0 / (1.0 + jnp.exp(-x))


def _rms(x, g):
    return x * lax.rsqrt(jnp.mean(x * x, axis=-1, keepdims=True) + EPS) * g


def _matmul_w(a, wbf, weights_nk):
    if weights_nk:
        return lax.dot_general(a, wbf, (((1,), (1,)), ((), ())), preferred_element_type=F32)
    return jnp.dot(a, wbf, preferred_element_type=F32)


def _proj_kernel(a_ref, w_ref, o_ref, wbf_ref, *, weights_nk):
    @pl.when(pl.program_id(1) == 0)
    def _():
        wbf_ref[...] = w_ref[...].astype(wbf_ref.dtype)

    o_ref[...] = _matmul_w(a_ref[...], wbf_ref[...], weights_nk).astype(o_ref.dtype)


def _stacked_proj(a, w, lead_of, col_of, n_tiles, tm, tn, out_dtype, name,
                  weights_nk=False, single_buffer_weight=False):
    m, k = a.shape
    lead = (None,) * (w.ndim - 2)
    w_kwargs = {"pipeline_mode": pl.Buffered(1)} if single_buffer_weight else {}
    if weights_nk:
        assert w.shape[-1] == k
        w_block, w_index = (tn, k), lambda j, i: (*lead_of(j), col_of(j), 0)
    else:
        assert w.shape[-2] == k
        w_block, w_index = (k, tn), lambda j, i: (*lead_of(j), 0, col_of(j))
    return pl.pallas_call(
        functools.partial(_proj_kernel, weights_nk=weights_nk),
        grid=(n_tiles, m // tm),
        in_specs=[pl.BlockSpec((tm, k), lambda j, i: (i, 0)),
                  pl.BlockSpec(lead + w_block, w_index, **w_kwargs)],
        out_specs=pl.BlockSpec((tm, tn), lambda j, i: (i, j)),
        out_shape=jax.ShapeDtypeStruct((m, n_tiles * tn), out_dtype),
        scratch_shapes=[pltpu.VMEM(w_block, BF16)],
        compiler_params=_cp("parallel", "arbitrary"),
        name=name,
    )(a, w)


def _wide_proj_kernel(*refs, has_bias, weights_nk, row_shift):
    refs = list(refs)
    a_ref, w_ref = refs[0], refs[1]
    pos = 2
    next_ref = b_ref = None
    if row_shift:
        next_ref = refs[pos]
        pos += 1
    if has_bias:
        b_ref = refs[pos]
        pos += 1
    o_ref, wbf_ref = refs[pos], refs[pos + 1]
    half = o_ref.shape[1] // 2
    i = pl.program_id(1)

    def cast_half(lo):
        if not weights_nk:
            wbf_ref[:, lo:lo + half] = w_ref[...].astype(wbf_ref.dtype)
        elif not row_shift:
            wbf_ref[lo:lo + half, :] = w_ref[...].astype(wbf_ref.dtype)
        else:
            keep = half - row_shift
            wbf_ref[lo:lo + keep, :] = w_ref[row_shift:half, :].astype(wbf_ref.dtype)
            wbf_ref[lo + keep:lo + half, :] = next_ref[0:row_shift, :].astype(wbf_ref.dtype)

    def emit(lo, width):
        wbf = wbf_ref[lo:lo + width, :] if weights_nk else wbf_ref[:, lo:lo + width]
        z = _matmul_w(a_ref[...], wbf, weights_nk)
        if has_bias:
            z = _sigmoid(z + b_ref[:, lo:lo + width])
        o_ref[:, lo:lo + width] = z.astype(o_ref.dtype)

    @pl.when(i == 0)
    def _():
        cast_half(0)
        emit(0, half)

    @pl.when(i == 1)
    def _():
        cast_half(half)
        emit(half, half)

    @pl.when(i >= 2)
    def _():
        emit(0, 2 * half)


def _wide_proj(a, w, lead_of, col_of, n_tiles, tm, half, out_dtype, name, bias=None,
               weights_nk=False, first_half_block=0, row_shift=0):
    m, k = a.shape
    assert w.shape[-1 if weights_nk else -2] == k
    assert row_shift == 0 or (weights_nk and row_shift % 8 == 0 and row_shift <= NEXT_ROWS)
    lead = (None,) * (w.ndim - 2)
    row = lambda i: jnp.maximum(i - 1, 0)

    def weight_window(j, i):
        jw = jnp.where(i >= 2, jnp.minimum(j + 1, n_tiles - 1), j)
        return lead_of(jw), first_half_block + 2 * col_of(jw) + jnp.where(i == 1, 1, 0)

    def w_index(j, i):
        stacked, hb = weight_window(j, i)
        return (*stacked, hb, 0) if weights_nk else (*stacked, 0, hb)

    in_specs = [pl.BlockSpec((tm, k), lambda j, i: (row(i), 0)),
                pl.BlockSpec(lead + ((half, k) if weights_nk else (k, half)), w_index)]
    args = [a, w]
    if row_shift:
        def next_index(j, i):
            stacked, hb = weight_window(j, i)
            return (*stacked, (hb + 1) * (half // NEXT_ROWS), 0)

        in_specs.append(pl.BlockSpec(lead + (NEXT_ROWS, k), next_index))
        args.append(w)
    if bias is not None:
        in_specs.append(pl.BlockSpec(lead + (1, 2 * half), lambda j, i: (*lead_of(j), 0, col_of(j))))
        args.append(bias)
    return pl.pallas_call(
        functools.partial(_wide_proj_kernel, has_bias=bias is not None, weights_nk=weights_nk,
                          row_shift=row_shift),
        grid=(n_tiles, m // tm + 1),
        in_specs=in_specs,
        out_specs=pl.BlockSpec((tm, 2 * half), lambda j, i: (row(i), j)),
        out_shape=jax.ShapeDtypeStruct((m, n_tiles * 2 * half), out_dtype),
        scratch_shapes=[pltpu.VMEM((2 * half, k) if weights_nk else (k, 2 * half), BF16)],
        compiler_params=_cp("parallel", "arbitrary"),
        name=name,
    )(*args)


def _swiglu_kernel(a_ref, wg_ref, wu_ref, o_ref, wgbf_ref, wubf_ref, *, n_half_blocks):
    half = wg_ref.shape[1]
    j = pl.program_id(0)
    i = pl.program_id(1)

    def cast_half(lo):
        wgbf_ref[:, lo:lo + half] = wg_ref[...].astype(wgbf_ref.dtype)
        wubf_ref[:, lo:lo + half] = wu_ref[...].astype(wubf_ref.dtype)

    def emit(lo, width):
        a = a_ref[...]
        g = jnp.dot(a, wgbf_ref[:, lo:lo + width], preferred_element_type=F32)
        u = jnp.dot(a, wubf_ref[:, lo:lo + width], preferred_element_type=F32)
        o_ref[:, lo:lo + width] = (g * _sigmoid(g) * u).astype(o_ref.dtype)

    has_right = 2 * j + 1 < n_half_blocks

    @pl.when(i == 0)
    def _():
        cast_half(0)
        emit(0, half)

    @pl.when((i == 1) & has_right)
    def _():
        cast_half(half)
        emit(half, half)

    @pl.when((i >= 2) & has_right)
    def _():
        emit(0, 2 * half)

    @pl.when((i >= 2) & jnp.logical_not(has_right))
    def _():
        emit(0, half)


def _swiglu(a, wg, wu, layer, tm, half):
    m, k = a.shape
    n = wg.shape[2]
    n_half_blocks = n // half
    assert n_half_blocks * half == n
    n_tiles = -(-n_half_blocks // 2)
    row = lambda i: jnp.maximum(i - 1, 0)
    def w_index(j, i):
        jw = jnp.where(i >= 2, jnp.minimum(j + 1, n_tiles - 1), j)
        return (layer, 0, jnp.minimum(2 * jw + jnp.where(i == 1, 1, 0), n_half_blocks - 1))

    w_spec = pl.BlockSpec((None, k, half), w_index)
    return pl.pallas_call(
        functools.partial(_swiglu_kernel, n_half_blocks=n_half_blocks),
        grid=(n_tiles, m // tm + 1),
        in_specs=[pl.BlockSpec((tm, k), lambda j, i: (row(i), 0)), w_spec, w_spec],
        out_specs=pl.BlockSpec((tm, 2 * half), lambda j, i: (row(i), j)),
        out_shape=jax.ShapeDtypeStruct((m, n), BF16),
        scratch_shapes=[pltpu.VMEM((k, 2 * half), BF16), pltpu.VMEM((k, 2 * half), BF16)],
        compiler_params=_cp("parallel", "arbitrary"),
        name="ffn_swiglu",
    )(a, wg, wu)


def _merge_kernel(y0_ref, y1_ref, y2_ref, y3_ref, g0_ref, g1_ref, g2_ref, g3_ref, w_ref,
                  o_ref, wbf_ref):
    @pl.when(pl.program_id(1) == 0)
    def _():
        wbf_ref[...] = w_ref[...].astype(wbf_ref.dtype)

    acc = None
    for b, (y_ref, g_ref) in enumerate(((y0_ref, g0_ref), (y1_ref, g1_ref),
                                        (y2_ref, g2_ref), (y3_ref, g3_ref))):
        val = g_ref[...].astype(F32) * jnp.dot(y_ref[...], wbf_ref[b], preferred_element_type=F32)
        acc = val if acc is None else acc + val
    o_ref[...] = acc.astype(o_ref.dtype)


def _gated_merge(ys, gates, w_branch, layer, tm, tn):
    m = ys[0].shape[0]
    nj = D_MODEL // tn
    y_spec = pl.BlockSpec((tm, BRANCH), lambda j, i: (i, 0))
    g_specs = [pl.BlockSpec((tm, tn), functools.partial(lambda j, i, b: (i, b * nj + j), b=b))
               for b in range(N_BRANCH)]
    return pl.pallas_call(
        _merge_kernel,
        grid=(nj, m // tm),
        in_specs=[y_spec] * N_BRANCH + g_specs
                 + [pl.BlockSpec((None, N_BRANCH, BRANCH, tn), lambda j, i: (layer, 0, 0, j))],
        out_specs=pl.BlockSpec((tm, tn), lambda j, i: (i, j)),
        out_shape=jax.ShapeDtypeStruct((m, D_MODEL), BF16),
        scratch_shapes=[pltpu.VMEM((N_BRANCH, BRANCH, tn), BF16)],
        compiler_params=_cp("parallel", "arbitrary"),
        name="gated_merge",
    )(*ys, gates, gates, gates, gates, w_branch)


def _assemble_kernel(x_ref, prev_ref, meta_ref, g_ref, h_ref, xn_ref, *, length):
    tr = x_ref.shape[0]
    i = pl.program_id(0)
    head = jnp.where(i == 0, meta_ref[...], prev_ref[...])
    tile = jnp.concatenate([head, x_ref[0:tr - N_META, :]], axis=0)
    pos = i * tr + lax.broadcasted_iota(jnp.int32, (tr, 1), 0)
    h = jnp.where(pos < length, tile, 0.0)
    h_ref[...] = h
    xn_ref[...] = _rms(h, g_ref[...]).astype(xn_ref.dtype)


def _assemble_prenorm(x2d, meta, g, rows):
    seq = x2d.shape[0]
    assert N_META == HALO and NORM_TILE % N_META == 0
    last_x = (seq - 1) // NORM_TILE
    last_prev = (seq - 1) // N_META
    per_tile = NORM_TILE // N_META
    row = pl.BlockSpec((NORM_TILE, D_MODEL), lambda i: (i, 0))
    return pl.pallas_call(
        functools.partial(_assemble_kernel, length=N_META + seq),
        grid=(rows // NORM_TILE,),
        in_specs=[pl.BlockSpec((NORM_TILE, D_MODEL), lambda i: (jnp.minimum(i, last_x), 0)),
                  pl.BlockSpec((N_META, D_MODEL),
                               lambda i: (jnp.clip(i * per_tile - 1, 0, last_prev), 0)),
                  pl.BlockSpec((N_META, D_MODEL), lambda i: (0, 0)),
                  pl.BlockSpec((1, D_MODEL), lambda i: (0, 0))],
        out_specs=[row, row],
        out_shape=[jax.ShapeDtypeStruct((rows, D_MODEL), F32),
                   jax.ShapeDtypeStruct((rows, D_MODEL), BF16)],
        compiler_params=_cp("parallel"),
        name="assemble_prenorm",
    )(x2d, x2d, meta, g)


def _resid_norm_kernel(h_ref, o_ref, gpost_ref, gnext_ref, hn_ref, xn_ref):
    hn = h_ref[...] + _rms(o_ref[...], gpost_ref[...])
    hn_ref[...] = hn
    xn_ref[...] = _rms(hn, gnext_ref[...]).astype(xn_ref.dtype)


def _resid_out_kernel(h_ref, o_ref, hnext_ref, onext_ref, gpost_ref, out_ref):
    g = gpost_ref[...]
    cur = h_ref[N_META:, :] + _rms(o_ref[N_META:, :], g)
    nxt = hnext_ref[...] + _rms(onext_ref[...], g)
    out_ref[...] = jnp.concatenate([cur, nxt], axis=0)


def _resid_norm(h, o, g_post, g_next):
    m = h.shape[0]
    row = pl.BlockSpec((NORM_TILE, D_MODEL), lambda i: (i, 0))
    gain = pl.BlockSpec((1, D_MODEL), lambda i: (0, 0))
    return pl.pallas_call(
        _resid_norm_kernel,
        grid=(m // NORM_TILE,),
        in_specs=[row, row, gain, gain],
        out_specs=[row, row],
        out_shape=[jax.ShapeDtypeStruct((m, D_MODEL), F32),
                   jax.ShapeDtypeStruct((m, D_MODEL), BF16)],
        compiler_params=_cp("parallel"),
        name="resid_norm",
    )(h, o, g_post, g_next)


def _resid_out(h, o, g_post, seq):
    m = h.shape[0]
    per_tile = NORM_TILE // N_META
    last_head = m // N_META - 1
    row = pl.BlockSpec((NORM_TILE, D_MODEL), lambda i: (i, 0))
    head = pl.BlockSpec((N_META, D_MODEL), lambda i: (jnp.minimum((i + 1) * per_tile, last_head), 0))
    return pl.pallas_call(
        _resid_out_kernel,
        grid=(-(-seq // NORM_TILE),),
        in_specs=[row, row, head, head, pl.BlockSpec((1, D_MODEL), lambda i: (0, 0))],
        out_specs=row,
        out_shape=jax.ShapeDtypeStruct((seq, D_MODEL), F32),
        compiler_params=_cp("parallel"),
        name="resid_out",
    )(h, o, h, o, g_post)


def _rope_half128(x, cos, sin_signed):
    return x * cos + pltpu.roll(x, 64, axis=1) * sin_signed


def _rope_half64(x, cos, sin_signed):
    lane = lax.broadcasted_iota(jnp.int32, x.shape, 1)
    partner = jnp.where((lane & 32) == 0, pltpu.roll(x, 96, axis=1), pltpu.roll(x, 32, axis=1))
    return x * cos + partner * sin_signed


def _conv_kernel(cb_ref, cc_ref, cu_ref, pc_ref, pu_ref, w_ref, y_ref, z_ref):
    tr = cb_ref.shape[0]
    i = pl.program_id(0)
    z = cc_ref[...].astype(F32) * cu_ref[...].astype(F32)
    zp = pc_ref[...].astype(F32) * pu_ref[...].astype(F32)
    z_ref[0:HALO, :] = jnp.where(i > 0, zp, 0.0)
    z_ref[HALO:HALO + tr, :] = z
    w = w_ref[...]
    conv = (w[2:3, :] * z
            + w[1:2, :] * z_ref[HALO - 1:HALO - 1 + tr, :]
            + w[0:1, :] * z_ref[HALO - 2:HALO - 2 + tr, :])
    y_ref[...] = (cb_ref[...].astype(F32) * conv).astype(y_ref.dtype)


def _conv_branch(proj, conv_w, layer, tr):
    m = proj.shape[0]
    cw = 256
    nb = BRANCH // cw

    def halo_row(i):
        return jnp.maximum(i * (tr // HALO) - 1, 0)

    return pl.pallas_call(
        _conv_kernel,
        grid=(m // tr, nb),
        in_specs=[pl.BlockSpec((tr, cw), lambda i, c: (i, c)),
                  pl.BlockSpec((tr, cw), lambda i, c: (i, nb + c)),
                  pl.BlockSpec((tr, cw), lambda i, c: (i, 2 * nb + c)),
                  pl.BlockSpec((HALO, cw), lambda i, c: (halo_row(i), nb + c)),
                  pl.BlockSpec((HALO, cw), lambda i, c: (halo_row(i), 2 * nb + c)),
                  pl.BlockSpec((None, CONV_K, cw), lambda i, c: (layer, 0, c))],
        out_specs=pl.BlockSpec((tr, cw), lambda i, c: (i, c)),
        out_shape=jax.ShapeDtypeStruct((m, BRANCH), BF16),
        scratch_shapes=[pltpu.VMEM((HALO + tr, cw), F32)],
        compiler_params=_cp("parallel", "parallel"),
        name="conv_branch",
    )(proj, proj, proj, proj, proj, conv_w)


def _pool_kernel(u_ref, pu_ref, w_ref, s_ref, y_ref, x_ref):
    tr = u_ref.shape[0]
    i = pl.program_id(0)
    g = pl.program_id(1)
    win = jnp.left_shift(jnp.int32(2), g)
    x = u_ref[...].astype(F32)
    x_ref[0:HALO, :] = jnp.where(i > 0, pu_ref[...].astype(F32), 0.0)
    x_ref[HALO:HALO + tr, :] = x
    total = x
    for j in range(1, max(POOL_WINDOWS)):
        keep = jnp.where(j < win, 1.0, 0.0).astype(F32)
        total = total + keep * x_ref[HALO - j:HALO - j + tr, :]
    t = i * tr + lax.broadcasted_iota(jnp.int32, x.shape, 0)
    count = jnp.minimum(t + 1, win).astype(F32)
    pooled = (total / count - x).astype(BF16)
    mixed = jnp.dot(pooled, w_ref[...].astype(BF16), preferred_element_type=F32)
    y_ref[...] = (mixed * s_ref[...]).astype(y_ref.dtype)


def _pool_branch(proj, pool_w, pool_scale, layer, tr):
    m = proj.shape[0]
    base = 0
    ng = len(POOL_WINDOWS)

    def halo_row(i):
        return jnp.maximum(i * (tr // HALO) - 1, 0)

    return pl.pallas_call(
        _pool_kernel,
        grid=(m // tr, ng),
        in_specs=[pl.BlockSpec((tr, POOL_GROUP), lambda i, g: (i, base + g)),
                  pl.BlockSpec((HALO, POOL_GROUP), lambda i, g: (halo_row(i), base + g)),
                  pl.BlockSpec((None, None, POOL_GROUP, POOL_GROUP), lambda i, g: (layer, g, 0, 0)),
                  pl.BlockSpec((None, 1, POOL_GROUP), lambda i, g: (layer, 0, g))],
        out_specs=pl.BlockSpec((tr, POOL_GROUP), lambda i, g: (i, g)),
        out_shape=jax.ShapeDtypeStruct((m, BRANCH), BF16),
        scratch_shapes=[pltpu.VMEM((HALO + tr, POOL_GROUP), F32)],
        compiler_params=_cp("parallel", "parallel"),
        name="pool_branch",
    )(proj, proj, pool_w, pool_scale)


def _store_value_t(vt_ref, v, heads):
    tr = v.shape[0]
    vt = v.T.astype(vt_ref.dtype)
    ones = jnp.ones((V_ROWS - LANES, tr), vt_ref.dtype)
    for h in range(heads):
        vt_ref[0, h * V_ROWS:h * V_ROWS + LANES, :] = vt[h * LANES:(h + 1) * LANES, :]
        vt_ref[0, h * V_ROWS + LANES:(h + 1) * V_ROWS, :] = ones


def _mla_prep_kernel(cq_ref, ckv_ref, sm_ref, gq_ref, gkv_ref, wuq_ref, wukv_ref,
                     cos_ref, sin_ref, qt_ref, kn_ref, kpe_ref, vt_ref):
    cos = cos_ref[...]
    sin = sin_ref[...]
    scale = (MLA_NOPE + MLA_ROPE) ** -0.5 * LOG2E
    cqn = _rms(cq_ref[...].astype(F32), gq_ref[...]).astype(BF16)
    q = jnp.dot(cqn, wuq_ref[...], preferred_element_type=F32)
    for h in range(MLA_HEADS):
        lo = 2 * h * LANES
        qt_ref[lo:lo + LANES, :] = (q[:, lo:lo + LANES] * scale).T.astype(qt_ref.dtype)
        pe = _rope_half64(q[:, lo + LANES:lo + 2 * LANES], cos, sin)
        qt_ref[lo + LANES:lo + 2 * LANES, :] = (pe * scale).T.astype(qt_ref.dtype)
    ckvn = _rms(ckv_ref[...].astype(F32), gkv_ref[...]).astype(BF16)
    kv = jnp.dot(ckvn, wukv_ref[...], preferred_element_type=F32)
    kn_ref[...] = kv[:, :BRANCH].astype(kn_ref.dtype)
    _store_value_t(vt_ref, kv[:, BRANCH:], MLA_HEADS)
    kpe_ref[...] = _rope_half64(sm_ref[:, 0:LANES], cos, sin).astype(kpe_ref.dtype)


def _mla_prep(proj, small, gq, gkv, wuq, wukv, cos64, sin64, layer, tr):
    m = proj.shape[0]
    qw = 2 * LANES * MLA_HEADS
    return pl.pallas_call(
        _mla_prep_kernel,
        grid=(m // tr,),
        in_specs=[pl.BlockSpec((tr, MLA_Q_RANK), lambda i: (i, 3 * BRANCH // MLA_Q_RANK)),
                  pl.BlockSpec((tr, MLA_KV_RANK), lambda i: (i, (3 * BRANCH + MLA_Q_RANK) // MLA_KV_RANK)),
                  pl.BlockSpec((tr, SMALL_COLS), lambda i: (i, 0)),
                  pl.BlockSpec((None, 1, MLA_Q_RANK), lambda i: (layer, 0, 0)),
                  pl.BlockSpec((None, 1, MLA_KV_RANK), lambda i: (layer, 0, 0)),
                  pl.BlockSpec((MLA_Q_RANK, qw), lambda i: (0, 0)),
                  pl.BlockSpec((MLA_KV_RANK, 2 * BRANCH), lambda i: (0, 0)),
                  pl.BlockSpec((tr, LANES), lambda i: (i, 0)),
                  pl.BlockSpec((tr, LANES), lambda i: (i, 0))],
        out_specs=[pl.BlockSpec((qw, tr), lambda i: (0, i)),
                   pl.BlockSpec((tr, BRANCH), lambda i: (i, 0)),
                   pl.BlockSpec((tr, LANES), lambda i: (i, 0)),
                   pl.BlockSpec((1, MLA_HEADS * V_ROWS, tr), lambda i: (i, 0, 0))],
        out_shape=[jax.ShapeDtypeStruct((qw, m), BF16),
                   jax.ShapeDtypeStruct((m, BRANCH), BF16),
                   jax.ShapeDtypeStruct((m, LANES), BF16),
                   jax.ShapeDtypeStruct((m // tr, MLA_HEADS * V_ROWS, tr), BF16)],
        compiler_params=_cp("parallel"),
        name="mla_prep",
    )(proj, proj, small, gq, gkv, wuq, wukv, cos64, sin64)


def _dsa_prep_kernel(dq_ref, dk_ref, dv_ref, iq_ref, sm_ref, c128_ref, s128_ref, c64_ref, s64_ref,
                     qt_ref, k_ref, vt_ref, iqr_ref, ik_ref, wt_ref):
    c128, s128 = c128_ref[...], s128_ref[...]
    c64, s64 = c64_ref[...], s64_ref[...]
    scale = DSA_DIM ** -0.5 * LOG2E
    for h in range(BRANCH // LANES):
        sl = slice(h * LANES, (h + 1) * LANES)
        qt_ref[sl, :] = (_rope_half128(dq_ref[:, sl].astype(F32), c128, s128) * scale).T.astype(qt_ref.dtype)
        k_ref[:, sl] = _rope_half128(dk_ref[:, sl].astype(F32), c128, s128).astype(k_ref.dtype)
        iqr_ref[:, sl] = _rope_half64(iq_ref[:, sl].astype(F32), c64, s64).astype(iqr_ref.dtype)
    _store_value_t(vt_ref, dv_ref[...].astype(F32), DSA_HEADS)
    tail = sm_ref[:, LANES:2 * LANES]
    lane = lax.broadcasted_iota(jnp.int32, tail.shape, 1)
    ik_lo = jnp.where(lane < IDX_DIM, _rope_half64(tail, c64, s64), 0.0)
    ik_ref[:, 0:LANES] = ik_lo.astype(ik_ref.dtype)
    ik_ref[:, LANES:2 * LANES] = pltpu.roll(ik_lo, IDX_DIM, axis=1).astype(ik_ref.dtype)
    idx_w_scale = (IDX_HEADS ** -0.5) * (IDX_DIM ** -0.5)
    wt_ref[...] = (tail * idx_w_scale).T


def _dsa_prep(proj_b, small, c128, s128, c64, s64, tr):
    m = proj_b.shape[0]
    col = lambda c: pl.BlockSpec((tr, BRANCH), lambda i: (i, c))
    tab = pl.BlockSpec((tr, LANES), lambda i: (i, 0))
    return pl.pallas_call(
        _dsa_prep_kernel,
        grid=(m // tr,),
        in_specs=[col(0), col(1), col(2), col(3),
                  pl.BlockSpec((tr, SMALL_COLS), lambda i: (i, 0)),
                  tab, tab, tab, tab],
        out_specs=[pl.BlockSpec((BRANCH, tr), lambda i: (0, i)),
                   pl.BlockSpec((tr, BRANCH), lambda i: (i, 0)),
                   pl.BlockSpec((1, DSA_HEADS * V_ROWS, tr), lambda i: (i, 0, 0)),
                   pl.BlockSpec((tr, BRANCH), lambda i: (i, 0)),
                   pl.BlockSpec((tr, 2 * LANES), lambda i: (i, 0)),
                   pl.BlockSpec((LANES, tr), lambda i: (0, i))],
        out_shape=[jax.ShapeDtypeStruct((BRANCH, m), BF16),
                   jax.ShapeDtypeStruct((m, BRANCH), BF16),
                   jax.ShapeDtypeStruct((m // tr, DSA_HEADS * V_ROWS, tr), BF16),
                   jax.ShapeDtypeStruct((m, BRANCH), BF16),
                   jax.ShapeDtypeStruct((m, 2 * LANES), BF16),
                   jax.ShapeDtypeStruct((LANES, m), F32)],
        compiler_params=_cp("parallel"),
        name="dsa_prep",
    )(proj_b, proj_b, proj_b, proj_b, small, c128, s128, c64, s64)


def _float_to_ordered_int(s):
    b = lax.bitcast_convert_type(s, jnp.int32)
    return b ^ ((b >> 31) & jnp.int32(0x7FFFFFFF))


def _indexer_kernel(iq_ref, ik_ref, wt_ref, bias_ref, key_ref, cut_ref, slot_ref, *, topk):
    tq = iq_ref.shape[0]
    total_rows = ik_ref.shape[0]
    tk = IDX_K_TILE
    i = pl.program_id(0)
    n_tiles = (i * tq + tq) // tk
    n_rows = n_tiles * tk
    qpos = i * tq + lax.broadcasted_iota(jnp.int32, (1, tq), 1)
    int_min = jnp.int32(-2 ** 31)

    def score_tile(kt, carry):
        start = pl.multiple_of(kt * tk, tk)
        ik_lo = ik_ref[pl.ds(start, tk), 0:LANES]
        ik_hi = ik_ref[pl.ds(start, tk), LANES:2 * LANES]
        acc = jnp.zeros((tk, tq), F32)
        for j in range(IDX_HEADS // 2):
            qpair = iq_ref[:, j * LANES:(j + 1) * LANES]
            for half, ik in enumerate((ik_lo, ik_hi)):
                g = 2 * j + half
                dots = lax.dot_general(ik, qpair, (((1,), (1,)), ((), ())),
                                       preferred_element_type=F32)
                acc = acc + jnp.maximum(dots, 0.0) * wt_ref[IDX_DIM + g:IDX_DIM + g + 1, :]
        acc = acc + 0.0
        kpos = start + lax.broadcasted_iota(jnp.int32, (tk, 1), 0)
        key_ref[pl.ds(start, tk), :] = jnp.where(kpos <= qpos, _float_to_ordered_int(acc), int_min)
        return carry

    lax.fori_loop(0, n_tiles, score_tile, 0)

    n_chunks = n_rows // COUNT_CHUNK

    def count(pred, with_pos=False):
        def body(c, accs):
            start = pl.multiple_of(c * COUNT_CHUNK, COUNT_CHUNK)
            accs = list(accs)
            chunk = key_ref[pl.ds(start, COUNT_CHUNK), :]
            for r in range(COUNT_CHUNK // 8):
                blk = chunk[8 * r:8 * r + 8, :]
                if with_pos:
                    pos = start + 8 * r + lax.broadcasted_iota(jnp.int32, (8, 1), 0)
                    hit = pred(blk, pos)
                else:
                    hit = pred(blk)
                accs[r % COUNT_ACCS] = accs[r % COUNT_ACCS] + jnp.where(hit, 1, 0).astype(jnp.int32)
            return tuple(accs)
        zero = jnp.zeros((8, tq), jnp.int32)
        accs = lax.fori_loop(0, n_chunks, body, (zero,) * COUNT_ACCS)
        return jnp.sum(functools.reduce(lambda a, b: a + b, accs), axis=0, keepdims=True)

    assert topk <= COUNT_CHUNK
    slot_ref[...] = key_ref[0:COUNT_CHUNK, :]

    def fold_slots(c, carry):
        start = pl.multiple_of(c * COUNT_CHUNK, COUNT_CHUNK)
        slot_ref[...] = jnp.maximum(slot_ref[...], key_ref[pl.ds(start, COUNT_CHUNK), :])
        return carry

    lax.fori_loop(1, n_chunks, fold_slots, 0)
    slots = slot_ref[...]
    hi = jnp.max(slots, axis=0, keepdims=True)
    lo = jnp.min(slots, axis=0, keepdims=True)
    n_bits = 32 - jnp.min(lax.clz(hi - lo))

    def bit_step(b, tau):
        cand = tau + jnp.left_shift(jnp.int32(1), n_bits - 1 - b)
        cnt = count(lambda blk: blk >= cand)
        return jnp.where((cnt >= topk) & (cand > tau), cand, tau)

    tau = lax.fori_loop(0, n_bits, bit_step, lo)

    n_gt = count(lambda blk: blk > tau)
    n_eq = count(lambda blk: blk == tau)
    need = topk - n_gt
    cut_bits = 14
    cut_ref[...] = jnp.full((1, tq), 1 << cut_bits, jnp.int32)

    @pl.when(jnp.max(n_eq - need) > 0)
    def _():
        def cut_step(b, cut):
            cand = cut + jnp.left_shift(jnp.int32(1), cut_bits - 1 - b)
            cnt = count(lambda blk, pos: (blk == tau) & (pos < cand), with_pos=True)
            return jnp.where(cnt <= need, cand, cut)
        cut_ref[...] = lax.fori_loop(0, cut_bits, cut_step, jnp.zeros((1, tq), jnp.int32))

    cut = cut_ref[...]

    def write_sel(c, carry):
        start = pl.multiple_of(c * WRITE_CHUNK, WRITE_CHUNK)
        blk = key_ref[pl.ds(start, WRITE_CHUNK), :]
        pos = start + lax.broadcasted_iota(jnp.int32, (WRITE_CHUNK, 1), 0)
        sel = ((blk > tau) | ((blk == tau) & (pos < cut))) & (pos <= qpos)
        bias_ref[pl.ds(start, WRITE_CHUNK), :] = jnp.where(sel, 0.0, NEG).astype(bias_ref.dtype)
        return carry

    lax.fori_loop(0, n_rows // WRITE_CHUNK, write_sel, 0)

    def write_neg(c, carry):
        start = pl.multiple_of(c * WRITE_CHUNK, WRITE_CHUNK)
        bias_ref[pl.ds(start, WRITE_CHUNK), :] = jnp.full((WRITE_CHUNK, tq), NEG, bias_ref.dtype)
        return carry

    lax.fori_loop(n_rows // WRITE_CHUNK, total_rows // WRITE_CHUNK, write_neg, 0)


def _indexer(iq_r, ik_ab, wt, topk):
    m = iq_r.shape[0]
    return pl.pallas_call(
        functools.partial(_indexer_kernel, topk=topk),
        grid=(m // Q_TILE,),
        in_specs=[pl.BlockSpec((Q_TILE, BRANCH), lambda i: (i, 0)),
                  pl.BlockSpec((m, 2 * LANES), lambda i: (0, 0)),
                  pl.BlockSpec((LANES, Q_TILE), lambda i: (0, i))],
        out_specs=pl.BlockSpec((m, Q_TILE), lambda i: (0, i)),
        out_shape=jax.ShapeDtypeStruct((m, m), BF16),
        scratch_shapes=[pltpu.VMEM((m, Q_TILE), jnp.int32),
                        pltpu.VMEM((1, Q_TILE), jnp.int32),
                        pltpu.VMEM((COUNT_CHUNK, Q_TILE), jnp.int32)],
        compiler_params=_cp("parallel"),
        name="dsa_indexer",
    )(iq_r, ik_ab, wt)


def _flash_kernel(*refs, q_axis, has_kpe, has_bias):
    refs = list(refs)
    q_ref, k_ref = refs[0], refs[1]
    pos = 2
    kpe_ref = bias_ref = None
    if has_kpe:
        kpe_ref = refs[pos]
        pos += 1
    vt_ref = refs[pos]
    pos += 1
    if has_bias:
        bias_ref = refs[pos]
        pos += 1
    o_ref, sa_ref, sb_ref = refs[pos], refs[pos + 1], refs[pos + 2]

    tq = q_ref.shape[1]
    heads = vt_ref.shape[1] // V_ROWS
    tk = vt_ref.shape[2]
    dq = q_ref.shape[0] // heads
    assert tk % tq == 0
    i = pl.program_id(q_axis)
    n_tiles = (i * tq + tq + tk - 1) // tk
    qpos = i * tq + lax.broadcasted_iota(jnp.int32, (1, tq), 1)

    def compute_scores(kt, s_ref):
        start = pl.multiple_of(kt * tk, tk)
        kpe = kpe_ref[pl.ds(start, tk), :] if has_kpe else None
        for h in range(heads):
            k = k_ref[pl.ds(start, tk), h * LANES:(h + 1) * LANES]
            if has_kpe:
                k = jnp.concatenate([k, kpe], axis=1)
            s_ref[h] = jnp.dot(k, q_ref[h * dq:(h + 1) * dq, :], preferred_element_type=F32)

    def consume_scores(kt, s_ref, carry, causal_mask):
        start = pl.multiple_of(kt * tk, tk)
        bias = bias_ref[pl.ds(start, tk), :].astype(F32) if has_bias else None
        new = []
        for h in range(heads):
            m_run, acc = carry[h]
            s = s_ref[h]
            if has_bias:
                s = s + bias
            if causal_mask:
                kpos = start + lax.broadcasted_iota(jnp.int32, (tk, 1), 0)
                s = jnp.where(kpos <= qpos, s, NEG)
            m_new = jnp.maximum(m_run, jnp.max(s, axis=0, keepdims=True))
            alpha = jnp.exp2(m_run - m_new)
            p = jnp.exp2(s - m_new).astype(BF16)
            pv = jnp.dot(vt_ref[kt, h * V_ROWS:(h + 1) * V_ROWS, :], p, preferred_element_type=F32)
            new.append((m_new, alpha * acc + pv))
        return tuple(new)

    def double_step(u, carry):
        kt = 2 * u
        compute_scores(kt + 1, sb_ref)
        carry = consume_scores(kt, sa_ref, carry, False)
        compute_scores(kt + 2, sa_ref)
        return consume_scores(kt + 1, sb_ref, carry, False)

    mask_last = not has_bias
    last = n_tiles - 1

    def odd_tail(carry):
        compute_scores(last, sb_ref)
        carry = consume_scores(last - 1, sa_ref, carry, False)
        return consume_scores(last, sb_ref, carry, mask_last)

    def even_tail(carry):
        return consume_scores(last, sa_ref, carry, mask_last)

    init = tuple((jnp.full((1, tq), NEG, F32), jnp.zeros((V_ROWS, tq), F32)) for _ in range(heads))
    compute_scores(0, sa_ref)
    carry = lax.fori_loop(0, last // 2, double_step, init)
    carry = lax.cond(last % 2 == 1, odd_tail, even_tail, carry)
    for h in range(heads):
        acc = carry[h][1]
        out = acc[0:LANES, :] / acc[LANES:LANES + 1, :]
        o_ref[:, h * LANES:(h + 1) * LANES] = out.T.astype(o_ref.dtype)


def _score_scratch(heads, tk):
    return [pltpu.VMEM((heads, tk, Q_TILE), F32), pltpu.VMEM((heads, tk, Q_TILE), F32)]


def _mla_attention(q, kn, kpe, vt):
    m = kn.shape[0]
    n_kt, _, tk = vt.shape
    hp = HEADS_PER_STEP
    return pl.pallas_call(
        functools.partial(_flash_kernel, q_axis=1, has_kpe=True, has_bias=False),
        grid=(MLA_HEADS // hp, m // Q_TILE),
        in_specs=[pl.BlockSpec((hp * 2 * LANES, Q_TILE), lambda h, i: (h, i)),
                  pl.BlockSpec((m, hp * LANES), lambda h, i: (0, h)),
                  pl.BlockSpec((m, LANES), lambda h, i: (0, 0)),
                  pl.BlockSpec((n_kt, hp * V_ROWS, tk), lambda h, i: (0, h, 0))],
        out_specs=pl.BlockSpec((Q_TILE, hp * MLA_V), lambda h, i: (i, h)),
        out_shape=jax.ShapeDtypeStruct((m, BRANCH), BF16),
        scratch_shapes=_score_scratch(hp, tk),
        compiler_params=_cp("parallel", "parallel"),
        name="mla_attention",
    )(q, kn, kpe, vt)


def _dsa_attention(q, k, vt, bias):
    m = k.shape[0]
    n_kt, _, tk = vt.shape
    hp = HEADS_PER_STEP
    return pl.pallas_call(
        functools.partial(_flash_kernel, q_axis=1, has_kpe=False, has_bias=True),
        grid=(DSA_HEADS // hp, m // Q_TILE),
        in_specs=[pl.BlockSpec((hp * DSA_DIM, Q_TILE), lambda h, i: (h, i)),
                  pl.BlockSpec((m, hp * DSA_DIM), lambda h, i: (0, h)),
                  pl.BlockSpec((n_kt, hp * V_ROWS, tk), lambda h, i: (0, h, 0)),
                  pl.BlockSpec((m, Q_TILE), lambda h, i: (0, i))],
        out_specs=pl.BlockSpec((Q_TILE, hp * DSA_DIM), lambda h, i: (i, h)),
        out_shape=jax.ShapeDtypeStruct((m, BRANCH), BF16),
        scratch_shapes=_score_scratch(hp, tk),
        compiler_params=_cp("parallel", "parallel"),
        name="dsa_attention",
    )(q, k, vt, bias)


def _rope_tables(rows, dim):
    inv = 1.0 / jnp.power(ROPE_THETA, jnp.arange(0, dim, 2, dtype=F32) / dim)
    ang = jnp.arange(rows, dtype=F32)[:, None] * inv[None, :]
    cos, sin = jnp.cos(ang), jnp.sin(ang)
    reps = LANES // dim
    return (jnp.tile(jnp.concatenate([cos, cos], axis=1), (1, reps)),
            jnp.tile(jnp.concatenate([-sin, sin], axis=1), (1, reps)))


W_IN_KR = 3 * BRANCH + MLA_Q_RANK + MLA_KV_RANK
W_IN_DQ = W_IN_KR + MLA_ROPE
W_IN_IK = W_IN_DQ + 4 * BRANCH
W_IN_IW = W_IN_IK + IDX_DIM
W_IN_PU = W_IN_IW + IDX_HEADS


def _small_w_in(w_t):
    zeros = lambda n: jnp.zeros((w_t.shape[0], n, w_t.shape[2]), w_t.dtype)
    return jnp.concatenate([w_t[:, W_IN_KR:W_IN_DQ], zeros(LANES - MLA_ROPE),
                            w_t[:, W_IN_IK:W_IN_IW], w_t[:, W_IN_IW:W_IN_PU],
                            zeros(LANES - IDX_DIM - IDX_HEADS)], axis=1)


def _layout_w_uq(w):
    w3 = w.reshape(MLA_Q_RANK, MLA_HEADS, MLA_NOPE + MLA_ROPE)
    w3 = jnp.pad(w3, ((0, 0), (0, 0), (0, 2 * LANES - MLA_NOPE - MLA_ROPE)))
    return w3.reshape(MLA_Q_RANK, MLA_HEADS * 2 * LANES).astype(BF16)


def _layout_w_ukv(w):
    w4 = w.reshape(MLA_KV_RANK, MLA_HEADS, 2, MLA_NOPE)
    return w4.transpose(0, 2, 1, 3).reshape(MLA_KV_RANK, 2 * BRANCH).astype(BF16)


def _forward(x, meta_tokens, norm_mix_pre, norm_mix_post, norm_ffn_pre, norm_ffn_post,
             w_in, conv_w, mla_q_norm, mla_w_uq, mla_kv_norm, mla_w_ukv, pool_w, pool_scale,
             w_branch, w_gate, b_gate, w_out, ffn_w_gate, ffn_w_up, ffn_w_down):
    assert x.shape[0] == 1 and x.shape[2] == D_MODEL
    depth = w_in.shape[0]
    seq = x.shape[1]
    length = N_META + seq
    topk = min(IDX_TOPK_MAX, length // 4)
    rows = -(-length // ROW_TILE) * ROW_TILE
    assert rows % Q_TILE == 0 and rows % K_TILE == 0 and rows % NORM_TILE == 0
    assert rows < (1 << 14)

    c64, s64 = _rope_tables(rows, 64)
    c128, s128 = _rope_tables(rows, 128)
    assert W_IN_KR == HALF_COLS
    w_in_t = jnp.swapaxes(w_in, 1, 2)
    w_in_small = _small_w_in(w_in_t)
    tn = 512
    wide = 2 * tn
    gate_tiles = D_MODEL // wide

    h, xn = _assemble_prenorm(x[0], meta_tokens.astype(F32), norm_mix_pre[0][None], rows)
    for l in range(depth):
        at_layer = lambda j, l=l: (l,)
        tile = lambda j: j
        proj_a = _wide_proj(xn, w_in_t, at_layer, tile, HALF_COLS // wide, ROW_TILE, tn, BF16,
                            "in_proj_a", weights_nk=True)
        proj_b = _wide_proj(xn, w_in_t, at_layer, tile, 4 * BRANCH // wide, ROW_TILE, tn, BF16,
                            "in_proj_b", weights_nk=True,
                            first_half_block=W_IN_DQ // tn, row_shift=W_IN_DQ % tn)
        proj_p = _wide_proj(xn, w_in_t, at_layer, tile, BRANCH // wide, ROW_TILE, tn, BF16,
                            "in_proj_pool", weights_nk=True,
                            first_half_block=W_IN_PU // tn, row_shift=W_IN_PU % tn)
        small = _stacked_proj(xn, w_in_small, at_layer, tile, 1, ROW_TILE, SMALL_COLS, F32,
                              "in_proj_small", weights_nk=True)
        gates = _wide_proj(xn, w_gate,
                           lambda j, l=l: (l, j // gate_tiles), lambda j: j % gate_tiles,
                           N_BRANCH * gate_tiles, ROW_TILE, tn, BF16, "gates",
                           bias=b_gate[:, :, None, :])

        y_conv = _conv_branch(proj_a, conv_w, l, ROW_TILE)
        y_pool = _pool_branch(proj_p, pool_w, pool_scale[:, None, :], l, ROW_TILE)

        q_m, kn_m, kpe_m, vt_m = _mla_prep(
            proj_a, small, mla_q_norm[:, None, :], mla_kv_norm[:, None, :],
            _layout_w_uq(mla_w_uq[l]), _layout_w_ukv(mla_w_ukv[l]), c64, s64, l, K_TILE)
        y_mla = _mla_attention(q_m, kn_m, kpe_m, vt_m)

        q_d, k_d, vt_d, iq_r, ik_ab, wt = _dsa_prep(proj_b, small, c128, s128, c64, s64, K_TILE)
        bias = _indexer(iq_r, ik_ab, wt, topk)
        y_dsa = _dsa_attention(q_d, k_d, vt_d, bias)

        merged = _gated_merge((y_conv, y_mla, y_dsa, y_pool), gates, w_branch, l, ROW_TILE, tn)
        mix = _wide_proj(merged, w_out, at_layer, tile, D_MODEL // wide, ROW_TILE, tn, F32, "out_proj")
        h, xn = _resid_norm(h, mix, norm_mix_post[l][None], norm_ffn_pre[l][None])

        act = _swiglu(xn, ffn_w_gate, ffn_w_up, l, ROW_TILE, 256)
        f = _stacked_proj(act, ffn_w_down, at_layer, tile, D_MODEL // tn, 384, tn, F32, "ffn_down",
                          single_buffer_weight=True)
        if l + 1 < depth:
            h, xn = _resid_norm(h, f, norm_ffn_post[l][None], norm_mix_pre[l + 1][None])
        else:
            out = _resid_out(h, f, norm_ffn_post[l][None], seq)

    return out[None]


def kernel(x, meta_tokens, norm_mix_pre, norm_mix_post, norm_ffn_pre, norm_ffn_post, w_in, conv_w, mla_q_norm, mla_w_uq, mla_kv_norm, mla_w_ukv, pool_w, pool_scale, w_branch, w_gate, b_gate, w_out, ffn_w_gate, ffn_w_up, ffn_w_down):
    return _forward(x, meta_tokens, norm_mix_pre, norm_mix_post, norm_ffn_pre, norm_ffn_post,
                    w_in, conv_w, mla_q_norm, mla_w_uq, mla_kv_norm, mla_w_ukv, pool_w, pool_scale,
                    w_branch, w_gate, b_gate, w_out, ffn_w_gate, ffn_w_up, ffn_w_down)
```

```python
import functools
import math

import jax
import jax.numpy as jnp
from jax import lax
from jax.experimental import pallas as pl
from jax.experimental.pallas import tpu as pltpu

F32 = jnp.float32
BF16 = jnp.bfloat16

D_MODEL = 4096
N_META = 16
ROPE_THETA = 10000.0
EPS = 1e-6
N_BRANCH = 4
BRANCH = 1024
CONV_K = 3
MLA_NOPE, MLA_ROPE, MLA_V, MLA_HEADS = 128, 64, 128, 8
MLA_Q_RANK, MLA_KV_RANK = 1536, 512
DSA_DIM, DSA_HEADS = 128, 8
IDX_HEADS, IDX_DIM, IDX_TOPK_MAX = 16, 64, 256
POOL_WINDOWS = (2, 4, 8, 16)
POOL_GROUP = 256
D_FF = 11008

LANES = 128
HALO = 16
ROW_TILE = 768
Q_TILE = 256
K_TILE = 768
V_ROWS = 144
HEADS_PER_STEP = 4
IDX_K_TILE = 256
NEXT_ROWS = 256
COUNT_CHUNK = 256
COUNT_ACCS = 4
WRITE_CHUNK = 64
NORM_TILE = 192
NEG = -1e30
LOG2E = math.log2(math.e)
VMEM_LIMIT = 58 * 1024 * 1024

HALF_COLS = 5120
SMALL_COLS = 256


def _cp(*sem):
    return pltpu.CompilerParams(dimension_semantics=sem, vmem_limit_bytes=VMEM_LIMIT)


def _sigmoid(x):
    return 1.0 / (1.0 + jnp.exp(-x))


def _rms(x, g):
    return x * lax.rsqrt(jnp.mean(x * x, axis=-1, keepdims=True) + EPS) * g


def _matmul_w(a, wbf, weights_nk):
    if weights_nk:
        return lax.dot_general(a, wbf, (((1,), (1,)), ((), ())), preferred_element_type=F32)
    return jnp.dot(a, wbf, preferred_element_type=F32)


def _proj_kernel(a_ref, w_ref, o_ref, wbf_ref, *, weights_nk):
    @pl.when(pl.program_id(1) == 0)
    def _():
        wbf_ref[...] = w_ref[...].astype(wbf_ref.dtype)

    o_ref[...] = _matmul_w(a_ref[...], wbf_ref[...], weights_nk).astype(o_ref.dtype)


def _stacked_proj(a, w, lead_of, col_of, n_tiles, tm, tn, out_dtype, name,
                  weights_nk=False):
    m, k = a.shape
    lead = (None,) * (w.ndim - 2)
    if weights_nk:
        assert w.shape[-1] == k
        w_block, w_index = (tn, k), lambda j, i: (*lead_of(j), col_of(j), 0)
    else:
        assert w.shape[-2] == k
        w_block, w_index = (k, tn), lambda j, i: (*lead_of(j), 0, col_of(j))
    return pl.pallas_call(
        functools.partial(_proj_kernel, weights_nk=weights_nk),
        grid=(n_tiles, m // tm),
        in_specs=[pl.BlockSpec((tm, k), lambda j, i: (i, 0)),
                  pl.BlockSpec(lead + w_block, w_index)],
        out_specs=pl.BlockSpec((tm, tn), lambda j, i: (i, j)),
        out_shape=jax.ShapeDtypeStruct((m, n_tiles * tn), out_dtype),
        scratch_shapes=[pltpu.VMEM(w_block, BF16)],
        compiler_params=_cp("parallel", "arbitrary"),
        name=name,
    )(a, w)


def _wide_proj_kernel(*refs, has_bias, weights_nk, row_shift):
    refs = list(refs)
    a_ref, w_ref = refs[0], refs[1]
    pos = 2
    next_ref = b_ref = None
    if row_shift:
        next_ref = refs[pos]
        pos += 1
    if has_bias:
        b_ref = refs[pos]
        pos += 1
    o_ref, wbf_ref = refs[pos], refs[pos + 1]
    half = o_ref.shape[1] // 2
    i = pl.program_id(1)

    def cast_half(lo):
        if not weights_nk:
            wbf_ref[:, lo:lo + half] = w_ref[...].astype(wbf_ref.dtype)
        elif not row_shift:
            wbf_ref[lo:lo + half, :] = w_ref[...].astype(wbf_ref.dtype)
        else:
            keep = half - row_shift
            wbf_ref[lo:lo + keep, :] = w_ref[row_shift:half, :].astype(wbf_ref.dtype)
            wbf_ref[lo + keep:lo + half, :] = next_ref[0:row_shift, :].astype(wbf_ref.dtype)

    def emit(lo, width):
        wbf = wbf_ref[lo:lo + width, :] if weights_nk else wbf_ref[:, lo:lo + width]
        z = _matmul_w(a_ref[...], wbf, weights_nk)
        if has_bias:
            z = _sigmoid(z + b_ref[:, lo:lo + width])
        o_ref[:, lo:lo + width] = z.astype(o_ref.dtype)

    @pl.when(i == 0)
    def _():
        cast_half(0)
        emit(0, half)

    @pl.when(i == 1)
    def _():
        cast_half(half)
        emit(half, half)

    @pl.when(i >= 2)
    def _():
        emit(0, 2 * half)


def _wide_proj(a, w, lead_of, col_of, n_tiles, tm, half, out_dtype, name, bias=None,
               weights_nk=False, first_half_block=0, row_shift=0):
    m, k = a.shape
    assert w.shape[-1 if weights_nk else -2] == k
    assert row_shift == 0 or (weights_nk and row_shift % 8 == 0 and row_shift <= NEXT_ROWS)
    lead = (None,) * (w.ndim - 2)
    row = lambda i: jnp.maximum(i - 1, 0)

    def weight_window(j, i):
        jw = jnp.where(i >= 2, jnp.minimum(j + 1, n_tiles - 1), j)
        return lead_of(jw), first_half_block + 2 * col_of(jw) + jnp.where(i == 1, 1, 0)

    def w_index(j, i):
        stacked, hb = weight_window(j, i)
        return (*stacked, hb, 0) if weights_nk else (*stacked, 0, hb)

    in_specs = [pl.BlockSpec((tm, k), lambda j, i: (row(i), 0)),
                pl.BlockSpec(lead + ((half, k) if weights_nk else (k, half)), w_index)]
    args = [a, w]
    if row_shift:
        def next_index(j, i):
            stacked, hb = weight_window(j, i)
            return (*stacked, (hb + 1) * (half // NEXT_ROWS), 0)

        in_specs.append(pl.BlockSpec(lead + (NEXT_ROWS, k), next_index))
        args.append(w)
    if bias is not None:
        in_specs.append(pl.BlockSpec(lead + (1, 2 * half), lambda j, i: (*lead_of(j), 0, col_of(j))))
        args.append(bias)
    return pl.pallas_call(
        functools.partial(_wide_proj_kernel, has_bias=bias is not None, weights_nk=weights_nk,
                          row_shift=row_shift),
        grid=(n_tiles, m // tm + 1),
        in_specs=in_specs,
        out_specs=pl.BlockSpec((tm, 2 * half), lambda j, i: (row(i), j)),
        out_shape=jax.ShapeDtypeStruct((m, n_tiles * 2 * half), out_dtype),
        scratch_shapes=[pltpu.VMEM((2 * half, k) if weights_nk else (k, 2 * half), BF16)],
        compiler_params=_cp("parallel", "arbitrary"),
        name=name,
    )(*args)


def _swiglu_kernel(a_ref, wg_ref, wu_ref, o_ref, wgbf_ref, wubf_ref, *, n_half_blocks):
    half = wg_ref.shape[1]
    j = pl.program_id(0)
    i = pl.program_id(1)

    def cast_half(lo):
        wgbf_ref[:, lo:lo + half] = wg_ref[...].astype(wgbf_ref.dtype)
        wubf_ref[:, lo:lo + half] = wu_ref[...].astype(wubf_ref.dtype)

    def emit(lo, width):
        a = a_ref[...]
        g = jnp.dot(a, wgbf_ref[:, lo:lo + width], preferred_element_type=F32)
        u = jnp.dot(a, wubf_ref[:, lo:lo + width], preferred_element_type=F32)
        o_ref[:, lo:lo + width] = (g * _sigmoid(g) * u).astype(o_ref.dtype)

    has_right = 2 * j + 1 < n_half_blocks

    @pl.when(i == 0)
    def _():
        cast_half(0)
        emit(0, half)

    @pl.when((i == 1) & has_right)
    def _():
        cast_half(half)
        emit(half, half)

    @pl.when((i >= 2) & has_right)
    def _():
        emit(0, 2 * half)

    @pl.when((i >= 2) & jnp.logical_not(has_right))
    def _():
        emit(0, half)


def _swiglu(a, wg, wu, layer, tm, half):
    m, k = a.shape
    n = wg.shape[2]
    n_half_blocks = n // half
    assert n_half_blocks * half == n
    n_tiles = -(-n_half_blocks // 2)
    row = lambda i: jnp.maximum(i - 1, 0)
    def w_index(j, i):
        jw = jnp.where(i >= 2, jnp.minimum(j + 1, n_tiles - 1), j)
        return (layer, 0, jnp.minimum(2 * jw + jnp.where(i == 1, 1, 0), n_half_blocks - 1))

    w_spec = pl.BlockSpec((None, k, half), w_index)
    return pl.pallas_call(
        functools.partial(_swiglu_kernel, n_half_blocks=n_half_blocks),
        grid=(n_tiles, m // tm + 1),
        in_specs=[pl.BlockSpec((tm, k), lambda j, i: (row(i), 0)), w_spec, w_spec],
        out_specs=pl.BlockSpec((tm, 2 * half), lambda j, i: (row(i), j)),
        out_shape=jax.ShapeDtypeStruct((m, n), BF16),
        scratch_shapes=[pltpu.VMEM((k, 2 * half), BF16), pltpu.VMEM((k, 2 * half), BF16)],
        compiler_params=_cp("parallel", "arbitrary"),
        name="ffn_swiglu",
    )(a, wg, wu)


def _merge_kernel(y0_ref, y1_ref, y2_ref, y3_ref, g0_ref, g1_ref, g2_ref, g3_ref, w_ref,
                  o_ref, wbf_ref):
    @pl.when(pl.program_id(1) == 0)
    def _():
        wbf_ref[...] = w_ref[...].astype(wbf_ref.dtype)

    acc = None
    for b, (y_ref, g_ref) in enumerate(((y0_ref, g0_ref), (y1_ref, g1_ref),
                                        (y2_ref, g2_ref), (y3_ref, g3_ref))):
        val = g_ref[...].astype(F32) * jnp.dot(y_ref[...], wbf_ref[b], preferred_element_type=F32)
        acc = val if acc is None else acc + val
    o_ref[...] = acc.astype(o_ref.dtype)


def _gated_merge(ys, gates, w_branch, layer, tm, tn):
    m = ys[0].shape[0]
    nj = D_MODEL // tn
    y_spec = pl.BlockSpec((tm, BRANCH), lambda j, i: (i, 0))
    g_specs = [pl.BlockSpec((tm, tn), functools.partial(lambda j, i, b: (i, b * nj + j), b=b))
               for b in range(N_BRANCH)]
    return pl.pallas_call(
        _merge_kernel,
        grid=(nj, m // tm),
        in_specs=[y_spec] * N_BRANCH + g_specs
                 + [pl.BlockSpec((None, N_BRANCH, BRANCH, tn), lambda j, i: (layer, 0, 0, j))],
        out_specs=pl.BlockSpec((tm, tn), lambda j, i: (i, j)),
        out_shape=jax.ShapeDtypeStruct((m, D_MODEL), BF16),
        scratch_shapes=[pltpu.VMEM((N_BRANCH, BRANCH, tn), BF16)],
        compiler_params=_cp("parallel", "arbitrary"),
        name="gated_merge",
    )(*ys, gates, gates, gates, gates, w_branch)


def _assemble_kernel(x_ref, prev_ref, meta_ref, g_ref, h_ref, xn_ref, *, length):
    tr = x_ref.shape[0]
    i = pl.program_id(0)
    head = jnp.where(i == 0, meta_ref[...], prev_ref[...])
    tile = jnp.concatenate([head, x_ref[0:tr - N_META, :]], axis=0)
    pos = i * tr + lax.broadcasted_iota(jnp.int32, (tr, 1), 0)
    h = jnp.where(pos < length, tile, 0.0)
    h_ref[...] = h
    xn_ref[...] = _rms(h, g_ref[...]).astype(xn_ref.dtype)


def _assemble_prenorm(x2d, meta, g, rows):
    seq = x2d.shape[0]
    assert N_META == HALO and NORM_TILE % N_META == 0
    last_x = (seq - 1) // NORM_TILE
    last_prev = (seq - 1) // N_META
    per_tile = NORM_TILE // N_META
    row = pl.BlockSpec((NORM_TILE, D_MODEL), lambda i: (i, 0))
    return pl.pallas_call(
        functools.partial(_assemble_kernel, length=N_META + seq),
        grid=(rows // NORM_TILE,),
        in_specs=[pl.BlockSpec((NORM_TILE, D_MODEL), lambda i: (jnp.minimum(i, last_x), 0)),
                  pl.BlockSpec((N_META, D_MODEL),
                               lambda i: (jnp.clip(i * per_tile - 1, 0, last_prev), 0)),
                  pl.BlockSpec((N_META, D_MODEL), lambda i: (0, 0)),
                  pl.BlockSpec((1, D_MODEL), lambda i: (0, 0))],
        out_specs=[row, row],
        out_shape=[jax.ShapeDtypeStruct((rows, D_MODEL), F32),
                   jax.ShapeDtypeStruct((rows, D_MODEL), BF16)],
        compiler_params=_cp("parallel"),
        name="assemble_prenorm",
    )(x2d, x2d, meta, g)


def _resid_norm_kernel(h_ref, o_ref, gpost_ref, gnext_ref, hn_ref, xn_ref):
    hn = h_ref[...] + _rms(o_ref[...], gpost_ref[...])
    hn_ref[...] = hn
    xn_ref[...] = _rms(hn, gnext_ref[...]).astype(xn_ref.dtype)


def _resid_out_kernel(h_ref, o_ref, hnext_ref, onext_ref, gpost_ref, out_ref):
    g = gpost_ref[...]
    cur = h_ref[N_META:, :] + _rms(o_ref[N_META:, :], g)
    nxt = hnext_ref[...] + _rms(onext_ref[...], g)
    out_ref[...] = jnp.concatenate([cur, nxt], axis=0)


def _resid_norm(h, o, g_post, g_next):
    m = h.shape[0]
    row = pl.BlockSpec((NORM_TILE, D_MODEL), lambda i: (i, 0))
    gain = pl.BlockSpec((1, D_MODEL), lambda i: (0, 0))
    return pl.pallas_call(
        _resid_norm_kernel,
        grid=(m // NORM_TILE,),
        in_specs=[row, row, gain, gain],
        out_specs=[row, row],
        out_shape=[jax.ShapeDtypeStruct((m, D_MODEL), F32),
                   jax.ShapeDtypeStruct((m, D_MODEL), BF16)],
        compiler_params=_cp("parallel"),
        name="resid_norm",
    )(h, o, g_post, g_next)


def _resid_out(h, o, g_post, seq):
    m = h.shape[0]
    per_tile = NORM_TILE // N_META
    last_head = m // N_META - 1
    row = pl.BlockSpec((NORM_TILE, D_MODEL), lambda i: (i, 0))
    head = pl.BlockSpec((N_META, D_MODEL), lambda i: (jnp.minimum((i + 1) * per_tile, last_head), 0))
    return pl.pallas_call(
        _resid_out_kernel,
        grid=(-(-seq // NORM_TILE),),
        in_specs=[row, row, head, head, pl.BlockSpec((1, D_MODEL), lambda i: (0, 0))],
        out_specs=row,
        out_shape=jax.ShapeDtypeStruct((seq, D_MODEL), F32),
        compiler_params=_cp("parallel"),
        name="resid_out",
    )(h, o, h, o, g_post)


def _rope_half128(x, cos, sin_signed):
    return x * cos + pltpu.roll(x, 64, axis=1) * sin_signed


def _rope_half64(x, cos, sin_signed):
    lane = lax.broadcasted_iota(jnp.int32, x.shape, 1)
    partner = jnp.where((lane & 32) == 0, pltpu.roll(x, 96, axis=1), pltpu.roll(x, 32, axis=1))
    return x * cos + partner * sin_signed


def _conv_kernel(cb_ref, cc_ref, cu_ref, pc_ref, pu_ref, w_ref, y_ref, z_ref):
    tr = cb_ref.shape[0]
    i = pl.program_id(0)
    z = cc_ref[...].astype(F32) * cu_ref[...].astype(F32)
    zp = pc_ref[...].astype(F32) * pu_ref[...].astype(F32)
    z_ref[0:HALO, :] = jnp.where(i > 0, zp, 0.0)
    z_ref[HALO:HALO + tr, :] = z
    w = w_ref[...]
    conv = (w[2:3, :] * z
            + w[1:2, :] * z_ref[HALO - 1:HALO - 1 + tr, :]
            + w[0:1, :] * z_ref[HALO - 2:HALO - 2 + tr, :])
    y_ref[...] = (cb_ref[...].astype(F32) * conv).astype(y_ref.dtype)


def _conv_branch(proj, conv_w, layer, tr):
    m = proj.shape[0]
    cw = 256
    nb = BRANCH // cw

    def halo_row(i):
        return jnp.maximum(i * (tr // HALO) - 1, 0)

    return pl.pallas_call(
        _conv_kernel,
        grid=(m // tr, nb),
        in_specs=[pl.BlockSpec((tr, cw), lambda i, c: (i, c)),
                  pl.BlockSpec((tr, cw), lambda i, c: (i, nb + c)),
                  pl.BlockSpec((tr, cw), lambda i, c: (i, 2 * nb + c)),
                  pl.BlockSpec((HALO, cw), lambda i, c: (halo_row(i), nb + c)),
                  pl.BlockSpec((HALO, cw), lambda i, c: (halo_row(i), 2 * nb + c)),
                  pl.BlockSpec((None, CONV_K, cw), lambda i, c: (layer, 0, c))],
        out_specs=pl.BlockSpec((tr, cw), lambda i, c: (i, c)),
        out_shape=jax.ShapeDtypeStruct((m, BRANCH), BF16),
        scratch_shapes=[pltpu.VMEM((HALO + tr, cw), F32)],
        compiler_params=_cp("parallel", "parallel"),
        name="conv_branch",
    )(proj, proj, proj, proj, proj, conv_w)


def _pool_kernel(u_ref, pu_ref, w_ref, s_ref, y_ref, x_ref):
    tr = u_ref.shape[0]
    i = pl.program_id(0)
    g = pl.program_id(1)
    win = jnp.left_shift(jnp.int32(2), g)
    x = u_ref[...].astype(F32)
    x_ref[0:HALO, :] = jnp.where(i > 0, pu_ref[...].astype(F32), 0.0)
    x_ref[HALO:HALO + tr, :] = x
    total = x
    for j in range(1, max(POOL_WINDOWS)):
        keep = jnp.where(j < win, 1.0, 0.0).astype(F32)
        total = total + keep * x_ref[HALO - j:HALO - j + tr, :]
    t = i * tr + lax.broadcasted_iota(jnp.int32, x.shape, 0)
    count = jnp.minimum(t + 1, win).astype(F32)
    pooled = (total / count - x).astype(BF16)
    mixed = jnp.dot(pooled, w_ref[...].astype(BF16), preferred_element_type=F32)
    y_ref[...] = (mixed * s_ref[...]).astype(y_ref.dtype)


def _pool_branch(proj, pool_w, pool_scale, layer, tr):
    m = proj.shape[0]
    base = 0
    ng = len(POOL_WINDOWS)

    def halo_row(i):
        return jnp.maximum(i * (tr // HALO) - 1, 0)

    return pl.pallas_call(
        _pool_kernel,
        grid=(m // tr, ng),
        in_specs=[pl.BlockSpec((tr, POOL_GROUP), lambda i, g: (i, base + g)),
                  pl.BlockSpec((HALO, POOL_GROUP), lambda i, g: (halo_row(i), base + g)),
                  pl.BlockSpec((None, None, POOL_GROUP, POOL_GROUP), lambda i, g: (layer, g, 0, 0)),
                  pl.BlockSpec((None, 1, POOL_GROUP), lambda i, g: (layer, 0, g))],
        out_specs=pl.BlockSpec((tr, POOL_GROUP), lambda i, g: (i, g)),
        out_shape=jax.ShapeDtypeStruct((m, BRANCH), BF16),
        scratch_shapes=[pltpu.VMEM((HALO + tr, POOL_GROUP), F32)],
        compiler_params=_cp("parallel", "parallel"),
        name="pool_branch",
    )(proj, proj, pool_w, pool_scale)


def _store_value_t(vt_ref, v, heads):
    tr = v.shape[0]
    vt = v.T.astype(vt_ref.dtype)
    ones = jnp.ones((V_ROWS - LANES, tr), vt_ref.dtype)
    for h in range(heads):
        vt_ref[0, h * V_ROWS:h * V_ROWS + LANES, :] = vt[h * LANES:(h + 1) * LANES, :]
        vt_ref[0, h * V_ROWS + LANES:(h + 1) * V_ROWS, :] = ones


def _mla_prep_kernel(cq_ref, ckv_ref, sm_ref, gq_ref, gkv_ref, wuq_ref, wukv_ref,
                     cos_ref, sin_ref, qt_ref, kn_ref, kpe_ref, vt_ref):
    cos = cos_ref[...]
    sin = sin_ref[...]
    scale = (MLA_NOPE + MLA_ROPE) ** -0.5 * LOG2E
    cqn = _rms(cq_ref[...].astype(F32), gq_ref[...]).astype(BF16)
    q = jnp.dot(cqn, wuq_ref[...], preferred_element_type=F32)
    for h in range(MLA_HEADS):
        lo = 2 * h * LANES
        qt_ref[lo:lo + LANES, :] = (q[:, lo:lo + LANES] * scale).T.astype(qt_ref.dtype)
        pe = _rope_half64(q[:, lo + LANES:lo + 2 * LANES], cos, sin)
        qt_ref[lo + LANES:lo + 2 * LANES, :] = (pe * scale).T.astype(qt_ref.dtype)
    ckvn = _rms(ckv_ref[...].astype(F32), gkv_ref[...]).astype(BF16)
    kv = jnp.dot(ckvn, wukv_ref[...], preferred_element_type=F32)
    kn_ref[...] = kv[:, :BRANCH].astype(kn_ref.dtype)
    _store_value_t(vt_ref, kv[:, BRANCH:], MLA_HEADS)
    kpe_ref[...] = _rope_half64(sm_ref[:, 0:LANES], cos, sin).astype(kpe_ref.dtype)


def _mla_prep(proj, small, gq, gkv, wuq, wukv, cos64, sin64, layer, tr):
    m = proj.shape[0]
    qw = 2 * LANES * MLA_HEADS
    return pl.pallas_call(
        _mla_prep_kernel,
        grid=(m // tr,),
        in_specs=[pl.BlockSpec((tr, MLA_Q_RANK), lambda i: (i, 3 * BRANCH // MLA_Q_RANK)),
                  pl.BlockSpec((tr, MLA_KV_RANK), lambda i: (i, (3 * BRANCH + MLA_Q_RANK) // MLA_KV_RANK)),
                  pl.BlockSpec((tr, SMALL_COLS), lambda i: (i, 0)),
                  pl.BlockSpec((None, 1, MLA_Q_RANK), lambda i: (layer, 0, 0)),
                  pl.BlockSpec((None, 1, MLA_KV_RANK), lambda i: (layer, 0, 0)),
                  pl.BlockSpec((MLA_Q_RANK, qw), lambda i: (0, 0)),
                  pl.BlockSpec((MLA_KV_RANK, 2 * BRANCH), lambda i: (0, 0)),
                  pl.BlockSpec((tr, LANES), lambda i: (i, 0)),
                  pl.BlockSpec((tr, LANES), lambda i: (i, 0))],
        out_specs=[pl.BlockSpec((qw, tr), lambda i: (0, i)),
                   pl.BlockSpec((tr, BRANCH), lambda i: (i, 0)),
                   pl.BlockSpec((tr, LANES), lambda i: (i, 0)),
                   pl.BlockSpec((1, MLA_HEADS * V_ROWS, tr), lambda i: (i, 0, 0))],
        out_shape=[jax.ShapeDtypeStruct((qw, m), BF16),
                   jax.ShapeDtypeStruct((m, BRANCH), BF16),
                   jax.ShapeDtypeStruct((m, LANES), BF16),
                   jax.ShapeDtypeStruct((m // tr, MLA_HEADS * V_ROWS, tr), BF16)],
        compiler_params=_cp("parallel"),
        name="mla_prep",
    )(proj, proj, small, gq, gkv, wuq, wukv, cos64, sin64)


def _dsa_prep_kernel(dq_ref, dk_ref, dv_ref, iq_ref, sm_ref, c128_ref, s128_ref, c64_ref, s64_ref,
                     qt_ref, k_ref, vt_ref, iqr_ref, ik_ref, wt_ref):
    c128, s128 = c128_ref[...], s128_ref[...]
    c64, s64 = c64_ref[...], s64_ref[...]
    scale = DSA_DIM ** -0.5 * LOG2E
    for h in range(BRANCH // LANES):
        sl = slice(h * LANES, (h + 1) * LANES)
        qt_ref[sl, :] = (_rope_half128(dq_ref[:, sl].astype(F32), c128, s128) * scale).T.astype(qt_ref.dtype)
        k_ref[:, sl] = _rope_half128(dk_ref[:, sl].astype(F32), c128, s128).astype(k_ref.dtype)
        iqr_ref[:, sl] = _rope_half64(iq_ref[:, sl].astype(F32), c64, s64).astype(iqr_ref.dtype)
    _store_value_t(vt_ref, dv_ref[...].astype(F32), DSA_HEADS)
    tail = sm_ref[:, LANES:2 * LANES]
    lane = lax.broadcasted_iota(jnp.int32, tail.shape, 1)
    ik_lo = jnp.where(lane < IDX_DIM, _rope_half64(tail, c64, s64), 0.0)
    ik_ref[:, 0:LANES] = ik_lo.astype(ik_ref.dtype)
    ik_ref[:, LANES:2 * LANES] = pltpu.roll(ik_lo, IDX_DIM, axis=1).astype(ik_ref.dtype)
    idx_w_scale = (IDX_HEADS ** -0.5) * (IDX_DIM ** -0.5)
    wt_ref[...] = (tail * idx_w_scale).T


def _dsa_prep(proj_b, small, c128, s128, c64, s64, tr):
    m = proj_b.shape[0]
    col = lambda c: pl.BlockSpec((tr, BRANCH), lambda i: (i, c))
    tab = pl.BlockSpec((tr, LANES), lambda i: (i, 0))
    return pl.pallas_call(
        _dsa_prep_kernel,
        grid=(m // tr,),
        in_specs=[col(0), col(1), col(2), col(3),
                  pl.BlockSpec((tr, SMALL_COLS), lambda i: (i, 0)),
                  tab, tab, tab, tab],
        out_specs=[pl.BlockSpec((BRANCH, tr), lambda i: (0, i)),
                   pl.BlockSpec((tr, BRANCH), lambda i: (i, 0)),
                   pl.BlockSpec((1, DSA_HEADS * V_ROWS, tr), lambda i: (i, 0, 0)),
                   pl.BlockSpec((tr, BRANCH), lambda i: (i, 0)),
                   pl.BlockSpec((tr, 2 * LANES), lambda i: (i, 0)),
                   pl.BlockSpec((LANES, tr), lambda i: (0, i))],
        out_shape=[jax.ShapeDtypeStruct((BRANCH, m), BF16),
                   jax.ShapeDtypeStruct((m, BRANCH), BF16),
                   jax.ShapeDtypeStruct((m // tr, DSA_HEADS * V_ROWS, tr), BF16),
                   jax.ShapeDtypeStruct((m, BRANCH), BF16),
                   jax.ShapeDtypeStruct((m, 2 * LANES), BF16),
                   jax.ShapeDtypeStruct((LANES, m), F32)],
        compiler_params=_cp("parallel"),
        name="dsa_prep",
    )(proj_b, proj_b, proj_b, proj_b, small, c128, s128, c64, s64)


def _float_to_ordered_int(s):
    b = lax.bitcast_convert_type(s, jnp.int32)
    return b ^ ((b >> 31) & jnp.int32(0x7FFFFFFF))


def _indexer_kernel(iq_ref, ik_ref, wt_ref, bias_ref, key_ref, cut_ref, high_ref, low_ref, *, topk):
    tq = iq_ref.shape[0]
    total_rows = ik_ref.shape[0]
    tk = IDX_K_TILE
    i = pl.program_id(0)
    n_tiles = (i * tq + tq) // tk
    n_rows = n_tiles * tk
    qpos = i * tq + lax.broadcasted_iota(jnp.int32, (1, tq), 1)
    int_min = jnp.int32(-2 ** 31)

    def score_tile(kt, carry):
        start = pl.multiple_of(kt * tk, tk)
        ik_lo = ik_ref[pl.ds(start, tk), 0:LANES]
        ik_hi = ik_ref[pl.ds(start, tk), LANES:2 * LANES]
        acc = jnp.zeros((tk, tq), F32)
        for j in range(IDX_HEADS // 2):
            qpair = iq_ref[:, j * LANES:(j + 1) * LANES]
            for half, ik in enumerate((ik_lo, ik_hi)):
                g = 2 * j + half
                dots = lax.dot_general(ik, qpair, (((1,), (1,)), ((), ())),
                                       preferred_element_type=F32)
                acc = acc + jnp.maximum(dots, 0.0) * wt_ref[IDX_DIM + g:IDX_DIM + g + 1, :]
        acc = acc + 0.0
        kpos = start + lax.broadcasted_iota(jnp.int32, (tk, 1), 0)
        key = jnp.where(kpos <= qpos, _float_to_ordered_int(acc), int_min)
        key_ref[pl.ds(start, tk), :] = key
        high_ref[pl.ds(start, tk), :] = (key >> 16).astype(jnp.int16)
        return carry

    lax.fori_loop(0, n_tiles, score_tile, 0)

    n_chunks = n_rows // COUNT_CHUNK

    def count(pred, with_pos=False):
        def body(c, accs):
            start = pl.multiple_of(c * COUNT_CHUNK, COUNT_CHUNK)
            accs = list(accs)
            chunk = key_ref[pl.ds(start, COUNT_CHUNK), :]
            for r in range(COUNT_CHUNK // 8):
                blk = chunk[8 * r:8 * r + 8, :]
                if with_pos:
                    pos = start + 8 * r + lax.broadcasted_iota(jnp.int32, (8, 1), 0)
                    hit = pred(blk, pos)
                else:
                    hit = pred(blk)
                accs[r % COUNT_ACCS] = accs[r % COUNT_ACCS] + jnp.where(hit, 1, 0).astype(jnp.int32)
            return tuple(accs)
        zero = jnp.zeros((8, tq), jnp.int32)
        accs = lax.fori_loop(0, n_chunks, body, (zero,) * COUNT_ACCS)
        return jnp.sum(functools.reduce(lambda a, b: a + b, accs), axis=0, keepdims=True)

    int16_min = -(1 << 15)

    def count16(ref, cand, strict=False):
        cand16 = jnp.broadcast_to(cand, (16, tq)).astype(jnp.int16)
        one16 = jnp.ones((16, tq), jnp.int16)
        zero16 = jnp.zeros((16, tq), jnp.int16)

        def body(c, accs):
            start = pl.multiple_of(c * COUNT_CHUNK, COUNT_CHUNK)
            accs = list(accs)
            chunk = ref[pl.ds(start, COUNT_CHUNK), :]
            for r in range(COUNT_CHUNK // 16):
                blk = chunk[16 * r:16 * r + 16, :]
                hit = (blk > cand16) if strict else (blk >= cand16)
                accs[r % COUNT_ACCS] = accs[r % COUNT_ACCS] + jnp.where(hit, one16, zero16)
            return tuple(accs)
        zero = jnp.zeros((16, tq), jnp.int16)
        accs = lax.fori_loop(0, n_chunks, body, (zero,) * COUNT_ACCS)
        total = functools.reduce(lambda a, b: a + b, [a.astype(jnp.int32) for a in accs])
        return jnp.sum(total, axis=0, keepdims=True)

    def largest16(ref, want):
        def step(b, t):
            cand = t + jnp.left_shift(jnp.int32(1), 15 - b)
            return jnp.where(count16(ref, cand) >= want, cand, t)
        return lax.fori_loop(0, 16, step, jnp.full((1, tq), int16_min, jnp.int32))

    tau_high = largest16(high_ref, topk)
    n_above = count16(high_ref, tau_high, strict=True)

    def park_low(c, carry):
        start = pl.multiple_of(c * COUNT_CHUNK, COUNT_CHUNK)
        key = key_ref[pl.ds(start, COUNT_CHUNK), :]
        low = (key & 0xFFFF) + int16_min
        low_ref[pl.ds(start, COUNT_CHUNK), :] = jnp.where(
            (key >> 16) == tau_high, low, int16_min).astype(jnp.int16)
        return carry

    lax.fori_loop(0, n_chunks, park_low, 0)
    tau_low = largest16(low_ref, topk - n_above)
    tau = jnp.left_shift(tau_high, 16) + (tau_low - int16_min)

    n_gt = count(lambda blk: blk > tau)
    n_eq = count(lambda blk: blk == tau)
    need = topk - n_gt
    cut_bits = 14
    cut_ref[...] = jnp.full((1, tq), 1 << cut_bits, jnp.int32)

    @pl.when(jnp.max(n_eq - need) > 0)
    def _():
        def cut_step(b, cut):
            cand = cut + jnp.left_shift(jnp.int32(1), cut_bits - 1 - b)
            cnt = count(lambda blk, pos: (blk == tau) & (pos < cand), with_pos=True)
            return jnp.where(cnt <= need, cand, cut)
        cut_ref[...] = lax.fori_loop(0, cut_bits, cut_step, jnp.zeros((1, tq), jnp.int32))

    cut = cut_ref[...]

    def write_sel(c, carry):
        start = pl.multiple_of(c * WRITE_CHUNK, WRITE_CHUNK)
        blk = key_ref[pl.ds(start, WRITE_CHUNK), :]
        pos = start + lax.broadcasted_iota(jnp.int32, (WRITE_CHUNK, 1), 0)
        sel = ((blk > tau) | ((blk == tau) & (pos < cut))) & (pos <= qpos)
        bias_ref[pl.ds(start, WRITE_CHUNK), :] = jnp.where(sel, 0.0, NEG).astype(bias_ref.dtype)
        return carry

    lax.fori_loop(0, n_rows // WRITE_CHUNK, write_sel, 0)

    def write_neg(c, carry):
        start = pl.multiple_of(c * WRITE_CHUNK, WRITE_CHUNK)
        bias_ref[pl.ds(start, WRITE_CHUNK), :] = jnp.full((WRITE_CHUNK, tq), NEG, bias_ref.dtype)
        return carry

    lax.fori_loop(n_rows // WRITE_CHUNK, total_rows // WRITE_CHUNK, write_neg, 0)


def _indexer(iq_r, ik_ab, wt, topk):
    m = iq_r.shape[0]
    return pl.pallas_call(
        functools.partial(_indexer_kernel, topk=topk),
        grid=(m // Q_TILE,),
        in_specs=[pl.BlockSpec((Q_TILE, BRANCH), lambda i: (i, 0)),
                  pl.BlockSpec((m, 2 * LANES), lambda i: (0, 0)),
                  pl.BlockSpec((LANES, Q_TILE), lambda i: (0, i))],
        out_specs=pl.BlockSpec((m, Q_TILE), lambda i: (0, i)),
        out_shape=jax.ShapeDtypeStruct((m, m), BF16),
        scratch_shapes=[pltpu.VMEM((m, Q_TILE), jnp.int32),
                        pltpu.VMEM((1, Q_TILE), jnp.int32),
                        pltpu.VMEM((m, Q_TILE), jnp.int16),
                        pltpu.VMEM((m, Q_TILE), jnp.int16)],
        compiler_params=_cp("parallel"),
        name="dsa_indexer",
    )(iq_r, ik_ab, wt)


def _flash_kernel(*refs, q_axis, has_kpe, has_bias):
    refs = list(refs)
    q_ref, k_ref = refs[0], refs[1]
    pos = 2
    kpe_ref = bias_ref = None
    if has_kpe:
        kpe_ref = refs[pos]
        pos += 1
    vt_ref = refs[pos]
    pos += 1
    if has_bias:
        bias_ref = refs[pos]
        pos += 1
    o_ref, sa_ref, sb_ref = refs[pos], refs[pos + 1], refs[pos + 2]

    tq = q_ref.shape[1]
    heads = vt_ref.shape[1] // V_ROWS
    tk = vt_ref.shape[2]
    dq = q_ref.shape[0] // heads
    assert tk % tq == 0
    i = pl.program_id(q_axis)
    n_tiles = (i * tq + tq + tk - 1) // tk
    qpos = i * tq + lax.broadcasted_iota(jnp.int32, (1, tq), 1)

    def compute_scores(kt, s_ref):
        start = pl.multiple_of(kt * tk, tk)
        kpe = kpe_ref[pl.ds(start, tk), :] if has_kpe else None
        for h in range(heads):
            k = k_ref[pl.ds(start, tk), h * LANES:(h + 1) * LANES]
            if has_kpe:
                k = jnp.concatenate([k, kpe], axis=1)
            s_ref[h] = jnp.dot(k, q_ref[h * dq:(h + 1) * dq, :], preferred_element_type=F32)

    def consume_scores(kt, s_ref, carry, causal_mask):
        start = pl.multiple_of(kt * tk, tk)
        bias = bias_ref[pl.ds(start, tk), :].astype(F32) if has_bias else None
        new = []
        for h in range(heads):
            m_run, acc = carry[h]
            s = s_ref[h]
            if has_bias:
                s = s + bias
            if causal_mask:
                kpos = start + lax.broadcasted_iota(jnp.int32, (tk, 1), 0)
                s = jnp.where(kpos <= qpos, s, NEG)
            m_new = jnp.maximum(m_run, jnp.max(s, axis=0, keepdims=True))
            alpha = jnp.exp2(m_run - m_new)
            p = jnp.exp2(s - m_new).astype(BF16)
            pv = jnp.dot(vt_ref[kt, h * V_ROWS:(h + 1) * V_ROWS, :], p, preferred_element_type=F32)
            new.append((m_new, alpha * acc + pv))
        return tuple(new)

    def double_step(u, carry):
        kt = 2 * u
        compute_scores(kt + 1, sb_ref)
        carry = consume_scores(kt, sa_ref, carry, False)
        compute_scores(kt + 2, sa_ref)
        return consume_scores(kt + 1, sb_ref, carry, False)

    mask_last = not has_bias
    last = n_tiles - 1

    def odd_tail(carry):
        compute_scores(last, sb_ref)
        carry = consume_scores(last - 1, sa_ref, carry, False)
        return consume_scores(last, sb_ref, carry, mask_last)

    def even_tail(carry):
        return consume_scores(last, sa_ref, carry, mask_last)

    init = tuple((jnp.full((1, tq), NEG, F32), jnp.zeros((V_ROWS, tq), F32)) for _ in range(heads))
    compute_scores(0, sa_ref)
    carry = lax.fori_loop(0, last // 2, double_step, init)
    carry = lax.cond(last % 2 == 1, odd_tail, even_tail, carry)
    for h in range(heads):
        acc = carry[h][1]
        out = acc[0:LANES, :] / acc[LANES:LANES + 1, :]
        o_ref[:, h * LANES:(h + 1) * LANES] = out.T.astype(o_ref.dtype)


def _score_scratch(heads, tk):
    return [pltpu.VMEM((heads, tk, Q_TILE), F32), pltpu.VMEM((heads, tk, Q_TILE), F32)]


def _mla_attention(q, kn, kpe, vt):
    m = kn.shape[0]
    n_kt, _, tk = vt.shape
    hp = HEADS_PER_STEP
    return pl.pallas_call(
        functools.partial(_flash_kernel, q_axis=1, has_kpe=True, has_bias=False),
        grid=(MLA_HEADS // hp, m // Q_TILE),
        in_specs=[pl.BlockSpec((hp * 2 * LANES, Q_TILE), lambda h, i: (h, i)),
                  pl.BlockSpec((m, hp * LANES), lambda h, i: (0, h)),
                  pl.BlockSpec((m, LANES), lambda h, i: (0, 0)),
                  pl.BlockSpec((n_kt, hp * V_ROWS, tk), lambda h, i: (0, h, 0))],
        out_specs=pl.BlockSpec((Q_TILE, hp * MLA_V), lambda h, i: (i, h)),
        out_shape=jax.ShapeDtypeStruct((m, BRANCH), BF16),
        scratch_shapes=_score_scratch(hp, tk),
        compiler_params=_cp("parallel", "parallel"),
        name="mla_attention",
    )(q, kn, kpe, vt)


def _dsa_attention(q, k, vt, bias):
    m = k.shape[0]
    n_kt, _, tk = vt.shape
    hp = HEADS_PER_STEP
    return pl.pallas_call(
        functools.partial(_flash_kernel, q_axis=1, has_kpe=False, has_bias=True),
        grid=(DSA_HEADS // hp, m // Q_TILE),
        in_specs=[pl.BlockSpec((hp * DSA_DIM, Q_TILE), lambda h, i: (h, i)),
                  pl.BlockSpec((m, hp * DSA_DIM), lambda h, i: (0, h)),
                  pl.BlockSpec((n_kt, hp * V_ROWS, tk), lambda h, i: (0, h, 0)),
                  pl.BlockSpec((m, Q_TILE), lambda h, i: (0, i))],
        out_specs=pl.BlockSpec((Q_TILE, hp * DSA_DIM), lambda h, i: (i, h)),
        out_shape=jax.ShapeDtypeStruct((m, BRANCH), BF16),
        scratch_shapes=_score_scratch(hp, tk),
        compiler_params=_cp("parallel", "parallel"),
        name="dsa_attention",
    )(q, k, vt, bias)


def _rope_tables(rows, dim):
    inv = 1.0 / jnp.power(ROPE_THETA, jnp.arange(0, dim, 2, dtype=F32) / dim)
    ang = jnp.arange(rows, dtype=F32)[:, None] * inv[None, :]
    cos, sin = jnp.cos(ang), jnp.sin(ang)
    reps = LANES // dim
    return (jnp.tile(jnp.concatenate([cos, cos], axis=1), (1, reps)),
            jnp.tile(jnp.concatenate([-sin, sin], axis=1), (1, reps)))


W_IN_KR = 3 * BRANCH + MLA_Q_RANK + MLA_KV_RANK
W_IN_DQ = W_IN_KR + MLA_ROPE
W_IN_IK = W_IN_DQ + 4 * BRANCH
W_IN_IW = W_IN_IK + IDX_DIM
W_IN_PU = W_IN_IW + IDX_HEADS


def _small_w_in(w_t):
    zeros = lambda n: jnp.zeros((w_t.shape[0], n, w_t.shape[2]), w_t.dtype)
    return jnp.concatenate([w_t[:, W_IN_KR:W_IN_DQ], zeros(LANES - MLA_ROPE),
                            w_t[:, W_IN_IK:W_IN_IW], w_t[:, W_IN_IW:W_IN_PU],
                            zeros(LANES - IDX_DIM - IDX_HEADS)], axis=1)


def _layout_w_uq(w):
    w3 = w.reshape(MLA_Q_RANK, MLA_HEADS, MLA_NOPE + MLA_ROPE)
    w3 = jnp.pad(w3, ((0, 0), (0, 0), (0, 2 * LANES - MLA_NOPE - MLA_ROPE)))
    return w3.reshape(MLA_Q_RANK, MLA_HEADS * 2 * LANES).astype(BF16)


def _layout_w_ukv(w):
    w4 = w.reshape(MLA_KV_RANK, MLA_HEADS, 2, MLA_NOPE)
    return w4.transpose(0, 2, 1, 3).reshape(MLA_KV_RANK, 2 * BRANCH).astype(BF16)


def _forward(x, meta_tokens, norm_mix_pre, norm_mix_post, norm_ffn_pre, norm_ffn_post,
             w_in, conv_w, mla_q_norm, mla_w_uq, mla_kv_norm, mla_w_ukv, pool_w, pool_scale,
             w_branch, w_gate, b_gate, w_out, ffn_w_gate, ffn_w_up, ffn_w_down):
    assert x.shape[0] == 1 and x.shape[2] == D_MODEL
    depth = w_in.shape[0]
    seq = x.shape[1]
    length = N_META + seq
    topk = min(IDX_TOPK_MAX, length // 4)
    rows = -(-length // ROW_TILE) * ROW_TILE
    assert rows % Q_TILE == 0 and rows % K_TILE == 0 and rows % NORM_TILE == 0
    assert rows < (1 << 14)

    c64, s64 = _rope_tables(rows, 64)
    c128, s128 = _rope_tables(rows, 128)
    assert W_IN_KR == HALF_COLS
    w_in_t = jnp.swapaxes(w_in, 1, 2)
    w_in_small = _small_w_in(w_in_t)
    tn = 512
    wide = 2 * tn
    gate_tiles = D_MODEL // wide

    h, xn = _assemble_prenorm(x[0], meta_tokens.astype(F32), norm_mix_pre[0][None], rows)
    for l in range(depth):
        at_layer = lambda j, l=l: (l,)
        tile = lambda j: j
        proj_a = _wide_proj(xn, w_in_t, at_layer, tile, HALF_COLS // wide, ROW_TILE, tn, BF16,
                            "in_proj_a", weights_nk=True)
        proj_b = _wide_proj(xn, w_in_t, at_layer, tile, 4 * BRANCH // wide, ROW_TILE, tn, BF16,
                            "in_proj_b", weights_nk=True,
                            first_half_block=W_IN_DQ // tn, row_shift=W_IN_DQ % tn)
        proj_p = _wide_proj(xn, w_in_t, at_layer, tile, BRANCH // wide, ROW_TILE, tn, BF16,
                            "in_proj_pool", weights_nk=True,
                            first_half_block=W_IN_PU // tn, row_shift=W_IN_PU % tn)
        small = _stacked_proj(xn, w_in_small, at_layer, tile, 1, ROW_TILE, SMALL_COLS, F32,
                              "in_proj_small", weights_nk=True)
        gates = _wide_proj(xn, w_gate,
                           lambda j, l=l: (l, j // gate_tiles), lambda j: j % gate_tiles,
                           N_BRANCH * gate_tiles, ROW_TILE, tn, BF16, "gates",
                           bias=b_gate[:, :, None, :])

        y_conv = _conv_branch(proj_a, conv_w, l, ROW_TILE)
        y_pool = _pool_branch(proj_p, pool_w, pool_scale[:, None, :], l, ROW_TILE)

        q_m, kn_m, kpe_m, vt_m = _mla_prep(
            proj_a, small, mla_q_norm[:, None, :], mla_kv_norm[:, None, :],
            _layout_w_uq(mla_w_uq[l]), _layout_w_ukv(mla_w_ukv[l]), c64, s64, l, K_TILE)
        y_mla = _mla_attention(q_m, kn_m, kpe_m, vt_m)

        q_d, k_d, vt_d, iq_r, ik_ab, wt = _dsa_prep(proj_b, small, c128, s128, c64, s64, K_TILE)
        bias = _indexer(iq_r, ik_ab, wt, topk)
        y_dsa = _dsa_attention(q_d, k_d, vt_d, bias)

        merged = _gated_merge((y_conv, y_mla, y_dsa, y_pool), gates, w_branch, l, ROW_TILE, tn)
        mix = _wide_proj(merged, w_out, at_layer, tile, D_MODEL // wide, ROW_TILE, tn, F32, "out_proj")
        h, xn = _resid_norm(h, mix, norm_mix_post[l][None], norm_ffn_pre[l][None])

        act = _swiglu(xn, ffn_w_gate, ffn_w_up, l, ROW_TILE, 256)
        f = _wide_proj(act, ffn_w_down, at_layer, tile, D_MODEL // tn, 384, tn // 2, F32, "ffn_down")
        if l + 1 < depth:
            h, xn = _resid_norm(h, f, norm_ffn_post[l][None], norm_mix_pre[l + 1][None])
        else:
            out = _resid_out(h, f, norm_ffn_post[l][None], seq)

    return out[None]


def kernel(x, meta_tokens, norm_mix_pre, norm_mix_post, norm_ffn_pre, norm_ffn_post, w_in, conv_w, mla_q_norm, mla_w_uq, mla_kv_norm, mla_w_ukv, pool_w, pool_scale, w_branch, w_gate, b_gate, w_out, ffn_w_gate, ffn_w_up, ffn_w_down):
    return _forward(x, meta_tokens, norm_mix_pre, norm_mix_post, norm_ffn_pre, norm_ffn_post,
                    w_in, conv_w, mla_q_norm, mla_w_uq, mla_kv_norm, mla_w_ukv, pool_w, pool_scale,
                    w_branch, w_gate, b_gate, w_out, ffn_w_gate, ffn_w_up, ffn_w_down)
```

```python
import functools
import math

import jax
import jax.numpy as jnp
from jax import lax
from jax.experimental import pallas as pl
from jax.experimental.pallas import tpu as pltpu

F32 = jnp.float32
BF16 = jnp.bfloat16

D_MODEL = 4096
N_META = 16
ROPE_THETA = 10000.0
EPS = 1e-6
N_BRANCH = 4
BRANCH = 1024
CONV_K = 3
MLA_NOPE, MLA_ROPE, MLA_V, MLA_HEADS = 128, 64, 128, 8
MLA_Q_RANK, MLA_KV_RANK = 1536, 512
DSA_DIM, DSA_HEADS = 128, 8
IDX_HEADS, IDX_DIM, IDX_TOPK_MAX = 16, 64, 256
POOL_WINDOWS = (2, 4, 8, 16)
POOL_GROUP = 256
D_FF = 11008

LANES = 128
HALO = 16
ROW_TILE = 768
HALF_ROW_TILE = ROW_TILE // 2
Q_TILE = 256
K_TILE = 768
V_ROWS = 144
HEADS_PER_STEP = 4
IDX_K_TILE = 256
NEXT_ROWS = 256
COUNT_CHUNK = 256
COUNT_ACCS = 4
WRITE_CHUNK = 64
NORM_TILE = 192
NEG = -1e30
LOG2E = math.log2(math.e)
VMEM_LIMIT = 58 * 1024 * 1024

HALF_COLS = 5120
SMALL_COLS = 256


def _cp(*sem):
    return pltpu.CompilerParams(dimension_semantics=sem, vmem_limit_bytes=VMEM_LIMIT)


def _sigmoid(x):
    return 1.0 / (1.0 + jnp.exp(-x))


def _rms(x, g):
    return x * lax.rsqrt(jnp.mean(x * x, axis=-1, keepdims=True) + EPS) * g


def _matmul_w(a, wbf, weights_nk):
    if weights_nk:
        return lax.dot_general(a, wbf, (((1,), (1,)), ((), ())), preferred_element_type=F32)
    return jnp.dot(a, wbf, preferred_element_type=F32)


def _proj_kernel(a_ref, w_ref, o_ref, wbf_ref, *, weights_nk):
    @pl.when(pl.program_id(1) == 0)
    def _():
        wbf_ref[...] = w_ref[...].astype(wbf_ref.dtype)

    o_ref[...] = _matmul_w(a_ref[...], wbf_ref[...], weights_nk).astype(o_ref.dtype)


def _stacked_proj(a, w, lead_of, col_of, n_tiles, tm, tn, out_dtype, name,
                  weights_nk=False):
    m, k = a.shape
    lead = (None,) * (w.ndim - 2)
    if weights_nk:
        assert w.shape[-1] == k
        w_block, w_index = (tn, k), lambda j, i: (*lead_of(j), col_of(j), 0)
    else:
        assert w.shape[-2] == k
        w_block, w_index = (k, tn), lambda j, i: (*lead_of(j), 0, col_of(j))
    return pl.pallas_call(
        functools.partial(_proj_kernel, weights_nk=weights_nk),
        grid=(n_tiles, m // tm),
        in_specs=[pl.BlockSpec((tm, k), lambda j, i: (i, 0)),
                  pl.BlockSpec(lead + w_block, w_index)],
        out_specs=pl.BlockSpec((tm, tn), lambda j, i: (i, j)),
        out_shape=jax.ShapeDtypeStruct((m, n_tiles * tn), out_dtype),
        scratch_shapes=[pltpu.VMEM(w_block, BF16)],
        compiler_params=_cp("parallel", "arbitrary"),
        name=name,
    )(a, w)


def _wide_proj_kernel(*refs, has_bias, weights_nk, row_shift):
    refs = list(refs)
    a_ref, w_ref = refs[0], refs[1]
    pos = 2
    next_ref = b_ref = None
    if row_shift:
        next_ref = refs[pos]
        pos += 1
    if has_bias:
        b_ref = refs[pos]
        pos += 1
    o_ref, wbf_ref = refs[pos], refs[pos + 1]
    half = o_ref.shape[1] // 2
    i = pl.program_id(1)

    def cast_half(lo):
        if not weights_nk:
            wbf_ref[:, lo:lo + half] = w_ref[...].astype(wbf_ref.dtype)
        elif not row_shift:
            wbf_ref[lo:lo + half, :] = w_ref[...].astype(wbf_ref.dtype)
        else:
            keep = half - row_shift
            wbf_ref[lo:lo + keep, :] = w_ref[row_shift:half, :].astype(wbf_ref.dtype)
            wbf_ref[lo + keep:lo + half, :] = next_ref[0:row_shift, :].astype(wbf_ref.dtype)

    def emit(lo, width):
        wbf = wbf_ref[lo:lo + width, :] if weights_nk else wbf_ref[:, lo:lo + width]
        z = _matmul_w(a_ref[...], wbf, weights_nk)
        if has_bias:
            z = _sigmoid(z + b_ref[:, lo:lo + width])
        o_ref[:, lo:lo + width] = z.astype(o_ref.dtype)

    @pl.when(i == 0)
    def _():
        cast_half(0)
        emit(0, half)

    @pl.when(i == 1)
    def _():
        cast_half(half)
        emit(half, half)

    @pl.when(i >= 2)
    def _():
        emit(0, 2 * half)


def _wide_proj(a, w, lead_of, col_of, n_tiles, tm, half, out_dtype, name, bias=None,
               weights_nk=False, first_half_block=0, row_shift=0):
    m, k = a.shape
    assert w.shape[-1 if weights_nk else -2] == k
    assert row_shift == 0 or (weights_nk and row_shift % 8 == 0 and row_shift <= NEXT_ROWS)
    lead = (None,) * (w.ndim - 2)
    row = lambda i: jnp.maximum(i - 1, 0)

    def weight_window(j, i):
        jw = jnp.where(i >= 2, jnp.minimum(j + 1, n_tiles - 1), j)
        return lead_of(jw), first_half_block + 2 * col_of(jw) + jnp.where(i == 1, 1, 0)

    def w_index(j, i):
        stacked, hb = weight_window(j, i)
        return (*stacked, hb, 0) if weights_nk else (*stacked, 0, hb)

    in_specs = [pl.BlockSpec((tm, k), lambda j, i: (row(i), 0)),
                pl.BlockSpec(lead + ((half, k) if weights_nk else (k, half)), w_index)]
    args = [a, w]
    if row_shift:
        def next_index(j, i):
            stacked, hb = weight_window(j, i)
            return (*stacked, (hb + 1) * (half // NEXT_ROWS), 0)

        in_specs.append(pl.BlockSpec(lead + (NEXT_ROWS, k), next_index))
        args.append(w)
    if bias is not None:
        in_specs.append(pl.BlockSpec(lead + (1, 2 * half), lambda j, i: (*lead_of(j), 0, col_of(j))))
        args.append(bias)
    return pl.pallas_call(
        functools.partial(_wide_proj_kernel, has_bias=bias is not None, weights_nk=weights_nk,
                          row_shift=row_shift),
        grid=(n_tiles, m // tm + 1),
        in_specs=in_specs,
        out_specs=pl.BlockSpec((tm, 2 * half), lambda j, i: (row(i), j)),
        out_shape=jax.ShapeDtypeStruct((m, n_tiles * 2 * half), out_dtype),
        scratch_shapes=[pltpu.VMEM((2 * half, k) if weights_nk else (k, 2 * half), BF16)],
        compiler_params=_cp("parallel", "arbitrary"),
        name=name,
    )(*args)


def _swiglu_kernel(a_ref, wg_ref, wu_ref, o_ref, wgbf_ref, wubf_ref, *, n_half_blocks):
    half = wg_ref.shape[1]
    j = pl.program_id(0)
    i = pl.program_id(1)

    def cast_half(lo):
        wgbf_ref[:, lo:lo + half] = wg_ref[...].astype(wgbf_ref.dtype)
        wubf_ref[:, lo:lo + half] = wu_ref[...].astype(wubf_ref.dtype)

    def emit(lo, width):
        a = a_ref[...]
        g = jnp.dot(a, wgbf_ref[:, lo:lo + width], preferred_element_type=F32)
        u = jnp.dot(a, wubf_ref[:, lo:lo + width], preferred_element_type=F32)
        o_ref[:, lo:lo + width] = (g * _sigmoid(g) * u).astype(o_ref.dtype)

    has_right = 2 * j + 1 < n_half_blocks

    @pl.when(i == 0)
    def _():
        cast_half(0)
        emit(0, half)

    @pl.when((i == 1) & has_right)
    def _():
        cast_half(half)
        emit(half, half)

    @pl.when((i >= 2) & has_right)
    def _():
        emit(0, 2 * half)

    @pl.when((i >= 2) & jnp.logical_not(has_right))
    def _():
        emit(0, half)


def _swiglu(a, wg, wu, layer, tm, half):
    m, k = a.shape
    n = wg.shape[2]
    n_half_blocks = n // half
    assert n_half_blocks * half == n
    n_tiles = -(-n_half_blocks // 2)
    row = lambda i: jnp.maximum(i - 1, 0)
    def w_index(j, i):
        jw = jnp.where(i >= 2, jnp.minimum(j + 1, n_tiles - 1), j)
        return (layer, 0, jnp.minimum(2 * jw + jnp.where(i == 1, 1, 0), n_half_blocks - 1))

    w_spec = pl.BlockSpec((None, k, half), w_index)
    return pl.pallas_call(
        functools.partial(_swiglu_kernel, n_half_blocks=n_half_blocks),
        grid=(n_tiles, m // tm + 1),
        in_specs=[pl.BlockSpec((tm, k), lambda j, i: (row(i), 0)), w_spec, w_spec],
        out_specs=pl.BlockSpec((tm, 2 * half), lambda j, i: (row(i), j)),
        out_shape=jax.ShapeDtypeStruct((m, n), BF16),
        scratch_shapes=[pltpu.VMEM((k, 2 * half), BF16), pltpu.VMEM((k, 2 * half), BF16)],
        compiler_params=_cp("parallel", "arbitrary"),
        name="ffn_swiglu",
    )(a, wg, wu)


def _merge_kernel(y0_ref, y1_ref, y2_ref, y3_ref, g0_ref, g1_ref, g2_ref, g3_ref, w_ref,
                  o_ref, wbf_ref):
    half = w_ref.shape[2]
    i = pl.program_id(1)
    branches = ((y0_ref, g0_ref), (y1_ref, g1_ref), (y2_ref, g2_ref), (y3_ref, g3_ref))

    def cast_half(lo):
        wbf_ref[:, :, lo:lo + half] = w_ref[...].astype(wbf_ref.dtype)

    def emit(lo, width):
        acc = None
        for b, (y_ref, g_ref) in enumerate(branches):
            val = g_ref[:, lo:lo + width].astype(F32) * jnp.dot(
                y_ref[...], wbf_ref[b, :, lo:lo + width], preferred_element_type=F32)
            acc = val if acc is None else acc + val
        o_ref[:, lo:lo + width] = acc.astype(o_ref.dtype)

    @pl.when(i == 0)
    def _():
        cast_half(0)
        emit(0, half)

    @pl.when(i == 1)
    def _():
        cast_half(half)
        emit(half, half)

    @pl.when(i >= 2)
    def _():
        emit(0, 2 * half)


def _gated_merge(ys, gates, w_branch, layer, tm, half):
    m = ys[0].shape[0]
    nj = D_MODEL // (2 * half)
    row = lambda i: jnp.maximum(i - 1, 0)

    def w_index(j, i):
        jw = jnp.where(i >= 2, jnp.minimum(j + 1, nj - 1), j)
        return (layer, 0, 0, 2 * jw + jnp.where(i == 1, 1, 0))

    y_spec = pl.BlockSpec((tm, BRANCH), lambda j, i: (row(i), 0))
    g_specs = [pl.BlockSpec((tm, 2 * half),
                            functools.partial(lambda j, i, b: (row(i), b * nj + j), b=b))
               for b in range(N_BRANCH)]
    return pl.pallas_call(
        _merge_kernel,
        grid=(nj, m // tm + 1),
        in_specs=[y_spec] * N_BRANCH + g_specs
                 + [pl.BlockSpec((None, N_BRANCH, BRANCH, half), w_index)],
        out_specs=pl.BlockSpec((tm, 2 * half), lambda j, i: (row(i), j)),
        out_shape=jax.ShapeDtypeStruct((m, D_MODEL), BF16),
        scratch_shapes=[pltpu.VMEM((N_BRANCH, BRANCH, 2 * half), BF16)],
        compiler_params=_cp("parallel", "arbitrary"),
        name="gated_merge",
    )(*ys, gates, gates, gates, gates, w_branch)


def _assemble_kernel(x_ref, prev_ref, meta_ref, g_ref, h_ref, xn_ref, *, length):
    tr = x_ref.shape[0]
    i = pl.program_id(0)
    head = jnp.where(i == 0, meta_ref[...], prev_ref[...])
    tile = jnp.concatenate([head, x_ref[0:tr - N_META, :]], axis=0)
    pos = i * tr + lax.broadcasted_iota(jnp.int32, (tr, 1), 0)
    h = jnp.where(pos < length, tile, 0.0)
    h_ref[...] = h
    xn_ref[...] = _rms(h, g_ref[...]).astype(xn_ref.dtype)


def _assemble_prenorm(x2d, meta, g, rows):
    seq = x2d.shape[0]
    assert N_META == HALO and NORM_TILE % N_META == 0
    last_x = (seq - 1) // NORM_TILE
    last_prev = (seq - 1) // N_META
    per_tile = NORM_TILE // N_META
    row = pl.BlockSpec((NORM_TILE, D_MODEL), lambda i: (i, 0))
    return pl.pallas_call(
        functools.partial(_assemble_kernel, length=N_META + seq),
        grid=(rows // NORM_TILE,),
        in_specs=[pl.BlockSpec((NORM_TILE, D_MODEL), lambda i: (jnp.minimum(i, last_x), 0)),
                  pl.BlockSpec((N_META, D_MODEL),
                               lambda i: (jnp.clip(i * per_tile - 1, 0, last_prev), 0)),
                  pl.BlockSpec((N_META, D_MODEL), lambda i: (0, 0)),
                  pl.BlockSpec((1, D_MODEL), lambda i: (0, 0))],
        out_specs=[row, row],
        out_shape=[jax.ShapeDtypeStruct((rows, D_MODEL), F32),
                   jax.ShapeDtypeStruct((rows, D_MODEL), BF16)],
        compiler_params=_cp("parallel"),
        name="assemble_prenorm",
    )(x2d, x2d, meta, g)


def _resid_norm_kernel(h_ref, o_ref, gpost_ref, gnext_ref, hn_ref, xn_ref):
    hn = h_ref[...] + _rms(o_ref[...], gpost_ref[...])
    hn_ref[...] = hn
    xn_ref[...] = _rms(hn, gnext_ref[...]).astype(xn_ref.dtype)


def _resid_out_kernel(h_ref, o_ref, hnext_ref, onext_ref, gpost_ref, out_ref):
    g = gpost_ref[...]
    cur = h_ref[N_META:, :] + _rms(o_ref[N_META:, :], g)
    nxt = hnext_ref[...] + _rms(onext_ref[...], g)
    out_ref[...] = jnp.concatenate([cur, nxt], axis=0)


def _resid_norm(h, o, g_post, g_next):
    m = h.shape[0]
    row = pl.BlockSpec((NORM_TILE, D_MODEL), lambda i: (i, 0))
    gain = pl.BlockSpec((1, D_MODEL), lambda i: (0, 0))
    return pl.pallas_call(
        _resid_norm_kernel,
        grid=(m // NORM_TILE,),
        in_specs=[row, row, gain, gain],
        out_specs=[row, row],
        out_shape=[jax.ShapeDtypeStruct((m, D_MODEL), F32),
                   jax.ShapeDtypeStruct((m, D_MODEL), BF16)],
        compiler_params=_cp("parallel"),
        name="resid_norm",
    )(h, o, g_post, g_next)


def _resid_out(h, o, g_post, seq):
    m = h.shape[0]
    per_tile = NORM_TILE // N_META
    last_head = m // N_META - 1
    row = pl.BlockSpec((NORM_TILE, D_MODEL), lambda i: (i, 0))
    head = pl.BlockSpec((N_META, D_MODEL), lambda i: (jnp.minimum((i + 1) * per_tile, last_head), 0))
    return pl.pallas_call(
        _resid_out_kernel,
        grid=(-(-seq // NORM_TILE),),
        in_specs=[row, row, head, head, pl.BlockSpec((1, D_MODEL), lambda i: (0, 0))],
        out_specs=row,
        out_shape=jax.ShapeDtypeStruct((seq, D_MODEL), F32),
        compiler_params=_cp("parallel"),
        name="resid_out",
    )(h, o, h, o, g_post)


def _rope_half128(x, cos, sin_signed):
    return x * cos + pltpu.roll(x, 64, axis=1) * sin_signed


def _rope_half64(x, cos, sin_signed):
    lane = lax.broadcasted_iota(jnp.int32, x.shape, 1)
    partner = jnp.where((lane & 32) == 0, pltpu.roll(x, 96, axis=1), pltpu.roll(x, 32, axis=1))
    return x * cos + partner * sin_signed


def _conv_kernel(cb_ref, cc_ref, cu_ref, pc_ref, pu_ref, w_ref, y_ref, z_ref):
    tr = cb_ref.shape[0]
    i = pl.program_id(0)
    z = cc_ref[...].astype(F32) * cu_ref[...].astype(F32)
    zp = pc_ref[...].astype(F32) * pu_ref[...].astype(F32)
    z_ref[0:HALO, :] = jnp.where(i > 0, zp, 0.0)
    z_ref[HALO:HALO + tr, :] = z
    w = w_ref[...]
    conv = (w[2:3, :] * z
            + w[1:2, :] * z_ref[HALO - 1:HALO - 1 + tr, :]
            + w[0:1, :] * z_ref[HALO - 2:HALO - 2 + tr, :])
    y_ref[...] = (cb_ref[...].astype(F32) * conv).astype(y_ref.dtype)


def _conv_branch(proj, conv_w, layer, tr):
    m = proj.shape[0]
    cw = 256
    nb = BRANCH // cw

    def halo_row(i):
        return jnp.maximum(i * (tr // HALO) - 1, 0)

    return pl.pallas_call(
        _conv_kernel,
        grid=(m // tr, nb),
        in_specs=[pl.BlockSpec((tr, cw), lambda i, c: (i, c)),
                  pl.BlockSpec((tr, cw), lambda i, c: (i, nb + c)),
                  pl.BlockSpec((tr, cw), lambda i, c: (i, 2 * nb + c)),
                  pl.BlockSpec((HALO, cw), lambda i, c: (halo_row(i), nb + c)),
                  pl.BlockSpec((HALO, cw), lambda i, c: (halo_row(i), 2 * nb + c)),
                  pl.BlockSpec((None, CONV_K, cw), lambda i, c: (layer, 0, c))],
        out_specs=pl.BlockSpec((tr, cw), lambda i, c: (i, c)),
        out_shape=jax.ShapeDtypeStruct((m, BRANCH), BF16),
        scratch_shapes=[pltpu.VMEM((HALO + tr, cw), F32)],
        compiler_params=_cp("parallel", "parallel"),
        name="conv_branch",
    )(proj, proj, proj, proj, proj, conv_w)


def _pool_kernel(u_ref, pu_ref, w_ref, s_ref, y_ref, x_ref):
    tr = u_ref.shape[0]
    i = pl.program_id(0)
    g = pl.program_id(1)
    win = jnp.left_shift(jnp.int32(2), g)
    x = u_ref[...].astype(F32)
    x_ref[0:HALO, :] = jnp.where(i > 0, pu_ref[...].astype(F32), 0.0)
    x_ref[HALO:HALO + tr, :] = x
    total = x
    for j in range(1, max(POOL_WINDOWS)):
        keep = jnp.where(j < win, 1.0, 0.0).astype(F32)
        total = total + keep * x_ref[HALO - j:HALO - j + tr, :]
    t = i * tr + lax.broadcasted_iota(jnp.int32, x.shape, 0)
    count = jnp.minimum(t + 1, win).astype(F32)
    pooled = (total / count - x).astype(BF16)
    mixed = jnp.dot(pooled, w_ref[...].astype(BF16), preferred_element_type=F32)
    y_ref[...] = (mixed * s_ref[...]).astype(y_ref.dtype)


def _pool_branch(proj, pool_w, pool_scale, layer, tr):
    m = proj.shape[0]
    base = 0
    ng = len(POOL_WINDOWS)

    def halo_row(i):
        return jnp.maximum(i * (tr // HALO) - 1, 0)

    return pl.pallas_call(
        _pool_kernel,
        grid=(m // tr, ng),
        in_specs=[pl.BlockSpec((tr, POOL_GROUP), lambda i, g: (i, base + g)),
                  pl.BlockSpec((HALO, POOL_GROUP), lambda i, g: (halo_row(i), base + g)),
                  pl.BlockSpec((None, None, POOL_GROUP, POOL_GROUP), lambda i, g: (layer, g, 0, 0)),
                  pl.BlockSpec((None, 1, POOL_GROUP), lambda i, g: (layer, 0, g))],
        out_specs=pl.BlockSpec((tr, POOL_GROUP), lambda i, g: (i, g)),
        out_shape=jax.ShapeDtypeStruct((m, BRANCH), BF16),
        scratch_shapes=[pltpu.VMEM((HALO + tr, POOL_GROUP), F32)],
        compiler_params=_cp("parallel", "parallel"),
        name="pool_branch",
    )(proj, proj, pool_w, pool_scale)


def _store_value_t(vt_ref, v, heads):
    tr = v.shape[0]
    vt = v.T.astype(vt_ref.dtype)
    ones = jnp.ones((V_ROWS - LANES, tr), vt_ref.dtype)
    for h in range(heads):
        vt_ref[0, h * V_ROWS:h * V_ROWS + LANES, :] = vt[h * LANES:(h + 1) * LANES, :]
        vt_ref[0, h * V_ROWS + LANES:(h + 1) * V_ROWS, :] = ones


def _mla_prep_kernel(cq_ref, ckv_ref, sm_ref, gq_ref, gkv_ref, wuq_ref, wukv_ref,
                     cos_ref, sin_ref, qt_ref, kn_ref, kpe_ref, vt_ref):
    cos = cos_ref[...]
    sin = sin_ref[...]
    scale = (MLA_NOPE + MLA_ROPE) ** -0.5 * LOG2E
    cqn = _rms(cq_ref[...].astype(F32), gq_ref[...]).astype(BF16)
    q = jnp.dot(cqn, wuq_ref[...], preferred_element_type=F32)
    for h in range(MLA_HEADS):
        lo = 2 * h * LANES
        qt_ref[lo:lo + LANES, :] = (q[:, lo:lo + LANES] * scale).T.astype(qt_ref.dtype)
        pe = _rope_half64(q[:, lo + LANES:lo + 2 * LANES], cos, sin)
        qt_ref[lo + LANES:lo + 2 * LANES, :] = (pe * scale).T.astype(qt_ref.dtype)
    ckvn = _rms(ckv_ref[...].astype(F32), gkv_ref[...]).astype(BF16)
    kv = jnp.dot(ckvn, wukv_ref[...], preferred_element_type=F32)
    kn_ref[...] = kv[:, :BRANCH].astype(kn_ref.dtype)
    _store_value_t(vt_ref, kv[:, BRANCH:], MLA_HEADS)
    kpe_ref[...] = _rope_half64(sm_ref[:, 0:LANES], cos, sin).astype(kpe_ref.dtype)


def _mla_prep(proj, small, gq, gkv, wuq, wukv, cos64, sin64, layer, tr):
    m = proj.shape[0]
    qw = 2 * LANES * MLA_HEADS
    return pl.pallas_call(
        _mla_prep_kernel,
        grid=(m // tr,),
        in_specs=[pl.BlockSpec((tr, MLA_Q_RANK), lambda i: (i, 3 * BRANCH // MLA_Q_RANK)),
                  pl.BlockSpec((tr, MLA_KV_RANK), lambda i: (i, (3 * BRANCH + MLA_Q_RANK) // MLA_KV_RANK)),
                  pl.BlockSpec((tr, SMALL_COLS), lambda i: (i, 0)),
                  pl.BlockSpec((None, 1, MLA_Q_RANK), lambda i: (layer, 0, 0)),
                  pl.BlockSpec((None, 1, MLA_KV_RANK), lambda i: (layer, 0, 0)),
                  pl.BlockSpec((MLA_Q_RANK, qw), lambda i: (0, 0)),
                  pl.BlockSpec((MLA_KV_RANK, 2 * BRANCH), lambda i: (0, 0)),
                  pl.BlockSpec((tr, LANES), lambda i: (i, 0)),
                  pl.BlockSpec((tr, LANES), lambda i: (i, 0))],
        out_specs=[pl.BlockSpec((qw, tr), lambda i: (0, i)),
                   pl.BlockSpec((tr, BRANCH), lambda i: (i, 0)),
                   pl.BlockSpec((tr, LANES), lambda i: (i, 0)),
                   pl.BlockSpec((1, MLA_HEADS * V_ROWS, tr), lambda i: (i, 0, 0))],
        out_shape=[jax.ShapeDtypeStruct((qw, m), BF16),
                   jax.ShapeDtypeStruct((m, BRANCH), BF16),
                   jax.ShapeDtypeStruct((m, LANES), BF16),
                   jax.ShapeDtypeStruct((m // tr, MLA_HEADS * V_ROWS, tr), BF16)],
        compiler_params=_cp("parallel"),
        name="mla_prep",
    )(proj, proj, small, gq, gkv, wuq, wukv, cos64, sin64)


def _dsa_prep_kernel(dq_ref, dk_ref, dv_ref, iq_ref, sm_ref, c128_ref, s128_ref, c64_ref, s64_ref,
                     qt_ref, k_ref, vt_ref, iqr_ref, ik_ref, wt_ref):
    c128, s128 = c128_ref[...], s128_ref[...]
    c64, s64 = c64_ref[...], s64_ref[...]
    scale = DSA_DIM ** -0.5 * LOG2E
    for h in range(BRANCH // LANES):
        sl = slice(h * LANES, (h + 1) * LANES)
        qt_ref[sl, :] = (_rope_half128(dq_ref[:, sl].astype(F32), c128, s128) * scale).T.astype(qt_ref.dtype)
        k_ref[:, sl] = _rope_half128(dk_ref[:, sl].astype(F32), c128, s128).astype(k_ref.dtype)
        iqr_ref[:, sl] = _rope_half64(iq_ref[:, sl].astype(F32), c64, s64).astype(iqr_ref.dtype)
    _store_value_t(vt_ref, dv_ref[...].astype(F32), DSA_HEADS)
    tail = sm_ref[:, LANES:2 * LANES]
    lane = lax.broadcasted_iota(jnp.int32, tail.shape, 1)
    ik_lo = jnp.where(lane < IDX_DIM, _rope_half64(tail, c64, s64), 0.0)
    ik_ref[:, 0:LANES] = ik_lo.astype(ik_ref.dtype)
    ik_ref[:, LANES:2 * LANES] = pltpu.roll(ik_lo, IDX_DIM, axis=1).astype(ik_ref.dtype)
    idx_w_scale = (IDX_HEADS ** -0.5) * (IDX_DIM ** -0.5)
    wt_ref[...] = (tail * idx_w_scale).T


def _dsa_prep(proj_b, small, c128, s128, c64, s64, tr):
    m = proj_b.shape[0]
    col = lambda c: pl.BlockSpec((tr, BRANCH), lambda i: (i, c))
    tab = pl.BlockSpec((tr, LANES), lambda i: (i, 0))
    return pl.pallas_call(
        _dsa_prep_kernel,
        grid=(m // tr,),
        in_specs=[col(0), col(1), col(2), col(3),
                  pl.BlockSpec((tr, SMALL_COLS), lambda i: (i, 0)),
                  tab, tab, tab, tab],
        out_specs=[pl.BlockSpec((BRANCH, tr), lambda i: (0, i)),
                   pl.BlockSpec((tr, BRANCH), lambda i: (i, 0)),
                   pl.BlockSpec((1, DSA_HEADS * V_ROWS, tr), lambda i: (i, 0, 0)),
                   pl.BlockSpec((tr, BRANCH), lambda i: (i, 0)),
                   pl.BlockSpec((tr, 2 * LANES), lambda i: (i, 0)),
                   pl.BlockSpec((LANES, tr), lambda i: (0, i))],
        out_shape=[jax.ShapeDtypeStruct((BRANCH, m), BF16),
                   jax.ShapeDtypeStruct((m, BRANCH), BF16),
                   jax.ShapeDtypeStruct((m // tr, DSA_HEADS * V_ROWS, tr), BF16),
                   jax.ShapeDtypeStruct((m, BRANCH), BF16),
                   jax.ShapeDtypeStruct((m, 2 * LANES), BF16),
                   jax.ShapeDtypeStruct((LANES, m), F32)],
        compiler_params=_cp("parallel"),
        name="dsa_prep",
    )(proj_b, proj_b, proj_b, proj_b, small, c128, s128, c64, s64)


def _float_to_ordered_int(s):
    b = lax.bitcast_convert_type(s, jnp.int32)
    return b ^ ((b >> 31) & jnp.int32(0x7FFFFFFF))


def _indexer_kernel(iq_ref, ik_ref, wt_ref, bias_ref, key_ref, cut_ref, high_ref, low_ref, *, topk):
    tq = iq_ref.shape[0]
    total_rows = ik_ref.shape[0]
    tk = IDX_K_TILE
    i = pl.program_id(0)
    n_tiles = (i * tq + tq) // tk
    n_rows = n_tiles * tk
    qpos = i * tq + lax.broadcasted_iota(jnp.int32, (1, tq), 1)
    int_min = jnp.int32(-2 ** 31)

    def score_tile(kt, carry):
        start = pl.multiple_of(kt * tk, tk)
        ik_lo = ik_ref[pl.ds(start, tk), 0:LANES]
        ik_hi = ik_ref[pl.ds(start, tk), LANES:2 * LANES]
        acc = jnp.zeros((tk, tq), F32)
        for j in range(IDX_HEADS // 2):
            qpair = iq_ref[:, j * LANES:(j + 1) * LANES]
            for half, ik in enumerate((ik_lo, ik_hi)):
                g = 2 * j + half
                dots = lax.dot_general(ik, qpair, (((1,), (1,)), ((), ())),
                                       preferred_element_type=F32)
                acc = acc + jnp.maximum(dots, 0.0) * wt_ref[IDX_DIM + g:IDX_DIM + g + 1, :]
        acc = acc + 0.0
        kpos = start + lax.broadcasted_iota(jnp.int32, (tk, 1), 0)
        key = jnp.where(kpos <= qpos, _float_to_ordered_int(acc), int_min)
        key_ref[pl.ds(start, tk), :] = key
        high_ref[pl.ds(start, tk), :] = (key >> 16).astype(jnp.int16)
        return carry

    lax.fori_loop(0, n_tiles, score_tile, 0)

    n_chunks = n_rows // COUNT_CHUNK

    def count(pred, with_pos=False):
        def body(c, accs):
            start = pl.multiple_of(c * COUNT_CHUNK, COUNT_CHUNK)
            accs = list(accs)
            chunk = key_ref[pl.ds(start, COUNT_CHUNK), :]
            for r in range(COUNT_CHUNK // 8):
                blk = chunk[8 * r:8 * r + 8, :]
                if with_pos:
                    pos = start + 8 * r + lax.broadcasted_iota(jnp.int32, (8, 1), 0)
                    hit = pred(blk, pos)
                else:
                    hit = pred(blk)
                accs[r % COUNT_ACCS] = accs[r % COUNT_ACCS] + jnp.where(hit, 1, 0).astype(jnp.int32)
            return tuple(accs)
        zero = jnp.zeros((8, tq), jnp.int32)
        accs = lax.fori_loop(0, n_chunks, body, (zero,) * COUNT_ACCS)
        return jnp.sum(functools.reduce(lambda a, b: a + b, accs), axis=0, keepdims=True)

    int16_min = -(1 << 15)

    def count16(ref, cand, strict=False):
        cand16 = jnp.broadcast_to(cand, (16, tq)).astype(jnp.int16)
        one16 = jnp.ones((16, tq), jnp.int16)
        zero16 = jnp.zeros((16, tq), jnp.int16)

        def body(c, accs):
            start = pl.multiple_of(c * COUNT_CHUNK, COUNT_CHUNK)
            accs = list(accs)
            chunk = ref[pl.ds(start, COUNT_CHUNK), :]
            for r in range(COUNT_CHUNK // 16):
                blk = chunk[16 * r:16 * r + 16, :]
                hit = (blk > cand16) if strict else (blk >= cand16)
                accs[r % COUNT_ACCS] = accs[r % COUNT_ACCS] + jnp.where(hit, one16, zero16)
            return tuple(accs)
        zero = jnp.zeros((16, tq), jnp.int16)
        accs = lax.fori_loop(0, n_chunks, body, (zero,) * COUNT_ACCS)
        total = functools.reduce(lambda a, b: a + b, [a.astype(jnp.int32) for a in accs])
        return jnp.sum(total, axis=0, keepdims=True)

    def largest16(ref, want):
        def step(b, t):
            cand = t + jnp.left_shift(jnp.int32(1), 15 - b)
            return jnp.where(count16(ref, cand) >= want, cand, t)
        return lax.fori_loop(0, 16, step, jnp.full((1, tq), int16_min, jnp.int32))

    tau_high = largest16(high_ref, topk)
    n_above = count16(high_ref, tau_high, strict=True)

    def park_low(c, carry):
        start = pl.multiple_of(c * COUNT_CHUNK, COUNT_CHUNK)
        key = key_ref[pl.ds(start, COUNT_CHUNK), :]
        low = (key & 0xFFFF) + int16_min
        low_ref[pl.ds(start, COUNT_CHUNK), :] = jnp.where(
            (key >> 16) == tau_high, low, int16_min).astype(jnp.int16)
        return carry

    lax.fori_loop(0, n_chunks, park_low, 0)
    tau_low = largest16(low_ref, topk - n_above)
    tau = jnp.left_shift(tau_high, 16) + (tau_low - int16_min)

    n_gt = count(lambda blk: blk > tau)
    n_eq = count(lambda blk: blk == tau)
    need = topk - n_gt
    cut_bits = 14
    cut_ref[...] = jnp.full((1, tq), 1 << cut_bits, jnp.int32)

    @pl.when(jnp.max(n_eq - need) > 0)
    def _():
        def cut_step(b, cut):
            cand = cut + jnp.left_shift(jnp.int32(1), cut_bits - 1 - b)
            cnt = count(lambda blk, pos: (blk == tau) & (pos < cand), with_pos=True)
            return jnp.where(cnt <= need, cand, cut)
        cut_ref[...] = lax.fori_loop(0, cut_bits, cut_step, jnp.zeros((1, tq), jnp.int32))

    cut = cut_ref[...]

    def write_sel(c, carry):
        start = pl.multiple_of(c * WRITE_CHUNK, WRITE_CHUNK)
        blk = key_ref[pl.ds(start, WRITE_CHUNK), :]
        pos = start + lax.broadcasted_iota(jnp.int32, (WRITE_CHUNK, 1), 0)
        causal = jnp.where(pos <= qpos, 0.0, NEG)
        tie = jnp.where(blk == tau, jnp.where(pos < cut, causal, NEG), NEG)
        bias_ref[pl.ds(start, WRITE_CHUNK), :] = jnp.where(blk > tau, causal, tie).astype(bias_ref.dtype)
        return carry

    lax.fori_loop(0, n_rows // WRITE_CHUNK, write_sel, 0)

    def write_neg(c, carry):
        start = pl.multiple_of(c * WRITE_CHUNK, WRITE_CHUNK)
        bias_ref[pl.ds(start, WRITE_CHUNK), :] = jnp.full((WRITE_CHUNK, tq), NEG, bias_ref.dtype)
        return carry

    lax.fori_loop(n_rows // WRITE_CHUNK, total_rows // WRITE_CHUNK, write_neg, 0)


def _indexer(iq_r, ik_ab, wt, topk):
    m = iq_r.shape[0]
    return pl.pallas_call(
        functools.partial(_indexer_kernel, topk=topk),
        grid=(m // Q_TILE,),
        in_specs=[pl.BlockSpec((Q_TILE, BRANCH), lambda i: (i, 0)),
                  pl.BlockSpec((m, 2 * LANES), lambda i: (0, 0)),
                  pl.BlockSpec((LANES, Q_TILE), lambda i: (0, i))],
        out_specs=pl.BlockSpec((m, Q_TILE), lambda i: (0, i)),
        out_shape=jax.ShapeDtypeStruct((m, m), BF16),
        scratch_shapes=[pltpu.VMEM((m, Q_TILE), jnp.int32),
                        pltpu.VMEM((1, Q_TILE), jnp.int32),
                        pltpu.VMEM((m, Q_TILE), jnp.int16),
                        pltpu.VMEM((m, Q_TILE), jnp.int16)],
        compiler_params=_cp("parallel"),
        name="dsa_indexer",
    )(iq_r, ik_ab, wt)


def _flash_kernel(*refs, q_axis, has_kpe, has_bias):
    refs = list(refs)
    q_ref, k_ref = refs[0], refs[1]
    pos = 2
    kpe_ref = bias_ref = None
    if has_kpe:
        kpe_ref = refs[pos]
        pos += 1
    vt_ref = refs[pos]
    pos += 1
    if has_bias:
        bias_ref = refs[pos]
        pos += 1
    o_ref, sa_ref, sb_ref = refs[pos], refs[pos + 1], refs[pos + 2]

    tq = q_ref.shape[1]
    heads = vt_ref.shape[1] // V_ROWS
    tk = vt_ref.shape[2]
    dq = q_ref.shape[0] // heads
    assert tk % tq == 0
    i = pl.program_id(q_axis)
    n_tiles = (i * tq + tq + tk - 1) // tk
    qpos = i * tq + lax.broadcasted_iota(jnp.int32, (1, tq), 1)

    def compute_scores(kt, s_ref):
        start = pl.multiple_of(kt * tk, tk)
        kpe = kpe_ref[pl.ds(start, tk), :] if has_kpe else None
        for h in range(heads):
            k = k_ref[pl.ds(start, tk), h * LANES:(h + 1) * LANES]
            if has_kpe:
                k = jnp.concatenate([k, kpe], axis=1)
            s_ref[h] = jnp.dot(k, q_ref[h * dq:(h + 1) * dq, :], preferred_element_type=F32)

    def consume_scores(kt, s_ref, carry, causal_mask):
        start = pl.multiple_of(kt * tk, tk)
        bias = bias_ref[pl.ds(start, tk), :].astype(F32) if has_bias else None
        new = []
        for h in range(heads):
            m_run, acc = carry[h]
            s = s_ref[h]
            if has_bias:
                s = s + bias
            if causal_mask:
                kpos = start + lax.broadcasted_iota(jnp.int32, (tk, 1), 0)
                s = jnp.where(kpos <= qpos, s, NEG)
            m_new = jnp.maximum(m_run, jnp.max(s, axis=0, keepdims=True))
            alpha = jnp.exp2(m_run - m_new)
            p = jnp.exp2(s - m_new).astype(BF16)
            pv = jnp.dot(vt_ref[kt, h * V_ROWS:(h + 1) * V_ROWS, :], p, preferred_element_type=F32)
            new.append((m_new, alpha * acc + pv))
        return tuple(new)

    def double_step(u, carry):
        kt = 2 * u
        compute_scores(kt + 1, sb_ref)
        carry = consume_scores(kt, sa_ref, carry, False)
        compute_scores(kt + 2, sa_ref)
        return consume_scores(kt + 1, sb_ref, carry, False)

    mask_last = not has_bias
    last = n_tiles - 1

    def odd_tail(carry):
        compute_scores(last, sb_ref)
        carry = consume_scores(last - 1, sa_ref, carry, False)
        return consume_scores(last, sb_ref, carry, mask_last)

    def even_tail(carry):
        return consume_scores(last, sa_ref, carry, mask_last)

    init = tuple((jnp.full((1, tq), NEG, F32), jnp.zeros((V_ROWS, tq), F32)) for _ in range(heads))
    compute_scores(0, sa_ref)
    carry = lax.fori_loop(0, last // 2, double_step, init)
    carry = lax.cond(last % 2 == 1, odd_tail, even_tail, carry)
    for h in range(heads):
        acc = carry[h][1]
        out = acc[0:LANES, :] / acc[LANES:LANES + 1, :]
        o_ref[:, h * LANES:(h + 1) * LANES] = out.T.astype(o_ref.dtype)


def _score_scratch(heads, tk):
    return [pltpu.VMEM((heads, tk, Q_TILE), F32), pltpu.VMEM((heads, tk, Q_TILE), F32)]


def _mla_attention(q, kn, kpe, vt):
    m = kn.shape[0]
    n_kt, _, tk = vt.shape
    hp = HEADS_PER_STEP
    return pl.pallas_call(
        functools.partial(_flash_kernel, q_axis=1, has_kpe=True, has_bias=False),
        grid=(MLA_HEADS // hp, m // Q_TILE),
        in_specs=[pl.BlockSpec((hp * 2 * LANES, Q_TILE), lambda h, i: (h, i)),
                  pl.BlockSpec((m, hp * LANES), lambda h, i: (0, h)),
                  pl.BlockSpec((m, LANES), lambda h, i: (0, 0)),
                  pl.BlockSpec((n_kt, hp * V_ROWS, tk), lambda h, i: (0, h, 0))],
        out_specs=pl.BlockSpec((Q_TILE, hp * MLA_V), lambda h, i: (i, h)),
        out_shape=jax.ShapeDtypeStruct((m, BRANCH), BF16),
        scratch_shapes=_score_scratch(hp, tk),
        compiler_params=_cp("parallel", "parallel"),
        name="mla_attention",
    )(q, kn, kpe, vt)


def _dsa_attention(q, k, vt, bias):
    m = k.shape[0]
    n_kt, _, tk = vt.shape
    hp = HEADS_PER_STEP
    return pl.pallas_call(
        functools.partial(_flash_kernel, q_axis=1, has_kpe=False, has_bias=True),
        grid=(DSA_HEADS // hp, m // Q_TILE),
        in_specs=[pl.BlockSpec((hp * DSA_DIM, Q_TILE), lambda h, i: (h, i)),
                  pl.BlockSpec((m, hp * DSA_DIM), lambda h, i: (0, h)),
                  pl.BlockSpec((n_kt, hp * V_ROWS, tk), lambda h, i: (0, h, 0)),
                  pl.BlockSpec((m, Q_TILE), lambda h, i: (0, i))],
        out_specs=pl.BlockSpec((Q_TILE, hp * DSA_DIM), lambda h, i: (i, h)),
        out_shape=jax.ShapeDtypeStruct((m, BRANCH), BF16),
        scratch_shapes=_score_scratch(hp, tk),
        compiler_params=_cp("parallel", "parallel"),
        name="dsa_attention",
    )(q, k, vt, bias)


def _rope_tables(rows, dim):
    inv = 1.0 / jnp.power(ROPE_THETA, jnp.arange(0, dim, 2, dtype=F32) / dim)
    ang = jnp.arange(rows, dtype=F32)[:, None] * inv[None, :]
    cos, sin = jnp.cos(ang), jnp.sin(ang)
    reps = LANES // dim
    return (jnp.tile(jnp.concatenate([cos, cos], axis=1), (1, reps)),
            jnp.tile(jnp.concatenate([-sin, sin], axis=1), (1, reps)))


W_IN_KR = 3 * BRANCH + MLA_Q_RANK + MLA_KV_RANK
W_IN_DQ = W_IN_KR + MLA_ROPE
W_IN_IK = W_IN_DQ + 4 * BRANCH
W_IN_IW = W_IN_IK + IDX_DIM
W_IN_PU = W_IN_IW + IDX_HEADS


def _small_w_in(w_t):
    zeros = lambda n: jnp.zeros((w_t.shape[0], n, w_t.shape[2]), w_t.dtype)
    return jnp.concatenate([w_t[:, W_IN_KR:W_IN_DQ], zeros(LANES - MLA_ROPE),
                            w_t[:, W_IN_IK:W_IN_IW], w_t[:, W_IN_IW:W_IN_PU],
                            zeros(LANES - IDX_DIM - IDX_HEADS)], axis=1)


def _layout_w_uq(w):
    w3 = w.reshape(MLA_Q_RANK, MLA_HEADS, MLA_NOPE + MLA_ROPE)
    w3 = jnp.pad(w3, ((0, 0), (0, 0), (0, 2 * LANES - MLA_NOPE - MLA_ROPE)))
    return w3.reshape(MLA_Q_RANK, MLA_HEADS * 2 * LANES).astype(BF16)


def _layout_w_ukv(w):
    w4 = w.reshape(MLA_KV_RANK, MLA_HEADS, 2, MLA_NOPE)
    return w4.transpose(0, 2, 1, 3).reshape(MLA_KV_RANK, 2 * BRANCH).astype(BF16)


def _forward(x, meta_tokens, norm_mix_pre, norm_mix_post, norm_ffn_pre, norm_ffn_post,
             w_in, conv_w, mla_q_norm, mla_w_uq, mla_kv_norm, mla_w_ukv, pool_w, pool_scale,
             w_branch, w_gate, b_gate, w_out, ffn_w_gate, ffn_w_up, ffn_w_down):
    assert x.shape[0] == 1 and x.shape[2] == D_MODEL
    depth = w_in.shape[0]
    seq = x.shape[1]
    length = N_META + seq
    topk = min(IDX_TOPK_MAX, length // 4)
    rows = -(-length // ROW_TILE) * ROW_TILE
    assert rows % Q_TILE == 0 and rows % K_TILE == 0 and rows % NORM_TILE == 0
    assert rows < (1 << 14)

    c64, s64 = _rope_tables(rows, 64)
    c128, s128 = _rope_tables(rows, 128)
    assert W_IN_KR == HALF_COLS
    w_in_t = jnp.swapaxes(w_in, 1, 2)
    w_in_small = _small_w_in(w_in_t)
    tn = 512
    wide = 2 * tn
    gate_tiles = D_MODEL // wide

    h, xn = _assemble_prenorm(x[0], meta_tokens.astype(F32), norm_mix_pre[0][None], rows)
    for l in range(depth):
        at_layer = lambda j, l=l: (l,)
        tile = lambda j: j
        proj_a = _wide_proj(xn, w_in_t, at_layer, tile, HALF_COLS // wide, ROW_TILE, tn, BF16,
                            "in_proj_a", weights_nk=True)
        proj_b = _wide_proj(xn, w_in_t, at_layer, tile, 4 * BRANCH // wide, ROW_TILE, tn, BF16,
                            "in_proj_b", weights_nk=True,
                            first_half_block=W_IN_DQ // tn, row_shift=W_IN_DQ % tn)
        proj_p = _wide_proj(xn, w_in_t, at_layer, tile, BRANCH // wide, ROW_TILE, tn, BF16,
                            "in_proj_pool", weights_nk=True,
                            first_half_block=W_IN_PU // tn, row_shift=W_IN_PU % tn)
        small = _stacked_proj(xn, w_in_small, at_layer, tile, 1, ROW_TILE, SMALL_COLS, F32,
                              "in_proj_small", weights_nk=True)
        gates = _wide_proj(xn, w_gate,
                           lambda j, l=l: (l, j // gate_tiles), lambda j: j % gate_tiles,
                           N_BRANCH * gate_tiles, ROW_TILE, tn, BF16, "gates",
                           bias=b_gate[:, :, None, :])

        y_conv = _conv_branch(proj_a, conv_w, l, ROW_TILE)
        y_pool = _pool_branch(proj_p, pool_w, pool_scale[:, None, :], l, ROW_TILE)

        q_m, kn_m, kpe_m, vt_m = _mla_prep(
            proj_a, small, mla_q_norm[:, None, :], mla_kv_norm[:, None, :],
            _layout_w_uq(mla_w_uq[l]), _layout_w_ukv(mla_w_ukv[l]), c64, s64, l, K_TILE)
        y_mla = _mla_attention(q_m, kn_m, kpe_m, vt_m)

        q_d, k_d, vt_d, iq_r, ik_ab, wt = _dsa_prep(proj_b, small, c128, s128, c64, s64, K_TILE)
        bias = _indexer(iq_r, ik_ab, wt, topk)
        y_dsa = _dsa_attention(q_d, k_d, vt_d, bias)

        merged = _gated_merge((y_conv, y_mla, y_dsa, y_pool), gates, w_branch, l, HALF_ROW_TILE, tn)
        mix = _wide_proj(merged, w_out, at_layer, tile, D_MODEL // wide, ROW_TILE, tn, F32, "out_proj")
        h, xn = _resid_norm(h, mix, norm_mix_post[l][None], norm_ffn_pre[l][None])

        act = _swiglu(xn, ffn_w_gate, ffn_w_up, l, ROW_TILE, 256)
        f = _wide_proj(act, ffn_w_down, at_layer, tile, D_MODEL // tn, HALF_ROW_TILE, tn // 2, F32,
                       "ffn_down")
        if l + 1 < depth:
            h, xn = _resid_norm(h, f, norm_ffn_post[l][None], norm_mix_pre[l + 1][None])
        else:
            out = _resid_out(h, f, norm_ffn_post[l][None], seq)

    return out[None]


def kernel(x, meta_tokens, norm_mix_pre, norm_mix_post, norm_ffn_pre, norm_ffn_post, w_in, conv_w, mla_q_norm, mla_w_uq, mla_kv_norm, mla_w_ukv, pool_w, pool_scale, w_branch, w_gate, b_gate, w_out, ffn_w_gate, ffn_w_up, ffn_w_down):
    return _forward(x, meta_tokens, norm_mix_pre, norm_mix_post, norm_ffn_pre, norm_ffn_post,
                    w_in, conv_w, mla_q_norm, mla_w_uq, mla_kv_norm, mla_w_ukv, pool_w, pool_scale,
                    w_branch, w_gate, b_gate, w_out, ffn_w_gate, ffn_w_up, ffn_w_down)
```

```python
import functools
import math

import jax
import jax.numpy as jnp
from jax import lax
from jax.experimental import pallas as pl
from jax.experimental.pallas import tpu as pltpu

F32 = jnp.float32
BF16 = jnp.bfloat16

D_MODEL = 4096
N_META = 16
ROPE_THETA = 10000.0
EPS = 1e-6
N_BRANCH = 4
BRANCH = 1024
CONV_K = 3
MLA_NOPE, MLA_ROPE, MLA_V, MLA_HEADS = 128, 64, 128, 8
MLA_Q_RANK, MLA_KV_RANK = 1536, 512
DSA_DIM, DSA_HEADS = 128, 8
IDX_HEADS, IDX_DIM, IDX_TOPK_MAX = 16, 64, 256
POOL_WINDOWS = (2, 4, 8, 16)
POOL_GROUP = 256
D_FF = 11008

LANES = 128
HALO = 16
ROW_TILE = 768
HALF_ROW_TILE = ROW_TILE // 2
Q_TILE = 256
K_TILE = 256
V_ROWS = 144
HEADS_PER_STEP = 4
IDX_K_TILE = 256
NEXT_ROWS = 256
COUNT_CHUNK = 256
COUNT_ACCS = 4
WRITE_CHUNK = 64
NORM_TILE = 192
NEG = -1e30
LOG2E = math.log2(math.e)
VMEM_LIMIT = 58 * 1024 * 1024

HALF_COLS = 5120
SMALL_COLS = 256


def _cp(*sem):
    return pltpu.CompilerParams(dimension_semantics=sem, vmem_limit_bytes=VMEM_LIMIT)


def _sigmoid(x):
    return 1.0 / (1.0 + jnp.exp(-x))


def _rms(x, g):
    return x * lax.rsqrt(jnp.mean(x * x, axis=-1, keepdims=True) + EPS) * g


def _matmul_w(a, wbf, weights_nk):
    if weights_nk:
        return lax.dot_general(a, wbf, (((1,), (1,)), ((), ())), preferred_element_type=F32)
    return jnp.dot(a, wbf, preferred_element_type=F32)


def _proj_kernel(a_ref, w_ref, o_ref, wbf_ref, *, weights_nk):
    @pl.when(pl.program_id(1) == 0)
    def _():
        wbf_ref[...] = w_ref[...].astype(wbf_ref.dtype)

    o_ref[...] = _matmul_w(a_ref[...], wbf_ref[...], weights_nk).astype(o_ref.dtype)


def _stacked_proj(a, w, lead_of, col_of, n_tiles, tm, tn, out_dtype, name,
                  weights_nk=False):
    m, k = a.shape
    lead = (None,) * (w.ndim - 2)
    if weights_nk:
        assert w.shape[-1] == k
        w_block, w_index = (tn, k), lambda j, i: (*lead_of(j), col_of(j), 0)
    else:
        assert w.shape[-2] == k
        w_block, w_index = (k, tn), lambda j, i: (*lead_of(j), 0, col_of(j))
    return pl.pallas_call(
        functools.partial(_proj_kernel, weights_nk=weights_nk),
        grid=(n_tiles, m // tm),
        in_specs=[pl.BlockSpec((tm, k), lambda j, i: (i, 0)),
                  pl.BlockSpec(lead + w_block, w_index)],
        out_specs=pl.BlockSpec((tm, tn), lambda j, i: (i, j)),
        out_shape=jax.ShapeDtypeStruct((m, n_tiles * tn), out_dtype),
        scratch_shapes=[pltpu.VMEM(w_block, BF16)],
        compiler_params=_cp("parallel", "arbitrary"),
        name=name,
    )(a, w)


def _wide_proj_kernel(*refs, has_bias, weights_nk, row_shift):
    refs = list(refs)
    a_ref, w_ref = refs[0], refs[1]
    pos = 2
    next_ref = b_ref = None
    if row_shift:
        next_ref = refs[pos]
        pos += 1
    if has_bias:
        b_ref = refs[pos]
        pos += 1
    o_ref, wbf_ref = refs[pos], refs[pos + 1]
    half = o_ref.shape[1] // 2
    i = pl.program_id(1)

    def cast_half(lo):
        if not weights_nk:
            wbf_ref[:, lo:lo + half] = w_ref[...].astype(wbf_ref.dtype)
        elif not row_shift:
            wbf_ref[lo:lo + half, :] = w_ref[...].astype(wbf_ref.dtype)
        else:
            keep = half - row_shift
            wbf_ref[lo:lo + keep, :] = w_ref[row_shift:half, :].astype(wbf_ref.dtype)
            wbf_ref[lo + keep:lo + half, :] = next_ref[0:row_shift, :].astype(wbf_ref.dtype)

    def emit(lo, width):
        wbf = wbf_ref[lo:lo + width, :] if weights_nk else wbf_ref[:, lo:lo + width]
        z = _matmul_w(a_ref[...], wbf, weights_nk)
        if has_bias:
            z = _sigmoid(z + b_ref[:, lo:lo + width])
        o_ref[:, lo:lo + width] = z.astype(o_ref.dtype)

    @pl.when(i == 0)
    def _():
        cast_half(0)
        emit(0, half)

    @pl.when(i == 1)
    def _():
        cast_half(half)
        emit(half, half)

    @pl.when(i >= 2)
    def _():
        emit(0, 2 * half)


def _wide_proj(a, w, lead_of, col_of, n_tiles, tm, half, out_dtype, name, bias=None,
               weights_nk=False, first_half_block=0, row_shift=0):
    m, k = a.shape
    assert w.shape[-1 if weights_nk else -2] == k
    assert row_shift == 0 or (weights_nk and row_shift % 8 == 0 and row_shift <= NEXT_ROWS)
    lead = (None,) * (w.ndim - 2)
    row = lambda i: jnp.maximum(i - 1, 0)

    def weight_window(j, i):
        jw = jnp.where(i >= 2, jnp.minimum(j + 1, n_tiles - 1), j)
        return lead_of(jw), first_half_block + 2 * col_of(jw) + jnp.where(i == 1, 1, 0)

    def w_index(j, i):
        stacked, hb = weight_window(j, i)
        return (*stacked, hb, 0) if weights_nk else (*stacked, 0, hb)

    in_specs = [pl.BlockSpec((tm, k), lambda j, i: (row(i), 0)),
                pl.BlockSpec(lead + ((half, k) if weights_nk else (k, half)), w_index)]
    args = [a, w]
    if row_shift:
        def next_index(j, i):
            stacked, hb = weight_window(j, i)
            return (*stacked, (hb + 1) * (half // NEXT_ROWS), 0)

        in_specs.append(pl.BlockSpec(lead + (NEXT_ROWS, k), next_index))
        args.append(w)
    if bias is not None:
        in_specs.append(pl.BlockSpec(lead + (1, 2 * half), lambda j, i: (*lead_of(j), 0, col_of(j))))
        args.append(bias)
    return pl.pallas_call(
        functools.partial(_wide_proj_kernel, has_bias=bias is not None, weights_nk=weights_nk,
                          row_shift=row_shift),
        grid=(n_tiles, m // tm + 1),
        in_specs=in_specs,
        out_specs=pl.BlockSpec((tm, 2 * half), lambda j, i: (row(i), j)),
        out_shape=jax.ShapeDtypeStruct((m, n_tiles * 2 * half), out_dtype),
        scratch_shapes=[pltpu.VMEM((2 * half, k) if weights_nk else (k, 2 * half), BF16)],
        compiler_params=_cp("parallel", "arbitrary"),
        name=name,
    )(*args)


def _swiglu_kernel(a_ref, wg_ref, wu_ref, o_ref, wgbf_ref, wubf_ref, *, n_half_blocks):
    half = wg_ref.shape[1]
    j = pl.program_id(0)
    i = pl.program_id(1)

    def cast_half(lo):
        wgbf_ref[:, lo:lo + half] = wg_ref[...].astype(wgbf_ref.dtype)
        wubf_ref[:, lo:lo + half] = wu_ref[...].astype(wubf_ref.dtype)

    def emit(lo, width):
        a = a_ref[...]
        g = jnp.dot(a, wgbf_ref[:, lo:lo + width], preferred_element_type=F32)
        u = jnp.dot(a, wubf_ref[:, lo:lo + width], preferred_element_type=F32)
        o_ref[:, lo:lo + width] = (g * _sigmoid(g) * u).astype(o_ref.dtype)

    has_right = 2 * j + 1 < n_half_blocks

    @pl.when(i == 0)
    def _():
        cast_half(0)
        emit(0, half)

    @pl.when((i == 1) & has_right)
    def _():
        cast_half(half)
        emit(half, half)

    @pl.when((i >= 2) & has_right)
    def _():
        emit(0, 2 * half)

    @pl.when((i >= 2) & jnp.logical_not(has_right))
    def _():
        emit(0, half)


def _swiglu(a, wg, wu, layer, tm, half):
    m, k = a.shape
    n = wg.shape[2]
    n_half_blocks = n // half
    assert n_half_blocks * half == n
    n_tiles = -(-n_half_blocks // 2)
    row = lambda i: jnp.maximum(i - 1, 0)
    def w_index(j, i):
        jw = jnp.where(i >= 2, jnp.minimum(j + 1, n_tiles - 1), j)
        return (layer, 0, jnp.minimum(2 * jw + jnp.where(i == 1, 1, 0), n_half_blocks - 1))

    w_spec = pl.BlockSpec((None, k, half), w_index)
    return pl.pallas_call(
        functools.partial(_swiglu_kernel, n_half_blocks=n_half_blocks),
        grid=(n_tiles, m // tm + 1),
        in_specs=[pl.BlockSpec((tm, k), lambda j, i: (row(i), 0)), w_spec, w_spec],
        out_specs=pl.BlockSpec((tm, 2 * half), lambda j, i: (row(i), j)),
        out_shape=jax.ShapeDtypeStruct((m, n), BF16),
        scratch_shapes=[pltpu.VMEM((k, 2 * half), BF16), pltpu.VMEM((k, 2 * half), BF16)],
        compiler_params=_cp("parallel", "arbitrary"),
        name="ffn_swiglu",
    )(a, wg, wu)


def _merge_kernel(y0_ref, y1_ref, y2_ref, y3_ref, g0_ref, g1_ref, g2_ref, g3_ref, w_ref,
                  o_ref, wbf_ref):
    half = w_ref.shape[2]
    i = pl.program_id(1)
    branches = ((y0_ref, g0_ref), (y1_ref, g1_ref), (y2_ref, g2_ref), (y3_ref, g3_ref))

    def cast_half(lo):
        wbf_ref[:, :, lo:lo + half] = w_ref[...].astype(wbf_ref.dtype)

    def emit(lo, width):
        acc = None
        for b, (y_ref, g_ref) in enumerate(branches):
            val = g_ref[:, lo:lo + width].astype(F32) * jnp.dot(
                y_ref[...], wbf_ref[b, :, lo:lo + width], preferred_element_type=F32)
            acc = val if acc is None else acc + val
        o_ref[:, lo:lo + width] = acc.astype(o_ref.dtype)

    @pl.when(i == 0)
    def _():
        cast_half(0)
        emit(0, half)

    @pl.when(i == 1)
    def _():
        cast_half(half)
        emit(half, half)

    @pl.when(i >= 2)
    def _():
        emit(0, 2 * half)


def _gated_merge(ys, gates, w_branch, layer, tm, half):
    m = ys[0].shape[0]
    nj = D_MODEL // (2 * half)
    row = lambda i: jnp.maximum(i - 1, 0)

    def w_index(j, i):
        jw = jnp.where(i >= 2, jnp.minimum(j + 1, nj - 1), j)
        return (layer, 0, 0, 2 * jw + jnp.where(i == 1, 1, 0))

    y_spec = pl.BlockSpec((tm, BRANCH), lambda j, i: (row(i), 0))
    g_specs = [pl.BlockSpec((tm, 2 * half),
                            functools.partial(lambda j, i, b: (row(i), b * nj + j), b=b))
               for b in range(N_BRANCH)]
    return pl.pallas_call(
        _merge_kernel,
        grid=(nj, m // tm + 1),
        in_specs=[y_spec] * N_BRANCH + g_specs
                 + [pl.BlockSpec((None, N_BRANCH, BRANCH, half), w_index)],
        out_specs=pl.BlockSpec((tm, 2 * half), lambda j, i: (row(i), j)),
        out_shape=jax.ShapeDtypeStruct((m, D_MODEL), BF16),
        scratch_shapes=[pltpu.VMEM((N_BRANCH, BRANCH, 2 * half), BF16)],
        compiler_params=_cp("parallel", "arbitrary"),
        name="gated_merge",
    )(*ys, gates, gates, gates, gates, w_branch)


def _assemble_kernel(x_ref, prev_ref, meta_ref, g_ref, h_ref, xn_ref, *, length):
    tr = x_ref.shape[0]
    i = pl.program_id(0)
    head = jnp.where(i == 0, meta_ref[...], prev_ref[...])
    tile = jnp.concatenate([head, x_ref[0:tr - N_META, :]], axis=0)
    pos = i * tr + lax.broadcasted_iota(jnp.int32, (tr, 1), 0)
    h = jnp.where(pos < length, tile, 0.0)
    h_ref[...] = h
    xn_ref[...] = _rms(h, g_ref[...]).astype(xn_ref.dtype)


def _assemble_prenorm(x2d, meta, g, rows):
    seq = x2d.shape[0]
    assert N_META == HALO and NORM_TILE % N_META == 0
    last_x = (seq - 1) // NORM_TILE
    last_prev = (seq - 1) // N_META
    per_tile = NORM_TILE // N_META
    row = pl.BlockSpec((NORM_TILE, D_MODEL), lambda i: (i, 0))
    return pl.pallas_call(
        functools.partial(_assemble_kernel, length=N_META + seq),
        grid=(rows // NORM_TILE,),
        in_specs=[pl.BlockSpec((NORM_TILE, D_MODEL), lambda i: (jnp.minimum(i, last_x), 0)),
                  pl.BlockSpec((N_META, D_MODEL),
                               lambda i: (jnp.clip(i * per_tile - 1, 0, last_prev), 0)),
                  pl.BlockSpec((N_META, D_MODEL), lambda i: (0, 0)),
                  pl.BlockSpec((1, D_MODEL), lambda i: (0, 0))],
        out_specs=[row, row],
        out_shape=[jax.ShapeDtypeStruct((rows, D_MODEL), F32),
                   jax.ShapeDtypeStruct((rows, D_MODEL), BF16)],
        compiler_params=_cp("parallel"),
        name="assemble_prenorm",
    )(x2d, x2d, meta, g)


def _resid_norm_kernel(h_ref, o_ref, gpost_ref, gnext_ref, hn_ref, xn_ref):
    hn = h_ref[...] + _rms(o_ref[...], gpost_ref[...])
    hn_ref[...] = hn
    xn_ref[...] = _rms(hn, gnext_ref[...]).astype(xn_ref.dtype)


def _resid_out_kernel(h_ref, o_ref, hnext_ref, onext_ref, gpost_ref, out_ref):
    g = gpost_ref[...]
    cur = h_ref[N_META:, :] + _rms(o_ref[N_META:, :], g)
    nxt = hnext_ref[...] + _rms(onext_ref[...], g)
    out_ref[...] = jnp.concatenate([cur, nxt], axis=0)


def _resid_norm(h, o, g_post, g_next):
    m = h.shape[0]
    row = pl.BlockSpec((NORM_TILE, D_MODEL), lambda i: (i, 0))
    gain = pl.BlockSpec((1, D_MODEL), lambda i: (0, 0))
    return pl.pallas_call(
        _resid_norm_kernel,
        grid=(m // NORM_TILE,),
        in_specs=[row, row, gain, gain],
        out_specs=[row, row],
        out_shape=[jax.ShapeDtypeStruct((m, D_MODEL), F32),
                   jax.ShapeDtypeStruct((m, D_MODEL), BF16)],
        compiler_params=_cp("parallel"),
        name="resid_norm",
    )(h, o, g_post, g_next)


def _resid_out(h, o, g_post, seq):
    m = h.shape[0]
    per_tile = NORM_TILE // N_META
    last_head = m // N_META - 1
    row = pl.BlockSpec((NORM_TILE, D_MODEL), lambda i: (i, 0))
    head = pl.BlockSpec((N_META, D_MODEL), lambda i: (jnp.minimum((i + 1) * per_tile, last_head), 0))
    return pl.pallas_call(
        _resid_out_kernel,
        grid=(-(-seq // NORM_TILE),),
        in_specs=[row, row, head, head, pl.BlockSpec((1, D_MODEL), lambda i: (0, 0))],
        out_specs=row,
        out_shape=jax.ShapeDtypeStruct((seq, D_MODEL), F32),
        compiler_params=_cp("parallel"),
        name="resid_out",
    )(h, o, h, o, g_post)


def _rope_half128(x, cos, sin_signed):
    return x * cos + pltpu.roll(x, 64, axis=1) * sin_signed


def _rope_half64(x, cos, sin_signed):
    lane = lax.broadcasted_iota(jnp.int32, x.shape, 1)
    partner = jnp.where((lane & 32) == 0, pltpu.roll(x, 96, axis=1), pltpu.roll(x, 32, axis=1))
    return x * cos + partner * sin_signed


def _conv_kernel(cb_ref, cc_ref, cu_ref, pc_ref, pu_ref, w_ref, y_ref, z_ref):
    tr = cb_ref.shape[0]
    i = pl.program_id(0)
    z = cc_ref[...].astype(F32) * cu_ref[...].astype(F32)
    zp = pc_ref[...].astype(F32) * pu_ref[...].astype(F32)
    z_ref[0:HALO, :] = jnp.where(i > 0, zp, 0.0)
    z_ref[HALO:HALO + tr, :] = z
    w = w_ref[...]
    conv = (w[2:3, :] * z
            + w[1:2, :] * z_ref[HALO - 1:HALO - 1 + tr, :]
            + w[0:1, :] * z_ref[HALO - 2:HALO - 2 + tr, :])
    y_ref[...] = (cb_ref[...].astype(F32) * conv).astype(y_ref.dtype)


def _conv_branch(proj, conv_w, layer, tr):
    m = proj.shape[0]
    cw = 256
    nb = BRANCH // cw

    def halo_row(i):
        return jnp.maximum(i * (tr // HALO) - 1, 0)

    return pl.pallas_call(
        _conv_kernel,
        grid=(m // tr, nb),
        in_specs=[pl.BlockSpec((tr, cw), lambda i, c: (i, c)),
                  pl.BlockSpec((tr, cw), lambda i, c: (i, nb + c)),
                  pl.BlockSpec((tr, cw), lambda i, c: (i, 2 * nb + c)),
                  pl.BlockSpec((HALO, cw), lambda i, c: (halo_row(i), nb + c)),
                  pl.BlockSpec((HALO, cw), lambda i, c: (halo_row(i), 2 * nb + c)),
                  pl.BlockSpec((None, CONV_K, cw), lambda i, c: (layer, 0, c))],
        out_specs=pl.BlockSpec((tr, cw), lambda i, c: (i, c)),
        out_shape=jax.ShapeDtypeStruct((m, BRANCH), BF16),
        scratch_shapes=[pltpu.VMEM((HALO + tr, cw), F32)],
        compiler_params=_cp("parallel", "parallel"),
        name="conv_branch",
    )(proj, proj, proj, proj, proj, conv_w)


def _pool_kernel(u_ref, pu_ref, w_ref, s_ref, y_ref, x_ref):
    tr = u_ref.shape[0]
    i = pl.program_id(0)
    x_ref[0:HALO, :] = jnp.where(i > 0, pu_ref[...].astype(F32), 0.0)
    x_ref[HALO:HALO + tr, :] = u_ref[...].astype(F32)
    t = i * tr + lax.broadcasted_iota(jnp.int32, (tr, POOL_GROUP), 0)
    for g, win in enumerate(POOL_WINDOWS):
        cols = slice(g * POOL_GROUP, (g + 1) * POOL_GROUP)
        x = x_ref[HALO:HALO + tr, cols]
        total = x
        for j in range(1, win):
            total = total + x_ref[HALO - j:HALO - j + tr, cols]
        count = jnp.minimum(t + 1, win).astype(F32)
        pooled = (total / count - x).astype(BF16)
        mixed = jnp.dot(pooled, w_ref[g].astype(BF16), preferred_element_type=F32)
        y_ref[:, cols] = (mixed * s_ref[:, cols]).astype(y_ref.dtype)


def _pool_branch(proj, pool_w, pool_scale, layer, tr):
    m = proj.shape[0]
    assert max(POOL_WINDOWS) <= HALO
    n_groups = len(POOL_WINDOWS)
    return pl.pallas_call(
        _pool_kernel,
        grid=(m // tr,),
        in_specs=[pl.BlockSpec((tr, BRANCH), lambda i: (i, 0)),
                  pl.BlockSpec((HALO, BRANCH), lambda i: (jnp.maximum(i * (tr // HALO) - 1, 0), 0)),
                  pl.BlockSpec((None, n_groups, POOL_GROUP, POOL_GROUP), lambda i: (layer, 0, 0, 0)),
                  pl.BlockSpec((None, 1, BRANCH), lambda i: (layer, 0, 0))],
        out_specs=pl.BlockSpec((tr, BRANCH), lambda i: (i, 0)),
        out_shape=jax.ShapeDtypeStruct((m, BRANCH), BF16),
        scratch_shapes=[pltpu.VMEM((HALO + tr, BRANCH), F32)],
        compiler_params=_cp("parallel"),
        name="pool_branch",
    )(proj, proj, pool_w, pool_scale)


def _store_value_t(vt_ref, v, heads):
    tr = v.shape[0]
    n_kt, _, tk = vt_ref.shape
    assert n_kt * tk == tr
    vt = v.T.astype(vt_ref.dtype)
    ones = jnp.ones((V_ROWS - LANES, tk), vt_ref.dtype)
    for c in range(n_kt):
        for h in range(heads):
            vt_ref[c, h * V_ROWS:h * V_ROWS + LANES, :] = vt[h * LANES:(h + 1) * LANES,
                                                             c * tk:(c + 1) * tk]
            vt_ref[c, h * V_ROWS + LANES:(h + 1) * V_ROWS, :] = ones


def _mla_prep_kernel(cq_ref, ckv_ref, sm_ref, gq_ref, gkv_ref, wuq_ref, wukv_ref,
                     cos_ref, sin_ref, qt_ref, kn_ref, kpe_ref, vt_ref):
    cos = cos_ref[...]
    sin = sin_ref[...]
    scale = (MLA_NOPE + MLA_ROPE) ** -0.5 * LOG2E
    cqn = _rms(cq_ref[...].astype(F32), gq_ref[...]).astype(BF16)
    q = jnp.dot(cqn, wuq_ref[...], preferred_element_type=F32)
    for h in range(MLA_HEADS):
        lo = 2 * h * LANES
        qt_ref[lo:lo + LANES, :] = (q[:, lo:lo + LANES] * scale).T.astype(qt_ref.dtype)
        pe = _rope_half64(q[:, lo + LANES:lo + 2 * LANES], cos, sin)
        qt_ref[lo + LANES:lo + 2 * LANES, :] = (pe * scale).T.astype(qt_ref.dtype)
    ckvn = _rms(ckv_ref[...].astype(F32), gkv_ref[...]).astype(BF16)
    kv = jnp.dot(ckvn, wukv_ref[...], preferred_element_type=F32)
    kn_ref[...] = kv[:, :BRANCH].astype(kn_ref.dtype)
    _store_value_t(vt_ref, kv[:, BRANCH:], MLA_HEADS)
    kpe_ref[...] = _rope_half64(sm_ref[:, 0:LANES], cos, sin).astype(kpe_ref.dtype)


def _mla_prep(proj, small, gq, gkv, wuq, wukv, cos64, sin64, layer, tr):
    m = proj.shape[0]
    qw = 2 * LANES * MLA_HEADS
    return pl.pallas_call(
        _mla_prep_kernel,
        grid=(m // tr,),
        in_specs=[pl.BlockSpec((tr, MLA_Q_RANK), lambda i: (i, 3 * BRANCH // MLA_Q_RANK)),
                  pl.BlockSpec((tr, MLA_KV_RANK), lambda i: (i, (3 * BRANCH + MLA_Q_RANK) // MLA_KV_RANK)),
                  pl.BlockSpec((tr, SMALL_COLS), lambda i: (i, 0)),
                  pl.BlockSpec((None, 1, MLA_Q_RANK), lambda i: (layer, 0, 0)),
                  pl.BlockSpec((None, 1, MLA_KV_RANK), lambda i: (layer, 0, 0)),
                  pl.BlockSpec((MLA_Q_RANK, qw), lambda i: (0, 0)),
                  pl.BlockSpec((MLA_KV_RANK, 2 * BRANCH), lambda i: (0, 0)),
                  pl.BlockSpec((tr, LANES), lambda i: (i, 0)),
                  pl.BlockSpec((tr, LANES), lambda i: (i, 0))],
        out_specs=[pl.BlockSpec((qw, tr), lambda i: (0, i)),
                   pl.BlockSpec((tr, BRANCH), lambda i: (i, 0)),
                   pl.BlockSpec((tr, LANES), lambda i: (i, 0)),
                   pl.BlockSpec((tr // K_TILE, MLA_HEADS * V_ROWS, K_TILE), lambda i: (i, 0, 0))],
        out_shape=[jax.ShapeDtypeStruct((qw, m), BF16),
                   jax.ShapeDtypeStruct((m, BRANCH), BF16),
                   jax.ShapeDtypeStruct((m, LANES), BF16),
                   jax.ShapeDtypeStruct((m // K_TILE, MLA_HEADS * V_ROWS, K_TILE), BF16)],
        compiler_params=_cp("parallel"),
        name="mla_prep",
    )(proj, proj, small, gq, gkv, wuq, wukv, cos64, sin64)


def _dsa_prep_kernel(dq_ref, dk_ref, dv_ref, iq_ref, sm_ref, c128_ref, s128_ref, c64_ref, s64_ref,
                     qt_ref, k_ref, vt_ref, iqt_ref, ik_ref, wt_ref):
    c128, s128 = c128_ref[...], s128_ref[...]
    c64, s64 = c64_ref[...], s64_ref[...]
    scale = DSA_DIM ** -0.5 * LOG2E
    for h in range(BRANCH // LANES):
        sl = slice(h * LANES, (h + 1) * LANES)
        qt_ref[sl, :] = (_rope_half128(dq_ref[:, sl].astype(F32), c128, s128) * scale).T.astype(qt_ref.dtype)
        k_ref[:, sl] = _rope_half128(dk_ref[:, sl].astype(F32), c128, s128).astype(k_ref.dtype)
        iqt_ref[sl, :] = _rope_half64(iq_ref[:, sl].astype(F32), c64, s64).T.astype(iqt_ref.dtype)
    _store_value_t(vt_ref, dv_ref[...].astype(F32), DSA_HEADS)
    tail = sm_ref[:, LANES:2 * LANES]
    lane = lax.broadcasted_iota(jnp.int32, tail.shape, 1)
    ik_lo = jnp.where(lane < IDX_DIM, _rope_half64(tail, c64, s64), 0.0)
    ik_ref[:, 0:LANES] = ik_lo.astype(ik_ref.dtype)
    ik_ref[:, LANES:2 * LANES] = pltpu.roll(ik_lo, IDX_DIM, axis=1).astype(ik_ref.dtype)
    idx_w_scale = (IDX_HEADS ** -0.5) * (IDX_DIM ** -0.5)
    wt_ref[...] = (tail * idx_w_scale).T


def _dsa_prep(proj_b, small, c128, s128, c64, s64, tr):
    m = proj_b.shape[0]
    col = lambda c: pl.BlockSpec((tr, BRANCH), lambda i: (i, c))
    tab = pl.BlockSpec((tr, LANES), lambda i: (i, 0))
    return pl.pallas_call(
        _dsa_prep_kernel,
        grid=(m // tr,),
        in_specs=[col(0), col(1), col(2), col(3),
                  pl.BlockSpec((tr, SMALL_COLS), lambda i: (i, 0)),
                  tab, tab, tab, tab],
        out_specs=[pl.BlockSpec((BRANCH, tr), lambda i: (0, i)),
                   pl.BlockSpec((tr, BRANCH), lambda i: (i, 0)),
                   pl.BlockSpec((tr // K_TILE, DSA_HEADS * V_ROWS, K_TILE), lambda i: (i, 0, 0)),
                   pl.BlockSpec((BRANCH, tr), lambda i: (0, i)),
                   pl.BlockSpec((tr, 2 * LANES), lambda i: (i, 0)),
                   pl.BlockSpec((LANES, tr), lambda i: (0, i))],
        out_shape=[jax.ShapeDtypeStruct((BRANCH, m), BF16),
                   jax.ShapeDtypeStruct((m, BRANCH), BF16),
                   jax.ShapeDtypeStruct((m // K_TILE, DSA_HEADS * V_ROWS, K_TILE), BF16),
                   jax.ShapeDtypeStruct((BRANCH, m), BF16),
                   jax.ShapeDtypeStruct((m, 2 * LANES), BF16),
                   jax.ShapeDtypeStruct((LANES, m), F32)],
        compiler_params=_cp("parallel"),
        name="dsa_prep",
    )(proj_b, proj_b, proj_b, proj_b, small, c128, s128, c64, s64)


def _float_to_ordered_int(s):
    b = lax.bitcast_convert_type(s, jnp.int32)
    return b ^ ((b >> 31) & jnp.int32(0x7FFFFFFF))


def _indexer_kernel(iq_ref, ik_ref, wt_ref, bias_ref, key_ref, cut_ref, high_ref, low_ref, *, topk):
    tq = iq_ref.shape[1]
    total_rows = ik_ref.shape[0]
    tk = IDX_K_TILE
    i = pl.program_id(0)
    n_tiles = (i * tq + tq) // tk
    n_rows = n_tiles * tk
    qpos = i * tq + lax.broadcasted_iota(jnp.int32, (1, tq), 1)
    int_min = jnp.int32(-2 ** 31)

    def score_tile(kt, carry):
        start = pl.multiple_of(kt * tk, tk)
        ik_lo = ik_ref[pl.ds(start, tk), 0:LANES]
        ik_hi = ik_ref[pl.ds(start, tk), LANES:2 * LANES]
        acc = jnp.zeros((tk, tq), F32)
        for j in range(IDX_HEADS // 2):
            qpair_t = iq_ref[j * LANES:(j + 1) * LANES, :]
            for half, ik in enumerate((ik_lo, ik_hi)):
                g = 2 * j + half
                dots = jnp.dot(ik, qpair_t, preferred_element_type=F32)
                acc = acc + jnp.maximum(dots, 0.0) * wt_ref[IDX_DIM + g:IDX_DIM + g + 1, :]
        acc = acc + 0.0
        kpos = start + lax.broadcasted_iota(jnp.int32, (tk, 1), 0)
        key = jnp.where(kpos <= qpos, _float_to_ordered_int(acc), int_min)
        key_ref[pl.ds(start, tk), :] = key
        high_ref[pl.ds(start, tk), :] = (key >> 16).astype(jnp.int16)
        return carry

    lax.fori_loop(0, n_tiles, score_tile, 0)

    n_chunks = n_rows // COUNT_CHUNK

    def count(pred, with_pos=False):
        def body(c, accs):
            start = pl.multiple_of(c * COUNT_CHUNK, COUNT_CHUNK)
            accs = list(accs)
            chunk = key_ref[pl.ds(start, COUNT_CHUNK), :]
            for r in range(COUNT_CHUNK // 8):
                blk = chunk[8 * r:8 * r + 8, :]
                if with_pos:
                    pos = start + 8 * r + lax.broadcasted_iota(jnp.int32, (8, 1), 0)
                    hit = pred(blk, pos)
                else:
                    hit = pred(blk)
                accs[r % COUNT_ACCS] = accs[r % COUNT_ACCS] + jnp.where(hit, 1, 0).astype(jnp.int32)
            return tuple(accs)
        zero = jnp.zeros((8, tq), jnp.int32)
        accs = lax.fori_loop(0, n_chunks, body, (zero,) * COUNT_ACCS)
        return jnp.sum(functools.reduce(lambda a, b: a + b, accs), axis=0, keepdims=True)

    int16_min = -(1 << 15)

    def count16(ref, cand, strict=False):
        cand16 = jnp.broadcast_to(cand, (16, tq)).astype(jnp.int16)
        one16 = jnp.ones((16, tq), jnp.int16)
        zero16 = jnp.zeros((16, tq), jnp.int16)

        def body(c, accs):
            start = pl.multiple_of(c * COUNT_CHUNK, COUNT_CHUNK)
            accs = list(accs)
            chunk = ref[pl.ds(start, COUNT_CHUNK), :]
            for r in range(COUNT_CHUNK // 16):
                blk = chunk[16 * r:16 * r + 16, :]
                hit = (blk > cand16) if strict else (blk >= cand16)
                accs[r % COUNT_ACCS] = accs[r % COUNT_ACCS] + jnp.where(hit, one16, zero16)
            return tuple(accs)
        zero = jnp.zeros((16, tq), jnp.int16)
        accs = lax.fori_loop(0, n_chunks, body, (zero,) * COUNT_ACCS)
        total = functools.reduce(lambda a, b: a + b, [a.astype(jnp.int32) for a in accs])
        return jnp.sum(total, axis=0, keepdims=True)

    def largest16(ref, want):
        def step(b, t):
            cand = t + jnp.left_shift(jnp.int32(1), 15 - b)
            return jnp.where(count16(ref, cand) >= want, cand, t)
        return lax.fori_loop(0, 16, step, jnp.full((1, tq), int16_min, jnp.int32))

    tau_high = largest16(high_ref, topk)
    n_above = count16(high_ref, tau_high, strict=True)

    def park_low(c, carry):
        start = pl.multiple_of(c * COUNT_CHUNK, COUNT_CHUNK)
        key = key_ref[pl.ds(start, COUNT_CHUNK), :]
        low = (key & 0xFFFF) + int16_min
        low_ref[pl.ds(start, COUNT_CHUNK), :] = jnp.where(
            (key >> 16) == tau_high, low, int16_min).astype(jnp.int16)
        return carry

    lax.fori_loop(0, n_chunks, park_low, 0)
    tau_low = largest16(low_ref, topk - n_above)
    tau = jnp.left_shift(tau_high, 16) + (tau_low - int16_min)

    n_gt = count(lambda blk: blk > tau)
    n_eq = count(lambda blk: blk == tau)
    need = topk - n_gt
    cut_bits = 14
    cut_ref[...] = jnp.full((1, tq), 1 << cut_bits, jnp.int32)

    @pl.when(jnp.max(n_eq - need) > 0)
    def _():
        def cut_step(b, cut):
            cand = cut + jnp.left_shift(jnp.int32(1), cut_bits - 1 - b)
            cnt = count(lambda blk, pos: (blk == tau) & (pos < cand), with_pos=True)
            return jnp.where(cnt <= need, cand, cut)
        cut_ref[...] = lax.fori_loop(0, cut_bits, cut_step, jnp.zeros((1, tq), jnp.int32))

    cut = cut_ref[...]

    def write_sel(c, carry):
        start = pl.multiple_of(c * WRITE_CHUNK, WRITE_CHUNK)
        blk = key_ref[pl.ds(start, WRITE_CHUNK), :]
        pos = start + lax.broadcasted_iota(jnp.int32, (WRITE_CHUNK, 1), 0)
        causal = jnp.where(pos <= qpos, 0.0, NEG)
        tie = jnp.where(blk == tau, jnp.where(pos < cut, causal, NEG), NEG)
        bias_ref[pl.ds(start, WRITE_CHUNK), :] = jnp.where(blk > tau, causal, tie).astype(bias_ref.dtype)
        return carry

    lax.fori_loop(0, n_rows // WRITE_CHUNK, write_sel, 0)

    def write_neg(c, carry):
        start = pl.multiple_of(c * WRITE_CHUNK, WRITE_CHUNK)
        bias_ref[pl.ds(start, WRITE_CHUNK), :] = jnp.full((WRITE_CHUNK, tq), NEG, bias_ref.dtype)
        return carry

    lax.fori_loop(n_rows // WRITE_CHUNK, total_rows // WRITE_CHUNK, write_neg, 0)


def _indexer(iq_t, ik_ab, wt, topk):
    m = ik_ab.shape[0]
    return pl.pallas_call(
        functools.partial(_indexer_kernel, topk=topk),
        grid=(m // Q_TILE,),
        in_specs=[pl.BlockSpec((BRANCH, Q_TILE), lambda i: (0, i)),
                  pl.BlockSpec((m, 2 * LANES), lambda i: (0, 0)),
                  pl.BlockSpec((LANES, Q_TILE), lambda i: (0, i))],
        out_specs=pl.BlockSpec((m, Q_TILE), lambda i: (0, i)),
        out_shape=jax.ShapeDtypeStruct((m, m), BF16),
        scratch_shapes=[pltpu.VMEM((m, Q_TILE), jnp.int32),
                        pltpu.VMEM((1, Q_TILE), jnp.int32),
                        pltpu.VMEM((m, Q_TILE), jnp.int16),
                        pltpu.VMEM((m, Q_TILE), jnp.int16)],
        compiler_params=_cp("parallel"),
        name="dsa_indexer",
    )(iq_t, ik_ab, wt)


def _flash_kernel(*refs, q_axis, has_kpe, has_bias):
    refs = list(refs)
    q_ref, k_ref = refs[0], refs[1]
    pos = 2
    kpe_ref = bias_ref = None
    if has_kpe:
        kpe_ref = refs[pos]
        pos += 1
    vt_ref = refs[pos]
    pos += 1
    if has_bias:
        bias_ref = refs[pos]
        pos += 1
    o_ref, sa_ref, sb_ref = refs[pos], refs[pos + 1], refs[pos + 2]

    tq = q_ref.shape[1]
    heads = vt_ref.shape[1] // V_ROWS
    tk = vt_ref.shape[2]
    dq = q_ref.shape[0] // heads
    assert tk % tq == 0
    i = pl.program_id(q_axis)
    n_tiles = (i * tq + tq + tk - 1) // tk
    qpos = i * tq + lax.broadcasted_iota(jnp.int32, (1, tq), 1)

    def compute_scores(kt, s_ref):
        start = pl.multiple_of(kt * tk, tk)
        kpe = kpe_ref[pl.ds(start, tk), :] if has_kpe else None
        for h in range(heads):
            k = k_ref[pl.ds(start, tk), h * LANES:(h + 1) * LANES]
            if has_kpe:
                k = jnp.concatenate([k, kpe], axis=1)
            s_ref[h] = jnp.dot(k, q_ref[h * dq:(h + 1) * dq, :], preferred_element_type=F32)

    def consume_scores(kt, s_ref, carry, causal_mask):
        start = pl.multiple_of(kt * tk, tk)
        bias = bias_ref[pl.ds(start, tk), :].astype(F32) if has_bias else None
        new = []
        for h in range(heads):
            m_run, acc = carry[h]
            s = s_ref[h]
            if has_bias:
                s = s + bias
            if causal_mask:
                kpos = start + lax.broadcasted_iota(jnp.int32, (tk, 1), 0)
                s = jnp.where(kpos <= qpos, s, NEG)
            m_new = jnp.maximum(m_run, jnp.max(s, axis=0, keepdims=True))
            alpha = jnp.exp2(m_run - m_new)
            p = jnp.exp2(s - m_new).astype(BF16)
            pv = jnp.dot(vt_ref[kt, h * V_ROWS:(h + 1) * V_ROWS, :], p, preferred_element_type=F32)
            new.append((m_new, alpha * acc + pv))
        return tuple(new)

    def double_step(u, carry):
        kt = 2 * u
        compute_scores(kt + 1, sb_ref)
        carry = consume_scores(kt, sa_ref, carry, False)
        compute_scores(kt + 2, sa_ref)
        return consume_scores(kt + 1, sb_ref, carry, False)

    mask_last = not has_bias
    last = n_tiles - 1

    def odd_tail(carry):
        compute_scores(last, sb_ref)
        carry = consume_scores(last - 1, sa_ref, carry, False)
        return consume_scores(last, sb_ref, carry, mask_last)

    def even_tail(carry):
        return consume_scores(last, sa_ref, carry, mask_last)

    init = tuple((jnp.full((1, tq), NEG, F32), jnp.zeros((V_ROWS, tq), F32)) for _ in range(heads))
    compute_scores(0, sa_ref)
    carry = lax.fori_loop(0, last // 2, double_step, init)
    carry = lax.cond(last % 2 == 1, odd_tail, even_tail, carry)
    for h in range(heads):
        acc = carry[h][1]
        out = acc[0:LANES, :] / acc[LANES:LANES + 1, :]
        o_ref[:, h * LANES:(h + 1) * LANES] = out.T.astype(o_ref.dtype)


def _score_scratch(heads, tk):
    return [pltpu.VMEM((heads, tk, Q_TILE), F32), pltpu.VMEM((heads, tk, Q_TILE), F32)]


def _mla_attention(q, kn, kpe, vt):
    m = kn.shape[0]
    n_kt, _, tk = vt.shape
    hp = HEADS_PER_STEP
    return pl.pallas_call(
        functools.partial(_flash_kernel, q_axis=1, has_kpe=True, has_bias=False),
        grid=(MLA_HEADS // hp, m // Q_TILE),
        in_specs=[pl.BlockSpec((hp * 2 * LANES, Q_TILE), lambda h, i: (h, i)),
                  pl.BlockSpec((m, hp * LANES), lambda h, i: (0, h)),
                  pl.BlockSpec((m, LANES), lambda h, i: (0, 0)),
                  pl.BlockSpec((n_kt, hp * V_ROWS, tk), lambda h, i: (0, h, 0))],
        out_specs=pl.BlockSpec((Q_TILE, hp * MLA_V), lambda h, i: (i, h)),
        out_shape=jax.ShapeDtypeStruct((m, BRANCH), BF16),
        scratch_shapes=_score_scratch(hp, tk),
        compiler_params=_cp("parallel", "parallel"),
        name="mla_attention",
    )(q, kn, kpe, vt)


def _dsa_attention(q, k, vt, bias):
    m = k.shape[0]
    n_kt, _, tk = vt.shape
    hp = HEADS_PER_STEP
    return pl.pallas_call(
        functools.partial(_flash_kernel, q_axis=1, has_kpe=False, has_bias=True),
        grid=(DSA_HEADS // hp, m // Q_TILE),
        in_specs=[pl.BlockSpec((hp * DSA_DIM, Q_TILE), lambda h, i: (h, i)),
                  pl.BlockSpec((m, hp * DSA_DIM), lambda h, i: (0, h)),
                  pl.BlockSpec((n_kt, hp * V_ROWS, tk), lambda h, i: (0, h, 0)),
                  pl.BlockSpec((m, Q_TILE), lambda h, i: (0, i))],
        out_specs=pl.BlockSpec((Q_TILE, hp * DSA_DIM), lambda h, i: (i, h)),
        out_shape=jax.ShapeDtypeStruct((m, BRANCH), BF16),
        scratch_shapes=_score_scratch(hp, tk),
        compiler_params=_cp("parallel", "parallel"),
        name="dsa_attention",
    )(q, k, vt, bias)


def _rope_tables(rows, dim):
    inv = 1.0 / jnp.power(ROPE_THETA, jnp.arange(0, dim, 2, dtype=F32) / dim)
    ang = jnp.arange(rows, dtype=F32)[:, None] * inv[None, :]
    cos, sin = jnp.cos(ang), jnp.sin(ang)
    reps = LANES // dim
    return (jnp.tile(jnp.concatenate([cos, cos], axis=1), (1, reps)),
            jnp.tile(jnp.concatenate([-sin, sin], axis=1), (1, reps)))


W_IN_KR = 3 * BRANCH + MLA_Q_RANK + MLA_KV_RANK
W_IN_DQ = W_IN_KR + MLA_ROPE
W_IN_IK = W_IN_DQ + 4 * BRANCH
W_IN_IW = W_IN_IK + IDX_DIM
W_IN_PU = W_IN_IW + IDX_HEADS


def _small_w_in(w_t):
    zeros = lambda n: jnp.zeros((w_t.shape[0], n, w_t.shape[2]), w_t.dtype)
    return jnp.concatenate([w_t[:, W_IN_KR:W_IN_DQ], zeros(LANES - MLA_ROPE),
                            w_t[:, W_IN_IK:W_IN_IW], w_t[:, W_IN_IW:W_IN_PU],
                            zeros(LANES - IDX_DIM - IDX_HEADS)], axis=1)


def _layout_w_uq(w):
    w3 = w.reshape(MLA_Q_RANK, MLA_HEADS, MLA_NOPE + MLA_ROPE)
    w3 = jnp.pad(w3, ((0, 0), (0, 0), (0, 2 * LANES - MLA_NOPE - MLA_ROPE)))
    return w3.reshape(MLA_Q_RANK, MLA_HEADS * 2 * LANES).astype(BF16)


def _layout_w_ukv(w):
    w4 = w.reshape(MLA_KV_RANK, MLA_HEADS, 2, MLA_NOPE)
    return w4.transpose(0, 2, 1, 3).reshape(MLA_KV_RANK, 2 * BRANCH).astype(BF16)


def _forward(x, meta_tokens, norm_mix_pre, norm_mix_post, norm_ffn_pre, norm_ffn_post,
             w_in, conv_w, mla_q_norm, mla_w_uq, mla_kv_norm, mla_w_ukv, pool_w, pool_scale,
             w_branch, w_gate, b_gate, w_out, ffn_w_gate, ffn_w_up, ffn_w_down):
    assert x.shape[0] == 1 and x.shape[2] == D_MODEL
    depth = w_in.shape[0]
    seq = x.shape[1]
    length = N_META + seq
    topk = min(IDX_TOPK_MAX, length // 4)
    rows = -(-length // ROW_TILE) * ROW_TILE
    assert rows % Q_TILE == 0 and rows % K_TILE == 0 and rows % NORM_TILE == 0
    assert rows < (1 << 14)

    c64, s64 = _rope_tables(rows, 64)
    c128, s128 = _rope_tables(rows, 128)
    assert W_IN_KR == HALF_COLS
    w_in_t = jnp.swapaxes(w_in, 1, 2)
    w_in_small = _small_w_in(w_in_t)
    tn = 512
    wide = 2 * tn
    gate_tiles = D_MODEL // wide

    h, xn = _assemble_prenorm(x[0], meta_tokens.astype(F32), norm_mix_pre[0][None], rows)
    for l in range(depth):
        at_layer = lambda j, l=l: (l,)
        tile = lambda j: j
        proj_a = _wide_proj(xn, w_in_t, at_layer, tile, HALF_COLS // wide, ROW_TILE, tn, BF16,
                            "in_proj_a", weights_nk=True)
        proj_b = _wide_proj(xn, w_in_t, at_layer, tile, 4 * BRANCH // wide, ROW_TILE, tn, BF16,
                            "in_proj_b", weights_nk=True,
                            first_half_block=W_IN_DQ // tn, row_shift=W_IN_DQ % tn)
        proj_p = _wide_proj(xn, w_in_t, at_layer, tile, BRANCH // wide, ROW_TILE, tn, BF16,
                            "in_proj_pool", weights_nk=True,
                            first_half_block=W_IN_PU // tn, row_shift=W_IN_PU % tn)
        small = _stacked_proj(xn, w_in_small, at_layer, tile, 1, ROW_TILE, SMALL_COLS, F32,
                              "in_proj_small", weights_nk=True)
        gates = _wide_proj(xn, w_gate,
                           lambda j, l=l: (l, j // gate_tiles), lambda j: j % gate_tiles,
                           N_BRANCH * gate_tiles, ROW_TILE, tn, BF16, "gates",
                           bias=b_gate[:, :, None, :])

        y_conv = _conv_branch(proj_a, conv_w, l, ROW_TILE)
        y_pool = _pool_branch(proj_p, pool_w, pool_scale[:, None, :], l, ROW_TILE)

        q_m, kn_m, kpe_m, vt_m = _mla_prep(
            proj_a, small, mla_q_norm[:, None, :], mla_kv_norm[:, None, :],
            _layout_w_uq(mla_w_uq[l]), _layout_w_ukv(mla_w_ukv[l]), c64, s64, l, ROW_TILE)
        y_mla = _mla_attention(q_m, kn_m, kpe_m, vt_m)

        q_d, k_d, vt_d, iq_t, ik_ab, wt = _dsa_prep(proj_b, small, c128, s128, c64, s64, ROW_TILE)
        bias = _indexer(iq_t, ik_ab, wt, topk)
        y_dsa = _dsa_attention(q_d, k_d, vt_d, bias)

        merged = _gated_merge((y_conv, y_mla, y_dsa, y_pool), gates, w_branch, l, HALF_ROW_TILE, tn)
        mix = _wide_proj(merged, w_out, at_layer, tile, D_MODEL // wide, ROW_TILE, tn, F32, "out_proj")
        h, xn = _resid_norm(h, mix, norm_mix_post[l][None], norm_ffn_pre[l][None])

        act = _swiglu(xn, ffn_w_gate, ffn_w_up, l, ROW_TILE, 256)
        f = _wide_proj(act, ffn_w_down, at_layer, tile, D_MODEL // tn, HALF_ROW_TILE, tn // 2, F32,
                       "ffn_down")
        if l + 1 < depth:
            h, xn = _resid_norm(h, f, norm_ffn_post[l][None], norm_mix_pre[l + 1][None])
        else:
            out = _resid_out(h, f, norm_ffn_post[l][None], seq)

    return out[None]


def kernel(x, meta_tokens, norm_mix_pre, norm_mix_post, norm_ffn_pre, norm_ffn_post, w_in, conv_w, mla_q_norm, mla_w_uq, mla_kv_norm, mla_w_ukv, pool_w, pool_scale, w_branch, w_gate, b_gate, w_out, ffn_w_gate, ffn_w_up, ffn_w_down):
    return _forward(x, meta_tokens, norm_mix_pre, norm_mix_post, norm_ffn_pre, norm_ffn_post,
                    w_in, conv_w, mla_q_norm, mla_w_uq, mla_kv_norm, mla_w_ukv, pool_w, pool_scale,
                    w_branch, w_gate, b_gate, w_out, ffn_w_gate, ffn_w_up, ffn_w_down)
```

```python
import functools
import math

import jax
import jax.numpy as jnp
from jax import lax
from jax.experimental import pallas as pl
from jax.experimental.pallas import tpu as pltpu

F32 = jnp.float32
BF16 = jnp.bfloat16

D_MODEL = 4096
N_META = 16
ROPE_THETA = 10000.0
EPS = 1e-6
N_BRANCH = 4
BRANCH = 1024
CONV_K = 3
MLA_NOPE, MLA_ROPE, MLA_V, MLA_HEADS = 128, 64, 128, 8
MLA_Q_RANK, MLA_KV_RANK = 1536, 512
DSA_DIM, DSA_HEADS = 128, 8
IDX_HEADS, IDX_DIM, IDX_TOPK_MAX = 16, 64, 256
POOL_WINDOWS = (2, 4, 8, 16)
POOL_GROUP = 256
D_FF = 11008

LANES = 128
HALO = 16
ROW_TILE = 768
HALF_ROW_TILE = ROW_TILE // 2
Q_TILE = 256
K_TILE = 768
V_ROWS = 144
HEADS_PER_STEP = 4
IDX_K_TILE = 256
NEXT_ROWS = 256
COUNT_CHUNK = 256
COUNT_ACCS = 4
WRITE_CHUNK = 64
NORM_TILE = 192
NEG = -1e30
LOG2E = math.log2(math.e)
VMEM_LIMIT = 58 * 1024 * 1024

HALF_COLS = 5120
SMALL_COLS = 256


def _cp(*sem):
    return pltpu.CompilerParams(dimension_semantics=sem, vmem_limit_bytes=VMEM_LIMIT)


def _sigmoid(x):
    return 1.0 / (1.0 + jnp.exp(-x))


def _rms(x, g):
    return x * lax.rsqrt(jnp.mean(x * x, axis=-1, keepdims=True) + EPS) * g


def _matmul_w(a, wbf, weights_nk):
    if weights_nk:
        return lax.dot_general(a, wbf, (((1,), (1,)), ((), ())), preferred_element_type=F32)
    return jnp.dot(a, wbf, preferred_element_type=F32)


def _proj_kernel(a_ref, w_ref, o_ref, wbf_ref, *, weights_nk):
    @pl.when(pl.program_id(1) == 0)
    def _():
        wbf_ref[...] = w_ref[...].astype(wbf_ref.dtype)

    o_ref[...] = _matmul_w(a_ref[...], wbf_ref[...], weights_nk).astype(o_ref.dtype)


def _stacked_proj(a, w, lead_of, col_of, n_tiles, tm, tn, out_dtype, name,
                  weights_nk=False):
    m, k = a.shape
    lead = (None,) * (w.ndim - 2)
    if weights_nk:
        assert w.shape[-1] == k
        w_block, w_index = (tn, k), lambda j, i: (*lead_of(j), col_of(j), 0)
    else:
        assert w.shape[-2] == k
        w_block, w_index = (k, tn), lambda j, i: (*lead_of(j), 0, col_of(j))
    return pl.pallas_call(
        functools.partial(_proj_kernel, weights_nk=weights_nk),
        grid=(n_tiles, m // tm),
        in_specs=[pl.BlockSpec((tm, k), lambda j, i: (i, 0)),
                  pl.BlockSpec(lead + w_block, w_index)],
        out_specs=pl.BlockSpec((tm, tn), lambda j, i: (i, j)),
        out_shape=jax.ShapeDtypeStruct((m, n_tiles * tn), out_dtype),
        scratch_shapes=[pltpu.VMEM(w_block, BF16)],
        compiler_params=_cp("parallel", "arbitrary"),
        name=name,
    )(a, w)


def _wide_proj_kernel(*refs, has_bias, weights_nk, row_shift):
    refs = list(refs)
    a_ref, w_ref = refs[0], refs[1]
    pos = 2
    next_ref = b_ref = None
    if row_shift:
        next_ref = refs[pos]
        pos += 1
    if has_bias:
        b_ref = refs[pos]
        pos += 1
    o_ref, wbf_ref = refs[pos], refs[pos + 1]
    half = o_ref.shape[1] // 2
    i = pl.program_id(1)

    def cast_half(lo):
        if not weights_nk:
            wbf_ref[:, lo:lo + half] = w_ref[...].astype(wbf_ref.dtype)
        elif not row_shift:
            wbf_ref[lo:lo + half, :] = w_ref[...].astype(wbf_ref.dtype)
        else:
            keep = half - row_shift
            wbf_ref[lo:lo + keep, :] = w_ref[row_shift:half, :].astype(wbf_ref.dtype)
            wbf_ref[lo + keep:lo + half, :] = next_ref[0:row_shift, :].astype(wbf_ref.dtype)

    def emit(lo, width):
        wbf = wbf_ref[lo:lo + width, :] if weights_nk else wbf_ref[:, lo:lo + width]
        z = _matmul_w(a_ref[...], wbf, weights_nk)
        if has_bias:
            z = _sigmoid(z + b_ref[:, lo:lo + width])
        o_ref[:, lo:lo + width] = z.astype(o_ref.dtype)

    @pl.when(i == 0)
    def _():
        cast_half(0)
        emit(0, half)

    @pl.when(i == 1)
    def _():
        cast_half(half)
        emit(half, half)

    @pl.when(i >= 2)
    def _():
        emit(0, 2 * half)


def _wide_proj(a, w, lead_of, col_of, n_tiles, tm, half, out_dtype, name, bias=None,
               weights_nk=False, first_half_block=0, row_shift=0):
    m, k = a.shape
    assert w.shape[-1 if weights_nk else -2] == k
    assert row_shift == 0 or (weights_nk and row_shift % 8 == 0 and row_shift <= NEXT_ROWS)
    lead = (None,) * (w.ndim - 2)
    row = lambda i: jnp.maximum(i - 1, 0)

    def weight_window(j, i):
        jw = jnp.where(i >= 2, jnp.minimum(j + 1, n_tiles - 1), j)
        return lead_of(jw), first_half_block + 2 * col_of(jw) + jnp.where(i == 1, 1, 0)

    def w_index(j, i):
        stacked, hb = weight_window(j, i)
        return (*stacked, hb, 0) if weights_nk else (*stacked, 0, hb)

    in_specs = [pl.BlockSpec((tm, k), lambda j, i: (row(i), 0)),
                pl.BlockSpec(lead + ((half, k) if weights_nk else (k, half)), w_index)]
    args = [a, w]
    if row_shift:
        def next_index(j, i):
            stacked, hb = weight_window(j, i)
            return (*stacked, (hb + 1) * (half // NEXT_ROWS), 0)

        in_specs.append(pl.BlockSpec(lead + (NEXT_ROWS, k), next_index))
        args.append(w)
    if bias is not None:
        in_specs.append(pl.BlockSpec(lead + (1, 2 * half), lambda j, i: (*lead_of(j), 0, col_of(j))))
        args.append(bias)
    return pl.pallas_call(
        functools.partial(_wide_proj_kernel, has_bias=bias is not None, weights_nk=weights_nk,
                          row_shift=row_shift),
        grid=(n_tiles, m // tm + 1),
        in_specs=in_specs,
        out_specs=pl.BlockSpec((tm, 2 * half), lambda j, i: (row(i), j)),
        out_shape=jax.ShapeDtypeStruct((m, n_tiles * 2 * half), out_dtype),
        scratch_shapes=[pltpu.VMEM((2 * half, k) if weights_nk else (k, 2 * half), BF16)],
        compiler_params=_cp("parallel", "arbitrary"),
        name=name,
    )(*args)


def _swiglu_kernel(a_ref, wg_ref, wu_ref, o_ref, wgbf_ref, wubf_ref, *, n_half_blocks):
    half = wg_ref.shape[1]
    j = pl.program_id(0)
    i = pl.program_id(1)

    def cast_half(lo):
        wgbf_ref[:, lo:lo + half] = wg_ref[...].astype(wgbf_ref.dtype)
        wubf_ref[:, lo:lo + half] = wu_ref[...].astype(wubf_ref.dtype)

    def emit(lo, width):
        a = a_ref[...]
        g = jnp.dot(a, wgbf_ref[:, lo:lo + width], preferred_element_type=F32)
        u = jnp.dot(a, wubf_ref[:, lo:lo + width], preferred_element_type=F32)
        o_ref[:, lo:lo + width] = (g * _sigmoid(g) * u).astype(o_ref.dtype)

    has_right = 2 * j + 1 < n_half_blocks

    @pl.when(i == 0)
    def _():
        cast_half(0)
        emit(0, half)

    @pl.when((i == 1) & has_right)
    def _():
        cast_half(half)
        emit(half, half)

    @pl.when((i >= 2) & has_right)
    def _():
        emit(0, 2 * half)

    @pl.when((i >= 2) & jnp.logical_not(has_right))
    def _():
        emit(0, half)


def _swiglu(a, wg, wu, layer, tm, half):
    m, k = a.shape
    n = wg.shape[2]
    n_half_blocks = n // half
    assert n_half_blocks * half == n
    n_tiles = -(-n_half_blocks // 2)
    row = lambda i: jnp.maximum(i - 1, 0)
    def w_index(j, i):
        jw = jnp.where(i >= 2, jnp.minimum(j + 1, n_tiles - 1), j)
        return (layer, 0, jnp.minimum(2 * jw + jnp.where(i == 1, 1, 0), n_half_blocks - 1))

    w_spec = pl.BlockSpec((None, k, half), w_index)
    return pl.pallas_call(
        functools.partial(_swiglu_kernel, n_half_blocks=n_half_blocks),
        grid=(n_tiles, m // tm + 1),
        in_specs=[pl.BlockSpec((tm, k), lambda j, i: (row(i), 0)), w_spec, w_spec],
        out_specs=pl.BlockSpec((tm, 2 * half), lambda j, i: (row(i), j)),
        out_shape=jax.ShapeDtypeStruct((m, n), BF16),
        scratch_shapes=[pltpu.VMEM((k, 2 * half), BF16), pltpu.VMEM((k, 2 * half), BF16)],
        compiler_params=_cp("parallel", "arbitrary"),
        name="ffn_swiglu",
    )(a, wg, wu)


def _merge_kernel(y0_ref, y1_ref, y2_ref, y3_ref, g0_ref, g1_ref, g2_ref, g3_ref, w_ref,
                  o_ref, wbf_ref):
    half = w_ref.shape[2]
    i = pl.program_id(1)
    branches = ((y0_ref, g0_ref), (y1_ref, g1_ref), (y2_ref, g2_ref), (y3_ref, g3_ref))

    def cast_half(lo):
        wbf_ref[:, :, lo:lo + half] = w_ref[...].astype(wbf_ref.dtype)

    def emit(lo, width):
        acc = None
        for b, (y_ref, g_ref) in enumerate(branches):
            val = g_ref[:, lo:lo + width].astype(F32) * jnp.dot(
                y_ref[...], wbf_ref[b, :, lo:lo + width], preferred_element_type=F32)
            acc = val if acc is None else acc + val
        o_ref[:, lo:lo + width] = acc.astype(o_ref.dtype)

    @pl.when(i == 0)
    def _():
        cast_half(0)
        emit(0, half)

    @pl.when(i == 1)
    def _():
        cast_half(half)
        emit(half, half)

    @pl.when(i >= 2)
    def _():
        emit(0, 2 * half)


def _gated_merge(ys, gates, w_branch, layer, tm, half):
    m = ys[0].shape[0]
    nj = D_MODEL // (2 * half)
    row = lambda i: jnp.maximum(i - 1, 0)

    def w_index(j, i):
        jw = jnp.where(i >= 2, jnp.minimum(j + 1, nj - 1), j)
        return (layer, 0, 0, 2 * jw + jnp.where(i == 1, 1, 0))

    y_spec = pl.BlockSpec((tm, BRANCH), lambda j, i: (row(i), 0))
    g_specs = [pl.BlockSpec((tm, 2 * half),
                            functools.partial(lambda j, i, b: (row(i), b * nj + j), b=b))
               for b in range(N_BRANCH)]
    return pl.pallas_call(
        _merge_kernel,
        grid=(nj, m // tm + 1),
        in_specs=[y_spec] * N_BRANCH + g_specs
                 + [pl.BlockSpec((None, N_BRANCH, BRANCH, half), w_index)],
        out_specs=pl.BlockSpec((tm, 2 * half), lambda j, i: (row(i), j)),
        out_shape=jax.ShapeDtypeStruct((m, D_MODEL), BF16),
        scratch_shapes=[pltpu.VMEM((N_BRANCH, BRANCH, 2 * half), BF16)],
        compiler_params=_cp("parallel", "arbitrary"),
        name="gated_merge",
    )(*ys, gates, gates, gates, gates, w_branch)


def _assemble_kernel(x_ref, prev_ref, meta_ref, g_ref, h_ref, xn_ref, *, length):
    tr = x_ref.shape[0]
    i = pl.program_id(0)
    head = jnp.where(i == 0, meta_ref[...], prev_ref[...])
    tile = jnp.concatenate([head, x_ref[0:tr - N_META, :]], axis=0)
    pos = i * tr + lax.broadcasted_iota(jnp.int32, (tr, 1), 0)
    h = jnp.where(pos < length, tile, 0.0)
    h_ref[...] = h
    xn_ref[...] = _rms(h, g_ref[...]).astype(xn_ref.dtype)


def _assemble_prenorm(x2d, meta, g, rows):
    seq = x2d.shape[0]
    assert N_META == HALO and NORM_TILE % N_META == 0
    last_x = (seq - 1) // NORM_TILE
    last_prev = (seq - 1) // N_META
    per_tile = NORM_TILE // N_META
    row = pl.BlockSpec((NORM_TILE, D_MODEL), lambda i: (i, 0))
    return pl.pallas_call(
        functools.partial(_assemble_kernel, length=N_META + seq),
        grid=(rows // NORM_TILE,),
        in_specs=[pl.BlockSpec((NORM_TILE, D_MODEL), lambda i: (jnp.minimum(i, last_x), 0)),
                  pl.BlockSpec((N_META, D_MODEL),
                               lambda i: (jnp.clip(i * per_tile - 1, 0, last_prev), 0)),
                  pl.BlockSpec((N_META, D_MODEL), lambda i: (0, 0)),
                  pl.BlockSpec((1, D_MODEL), lambda i: (0, 0))],
        out_specs=[row, row],
        out_shape=[jax.ShapeDtypeStruct((rows, D_MODEL), F32),
                   jax.ShapeDtypeStruct((rows, D_MODEL), BF16)],
        compiler_params=_cp("parallel"),
        name="assemble_prenorm",
    )(x2d, x2d, meta, g)


def _resid_norm_kernel(h_ref, o_ref, gpost_ref, gnext_ref, hn_ref, xn_ref):
    hn = h_ref[...] + _rms(o_ref[...], gpost_ref[...])
    hn_ref[...] = hn
    xn_ref[...] = _rms(hn, gnext_ref[...]).astype(xn_ref.dtype)


def _resid_out_kernel(h_ref, o_ref, hnext_ref, onext_ref, gpost_ref, out_ref):
    g = gpost_ref[...]
    cur = h_ref[N_META:, :] + _rms(o_ref[N_META:, :], g)
    nxt = hnext_ref[...] + _rms(onext_ref[...], g)
    out_ref[...] = jnp.concatenate([cur, nxt], axis=0)


def _resid_norm(h, o, g_post, g_next):
    m = h.shape[0]
    row = pl.BlockSpec((NORM_TILE, D_MODEL), lambda i: (i, 0))
    gain = pl.BlockSpec((1, D_MODEL), lambda i: (0, 0))
    return pl.pallas_call(
        _resid_norm_kernel,
        grid=(m // NORM_TILE,),
        in_specs=[row, row, gain, gain],
        out_specs=[row, row],
        out_shape=[jax.ShapeDtypeStruct((m, D_MODEL), F32),
                   jax.ShapeDtypeStruct((m, D_MODEL), BF16)],
        compiler_params=_cp("parallel"),
        name="resid_norm",
    )(h, o, g_post, g_next)


def _resid_out(h, o, g_post, seq):
    m = h.shape[0]
    per_tile = NORM_TILE // N_META
    last_head = m // N_META - 1
    row = pl.BlockSpec((NORM_TILE, D_MODEL), lambda i: (i, 0))
    head = pl.BlockSpec((N_META, D_MODEL), lambda i: (jnp.minimum((i + 1) * per_tile, last_head), 0))
    return pl.pallas_call(
        _resid_out_kernel,
        grid=(-(-seq // NORM_TILE),),
        in_specs=[row, row, head, head, pl.BlockSpec((1, D_MODEL), lambda i: (0, 0))],
        out_specs=row,
        out_shape=jax.ShapeDtypeStruct((seq, D_MODEL), F32),
        compiler_params=_cp("parallel"),
        name="resid_out",
    )(h, o, h, o, g_post)


def _rope_half128(x, cos, sin_signed):
    return x * cos + pltpu.roll(x, 64, axis=1) * sin_signed


def _rope_half64(x, cos, sin_signed):
    lane = lax.broadcasted_iota(jnp.int32, x.shape, 1)
    partner = jnp.where((lane & 32) == 0, pltpu.roll(x, 96, axis=1), pltpu.roll(x, 32, axis=1))
    return x * cos + partner * sin_signed


def _conv_kernel(cb_ref, cc_ref, cu_ref, pc_ref, pu_ref, w_ref, y_ref, z_ref):
    tr = cb_ref.shape[0]
    i = pl.program_id(0)
    z = cc_ref[...].astype(F32) * cu_ref[...].astype(F32)
    zp = pc_ref[...].astype(F32) * pu_ref[...].astype(F32)
    z_ref[0:HALO, :] = jnp.where(i > 0, zp, 0.0)
    z_ref[HALO:HALO + tr, :] = z
    w = w_ref[...]
    conv = (w[2:3, :] * z
            + w[1:2, :] * z_ref[HALO - 1:HALO - 1 + tr, :]
            + w[0:1, :] * z_ref[HALO - 2:HALO - 2 + tr, :])
    y_ref[...] = (cb_ref[...].astype(F32) * conv).astype(y_ref.dtype)


def _conv_branch(proj, conv_w, layer, tr):
    m = proj.shape[0]
    cw = 256
    nb = BRANCH // cw

    def halo_row(i):
        return jnp.maximum(i * (tr // HALO) - 1, 0)

    return pl.pallas_call(
        _conv_kernel,
        grid=(m // tr, nb),
        in_specs=[pl.BlockSpec((tr, cw), lambda i, c: (i, c)),
                  pl.BlockSpec((tr, cw), lambda i, c: (i, nb + c)),
                  pl.BlockSpec((tr, cw), lambda i, c: (i, 2 * nb + c)),
                  pl.BlockSpec((HALO, cw), lambda i, c: (halo_row(i), nb + c)),
                  pl.BlockSpec((HALO, cw), lambda i, c: (halo_row(i), 2 * nb + c)),
                  pl.BlockSpec((None, CONV_K, cw), lambda i, c: (layer, 0, c))],
        out_specs=pl.BlockSpec((tr, cw), lambda i, c: (i, c)),
        out_shape=jax.ShapeDtypeStruct((m, BRANCH), BF16),
        scratch_shapes=[pltpu.VMEM((HALO + tr, cw), F32)],
        compiler_params=_cp("parallel", "parallel"),
        name="conv_branch",
    )(proj, proj, proj, proj, proj, conv_w)


def _pool_kernel(u_ref, pu_ref, w_ref, s_ref, y_ref, x_ref):
    tr = u_ref.shape[0]
    i = pl.program_id(0)
    x_ref[0:HALO, :] = jnp.where(i > 0, pu_ref[...].astype(F32), 0.0)
    x_ref[HALO:HALO + tr, :] = u_ref[...].astype(F32)
    t = i * tr + lax.broadcasted_iota(jnp.int32, (tr, POOL_GROUP), 0)
    for g, win in enumerate(POOL_WINDOWS):
        cols = slice(g * POOL_GROUP, (g + 1) * POOL_GROUP)
        x = x_ref[HALO:HALO + tr, cols]
        total = x
        for j in range(1, win):
            total = total + x_ref[HALO - j:HALO - j + tr, cols]
        count = jnp.minimum(t + 1, win).astype(F32)
        pooled = (total / count - x).astype(BF16)
        mixed = jnp.dot(pooled, w_ref[g].astype(BF16), preferred_element_type=F32)
        y_ref[:, cols] = (mixed * s_ref[:, cols]).astype(y_ref.dtype)


def _pool_branch(proj, pool_w, pool_scale, layer, tr):
    m = proj.shape[0]
    assert max(POOL_WINDOWS) <= HALO
    n_groups = len(POOL_WINDOWS)
    return pl.pallas_call(
        _pool_kernel,
        grid=(m // tr,),
        in_specs=[pl.BlockSpec((tr, BRANCH), lambda i: (i, 0)),
                  pl.BlockSpec((HALO, BRANCH), lambda i: (jnp.maximum(i * (tr // HALO) - 1, 0), 0)),
                  pl.BlockSpec((None, n_groups, POOL_GROUP, POOL_GROUP), lambda i: (layer, 0, 0, 0)),
                  pl.BlockSpec((None, 1, BRANCH), lambda i: (layer, 0, 0))],
        out_specs=pl.BlockSpec((tr, BRANCH), lambda i: (i, 0)),
        out_shape=jax.ShapeDtypeStruct((m, BRANCH), BF16),
        scratch_shapes=[pltpu.VMEM((HALO + tr, BRANCH), F32)],
        compiler_params=_cp("parallel"),
        name="pool_branch",
    )(proj, proj, pool_w, pool_scale)


def _store_value_t(vt_ref, v, heads):
    tr = v.shape[0]
    n_kt, _, tk = vt_ref.shape
    assert n_kt * tk == tr
    vt = v.T.astype(vt_ref.dtype)
    ones = jnp.ones((V_ROWS - LANES, tk), vt_ref.dtype)
    for c in range(n_kt):
        for h in range(heads):
            vt_ref[c, h * V_ROWS:h * V_ROWS + LANES, :] = vt[h * LANES:(h + 1) * LANES,
                                                             c * tk:(c + 1) * tk]
            vt_ref[c, h * V_ROWS + LANES:(h + 1) * V_ROWS, :] = ones


def _mla_prep_kernel(cq_ref, ckv_ref, sm_ref, gq_ref, gkv_ref, wuq_ref, wukv_ref,
                     cos_ref, sin_ref, qt_ref, kn_ref, kpe_ref, vt_ref):
    cos = cos_ref[...]
    sin = sin_ref[...]
    scale = (MLA_NOPE + MLA_ROPE) ** -0.5 * LOG2E
    cqn = _rms(cq_ref[...].astype(F32), gq_ref[...]).astype(BF16)
    q = jnp.dot(cqn, wuq_ref[...], preferred_element_type=F32)
    for h in range(MLA_HEADS):
        lo = 2 * h * LANES
        qt_ref[lo:lo + LANES, :] = (q[:, lo:lo + LANES] * scale).T.astype(qt_ref.dtype)
        pe = _rope_half64(q[:, lo + LANES:lo + 2 * LANES], cos, sin)
        qt_ref[lo + LANES:lo + 2 * LANES, :] = (pe * scale).T.astype(qt_ref.dtype)
    ckvn = _rms(ckv_ref[...].astype(F32), gkv_ref[...]).astype(BF16)
    kv = jnp.dot(ckvn, wukv_ref[...], preferred_element_type=F32)
    kn_ref[...] = kv[:, :BRANCH].astype(kn_ref.dtype)
    _store_value_t(vt_ref, kv[:, BRANCH:], MLA_HEADS)
    kpe_ref[...] = _rope_half64(sm_ref[:, 0:LANES], cos, sin).astype(kpe_ref.dtype)


def _mla_prep(proj, small, gq, gkv, wuq, wukv, cos64, sin64, layer, tr):
    m = proj.shape[0]
    qw = 2 * LANES * MLA_HEADS
    return pl.pallas_call(
        _mla_prep_kernel,
        grid=(m // tr,),
        in_specs=[pl.BlockSpec((tr, MLA_Q_RANK), lambda i: (i, 3 * BRANCH // MLA_Q_RANK)),
                  pl.BlockSpec((tr, MLA_KV_RANK), lambda i: (i, (3 * BRANCH + MLA_Q_RANK) // MLA_KV_RANK)),
                  pl.BlockSpec((tr, SMALL_COLS), lambda i: (i, 0)),
                  pl.BlockSpec((None, 1, MLA_Q_RANK), lambda i: (layer, 0, 0)),
                  pl.BlockSpec((None, 1, MLA_KV_RANK), lambda i: (layer, 0, 0)),
                  pl.BlockSpec((MLA_Q_RANK, qw), lambda i: (0, 0)),
                  pl.BlockSpec((MLA_KV_RANK, 2 * BRANCH), lambda i: (0, 0)),
                  pl.BlockSpec((tr, LANES), lambda i: (i, 0)),
                  pl.BlockSpec((tr, LANES), lambda i: (i, 0))],
        out_specs=[pl.BlockSpec((qw, tr), lambda i: (0, i)),
                   pl.BlockSpec((tr, BRANCH), lambda i: (i, 0)),
                   pl.BlockSpec((tr, LANES), lambda i: (i, 0)),
                   pl.BlockSpec((tr // K_TILE, MLA_HEADS * V_ROWS, K_TILE), lambda i: (i, 0, 0))],
        out_shape=[jax.ShapeDtypeStruct((qw, m), BF16),
                   jax.ShapeDtypeStruct((m, BRANCH), BF16),
                   jax.ShapeDtypeStruct((m, LANES), BF16),
                   jax.ShapeDtypeStruct((m // K_TILE, MLA_HEADS * V_ROWS, K_TILE), BF16)],
        compiler_params=_cp("parallel"),
        name="mla_prep",
    )(proj, proj, small, gq, gkv, wuq, wukv, cos64, sin64)


def _dsa_prep_kernel(dq_ref, dk_ref, dv_ref, iq_ref, sm_ref, c128_ref, s128_ref, c64_ref, s64_ref,
                     qt_ref, k_ref, vt_ref, iqt_ref, ik_ref, wt_ref):
    c128, s128 = c128_ref[...], s128_ref[...]
    c64, s64 = c64_ref[...], s64_ref[...]
    scale = DSA_DIM ** -0.5 * LOG2E
    for h in range(BRANCH // LANES):
        sl = slice(h * LANES, (h + 1) * LANES)
        qt_ref[sl, :] = (_rope_half128(dq_ref[:, sl].astype(F32), c128, s128) * scale).T.astype(qt_ref.dtype)
        k_ref[:, sl] = _rope_half128(dk_ref[:, sl].astype(F32), c128, s128).astype(k_ref.dtype)
        iqt_ref[sl, :] = _rope_half64(iq_ref[:, sl].astype(F32), c64, s64).T.astype(iqt_ref.dtype)
    _store_value_t(vt_ref, dv_ref[...].astype(F32), DSA_HEADS)
    tail = sm_ref[:, LANES:2 * LANES]
    lane = lax.broadcasted_iota(jnp.int32, tail.shape, 1)
    ik_lo = jnp.where(lane < IDX_DIM, _rope_half64(tail, c64, s64), 0.0)
    ik_ref[:, 0:LANES] = ik_lo.astype(ik_ref.dtype)
    ik_ref[:, LANES:2 * LANES] = pltpu.roll(ik_lo, IDX_DIM, axis=1).astype(ik_ref.dtype)
    idx_w_scale = (IDX_HEADS ** -0.5) * (IDX_DIM ** -0.5)
    wt_ref[...] = (tail * idx_w_scale).T


def _dsa_prep(proj_b, small, c128, s128, c64, s64, tr):
    m = proj_b.shape[0]
    col = lambda c: pl.BlockSpec((tr, BRANCH), lambda i: (i, c))
    tab = pl.BlockSpec((tr, LANES), lambda i: (i, 0))
    return pl.pallas_call(
        _dsa_prep_kernel,
        grid=(m // tr,),
        in_specs=[col(0), col(1), col(2), col(3),
                  pl.BlockSpec((tr, SMALL_COLS), lambda i: (i, 0)),
                  tab, tab, tab, tab],
        out_specs=[pl.BlockSpec((BRANCH, tr), lambda i: (0, i)),
                   pl.BlockSpec((tr, BRANCH), lambda i: (i, 0)),
                   pl.BlockSpec((tr // K_TILE, DSA_HEADS * V_ROWS, K_TILE), lambda i: (i, 0, 0)),
                   pl.BlockSpec((BRANCH, tr), lambda i: (0, i)),
                   pl.BlockSpec((tr, 2 * LANES), lambda i: (i, 0)),
                   pl.BlockSpec((LANES, tr), lambda i: (0, i))],
        out_shape=[jax.ShapeDtypeStruct((BRANCH, m), BF16),
                   jax.ShapeDtypeStruct((m, BRANCH), BF16),
                   jax.ShapeDtypeStruct((m // K_TILE, DSA_HEADS * V_ROWS, K_TILE), BF16),
                   jax.ShapeDtypeStruct((BRANCH, m), BF16),
                   jax.ShapeDtypeStruct((m, 2 * LANES), BF16),
                   jax.ShapeDtypeStruct((LANES, m), F32)],
        compiler_params=_cp("parallel"),
        name="dsa_prep",
    )(proj_b, proj_b, proj_b, proj_b, small, c128, s128, c64, s64)


def _float_to_ordered_int(s):
    b = lax.bitcast_convert_type(s, jnp.int32)
    return b ^ ((b >> 31) & jnp.int32(0x7FFFFFFF))


def _indexer_kernel(iq_ref, ik_ref, wt_ref, bias_ref, key_ref, cut_ref, high_ref, low_ref, *, topk):
    tq = iq_ref.shape[1]
    total_rows = ik_ref.shape[0]
    tk = IDX_K_TILE
    i = pl.program_id(0)
    n_tiles = (i * tq + tq) // tk
    n_rows = n_tiles * tk
    qpos = i * tq + lax.broadcasted_iota(jnp.int32, (1, tq), 1)
    int_min = jnp.int32(-2 ** 31)

    def score_tile(kt, carry):
        start = pl.multiple_of(kt * tk, tk)
        ik_lo = ik_ref[pl.ds(start, tk), 0:LANES]
        ik_hi = ik_ref[pl.ds(start, tk), LANES:2 * LANES]
        acc = jnp.zeros((tk, tq), F32)
        for j in range(IDX_HEADS // 2):
            qpair_t = iq_ref[j * LANES:(j + 1) * LANES, :]
            for half, ik in enumerate((ik_lo, ik_hi)):
                g = 2 * j + half
                dots = jnp.dot(ik, qpair_t, preferred_element_type=F32)
                acc = acc + jnp.maximum(dots, 0.0) * wt_ref[IDX_DIM + g:IDX_DIM + g + 1, :]
        acc = acc + 0.0
        kpos = start + lax.broadcasted_iota(jnp.int32, (tk, 1), 0)
        key = jnp.where(kpos <= qpos, _float_to_ordered_int(acc), int_min)
        key_ref[pl.ds(start, tk), :] = key
        high_ref[pl.ds(start, tk), :] = (key >> 16).astype(jnp.int16)
        return carry

    lax.fori_loop(0, n_tiles, score_tile, 0)

    n_chunks = n_rows // COUNT_CHUNK

    def count(pred, with_pos=False):
        def body(c, accs):
            start = pl.multiple_of(c * COUNT_CHUNK, COUNT_CHUNK)
            accs = list(accs)
            chunk = key_ref[pl.ds(start, COUNT_CHUNK), :]
            for r in range(COUNT_CHUNK // 8):
                blk = chunk[8 * r:8 * r + 8, :]
                if with_pos:
                    pos = start + 8 * r + lax.broadcasted_iota(jnp.int32, (8, 1), 0)
                    hit = pred(blk, pos)
                else:
                    hit = pred(blk)
                accs[r % COUNT_ACCS] = accs[r % COUNT_ACCS] + jnp.where(hit, 1, 0).astype(jnp.int32)
            return tuple(accs)
        zero = jnp.zeros((8, tq), jnp.int32)
        accs = lax.fori_loop(0, n_chunks, body, (zero,) * COUNT_ACCS)
        return jnp.sum(functools.reduce(lambda a, b: a + b, accs), axis=0, keepdims=True)

    int16_min = -(1 << 15)

    def count16(ref, cand, strict=False):
        cand16 = jnp.broadcast_to(cand, (16, tq)).astype(jnp.int16)
        one16 = jnp.ones((16, tq), jnp.int16)
        zero16 = jnp.zeros((16, tq), jnp.int16)

        def body(c, accs):
            start = pl.multiple_of(c * COUNT_CHUNK, COUNT_CHUNK)
            accs = list(accs)
            chunk = ref[pl.ds(start, COUNT_CHUNK), :]
            for r in range(COUNT_CHUNK // 16):
                blk = chunk[16 * r:16 * r + 16, :]
                hit = (blk > cand16) if strict else (blk >= cand16)
                accs[r % COUNT_ACCS] = accs[r % COUNT_ACCS] + jnp.where(hit, one16, zero16)
            return tuple(accs)
        zero = jnp.zeros((16, tq), jnp.int16)
        accs = lax.fori_loop(0, n_chunks, body, (zero,) * COUNT_ACCS)
        total = functools.reduce(lambda a, b: a + b, [a.astype(jnp.int32) for a in accs])
        return jnp.sum(total, axis=0, keepdims=True)

    def largest16(ref, want):
        def step(b, t):
            cand = t + jnp.left_shift(jnp.int32(1), 15 - b)
            return jnp.where(count16(ref, cand) >= want, cand, t)
        return lax.fori_loop(0, 16, step, jnp.full((1, tq), int16_min, jnp.int32))

    tau_high = largest16(high_ref, topk)
    n_above = count16(high_ref, tau_high, strict=True)

    def park_low(c, carry):
        start = pl.multiple_of(c * COUNT_CHUNK, COUNT_CHUNK)
        key = key_ref[pl.ds(start, COUNT_CHUNK), :]
        low = (key & 0xFFFF) + int16_min
        low_ref[pl.ds(start, COUNT_CHUNK), :] = jnp.where(
            (key >> 16) == tau_high, low, int16_min).astype(jnp.int16)
        return carry

    lax.fori_loop(0, n_chunks, park_low, 0)
    tau_low = largest16(low_ref, topk - n_above)
    tau = jnp.left_shift(tau_high, 16) + (tau_low - int16_min)

    n_gt = count(lambda blk: blk > tau)
    n_eq = count(lambda blk: blk == tau)
    need = topk - n_gt
    cut_bits = 14
    cut_ref[...] = jnp.full((1, tq), 1 << cut_bits, jnp.int32)

    @pl.when(jnp.max(n_eq - need) > 0)
    def _():
        def cut_step(b, cut):
            cand = cut + jnp.left_shift(jnp.int32(1), cut_bits - 1 - b)
            cnt = count(lambda blk, pos: (blk == tau) & (pos < cand), with_pos=True)
            return jnp.where(cnt <= need, cand, cut)
        cut_ref[...] = lax.fori_loop(0, cut_bits, cut_step, jnp.zeros((1, tq), jnp.int32))

    cut = cut_ref[...]

    def write_sel(c, carry):
        start = pl.multiple_of(c * WRITE_CHUNK, WRITE_CHUNK)
        blk = key_ref[pl.ds(start, WRITE_CHUNK), :]
        pos = start + lax.broadcasted_iota(jnp.int32, (WRITE_CHUNK, 1), 0)
        causal = jnp.where(pos <= qpos, 0.0, NEG)
        tie = jnp.where(blk == tau, jnp.where(pos < cut, causal, NEG), NEG)
        bias_ref[pl.ds(start, WRITE_CHUNK), :] = jnp.where(blk > tau, causal, tie).astype(bias_ref.dtype)
        return carry

    lax.fori_loop(0, n_rows // WRITE_CHUNK, write_sel, 0)

    def write_neg(c, carry):
        start = pl.multiple_of(c * WRITE_CHUNK, WRITE_CHUNK)
        bias_ref[pl.ds(start, WRITE_CHUNK), :] = jnp.full((WRITE_CHUNK, tq), NEG, bias_ref.dtype)
        return carry

    lax.fori_loop(n_rows // WRITE_CHUNK, total_rows // WRITE_CHUNK, write_neg, 0)


def _indexer(iq_t, ik_ab, wt, topk):
    m = ik_ab.shape[0]
    return pl.pallas_call(
        functools.partial(_indexer_kernel, topk=topk),
        grid=(m // Q_TILE,),
        in_specs=[pl.BlockSpec((BRANCH, Q_TILE), lambda i: (0, i)),
                  pl.BlockSpec((m, 2 * LANES), lambda i: (0, 0)),
                  pl.BlockSpec((LANES, Q_TILE), lambda i: (0, i))],
        out_specs=pl.BlockSpec((m, Q_TILE), lambda i: (0, i)),
        out_shape=jax.ShapeDtypeStruct((m, m), BF16),
        scratch_shapes=[pltpu.VMEM((m, Q_TILE), jnp.int32),
                        pltpu.VMEM((1, Q_TILE), jnp.int32),
                        pltpu.VMEM((m, Q_TILE), jnp.int16),
                        pltpu.VMEM((m, Q_TILE), jnp.int16)],
        compiler_params=_cp("parallel"),
        name="dsa_indexer",
    )(iq_t, ik_ab, wt)


def _flash_kernel(*refs, q_axis, has_kpe, has_bias):
    refs = list(refs)
    q_ref, k_ref = refs[0], refs[1]
    pos = 2
    kpe_ref = bias_ref = None
    if has_kpe:
        kpe_ref = refs[pos]
        pos += 1
    vt_ref = refs[pos]
    pos += 1
    if has_bias:
        bias_ref = refs[pos]
        pos += 1
    o_ref, sa_ref, sb_ref = refs[pos], refs[pos + 1], refs[pos + 2]

    tq = q_ref.shape[1]
    heads = vt_ref.shape[1] // V_ROWS
    tk = vt_ref.shape[2]
    dq = q_ref.shape[0] // heads
    assert tk % tq == 0
    i = pl.program_id(q_axis)
    n_tiles = (i * tq + tq + tk - 1) // tk
    qpos = i * tq + lax.broadcasted_iota(jnp.int32, (1, tq), 1)

    def compute_scores(kt, s_ref):
        start = pl.multiple_of(kt * tk, tk)
        kpe = kpe_ref[pl.ds(start, tk), :] if has_kpe else None
        for h in range(heads):
            k = k_ref[pl.ds(start, tk), h * LANES:(h + 1) * LANES]
            if has_kpe:
                k = jnp.concatenate([k, kpe], axis=1)
            s_ref[h] = jnp.dot(k, q_ref[h * dq:(h + 1) * dq, :], preferred_element_type=F32)

    def consume_scores(kt, s_ref, carry, causal_mask):
        start = pl.multiple_of(kt * tk, tk)
        bias = bias_ref[pl.ds(start, tk), :].astype(F32) if has_bias else None
        new = []
        for h in range(heads):
            m_run, acc = carry[h]
            s = s_ref[h]
            if has_bias:
                s = s + bias
            if causal_mask:
                kpos = start + lax.broadcasted_iota(jnp.int32, (tk, 1), 0)
                s = jnp.where(kpos <= qpos, s, NEG)
            m_new = jnp.maximum(m_run, jnp.max(s, axis=0, keepdims=True))
            alpha = jnp.exp2(m_run - m_new)
            p = jnp.exp2(s - m_new).astype(BF16)
            pv = jnp.dot(vt_ref[kt, h * V_ROWS:(h + 1) * V_ROWS, :], p, preferred_element_type=F32)
            new.append((m_new, alpha * acc + pv))
        return tuple(new)

    def double_step(u, carry):
        kt = 2 * u
        compute_scores(kt + 1, sb_ref)
        carry = consume_scores(kt, sa_ref, carry, False)
        compute_scores(kt + 2, sa_ref)
        return consume_scores(kt + 1, sb_ref, carry, False)

    mask_last = not has_bias
    last = n_tiles - 1

    def odd_tail(carry):
        compute_scores(last, sb_ref)
        carry = consume_scores(last - 1, sa_ref, carry, False)
        return consume_scores(last, sb_ref, carry, mask_last)

    def even_tail(carry):
        return consume_scores(last, sa_ref, carry, mask_last)

    init = tuple((jnp.full((1, tq), NEG, F32), jnp.zeros((V_ROWS, tq), F32)) for _ in range(heads))
    compute_scores(0, sa_ref)
    carry = lax.fori_loop(0, last // 2, double_step, init)
    carry = lax.cond(last % 2 == 1, odd_tail, even_tail, carry)
    for h in range(heads):
        acc = carry[h][1]
        out = acc[0:LANES, :] / acc[LANES:LANES + 1, :]
        o_ref[:, h * LANES:(h + 1) * LANES] = out.T.astype(o_ref.dtype)


def _score_scratch(heads, tk):
    return [pltpu.VMEM((heads, tk, Q_TILE), F32), pltpu.VMEM((heads, tk, Q_TILE), F32)]


def _mla_attention(q, kn, kpe, vt):
    m = kn.shape[0]
    n_kt, _, tk = vt.shape
    hp = HEADS_PER_STEP
    return pl.pallas_call(
        functools.partial(_flash_kernel, q_axis=1, has_kpe=True, has_bias=False),
        grid=(MLA_HEADS // hp, m // Q_TILE),
        in_specs=[pl.BlockSpec((hp * 2 * LANES, Q_TILE), lambda h, i: (h, i)),
                  pl.BlockSpec((m, hp * LANES), lambda h, i: (0, h)),
                  pl.BlockSpec((m, LANES), lambda h, i: (0, 0)),
                  pl.BlockSpec((n_kt, hp * V_ROWS, tk), lambda h, i: (0, h, 0))],
        out_specs=pl.BlockSpec((Q_TILE, hp * MLA_V), lambda h, i: (i, h)),
        out_shape=jax.ShapeDtypeStruct((m, BRANCH), BF16),
        scratch_shapes=_score_scratch(hp, tk),
        compiler_params=_cp("parallel", "parallel"),
        name="mla_attention",
    )(q, kn, kpe, vt)


def _dsa_attention(q, k, vt, bias):
    m = k.shape[0]
    n_kt, _, tk = vt.shape
    hp = HEADS_PER_STEP
    return pl.pallas_call(
        functools.partial(_flash_kernel, q_axis=1, has_kpe=False, has_bias=True),
        grid=(DSA_HEADS // hp, m // Q_TILE),
        in_specs=[pl.BlockSpec((hp * DSA_DIM, Q_TILE), lambda h, i: (h, i)),
                  pl.BlockSpec((m, hp * DSA_DIM), lambda h, i: (0, h)),
                  pl.BlockSpec((n_kt, hp * V_ROWS, tk), lambda h, i: (0, h, 0)),
                  pl.BlockSpec((m, Q_TILE), lambda h, i: (0, i))],
        out_specs=pl.BlockSpec((Q_TILE, hp * DSA_DIM), lambda h, i: (i, h)),
        out_shape=jax.ShapeDtypeStruct((m, BRANCH), BF16),
        scratch_shapes=_score_scratch(hp, tk),
        compiler_params=_cp("parallel", "parallel"),
        name="dsa_attention",
    )(q, k, vt, bias)


def _rope_tables(rows, dim):
    inv = 1.0 / jnp.power(ROPE_THETA, jnp.arange(0, dim, 2, dtype=F32) / dim)
    ang = jnp.arange(rows, dtype=F32)[:, None] * inv[None, :]
    cos, sin = jnp.cos(ang), jnp.sin(ang)
    reps = LANES // dim
    return (jnp.tile(jnp.concatenate([cos, cos], axis=1), (1, reps)),
            jnp.tile(jnp.concatenate([-sin, sin], axis=1), (1, reps)))


W_IN_KR = 3 * BRANCH + MLA_Q_RANK + MLA_KV_RANK
W_IN_DQ = W_IN_KR + MLA_ROPE
W_IN_IK = W_IN_DQ + 4 * BRANCH
W_IN_IW = W_IN_IK + IDX_DIM
W_IN_PU = W_IN_IW + IDX_HEADS


def _small_w_in(w_t):
    zeros = lambda n: jnp.zeros((w_t.shape[0], n, w_t.shape[2]), w_t.dtype)
    return jnp.concatenate([w_t[:, W_IN_KR:W_IN_DQ], zeros(LANES - MLA_ROPE),
                            w_t[:, W_IN_IK:W_IN_IW], w_t[:, W_IN_IW:W_IN_PU],
                            zeros(LANES - IDX_DIM - IDX_HEADS)], axis=1)


def _layout_w_uq(w):
    w3 = w.reshape(MLA_Q_RANK, MLA_HEADS, MLA_NOPE + MLA_ROPE)
    w3 = jnp.pad(w3, ((0, 0), (0, 0), (0, 2 * LANES - MLA_NOPE - MLA_ROPE)))
    return w3.reshape(MLA_Q_RANK, MLA_HEADS * 2 * LANES).astype(BF16)


def _layout_w_ukv(w):
    w4 = w.reshape(MLA_KV_RANK, MLA_HEADS, 2, MLA_NOPE)
    return w4.transpose(0, 2, 1, 3).reshape(MLA_KV_RANK, 2 * BRANCH).astype(BF16)


def _forward(x, meta_tokens, norm_mix_pre, norm_mix_post, norm_ffn_pre, norm_ffn_post,
             w_in, conv_w, mla_q_norm, mla_w_uq, mla_kv_norm, mla_w_ukv, pool_w, pool_scale,
             w_branch, w_gate, b_gate, w_out, ffn_w_gate, ffn_w_up, ffn_w_down):
    assert x.shape[0] == 1 and x.shape[2] == D_MODEL
    depth = w_in.shape[0]
    seq = x.shape[1]
    length = N_META + seq
    topk = min(IDX_TOPK_MAX, length // 4)
    rows = -(-length // ROW_TILE) * ROW_TILE
    assert rows % Q_TILE == 0 and rows % K_TILE == 0 and rows % NORM_TILE == 0
    assert rows < (1 << 14)

    c64, s64 = _rope_tables(rows, 64)
    c128, s128 = _rope_tables(rows, 128)
    assert W_IN_KR == HALF_COLS
    w_in_t = jnp.swapaxes(w_in, 1, 2)
    w_in_small = _small_w_in(w_in_t)
    tn = 512
    wide = 2 * tn
    gate_tiles = D_MODEL // wide

    h, xn = _assemble_prenorm(x[0], meta_tokens.astype(F32), norm_mix_pre[0][None], rows)
    for l in range(depth):
        at_layer = lambda j, l=l: (l,)
        tile = lambda j: j
        proj_a = _wide_proj(xn, w_in_t, at_layer, tile, HALF_COLS // wide, ROW_TILE, tn, BF16,
                            "in_proj_a", weights_nk=True)
        proj_b = _wide_proj(xn, w_in_t, at_layer, tile, 4 * BRANCH // wide, ROW_TILE, tn, BF16,
                            "in_proj_b", weights_nk=True,
                            first_half_block=W_IN_DQ // tn, row_shift=W_IN_DQ % tn)
        proj_p = _wide_proj(xn, w_in_t, at_layer, tile, BRANCH // wide, ROW_TILE, tn, BF16,
                            "in_proj_pool", weights_nk=True,
                            first_half_block=W_IN_PU // tn, row_shift=W_IN_PU % tn)
        small = _stacked_proj(xn, w_in_small, at_layer, tile, 1, ROW_TILE, SMALL_COLS, F32,
                              "in_proj_small", weights_nk=True)
        gates = _wide_proj(xn, w_gate,
                           lambda j, l=l: (l, j // gate_tiles), lambda j: j % gate_tiles,
                           N_BRANCH * gate_tiles, ROW_TILE, tn, BF16, "gates",
                           bias=b_gate[:, :, None, :])

        y_conv = _conv_branch(proj_a, conv_w, l, ROW_TILE)
        y_pool = _pool_branch(proj_p, pool_w, pool_scale[:, None, :], l, ROW_TILE)

        q_m, kn_m, kpe_m, vt_m = _mla_prep(
            proj_a, small, mla_q_norm[:, None, :], mla_kv_norm[:, None, :],
            _layout_w_uq(mla_w_uq[l]), _layout_w_ukv(mla_w_ukv[l]), c64, s64, l, ROW_TILE)
        y_mla = _mla_attention(q_m, kn_m, kpe_m, vt_m)

        q_d, k_d, vt_d, iq_t, ik_ab, wt = _dsa_prep(proj_b, small, c128, s128, c64, s64, ROW_TILE)
        bias = _indexer(iq_t, ik_ab, wt, topk)
        y_dsa = _dsa_attention(q_d, k_d, vt_d, bias)

        merged = _gated_merge((y_conv, y_mla, y_dsa, y_pool), gates, w_branch, l, HALF_ROW_TILE, tn)
        mix = _wide_proj(merged, w_out, at_layer, tile, D_MODEL // wide, ROW_TILE, tn, F32, "out_proj")
        h, xn = _resid_norm(h, mix, norm_mix_post[l][None], norm_ffn_pre[l][None])

        act = _swiglu(xn, ffn_w_gate, ffn_w_up, l, ROW_TILE, 256)
        f = _wide_proj(act, ffn_w_down, at_layer, tile, D_MODEL // tn, HALF_ROW_TILE, tn // 2, F32,
                       "ffn_down")
        if l + 1 < depth:
            h, xn = _resid_norm(h, f, norm_ffn_post[l][None], norm_mix_pre[l + 1][None])
        else:
            out = _resid_out(h, f, norm_ffn_post[l][None], seq)

    return out[None]


def kernel(x, meta_tokens, norm_mix_pre, norm_mix_post, norm_ffn_pre, norm_ffn_post, w_in, conv_w, mla_q_norm, mla_w_uq, mla_kv_norm, mla_w_ukv, pool_w, pool_scale, w_branch, w_gate, b_gate, w_out, ffn_w_gate, ffn_w_up, ffn_w_down):
    return _forward(x, meta_tokens, norm_mix_pre, norm_mix_post, norm_ffn_pre, norm_ffn_post,
                    w_in, conv_w, mla_q_norm, mla_w_uq, mla_kv_norm, mla_w_ukv, pool_w, pool_scale,
                    w_branch, w_gate, b_gate, w_out, ffn_w_gate, ffn_w_up, ffn_w_down)
```

```python
import functools
import math

import jax
import jax.numpy as jnp
from jax import lax
from jax.experimental import pallas as pl
from jax.experimental.pallas import tpu as pltpu

F32 = jnp.float32
BF16 = jnp.bfloat16

D_MODEL = 4096
N_META = 16
ROPE_THETA = 10000.0
EPS = 1e-6
N_BRANCH = 4
BRANCH = 1024
CONV_K = 3
MLA_NOPE, MLA_ROPE, MLA_V, MLA_HEADS = 128, 64, 128, 8
MLA_Q_RANK, MLA_KV_RANK = 1536, 512
DSA_DIM, DSA_HEADS = 128, 8
IDX_HEADS, IDX_DIM, IDX_TOPK_MAX = 16, 64, 256
POOL_WINDOWS = (2, 4, 8, 16)
POOL_GROUP = 256
D_FF = 11008

LANES = 128
HALO = 16
ROW_TILE = 768
HALF_ROW_TILE = ROW_TILE // 2
Q_TILE = 256
K_TILE = 768
V_ROWS = 144
HEADS_PER_STEP = 4
IDX_K_TILE = 256
NEXT_ROWS = 256
COUNT_CHUNK = 256
COUNT_ACCS = 4
WRITE_CHUNK = 64
NORM_TILE = 192
NEG = -1e30
LOG2E = math.log2(math.e)
VMEM_LIMIT = 58 * 1024 * 1024

HALF_COLS = 5120
SMALL_COLS = 256


def _cp(*sem):
    return pltpu.CompilerParams(dimension_semantics=sem, vmem_limit_bytes=VMEM_LIMIT)


def _sigmoid(x):
    return 1.0 / (1.0 + jnp.exp(-x))


def _rms(x, g):
    return x * lax.rsqrt(jnp.mean(x * x, axis=-1, keepdims=True) + EPS) * g


def _matmul_w(a, wbf, weights_nk):
    if weights_nk:
        return lax.dot_general(a, wbf, (((1,), (1,)), ((), ())), preferred_element_type=F32)
    return jnp.dot(a, wbf, preferred_element_type=F32)


def _proj_kernel(a_ref, w_ref, o_ref, wbf_ref, *, weights_nk):
    @pl.when(pl.program_id(1) == 0)
    def _():
        wbf_ref[...] = w_ref[...].astype(wbf_ref.dtype)

    o_ref[...] = _matmul_w(a_ref[...], wbf_ref[...], weights_nk).astype(o_ref.dtype)


def _stacked_proj(a, w, lead_of, col_of, n_tiles, tm, tn, out_dtype, name,
                  weights_nk=False):
    m, k = a.shape
    lead = (None,) * (w.ndim - 2)
    if weights_nk:
        assert w.shape[-1] == k
        w_block, w_index = (tn, k), lambda j, i: (*lead_of(j), col_of(j), 0)
    else:
        assert w.shape[-2] == k
        w_block, w_index = (k, tn), lambda j, i: (*lead_of(j), 0, col_of(j))
    return pl.pallas_call(
        functools.partial(_proj_kernel, weights_nk=weights_nk),
        grid=(n_tiles, m // tm),
        in_specs=[pl.BlockSpec((tm, k), lambda j, i: (i, 0)),
                  pl.BlockSpec(lead + w_block, w_index)],
        out_specs=pl.BlockSpec((tm, tn), lambda j, i: (i, j)),
        out_shape=jax.ShapeDtypeStruct((m, n_tiles * tn), out_dtype),
        scratch_shapes=[pltpu.VMEM(w_block, BF16)],
        compiler_params=_cp("parallel", "arbitrary"),
        name=name,
    )(a, w)


def _wide_proj_kernel(*refs, has_bias, weights_nk, row_shift):
    refs = list(refs)
    a_ref, w_ref = refs[0], refs[1]
    pos = 2
    next_ref = b_ref = None
    if row_shift:
        next_ref = refs[pos]
        pos += 1
    if has_bias:
        b_ref = refs[pos]
        pos += 1
    o_ref, wbf_ref = refs[pos], refs[pos + 1]
    half = o_ref.shape[1] // 2
    i = pl.program_id(1)

    def cast_half(lo):
        if not weights_nk:
            wbf_ref[:, lo:lo + half] = w_ref[...].astype(wbf_ref.dtype)
        elif not row_shift:
            wbf_ref[lo:lo + half, :] = w_ref[...].astype(wbf_ref.dtype)
        else:
            keep = half - row_shift
            wbf_ref[lo:lo + keep, :] = w_ref[row_shift:half, :].astype(wbf_ref.dtype)
            wbf_ref[lo + keep:lo + half, :] = next_ref[0:row_shift, :].astype(wbf_ref.dtype)

    def emit(lo, width):
        wbf = wbf_ref[lo:lo + width, :] if weights_nk else wbf_ref[:, lo:lo + width]
        z = _matmul_w(a_ref[...], wbf, weights_nk)
        if has_bias:
            z = _sigmoid(z + b_ref[:, lo:lo + width])
        o_ref[:, lo:lo + width] = z.astype(o_ref.dtype)

    @pl.when(i == 0)
    def _():
        cast_half(0)
        emit(0, half)

    @pl.when(i == 1)
    def _():
        cast_half(half)
        emit(half, half)

    @pl.when(i >= 2)
    def _():
        emit(0, 2 * half)


def _wide_proj(a, w, lead_of, col_of, n_tiles, tm, half, out_dtype, name, bias=None,
               weights_nk=False, first_half_block=0, row_shift=0):
    m, k = a.shape
    assert w.shape[-1 if weights_nk else -2] == k
    assert row_shift == 0 or (weights_nk and row_shift % 8 == 0 and row_shift <= NEXT_ROWS)
    lead = (None,) * (w.ndim - 2)
    row = lambda i: jnp.maximum(i - 1, 0)

    def weight_window(j, i):
        jw = jnp.where(i >= 2, jnp.minimum(j + 1, n_tiles - 1), j)
        return lead_of(jw), first_half_block + 2 * col_of(jw) + jnp.where(i == 1, 1, 0)

    def w_index(j, i):
        stacked, hb = weight_window(j, i)
        return (*stacked, hb, 0) if weights_nk else (*stacked, 0, hb)

    in_specs = [pl.BlockSpec((tm, k), lambda j, i: (row(i), 0)),
                pl.BlockSpec(lead + ((half, k) if weights_nk else (k, half)), w_index)]
    args = [a, w]
    if row_shift:
        def next_index(j, i):
            stacked, hb = weight_window(j, i)
            return (*stacked, (hb + 1) * (half // NEXT_ROWS), 0)

        in_specs.append(pl.BlockSpec(lead + (NEXT_ROWS, k), next_index))
        args.append(w)
    if bias is not None:
        in_specs.append(pl.BlockSpec(lead + (1, 2 * half), lambda j, i: (*lead_of(j), 0, col_of(j))))
        args.append(bias)
    return pl.pallas_call(
        functools.partial(_wide_proj_kernel, has_bias=bias is not None, weights_nk=weights_nk,
                          row_shift=row_shift),
        grid=(n_tiles, m // tm + 1),
        in_specs=in_specs,
        out_specs=pl.BlockSpec((tm, 2 * half), lambda j, i: (row(i), j)),
        out_shape=jax.ShapeDtypeStruct((m, n_tiles * 2 * half), out_dtype),
        scratch_shapes=[pltpu.VMEM((2 * half, k) if weights_nk else (k, 2 * half), BF16)],
        compiler_params=_cp("parallel", "arbitrary"),
        name=name,
    )(*args)


def _swiglu_kernel(a_ref, wg_ref, wu_ref, o_ref, wgbf_ref, wubf_ref, *, n_half_blocks):
    half = wg_ref.shape[1]
    j = pl.program_id(0)
    i = pl.program_id(1)

    def cast_half(lo):
        wgbf_ref[:, lo:lo + half] = wg_ref[...].astype(wgbf_ref.dtype)
        wubf_ref[:, lo:lo + half] = wu_ref[...].astype(wubf_ref.dtype)

    def emit(lo, width):
        a = a_ref[...]
        g = jnp.dot(a, wgbf_ref[:, lo:lo + width], preferred_element_type=F32)
        u = jnp.dot(a, wubf_ref[:, lo:lo + width], preferred_element_type=F32)
        o_ref[:, lo:lo + width] = (g * _sigmoid(g) * u).astype(o_ref.dtype)

    has_right = 2 * j + 1 < n_half_blocks

    @pl.when(i == 0)
    def _():
        cast_half(0)
        emit(0, half)

    @pl.when((i == 1) & has_right)
    def _():
        cast_half(half)
        emit(half, half)

    @pl.when((i >= 2) & has_right)
    def _():
        emit(0, 2 * half)

    @pl.when((i >= 2) & jnp.logical_not(has_right))
    def _():
        emit(0, half)


def _swiglu(a, wg, wu, layer, tm, half):
    m, k = a.shape
    n = wg.shape[2]
    n_half_blocks = n // half
    assert n_half_blocks * half == n
    n_tiles = -(-n_half_blocks // 2)
    row = lambda i: jnp.maximum(i - 1, 0)
    def w_index(j, i):
        jw = jnp.where(i >= 2, jnp.minimum(j + 1, n_tiles - 1), j)
        return (layer, 0, jnp.minimum(2 * jw + jnp.where(i == 1, 1, 0), n_half_blocks - 1))

    w_spec = pl.BlockSpec((None, k, half), w_index)
    return pl.pallas_call(
        functools.partial(_swiglu_kernel, n_half_blocks=n_half_blocks),
        grid=(n_tiles, m // tm + 1),
        in_specs=[pl.BlockSpec((tm, k), lambda j, i: (row(i), 0)), w_spec, w_spec],
        out_specs=pl.BlockSpec((tm, 2 * half), lambda j, i: (row(i), j)),
        out_shape=jax.ShapeDtypeStruct((m, n), BF16),
        scratch_shapes=[pltpu.VMEM((k, 2 * half), BF16), pltpu.VMEM((k, 2 * half), BF16)],
        compiler_params=_cp("parallel", "arbitrary"),
        name="ffn_swiglu",
    )(a, wg, wu)


def _merge_kernel(y0_ref, y1_ref, y2_ref, y3_ref, g0_ref, g1_ref, g2_ref, g3_ref, w_ref,
                  o_ref, wbf_ref):
    half = w_ref.shape[2]
    i = pl.program_id(1)
    branches = ((y0_ref, g0_ref), (y1_ref, g1_ref), (y2_ref, g2_ref), (y3_ref, g3_ref))

    def cast_half(lo):
        wbf_ref[:, :, lo:lo + half] = w_ref[...].astype(wbf_ref.dtype)

    def emit(lo, width):
        acc = None
        for b, (y_ref, g_ref) in enumerate(branches):
            val = g_ref[:, lo:lo + width].astype(F32) * jnp.dot(
                y_ref[...], wbf_ref[b, :, lo:lo + width], preferred_element_type=F32)
            acc = val if acc is None else acc + val
        o_ref[:, lo:lo + width] = acc.astype(o_ref.dtype)

    @pl.when(i == 0)
    def _():
        cast_half(0)
        emit(0, half)

    @pl.when(i == 1)
    def _():
        cast_half(half)
        emit(half, half)

    @pl.when(i >= 2)
    def _():
        emit(0, 2 * half)


def _gated_merge(ys, gates, w_branch, layer, tm, half):
    m = ys[0].shape[0]
    nj = D_MODEL // (2 * half)
    row = lambda i: jnp.maximum(i - 1, 0)

    def w_index(j, i):
        jw = jnp.where(i >= 2, jnp.minimum(j + 1, nj - 1), j)
        return (layer, 0, 0, 2 * jw + jnp.where(i == 1, 1, 0))

    y_spec = pl.BlockSpec((tm, BRANCH), lambda j, i: (row(i), 0))
    g_specs = [pl.BlockSpec((tm, 2 * half),
                            functools.partial(lambda j, i, b: (row(i), b * nj + j), b=b))
               for b in range(N_BRANCH)]
    return pl.pallas_call(
        _merge_kernel,
        grid=(nj, m // tm + 1),
        in_specs=[y_spec] * N_BRANCH + g_specs
                 + [pl.BlockSpec((None, N_BRANCH, BRANCH, half), w_index)],
        out_specs=pl.BlockSpec((tm, 2 * half), lambda j, i: (row(i), j)),
        out_shape=jax.ShapeDtypeStruct((m, D_MODEL), BF16),
        scratch_shapes=[pltpu.VMEM((N_BRANCH, BRANCH, 2 * half), BF16)],
        compiler_params=_cp("parallel", "arbitrary"),
        name="gated_merge",
    )(*ys, gates, gates, gates, gates, w_branch)


def _assemble_kernel(x_ref, prev_ref, meta_ref, g_ref, h_ref, xn_ref, *, length):
    tr = x_ref.shape[0]
    i = pl.program_id(0)
    head = jnp.where(i == 0, meta_ref[...], prev_ref[...])
    tile = jnp.concatenate([head, x_ref[0:tr - N_META, :]], axis=0)
    pos = i * tr + lax.broadcasted_iota(jnp.int32, (tr, 1), 0)
    h = jnp.where(pos < length, tile, 0.0)
    h_ref[...] = h
    xn_ref[...] = _rms(h, g_ref[...]).astype(xn_ref.dtype)


def _assemble_prenorm(x2d, meta, g, rows):
    seq = x2d.shape[0]
    assert N_META == HALO and NORM_TILE % N_META == 0
    last_x = (seq - 1) // NORM_TILE
    last_prev = (seq - 1) // N_META
    per_tile = NORM_TILE // N_META
    row = pl.BlockSpec((NORM_TILE, D_MODEL), lambda i: (i, 0))
    return pl.pallas_call(
        functools.partial(_assemble_kernel, length=N_META + seq),
        grid=(rows // NORM_TILE,),
        in_specs=[pl.BlockSpec((NORM_TILE, D_MODEL), lambda i: (jnp.minimum(i, last_x), 0)),
                  pl.BlockSpec((N_META, D_MODEL),
                               lambda i: (jnp.clip(i * per_tile - 1, 0, last_prev), 0)),
                  pl.BlockSpec((N_META, D_MODEL), lambda i: (0, 0)),
                  pl.BlockSpec((1, D_MODEL), lambda i: (0, 0))],
        out_specs=[row, row],
        out_shape=[jax.ShapeDtypeStruct((rows, D_MODEL), F32),
                   jax.ShapeDtypeStruct((rows, D_MODEL), BF16)],
        compiler_params=_cp("parallel"),
        name="assemble_prenorm",
    )(x2d, x2d, meta, g)


def _resid_norm_kernel(h_ref, o_ref, gpost_ref, gnext_ref, hn_ref, xn_ref):
    hn = h_ref[...] + _rms(o_ref[...], gpost_ref[...])
    hn_ref[...] = hn
    xn_ref[...] = _rms(hn, gnext_ref[...]).astype(xn_ref.dtype)


def _resid_out_kernel(h_ref, o_ref, hnext_ref, onext_ref, gpost_ref, out_ref):
    g = gpost_ref[...]
    cur = h_ref[N_META:, :] + _rms(o_ref[N_META:, :], g)
    nxt = hnext_ref[...] + _rms(onext_ref[...], g)
    out_ref[...] = jnp.concatenate([cur, nxt], axis=0)


def _resid_norm(h, o, g_post, g_next):
    m = h.shape[0]
    row = pl.BlockSpec((NORM_TILE, D_MODEL), lambda i: (i, 0))
    gain = pl.BlockSpec((1, D_MODEL), lambda i: (0, 0))
    return pl.pallas_call(
        _resid_norm_kernel,
        grid=(m // NORM_TILE,),
        in_specs=[row, row, gain, gain],
        out_specs=[row, row],
        out_shape=[jax.ShapeDtypeStruct((m, D_MODEL), F32),
                   jax.ShapeDtypeStruct((m, D_MODEL), BF16)],
        compiler_params=_cp("parallel"),
        name="resid_norm",
    )(h, o, g_post, g_next)


def _resid_out(h, o, g_post, seq):
    m = h.shape[0]
    per_tile = NORM_TILE // N_META
    last_head = m // N_META - 1
    row = pl.BlockSpec((NORM_TILE, D_MODEL), lambda i: (i, 0))
    head = pl.BlockSpec((N_META, D_MODEL), lambda i: (jnp.minimum((i + 1) * per_tile, last_head), 0))
    return pl.pallas_call(
        _resid_out_kernel,
        grid=(-(-seq // NORM_TILE),),
        in_specs=[row, row, head, head, pl.BlockSpec((1, D_MODEL), lambda i: (0, 0))],
        out_specs=row,
        out_shape=jax.ShapeDtypeStruct((seq, D_MODEL), F32),
        compiler_params=_cp("parallel"),
        name="resid_out",
    )(h, o, h, o, g_post)


def _rope_half128(x, cos, sin_signed):
    return x * cos + pltpu.roll(x, 64, axis=1) * sin_signed


def _rope_half64(x, cos, sin_signed):
    lane = lax.broadcasted_iota(jnp.int32, x.shape, 1)
    partner = jnp.where((lane & 32) == 0, pltpu.roll(x, 96, axis=1), pltpu.roll(x, 32, axis=1))
    return x * cos + partner * sin_signed


def _conv_kernel(cb_ref, cc_ref, cu_ref, pc_ref, pu_ref, w_ref, y_ref, z_ref):
    tr = cb_ref.shape[0]
    i = pl.program_id(0)
    z = cc_ref[...].astype(F32) * cu_ref[...].astype(F32)
    zp = pc_ref[...].astype(F32) * pu_ref[...].astype(F32)
    z_ref[0:HALO, :] = jnp.where(i > 0, zp, 0.0)
    z_ref[HALO:HALO + tr, :] = z
    w = w_ref[...]
    conv = (w[2:3, :] * z
            + w[1:2, :] * z_ref[HALO - 1:HALO - 1 + tr, :]
            + w[0:1, :] * z_ref[HALO - 2:HALO - 2 + tr, :])
    y_ref[...] = (cb_ref[...].astype(F32) * conv).astype(y_ref.dtype)


def _conv_branch(proj, conv_w, layer, tr):
    m = proj.shape[0]
    cw = 256
    nb = BRANCH // cw

    def halo_row(i):
        return jnp.maximum(i * (tr // HALO) - 1, 0)

    return pl.pallas_call(
        _conv_kernel,
        grid=(m // tr, nb),
        in_specs=[pl.BlockSpec((tr, cw), lambda i, c: (i, c)),
                  pl.BlockSpec((tr, cw), lambda i, c: (i, nb + c)),
                  pl.BlockSpec((tr, cw), lambda i, c: (i, 2 * nb + c)),
                  pl.BlockSpec((HALO, cw), lambda i, c: (halo_row(i), nb + c)),
                  pl.BlockSpec((HALO, cw), lambda i, c: (halo_row(i), 2 * nb + c)),
                  pl.BlockSpec((None, CONV_K, cw), lambda i, c: (layer, 0, c))],
        out_specs=pl.BlockSpec((tr, cw), lambda i, c: (i, c)),
        out_shape=jax.ShapeDtypeStruct((m, BRANCH), BF16),
        scratch_shapes=[pltpu.VMEM((HALO + tr, cw), F32)],
        compiler_params=_cp("parallel", "parallel"),
        name="conv_branch",
    )(proj, proj, proj, proj, proj, conv_w)


def _pool_kernel(u_ref, pu_ref, w_ref, s_ref, y_ref, x_ref):
    tr = u_ref.shape[0]
    i = pl.program_id(0)
    x_ref[0:HALO, :] = jnp.where(i > 0, pu_ref[...].astype(F32), 0.0)
    x_ref[HALO:HALO + tr, :] = u_ref[...].astype(F32)
    t = i * tr + lax.broadcasted_iota(jnp.int32, (tr, POOL_GROUP), 0)
    for g, win in enumerate(POOL_WINDOWS):
        cols = slice(g * POOL_GROUP, (g + 1) * POOL_GROUP)
        x = x_ref[HALO:HALO + tr, cols]
        total = x
        for j in range(1, win):
            total = total + x_ref[HALO - j:HALO - j + tr, cols]
        count = jnp.minimum(t + 1, win).astype(F32)
        pooled = (total / count - x).astype(BF16)
        mixed = jnp.dot(pooled, w_ref[g].astype(BF16), preferred_element_type=F32)
        y_ref[:, cols] = (mixed * s_ref[:, cols]).astype(y_ref.dtype)


def _pool_branch(proj, pool_w, pool_scale, layer, tr):
    m = proj.shape[0]
    assert max(POOL_WINDOWS) <= HALO
    n_groups = len(POOL_WINDOWS)
    return pl.pallas_call(
        _pool_kernel,
        grid=(m // tr,),
        in_specs=[pl.BlockSpec((tr, BRANCH), lambda i: (i, 0)),
                  pl.BlockSpec((HALO, BRANCH), lambda i: (jnp.maximum(i * (tr // HALO) - 1, 0), 0)),
                  pl.BlockSpec((None, n_groups, POOL_GROUP, POOL_GROUP), lambda i: (layer, 0, 0, 0)),
                  pl.BlockSpec((None, 1, BRANCH), lambda i: (layer, 0, 0))],
        out_specs=pl.BlockSpec((tr, BRANCH), lambda i: (i, 0)),
        out_shape=jax.ShapeDtypeStruct((m, BRANCH), BF16),
        scratch_shapes=[pltpu.VMEM((HALO + tr, BRANCH), F32)],
        compiler_params=_cp("parallel"),
        name="pool_branch",
    )(proj, proj, pool_w, pool_scale)


def _store_value_t(vt_ref, v, heads):
    tr = v.shape[0]
    n_kt, _, tk = vt_ref.shape
    assert n_kt * tk == tr
    vt = v.T.astype(vt_ref.dtype)
    ones = jnp.ones((V_ROWS - LANES, tk), vt_ref.dtype)
    for c in range(n_kt):
        for h in range(heads):
            vt_ref[c, h * V_ROWS:h * V_ROWS + LANES, :] = vt[h * LANES:(h + 1) * LANES,
                                                             c * tk:(c + 1) * tk]
            vt_ref[c, h * V_ROWS + LANES:(h + 1) * V_ROWS, :] = ones


def _mla_prep_kernel(cq_ref, ckv_ref, sm_ref, gq_ref, gkv_ref, wuq_ref, wukv_ref,
                     cos_ref, sin_ref, qt_ref, kn_ref, kpe_ref, vt_ref):
    cos = cos_ref[...]
    sin = sin_ref[...]
    scale = (MLA_NOPE + MLA_ROPE) ** -0.5 * LOG2E
    cqn = _rms(cq_ref[...].astype(F32), gq_ref[...]).astype(BF16)
    q = jnp.dot(cqn, wuq_ref[...], preferred_element_type=F32)
    for h in range(MLA_HEADS):
        lo = 2 * h * LANES
        qt_ref[lo:lo + LANES, :] = (q[:, lo:lo + LANES] * scale).T.astype(qt_ref.dtype)
        pe = _rope_half64(q[:, lo + LANES:lo + 2 * LANES], cos, sin)
        qt_ref[lo + LANES:lo + 2 * LANES, :] = (pe * scale).T.astype(qt_ref.dtype)
    ckvn = _rms(ckv_ref[...].astype(F32), gkv_ref[...]).astype(BF16)
    kv = jnp.dot(ckvn, wukv_ref[...], preferred_element_type=F32)
    kn_ref[...] = kv[:, :BRANCH].astype(kn_ref.dtype)
    _store_value_t(vt_ref, kv[:, BRANCH:], MLA_HEADS)
    kpe_ref[...] = _rope_half64(sm_ref[:, 0:LANES], cos, sin).astype(kpe_ref.dtype)


def _mla_prep(proj, small, gq, gkv, wuq, wukv, cos64, sin64, layer, tr):
    m = proj.shape[0]
    qw = 2 * LANES * MLA_HEADS
    return pl.pallas_call(
        _mla_prep_kernel,
        grid=(m // tr,),
        in_specs=[pl.BlockSpec((tr, MLA_Q_RANK), lambda i: (i, 3 * BRANCH // MLA_Q_RANK)),
                  pl.BlockSpec((tr, MLA_KV_RANK), lambda i: (i, (3 * BRANCH + MLA_Q_RANK) // MLA_KV_RANK)),
                  pl.BlockSpec((tr, SMALL_COLS), lambda i: (i, 0)),
                  pl.BlockSpec((None, 1, MLA_Q_RANK), lambda i: (layer, 0, 0)),
                  pl.BlockSpec((None, 1, MLA_KV_RANK), lambda i: (layer, 0, 0)),
                  pl.BlockSpec((MLA_Q_RANK, qw), lambda i: (0, 0)),
                  pl.BlockSpec((MLA_KV_RANK, 2 * BRANCH), lambda i: (0, 0)),
                  pl.BlockSpec((tr, LANES), lambda i: (i, 0)),
                  pl.BlockSpec((tr, LANES), lambda i: (i, 0))],
        out_specs=[pl.BlockSpec((qw, tr), lambda i: (0, i)),
                   pl.BlockSpec((tr, BRANCH), lambda i: (i, 0)),
                   pl.BlockSpec((tr, LANES), lambda i: (i, 0)),
                   pl.BlockSpec((tr // K_TILE, MLA_HEADS * V_ROWS, K_TILE), lambda i: (i, 0, 0))],
        out_shape=[jax.ShapeDtypeStruct((qw, m), BF16),
                   jax.ShapeDtypeStruct((m, BRANCH), BF16),
                   jax.ShapeDtypeStruct((m, LANES), BF16),
                   jax.ShapeDtypeStruct((m // K_TILE, MLA_HEADS * V_ROWS, K_TILE), BF16)],
        compiler_params=_cp("parallel"),
        name="mla_prep",
    )(proj, proj, small, gq, gkv, wuq, wukv, cos64, sin64)


def _dsa_prep_kernel(dq_ref, dk_ref, dv_ref, iq_ref, sm_ref, c128_ref, s128_ref, c64_ref, s64_ref,
                     qt_ref, k_ref, vt_ref, iqt_ref, ik_ref, wt_ref):
    c128, s128 = c128_ref[...], s128_ref[...]
    c64, s64 = c64_ref[...], s64_ref[...]
    scale = DSA_DIM ** -0.5 * LOG2E
    for h in range(BRANCH // LANES):
        sl = slice(h * LANES, (h + 1) * LANES)
        qt_ref[sl, :] = (_rope_half128(dq_ref[:, sl].astype(F32), c128, s128) * scale).T.astype(qt_ref.dtype)
        k_ref[:, sl] = _rope_half128(dk_ref[:, sl].astype(F32), c128, s128).astype(k_ref.dtype)
        iqt_ref[sl, :] = _rope_half64(iq_ref[:, sl].astype(F32), c64, s64).T.astype(iqt_ref.dtype)
    _store_value_t(vt_ref, dv_ref[...].astype(F32), DSA_HEADS)
    tail = sm_ref[:, LANES:2 * LANES]
    lane = lax.broadcasted_iota(jnp.int32, tail.shape, 1)
    ik_lo = jnp.where(lane < IDX_DIM, _rope_half64(tail, c64, s64), 0.0)
    ik_ref[:, 0:LANES] = ik_lo.astype(ik_ref.dtype)
    ik_ref[:, LANES:2 * LANES] = pltpu.roll(ik_lo, IDX_DIM, axis=1).astype(ik_ref.dtype)
    idx_w_scale = (IDX_HEADS ** -0.5) * (IDX_DIM ** -0.5)
    wt_ref[...] = (tail * idx_w_scale).T


def _dsa_prep(proj_b, small, c128, s128, c64, s64, tr):
    m = proj_b.shape[0]
    col = lambda c: pl.BlockSpec((tr, BRANCH), lambda i: (i, c))
    tab = pl.BlockSpec((tr, LANES), lambda i: (i, 0))
    return pl.pallas_call(
        _dsa_prep_kernel,
        grid=(m // tr,),
        in_specs=[col(0), col(1), col(2), col(3),
                  pl.BlockSpec((tr, SMALL_COLS), lambda i: (i, 0)),
                  tab, tab, tab, tab],
        out_specs=[pl.BlockSpec((BRANCH, tr), lambda i: (0, i)),
                   pl.BlockSpec((tr, BRANCH), lambda i: (i, 0)),
                   pl.BlockSpec((tr // K_TILE, DSA_HEADS * V_ROWS, K_TILE), lambda i: (i, 0, 0)),
                   pl.BlockSpec((BRANCH, tr), lambda i: (0, i)),
                   pl.BlockSpec((tr, 2 * LANES), lambda i: (i, 0)),
                   pl.BlockSpec((LANES, tr), lambda i: (0, i))],
        out_shape=[jax.ShapeDtypeStruct((BRANCH, m), BF16),
                   jax.ShapeDtypeStruct((m, BRANCH), BF16),
                   jax.ShapeDtypeStruct((m // K_TILE, DSA_HEADS * V_ROWS, K_TILE), BF16),
                   jax.ShapeDtypeStruct((BRANCH, m), BF16),
                   jax.ShapeDtypeStruct((m, 2 * LANES), BF16),
                   jax.ShapeDtypeStruct((LANES, m), F32)],
        compiler_params=_cp("parallel"),
        name="dsa_prep",
    )(proj_b, proj_b, proj_b, proj_b, small, c128, s128, c64, s64)


def _float_to_ordered_int(s):
    b = lax.bitcast_convert_type(s, jnp.int32)
    return b ^ ((b >> 31) & jnp.int32(0x7FFFFFFF))


def _indexer_kernel(iq_ref, ik_ref, wt_ref, bias_ref, key_ref, cut_ref, high_ref, low_ref, *, topk):
    tq = iq_ref.shape[1]
    total_rows = ik_ref.shape[0]
    tk = IDX_K_TILE
    i = pl.program_id(0)
    n_tiles = (i * tq + tq) // tk
    n_rows = n_tiles * tk
    qpos = i * tq + lax.broadcasted_iota(jnp.int32, (1, tq), 1)
    int_min = jnp.int32(-2 ** 31)

    def score_tile(kt, carry):
        start = pl.multiple_of(kt * tk, tk)
        ik_lo = ik_ref[pl.ds(start, tk), 0:LANES]
        ik_hi = ik_ref[pl.ds(start, tk), LANES:2 * LANES]
        acc = jnp.zeros((tk, tq), F32)
        for j in range(IDX_HEADS // 2):
            qpair_t = iq_ref[j * LANES:(j + 1) * LANES, :]
            for half, ik in enumerate((ik_lo, ik_hi)):
                g = 2 * j + half
                dots = jnp.dot(ik, qpair_t, preferred_element_type=F32)
                acc = acc + jnp.maximum(dots, 0.0) * wt_ref[IDX_DIM + g:IDX_DIM + g + 1, :]
        acc = acc + 0.0
        kpos = start + lax.broadcasted_iota(jnp.int32, (tk, 1), 0)
        key = jnp.where(kpos <= qpos, _float_to_ordered_int(acc), int_min)
        key_ref[pl.ds(start, tk), :] = key
        high_ref[pl.ds(start, tk), :] = (key >> 16).astype(jnp.int16)
        return carry

    lax.fori_loop(0, n_tiles, score_tile, 0)

    n_chunks = n_rows // COUNT_CHUNK

    def count(pred, with_pos=False):
        def body(c, accs):
            start = pl.multiple_of(c * COUNT_CHUNK, COUNT_CHUNK)
            accs = list(accs)
            chunk = key_ref[pl.ds(start, COUNT_CHUNK), :]
            for r in range(COUNT_CHUNK // 8):
                blk = chunk[8 * r:8 * r + 8, :]
                if with_pos:
                    pos = start + 8 * r + lax.broadcasted_iota(jnp.int32, (8, 1), 0)
                    hit = pred(blk, pos)
                else:
                    hit = pred(blk)
                accs[r % COUNT_ACCS] = accs[r % COUNT_ACCS] + jnp.where(hit, 1, 0).astype(jnp.int32)
            return tuple(accs)
        zero = jnp.zeros((8, tq), jnp.int32)
        accs = lax.fori_loop(0, n_chunks, body, (zero,) * COUNT_ACCS)
        return jnp.sum(functools.reduce(lambda a, b: a + b, accs), axis=0, keepdims=True)

    int16_min = -(1 << 15)

    def count16(ref, cand, strict=False):
        cand16 = jnp.broadcast_to(cand, (16, tq)).astype(jnp.int16)
        one16 = jnp.ones((16, tq), jnp.int16)
        zero16 = jnp.zeros((16, tq), jnp.int16)

        def body(c, accs):
            start = pl.multiple_of(c * COUNT_CHUNK, COUNT_CHUNK)
            accs = list(accs)
            chunk = ref[pl.ds(start, COUNT_CHUNK), :]
            for r in range(COUNT_CHUNK // 16):
                blk = chunk[16 * r:16 * r + 16, :]
                hit = (blk > cand16) if strict else (blk >= cand16)
                accs[r % COUNT_ACCS] = accs[r % COUNT_ACCS] + jnp.where(hit, one16, zero16)
            return tuple(accs)
        zero = jnp.zeros((16, tq), jnp.int16)
        accs = lax.fori_loop(0, n_chunks, body, (zero,) * COUNT_ACCS)
        total = functools.reduce(lambda a, b: a + b, [a.astype(jnp.int32) for a in accs])
        return jnp.sum(total, axis=0, keepdims=True)

    def largest16(ref, want, n_all):
        def step(b, carry):
            t, n_t = carry
            cand = t + jnp.left_shift(jnp.int32(1), 15 - b)
            n_cand = count16(ref, cand)
            ok = n_cand >= want
            return jnp.where(ok, cand, t), jnp.where(ok, n_cand, n_t)
        return lax.fori_loop(0, 16, step, (jnp.full((1, tq), int16_min, jnp.int32), n_all))

    tau_high, n_high = largest16(high_ref, topk, jnp.full((1, tq), n_rows, jnp.int32))
    n_above = count16(high_ref, tau_high, strict=True)

    def park_low(c, carry):
        start = pl.multiple_of(c * COUNT_CHUNK, COUNT_CHUNK)
        key = key_ref[pl.ds(start, COUNT_CHUNK), :]
        low = (key & 0xFFFF) + int16_min
        low_ref[pl.ds(start, COUNT_CHUNK), :] = jnp.where(
            (key >> 16) == tau_high, low, int16_min).astype(jnp.int16)
        return carry

    lax.fori_loop(0, n_chunks, park_low, 0)
    tau_low, n_low = largest16(low_ref, topk - n_above, n_high - n_above)
    tau = jnp.left_shift(tau_high, 16) + (tau_low - int16_min)

    n_ge = n_above + n_low
    n_gt = n_above + count16(low_ref, tau_low, strict=True)
    need = topk - n_gt
    cut_bits = 14
    cut_ref[...] = jnp.full((1, tq), 1 << cut_bits, jnp.int32)

    @pl.when(jnp.max(n_ge) > topk)
    def _():
        def cut_step(b, cut):
            cand = cut + jnp.left_shift(jnp.int32(1), cut_bits - 1 - b)
            cnt = count(lambda blk, pos: (blk == tau) & (pos < cand), with_pos=True)
            return jnp.where(cnt <= need, cand, cut)
        cut_ref[...] = lax.fori_loop(0, cut_bits, cut_step, jnp.zeros((1, tq), jnp.int32))

    cut = cut_ref[...]

    def write_sel(c, carry):
        start = pl.multiple_of(c * WRITE_CHUNK, WRITE_CHUNK)
        blk = key_ref[pl.ds(start, WRITE_CHUNK), :]
        pos = start + lax.broadcasted_iota(jnp.int32, (WRITE_CHUNK, 1), 0)
        causal = jnp.where(pos <= qpos, 0.0, NEG)
        tie = jnp.where(blk == tau, jnp.where(pos < cut, causal, NEG), NEG)
        bias_ref[pl.ds(start, WRITE_CHUNK), :] = jnp.where(blk > tau, causal, tie).astype(bias_ref.dtype)
        return carry

    lax.fori_loop(0, n_rows // WRITE_CHUNK, write_sel, 0)

    def write_neg(c, carry):
        start = pl.multiple_of(c * WRITE_CHUNK, WRITE_CHUNK)
        bias_ref[pl.ds(start, WRITE_CHUNK), :] = jnp.full((WRITE_CHUNK, tq), NEG, bias_ref.dtype)
        return carry

    lax.fori_loop(n_rows // WRITE_CHUNK, total_rows // WRITE_CHUNK, write_neg, 0)


def _indexer(iq_t, ik_ab, wt, topk):
    m = ik_ab.shape[0]
    return pl.pallas_call(
        functools.partial(_indexer_kernel, topk=topk),
        grid=(m // Q_TILE,),
        in_specs=[pl.BlockSpec((BRANCH, Q_TILE), lambda i: (0, i)),
                  pl.BlockSpec((m, 2 * LANES), lambda i: (0, 0)),
                  pl.BlockSpec((LANES, Q_TILE), lambda i: (0, i))],
        out_specs=pl.BlockSpec((m, Q_TILE), lambda i: (0, i)),
        out_shape=jax.ShapeDtypeStruct((m, m), BF16),
        scratch_shapes=[pltpu.VMEM((m, Q_TILE), jnp.int32),
                        pltpu.VMEM((1, Q_TILE), jnp.int32),
                        pltpu.VMEM((m, Q_TILE), jnp.int16),
                        pltpu.VMEM((m, Q_TILE), jnp.int16)],
        compiler_params=_cp("parallel"),
        name="dsa_indexer",
    )(iq_t, ik_ab, wt)


def _flash_kernel(*refs, q_axis, has_kpe, has_bias):
    refs = list(refs)
    q_ref, k_ref = refs[0], refs[1]
    pos = 2
    kpe_ref = bias_ref = None
    if has_kpe:
        kpe_ref = refs[pos]
        pos += 1
    vt_ref = refs[pos]
    pos += 1
    if has_bias:
        bias_ref = refs[pos]
        pos += 1
    o_ref, sa_ref, sb_ref = refs[pos], refs[pos + 1], refs[pos + 2]

    tq = q_ref.shape[1]
    heads = vt_ref.shape[1] // V_ROWS
    tk = vt_ref.shape[2]
    dq = q_ref.shape[0] // heads
    assert tk % tq == 0
    i = pl.program_id(q_axis)
    n_tiles = (i * tq + tq + tk - 1) // tk
    qpos = i * tq + lax.broadcasted_iota(jnp.int32, (1, tq), 1)

    def compute_scores(kt, s_ref):
        start = pl.multiple_of(kt * tk, tk)
        kpe = kpe_ref[pl.ds(start, tk), :] if has_kpe else None
        for h in range(heads):
            k = k_ref[pl.ds(start, tk), h * LANES:(h + 1) * LANES]
            if has_kpe:
                k = jnp.concatenate([k, kpe], axis=1)
            s_ref[h] = jnp.dot(k, q_ref[h * dq:(h + 1) * dq, :], preferred_element_type=F32)

    def consume_scores(kt, s_ref, carry, causal_mask):
        start = pl.multiple_of(kt * tk, tk)
        bias = bias_ref[pl.ds(start, tk), :].astype(F32) if has_bias else None
        new = []
        for h in range(heads):
            m_run, acc = carry[h]
            s = s_ref[h]
            if has_bias:
                s = s + bias
            if causal_mask:
                kpos = start + lax.broadcasted_iota(jnp.int32, (tk, 1), 0)
                s = jnp.where(kpos <= qpos, s, NEG)
            m_new = jnp.maximum(m_run, jnp.max(s, axis=0, keepdims=True))
            alpha = jnp.exp2(m_run - m_new)
            p = jnp.exp2(s - m_new).astype(BF16)
            pv = jnp.dot(vt_ref[kt, h * V_ROWS:(h + 1) * V_ROWS, :], p, preferred_element_type=F32)
            new.append((m_new, alpha * acc + pv))
        return tuple(new)

    def double_step(u, carry):
        kt = 2 * u
        compute_scores(kt + 1, sb_ref)
        carry = consume_scores(kt, sa_ref, carry, False)
        compute_scores(kt + 2, sa_ref)
        return consume_scores(kt + 1, sb_ref, carry, False)

    mask_last = not has_bias
    last = n_tiles - 1

    def odd_tail(carry):
        compute_scores(last, sb_ref)
        carry = consume_scores(last - 1, sa_ref, carry, False)
        return consume_scores(last, sb_ref, carry, mask_last)

    def even_tail(carry):
        return consume_scores(last, sa_ref, carry, mask_last)

    init = tuple((jnp.full((1, tq), NEG, F32), jnp.zeros((V_ROWS, tq), F32)) for _ in range(heads))
    compute_scores(0, sa_ref)
    carry = lax.fori_loop(0, last // 2, double_step, init)
    carry = lax.cond(last % 2 == 1, odd_tail, even_tail, carry)
    for h in range(heads):
        acc = carry[h][1]
        out = acc[0:LANES, :] / acc[LANES:LANES + 1, :]
        o_ref[:, h * LANES:(h + 1) * LANES] = out.T.astype(o_ref.dtype)


def _score_scratch(heads, tk):
    return [pltpu.VMEM((heads, tk, Q_TILE), F32), pltpu.VMEM((heads, tk, Q_TILE), F32)]


def _mla_attention(q, kn, kpe, vt):
    m = kn.shape[0]
    n_kt, _, tk = vt.shape
    hp = HEADS_PER_STEP
    return pl.pallas_call(
        functools.partial(_flash_kernel, q_axis=1, has_kpe=True, has_bias=False),
        grid=(MLA_HEADS // hp, m // Q_TILE),
        in_specs=[pl.BlockSpec((hp * 2 * LANES, Q_TILE), lambda h, i: (h, i)),
                  pl.BlockSpec((m, hp * LANES), lambda h, i: (0, h)),
                  pl.BlockSpec((m, LANES), lambda h, i: (0, 0)),
                  pl.BlockSpec((n_kt, hp * V_ROWS, tk), lambda h, i: (0, h, 0))],
        out_specs=pl.BlockSpec((Q_TILE, hp * MLA_V), lambda h, i: (i, h)),
        out_shape=jax.ShapeDtypeStruct((m, BRANCH), BF16),
        scratch_shapes=_score_scratch(hp, tk),
        compiler_params=_cp("parallel", "parallel"),
        name="mla_attention",
    )(q, kn, kpe, vt)


def _dsa_attention(q, k, vt, bias):
    m = k.shape[0]
    n_kt, _, tk = vt.shape
    hp = HEADS_PER_STEP
    return pl.pallas_call(
        functools.partial(_flash_kernel, q_axis=1, has_kpe=False, has_bias=True),
        grid=(DSA_HEADS // hp, m // Q_TILE),
        in_specs=[pl.BlockSpec((hp * DSA_DIM, Q_TILE), lambda h, i: (h, i)),
                  pl.BlockSpec((m, hp * DSA_DIM), lambda h, i: (0, h)),
                  pl.BlockSpec((n_kt, hp * V_ROWS, tk), lambda h, i: (0, h, 0)),
                  pl.BlockSpec((m, Q_TILE), lambda h, i: (0, i))],
        out_specs=pl.BlockSpec((Q_TILE, hp * DSA_DIM), lambda h, i: (i, h)),
        out_shape=jax.ShapeDtypeStruct((m, BRANCH), BF16),
        scratch_shapes=_score_scratch(hp, tk),
        compiler_params=_cp("parallel", "parallel"),
        name="dsa_attention",
    )(q, k, vt, bias)


def _rope_tables(rows, dim):
    inv = 1.0 / jnp.power(ROPE_THETA, jnp.arange(0, dim, 2, dtype=F32) / dim)
    ang = jnp.arange(rows, dtype=F32)[:, None] * inv[None, :]
    cos, sin = jnp.cos(ang), jnp.sin(ang)
    reps = LANES // dim
    return (jnp.tile(jnp.concatenate([cos, cos], axis=1), (1, reps)),
            jnp.tile(jnp.concatenate([-sin, sin], axis=1), (1, reps)))


W_IN_KR = 3 * BRANCH + MLA_Q_RANK + MLA_KV_RANK
W_IN_DQ = W_IN_KR + MLA_ROPE
W_IN_IK = W_IN_DQ + 4 * BRANCH
W_IN_IW = W_IN_IK + IDX_DIM
W_IN_PU = W_IN_IW + IDX_HEADS


def _small_w_in(w_t):
    zeros = lambda n: jnp.zeros((w_t.shape[0], n, w_t.shape[2]), w_t.dtype)
    return jnp.concatenate([w_t[:, W_IN_KR:W_IN_DQ], zeros(LANES - MLA_ROPE),
                            w_t[:, W_IN_IK:W_IN_IW], w_t[:, W_IN_IW:W_IN_PU],
                            zeros(LANES - IDX_DIM - IDX_HEADS)], axis=1)


def _layout_w_uq(w):
    w3 = w.reshape(MLA_Q_RANK, MLA_HEADS, MLA_NOPE + MLA_ROPE)
    w3 = jnp.pad(w3, ((0, 0), (0, 0), (0, 2 * LANES - MLA_NOPE - MLA_ROPE)))
    return w3.reshape(MLA_Q_RANK, MLA_HEADS * 2 * LANES).astype(BF16)


def _layout_w_ukv(w):
    w4 = w.reshape(MLA_KV_RANK, MLA_HEADS, 2, MLA_NOPE)
    return w4.transpose(0, 2, 1, 3).reshape(MLA_KV_RANK, 2 * BRANCH).astype(BF16)


def _forward(x, meta_tokens, norm_mix_pre, norm_mix_post, norm_ffn_pre, norm_ffn_post,
             w_in, conv_w, mla_q_norm, mla_w_uq, mla_kv_norm, mla_w_ukv, pool_w, pool_scale,
             w_branch, w_gate, b_gate, w_out, ffn_w_gate, ffn_w_up, ffn_w_down):
    assert x.shape[0] == 1 and x.shape[2] == D_MODEL
    depth = w_in.shape[0]
    seq = x.shape[1]
    length = N_META + seq
    topk = min(IDX_TOPK_MAX, length // 4)
    rows = -(-length // ROW_TILE) * ROW_TILE
    assert rows % Q_TILE == 0 and rows % K_TILE == 0 and rows % NORM_TILE == 0
    assert rows < (1 << 14)

    c64, s64 = _rope_tables(rows, 64)
    c128, s128 = _rope_tables(rows, 128)
    assert W_IN_KR == HALF_COLS
    w_in_t = jnp.swapaxes(w_in, 1, 2)
    w_in_small = _small_w_in(w_in_t)
    tn = 512
    wide = 2 * tn
    gate_tiles = D_MODEL // wide

    h, xn = _assemble_prenorm(x[0], meta_tokens.astype(F32), norm_mix_pre[0][None], rows)
    for l in range(depth):
        at_layer = lambda j, l=l: (l,)
        tile = lambda j: j
        proj_a = _wide_proj(xn, w_in_t, at_layer, tile, HALF_COLS // wide, ROW_TILE, tn, BF16,
                            "in_proj_a", weights_nk=True)
        proj_b = _wide_proj(xn, w_in_t, at_layer, tile, 4 * BRANCH // wide, ROW_TILE, tn, BF16,
                            "in_proj_b", weights_nk=True,
                            first_half_block=W_IN_DQ // tn, row_shift=W_IN_DQ % tn)
        proj_p = _wide_proj(xn, w_in_t, at_layer, tile, BRANCH // wide, ROW_TILE, tn, BF16,
                            "in_proj_pool", weights_nk=True,
                            first_half_block=W_IN_PU // tn, row_shift=W_IN_PU % tn)
        small = _stacked_proj(xn, w_in_small, at_layer, tile, 1, ROW_TILE, SMALL_COLS, F32,
                              "in_proj_small", weights_nk=True)
        gates = _wide_proj(xn, w_gate,
                           lambda j, l=l: (l, j // gate_tiles), lambda j: j % gate_tiles,
                           N_BRANCH * gate_tiles, ROW_TILE, tn, BF16, "gates",
                           bias=b_gate[:, :, None, :])

        y_conv = _conv_branch(proj_a, conv_w, l, ROW_TILE)
        y_pool = _pool_branch(proj_p, pool_w, pool_scale[:, None, :], l, ROW_TILE)

        q_m, kn_m, kpe_m, vt_m = _mla_prep(
            proj_a, small, mla_q_norm[:, None, :], mla_kv_norm[:, None, :],
            _layout_w_uq(mla_w_uq[l]), _layout_w_ukv(mla_w_ukv[l]), c64, s64, l, ROW_TILE)
        y_mla = _mla_attention(q_m, kn_m, kpe_m, vt_m)

        q_d, k_d, vt_d, iq_t, ik_ab, wt = _dsa_prep(proj_b, small, c128, s128, c64, s64, ROW_TILE)
        bias = _indexer(iq_t, ik_ab, wt, topk)
        y_dsa = _dsa_attention(q_d, k_d, vt_d, bias)

        merged = _gated_merge((y_conv, y_mla, y_dsa, y_pool), gates, w_branch, l, HALF_ROW_TILE, tn)
        mix = _wide_proj(merged, w_out, at_layer, tile, D_MODEL // wide, ROW_TILE, tn, F32, "out_proj")
        h, xn = _resid_norm(h, mix, norm_mix_post[l][None], norm_ffn_pre[l][None])

        act = _swiglu(xn, ffn_w_gate, ffn_w_up, l, ROW_TILE, 256)
        f = _wide_proj(act, ffn_w_down, at_layer, tile, D_MODEL // tn, HALF_ROW_TILE, tn // 2, F32,
                       "ffn_down")
        if l + 1 < depth:
            h, xn = _resid_norm(h, f, norm_ffn_post[l][None], norm_mix_pre[l + 1][None])
        else:
            out = _resid_out(h, f, norm_ffn_post[l][None], seq)

    return out[None]


def kernel(x, meta_tokens, norm_mix_pre, norm_mix_post, norm_ffn_pre, norm_ffn_post, w_in, conv_w, mla_q_norm, mla_w_uq, mla_kv_norm, mla_w_ukv, pool_w, pool_scale, w_branch, w_gate, b_gate, w_out, ffn_w_gate, ffn_w_up, ffn_w_down):
    return _forward(x, meta_tokens, norm_mix_pre, norm_mix_post, norm_ffn_pre, norm_ffn_post,
                    w_in, conv_w, mla_q_norm, mla_w_uq, mla_kv_norm, mla_w_ukv, pool_w, pool_scale,
                    w_branch, w_gate, b_gate, w_out, ffn_w_gate, ffn_w_up, ffn_w_down)
```

```python
import functools
import math

import jax
import jax.numpy as jnp
from jax import lax
from jax.experimental import pallas as pl
from jax.experimental.pallas import tpu as pltpu

F32 = jnp.float32
BF16 = jnp.bfloat16

D_MODEL = 4096
N_META = 16
ROPE_THETA = 10000.0
EPS = 1e-6
N_BRANCH = 4
BRANCH = 1024
CONV_K = 3
MLA_NOPE, MLA_ROPE, MLA_V, MLA_HEADS = 128, 64, 128, 8
MLA_Q_RANK, MLA_KV_RANK = 1536, 512
DSA_DIM, DSA_HEADS = 128, 8
IDX_HEADS, IDX_DIM, IDX_TOPK_MAX = 16, 64, 256
POOL_WINDOWS = (2, 4, 8, 16)
POOL_GROUP = 256
D_FF = 11008

LANES = 128
HALO = 16
ROW_TILE = 768
HALF_ROW_TILE = ROW_TILE // 2
Q_TILE = 256
K_TILE = 768
V_ROWS = 144
HEADS_PER_STEP = 4
IDX_K_TILE = 256
NEXT_ROWS = 256
COUNT_CHUNK = 256
COUNT_ACCS = 4
WRITE_CHUNK = 64
NORM_TILE = 192
NEG = -1e30
LOG2E = math.log2(math.e)
VMEM_LIMIT = 58 * 1024 * 1024

HALF_COLS = 5120
SMALL_COLS = 256


def _cp(*sem):
    return pltpu.CompilerParams(dimension_semantics=sem, vmem_limit_bytes=VMEM_LIMIT)


def _sigmoid(x):
    return 0.5 * jnp.tanh(0.5 * x) + 0.5


def _rms(x, g):
    return x * lax.rsqrt(jnp.mean(x * x, axis=-1, keepdims=True) + EPS) * g


def _matmul_w(a, wbf, weights_nk):
    if weights_nk:
        return lax.dot_general(a, wbf, (((1,), (1,)), ((), ())), preferred_element_type=F32)
    return jnp.dot(a, wbf, preferred_element_type=F32)


def _proj_kernel(a_ref, w_ref, o_ref, wbf_ref, *, weights_nk):
    @pl.when(pl.program_id(1) == 0)
    def _():
        wbf_ref[...] = w_ref[...].astype(wbf_ref.dtype)

    o_ref[...] = _matmul_w(a_ref[...], wbf_ref[...], weights_nk).astype(o_ref.dtype)


def _stacked_proj(a, w, lead_of, col_of, n_tiles, tm, tn, out_dtype, name,
                  weights_nk=False):
    m, k = a.shape
    lead = (None,) * (w.ndim - 2)
    if weights_nk:
        assert w.shape[-1] == k
        w_block, w_index = (tn, k), lambda j, i: (*lead_of(j), col_of(j), 0)
    else:
        assert w.shape[-2] == k
        w_block, w_index = (k, tn), lambda j, i: (*lead_of(j), 0, col_of(j))
    return pl.pallas_call(
        functools.partial(_proj_kernel, weights_nk=weights_nk),
        grid=(n_tiles, m // tm),
        in_specs=[pl.BlockSpec((tm, k), lambda j, i: (i, 0)),
                  pl.BlockSpec(lead + w_block, w_index)],
        out_specs=pl.BlockSpec((tm, tn), lambda j, i: (i, j)),
        out_shape=jax.ShapeDtypeStruct((m, n_tiles * tn), out_dtype),
        scratch_shapes=[pltpu.VMEM(w_block, BF16)],
        compiler_params=_cp("parallel", "arbitrary"),
        name=name,
    )(a, w)


def _wide_proj_kernel(*refs, has_bias, weights_nk, row_shift):
    refs = list(refs)
    a_ref, w_ref = refs[0], refs[1]
    pos = 2
    next_ref = b_ref = None
    if row_shift:
        next_ref = refs[pos]
        pos += 1
    if has_bias:
        b_ref = refs[pos]
        pos += 1
    o_ref, wbf_ref = refs[pos], refs[pos + 1]
    half = o_ref.shape[1] // 2
    i = pl.program_id(1)

    def cast_half(lo):
        if not weights_nk:
            wbf_ref[:, lo:lo + half] = w_ref[...].astype(wbf_ref.dtype)
        elif not row_shift:
            wbf_ref[lo:lo + half, :] = w_ref[...].astype(wbf_ref.dtype)
        else:
            keep = half - row_shift
            wbf_ref[lo:lo + keep, :] = w_ref[row_shift:half, :].astype(wbf_ref.dtype)
            wbf_ref[lo + keep:lo + half, :] = next_ref[0:row_shift, :].astype(wbf_ref.dtype)

    def emit(lo, width):
        wbf = wbf_ref[lo:lo + width, :] if weights_nk else wbf_ref[:, lo:lo + width]
        z = _matmul_w(a_ref[...], wbf, weights_nk)
        if has_bias:
            z = _sigmoid(z + b_ref[:, lo:lo + width])
        o_ref[:, lo:lo + width] = z.astype(o_ref.dtype)

    @pl.when(i == 0)
    def _():
        cast_half(0)
        emit(0, half)

    @pl.when(i == 1)
    def _():
        cast_half(half)
        emit(half, half)

    @pl.when(i >= 2)
    def _():
        emit(0, 2 * half)


def _wide_proj(a, w, lead_of, col_of, n_tiles, tm, half, out_dtype, name, bias=None,
               weights_nk=False, first_half_block=0, row_shift=0):
    m, k = a.shape
    assert w.shape[-1 if weights_nk else -2] == k
    assert row_shift == 0 or (weights_nk and row_shift % 8 == 0 and row_shift <= NEXT_ROWS)
    lead = (None,) * (w.ndim - 2)
    row = lambda i: jnp.maximum(i - 1, 0)

    def weight_window(j, i):
        jw = jnp.where(i >= 2, jnp.minimum(j + 1, n_tiles - 1), j)
        return lead_of(jw), first_half_block + 2 * col_of(jw) + jnp.where(i == 1, 1, 0)

    def w_index(j, i):
        stacked, hb = weight_window(j, i)
        return (*stacked, hb, 0) if weights_nk else (*stacked, 0, hb)

    in_specs = [pl.BlockSpec((tm, k), lambda j, i: (row(i), 0)),
                pl.BlockSpec(lead + ((half, k) if weights_nk else (k, half)), w_index)]
    args = [a, w]
    if row_shift:
        def next_index(j, i):
            stacked, hb = weight_window(j, i)
            return (*stacked, (hb + 1) * (half // NEXT_ROWS), 0)

        in_specs.append(pl.BlockSpec(lead + (NEXT_ROWS, k), next_index))
        args.append(w)
    if bias is not None:
        in_specs.append(pl.BlockSpec(lead + (1, 2 * half), lambda j, i: (*lead_of(j), 0, col_of(j))))
        args.append(bias)
    return pl.pallas_call(
        functools.partial(_wide_proj_kernel, has_bias=bias is not None, weights_nk=weights_nk,
                          row_shift=row_shift),
        grid=(n_tiles, m // tm + 1),
        in_specs=in_specs,
        out_specs=pl.BlockSpec((tm, 2 * half), lambda j, i: (row(i), j)),
        out_shape=jax.ShapeDtypeStruct((m, n_tiles * 2 * half), out_dtype),
        scratch_shapes=[pltpu.VMEM((2 * half, k) if weights_nk else (k, 2 * half), BF16)],
        compiler_params=_cp("parallel", "arbitrary"),
        name=name,
    )(*args)


def _swiglu_kernel(a_ref, wg_ref, wu_ref, o_ref, wgbf_ref, wubf_ref, *, n_half_blocks):
    half = wg_ref.shape[1]
    j = pl.program_id(0)
    i = pl.program_id(1)

    def cast_half(lo):
        wgbf_ref[:, lo:lo + half] = wg_ref[...].astype(wgbf_ref.dtype)
        wubf_ref[:, lo:lo + half] = wu_ref[...].astype(wubf_ref.dtype)

    def emit(lo, width):
        a = a_ref[...]
        g = jnp.dot(a, wgbf_ref[:, lo:lo + width], preferred_element_type=F32)
        u = jnp.dot(a, wubf_ref[:, lo:lo + width], preferred_element_type=F32)
        o_ref[:, lo:lo + width] = (g * _sigmoid(g) * u).astype(o_ref.dtype)

    has_right = 2 * j + 1 < n_half_blocks

    @pl.when(i == 0)
    def _():
        cast_half(0)
        emit(0, half)

    @pl.when((i == 1) & has_right)
    def _():
        cast_half(half)
        emit(half, half)

    @pl.when((i >= 2) & has_right)
    def _():
        emit(0, 2 * half)

    @pl.when((i >= 2) & jnp.logical_not(has_right))
    def _():
        emit(0, half)


def _swiglu(a, wg, wu, layer, tm, half):
    m, k = a.shape
    n = wg.shape[2]
    n_half_blocks = n // half
    assert n_half_blocks * half == n
    n_tiles = -(-n_half_blocks // 2)
    row = lambda i: jnp.maximum(i - 1, 0)
    def w_index(j, i):
        jw = jnp.where(i >= 2, jnp.minimum(j + 1, n_tiles - 1), j)
        return (layer, 0, jnp.minimum(2 * jw + jnp.where(i == 1, 1, 0), n_half_blocks - 1))

    w_spec = pl.BlockSpec((None, k, half), w_index)
    return pl.pallas_call(
        functools.partial(_swiglu_kernel, n_half_blocks=n_half_blocks),
        grid=(n_tiles, m // tm + 1),
        in_specs=[pl.BlockSpec((tm, k), lambda j, i: (row(i), 0)), w_spec, w_spec],
        out_specs=pl.BlockSpec((tm, 2 * half), lambda j, i: (row(i), j)),
        out_shape=jax.ShapeDtypeStruct((m, n), BF16),
        scratch_shapes=[pltpu.VMEM((k, 2 * half), BF16), pltpu.VMEM((k, 2 * half), BF16)],
        compiler_params=_cp("parallel", "arbitrary"),
        name="ffn_swiglu",
    )(a, wg, wu)


def _merge_kernel(y0_ref, y1_ref, y2_ref, y3_ref, g0_ref, g1_ref, g2_ref, g3_ref, w_ref,
                  o_ref, wbf_ref):
    half = w_ref.shape[2]
    i = pl.program_id(1)
    branches = ((y0_ref, g0_ref), (y1_ref, g1_ref), (y2_ref, g2_ref), (y3_ref, g3_ref))

    def cast_half(lo):
        wbf_ref[:, :, lo:lo + half] = w_ref[...].astype(wbf_ref.dtype)

    def emit(lo, width):
        acc = None
        for b, (y_ref, g_ref) in enumerate(branches):
            val = g_ref[:, lo:lo + width].astype(F32) * jnp.dot(
                y_ref[...], wbf_ref[b, :, lo:lo + width], preferred_element_type=F32)
            acc = val if acc is None else acc + val
        o_ref[:, lo:lo + width] = acc.astype(o_ref.dtype)

    @pl.when(i == 0)
    def _():
        cast_half(0)
        emit(0, half)

    @pl.when(i == 1)
    def _():
        cast_half(half)
        emit(half, half)

    @pl.when(i >= 2)
    def _():
        emit(0, 2 * half)


def _gated_merge(ys, gates, w_branch, layer, tm, half):
    m = ys[0].shape[0]
    nj = D_MODEL // (2 * half)
    row = lambda i: jnp.maximum(i - 1, 0)

    def w_index(j, i):
        jw = jnp.where(i >= 2, jnp.minimum(j + 1, nj - 1), j)
        return (layer, 0, 0, 2 * jw + jnp.where(i == 1, 1, 0))

    y_spec = pl.BlockSpec((tm, BRANCH), lambda j, i: (row(i), 0))
    g_specs = [pl.BlockSpec((tm, 2 * half),
                            functools.partial(lambda j, i, b: (row(i), b * nj + j), b=b))
               for b in range(N_BRANCH)]
    return pl.pallas_call(
        _merge_kernel,
        grid=(nj, m // tm + 1),
        in_specs=[y_spec] * N_BRANCH + g_specs
                 + [pl.BlockSpec((None, N_BRANCH, BRANCH, half), w_index)],
        out_specs=pl.BlockSpec((tm, 2 * half), lambda j, i: (row(i), j)),
        out_shape=jax.ShapeDtypeStruct((m, D_MODEL), BF16),
        scratch_shapes=[pltpu.VMEM((N_BRANCH, BRANCH, 2 * half), BF16)],
        compiler_params=_cp("parallel", "arbitrary"),
        name="gated_merge",
    )(*ys, gates, gates, gates, gates, w_branch)


def _assemble_kernel(x_ref, prev_ref, meta_ref, g_ref, h_ref, xn_ref, *, length):
    tr = x_ref.shape[0]
    i = pl.program_id(0)
    head = jnp.where(i == 0, meta_ref[...], prev_ref[...])
    tile = jnp.concatenate([head, x_ref[0:tr - N_META, :]], axis=0)
    pos = i * tr + lax.broadcasted_iota(jnp.int32, (tr, 1), 0)
    h = jnp.where(pos < length, tile, 0.0)
    h_ref[...] = h
    xn_ref[...] = _rms(h, g_ref[...]).astype(xn_ref.dtype)


def _assemble_prenorm(x2d, meta, g, rows):
    seq = x2d.shape[0]
    assert N_META == HALO and NORM_TILE % N_META == 0
    last_x = (seq - 1) // NORM_TILE
    last_prev = (seq - 1) // N_META
    per_tile = NORM_TILE // N_META
    row = pl.BlockSpec((NORM_TILE, D_MODEL), lambda i: (i, 0))
    return pl.pallas_call(
        functools.partial(_assemble_kernel, length=N_META + seq),
        grid=(rows // NORM_TILE,),
        in_specs=[pl.BlockSpec((NORM_TILE, D_MODEL), lambda i: (jnp.minimum(i, last_x), 0)),
                  pl.BlockSpec((N_META, D_MODEL),
                               lambda i: (jnp.clip(i * per_tile - 1, 0, last_prev), 0)),
                  pl.BlockSpec((N_META, D_MODEL), lambda i: (0, 0)),
                  pl.BlockSpec((1, D_MODEL), lambda i: (0, 0))],
        out_specs=[row, row],
        out_shape=[jax.ShapeDtypeStruct((rows, D_MODEL), F32),
                   jax.ShapeDtypeStruct((rows, D_MODEL), BF16)],
        compiler_params=_cp("parallel"),
        name="assemble_prenorm",
    )(x2d, x2d, meta, g)


def _resid_norm_kernel(h_ref, o_ref, gpost_ref, gnext_ref, hn_ref, xn_ref):
    hn = h_ref[...] + _rms(o_ref[...], gpost_ref[...])
    hn_ref[...] = hn
    xn_ref[...] = _rms(hn, gnext_ref[...]).astype(xn_ref.dtype)


def _resid_out_kernel(h_ref, o_ref, hnext_ref, onext_ref, gpost_ref, out_ref):
    g = gpost_ref[...]
    cur = h_ref[N_META:, :] + _rms(o_ref[N_META:, :], g)
    nxt = hnext_ref[...] + _rms(onext_ref[...], g)
    out_ref[...] = jnp.concatenate([cur, nxt], axis=0)


def _resid_norm(h, o, g_post, g_next):
    m = h.shape[0]
    row = pl.BlockSpec((NORM_TILE, D_MODEL), lambda i: (i, 0))
    gain = pl.BlockSpec((1, D_MODEL), lambda i: (0, 0))
    return pl.pallas_call(
        _resid_norm_kernel,
        grid=(m // NORM_TILE,),
        in_specs=[row, row, gain, gain],
        out_specs=[row, row],
        out_shape=[jax.ShapeDtypeStruct((m, D_MODEL), F32),
                   jax.ShapeDtypeStruct((m, D_MODEL), BF16)],
        compiler_params=_cp("parallel"),
        name="resid_norm",
    )(h, o, g_post, g_next)


def _resid_out(h, o, g_post, seq):
    m = h.shape[0]
    per_tile = NORM_TILE // N_META
    last_head = m // N_META - 1
    row = pl.BlockSpec((NORM_TILE, D_MODEL), lambda i: (i, 0))
    head = pl.BlockSpec((N_META, D_MODEL), lambda i: (jnp.minimum((i + 1) * per_tile, last_head), 0))
    return pl.pallas_call(
        _resid_out_kernel,
        grid=(-(-seq // NORM_TILE),),
        in_specs=[row, row, head, head, pl.BlockSpec((1, D_MODEL), lambda i: (0, 0))],
        out_specs=row,
        out_shape=jax.ShapeDtypeStruct((seq, D_MODEL), F32),
        compiler_params=_cp("parallel"),
        name="resid_out",
    )(h, o, h, o, g_post)


def _rope_half128(x, cos, sin_signed):
    return x * cos + pltpu.roll(x, 64, axis=1) * sin_signed


def _rope_half64(x, cos, sin_signed):
    lane = lax.broadcasted_iota(jnp.int32, x.shape, 1)
    partner = jnp.where((lane & 32) == 0, pltpu.roll(x, 96, axis=1), pltpu.roll(x, 32, axis=1))
    return x * cos + partner * sin_signed


def _conv_kernel(cb_ref, cc_ref, cu_ref, pc_ref, pu_ref, w_ref, y_ref, z_ref):
    tr = cb_ref.shape[0]
    i = pl.program_id(0)
    z = cc_ref[...].astype(F32) * cu_ref[...].astype(F32)
    zp = pc_ref[...].astype(F32) * pu_ref[...].astype(F32)
    z_ref[0:HALO, :] = jnp.where(i > 0, zp, 0.0)
    z_ref[HALO:HALO + tr, :] = z
    w = w_ref[...]
    conv = (w[2:3, :] * z
            + w[1:2, :] * z_ref[HALO - 1:HALO - 1 + tr, :]
            + w[0:1, :] * z_ref[HALO - 2:HALO - 2 + tr, :])
    y_ref[...] = (cb_ref[...].astype(F32) * conv).astype(y_ref.dtype)


def _conv_branch(proj, conv_w, layer, tr):
    m = proj.shape[0]
    cw = 256
    nb = BRANCH // cw

    def halo_row(i):
        return jnp.maximum(i * (tr // HALO) - 1, 0)

    return pl.pallas_call(
        _conv_kernel,
        grid=(m // tr, nb),
        in_specs=[pl.BlockSpec((tr, cw), lambda i, c: (i, c)),
                  pl.BlockSpec((tr, cw), lambda i, c: (i, nb + c)),
                  pl.BlockSpec((tr, cw), lambda i, c: (i, 2 * nb + c)),
                  pl.BlockSpec((HALO, cw), lambda i, c: (halo_row(i), nb + c)),
                  pl.BlockSpec((HALO, cw), lambda i, c: (halo_row(i), 2 * nb + c)),
                  pl.BlockSpec((None, CONV_K, cw), lambda i, c: (layer, 0, c))],
        out_specs=pl.BlockSpec((tr, cw), lambda i, c: (i, c)),
        out_shape=jax.ShapeDtypeStruct((m, BRANCH), BF16),
        scratch_shapes=[pltpu.VMEM((HALO + tr, cw), F32)],
        compiler_params=_cp("parallel", "parallel"),
        name="conv_branch",
    )(proj, proj, proj, proj, proj, conv_w)


def _pool_kernel(u_ref, pu_ref, w_ref, s_ref, y_ref, x_ref):
    tr = u_ref.shape[0]
    i = pl.program_id(0)
    x_ref[0:HALO, :] = jnp.where(i > 0, pu_ref[...].astype(F32), 0.0)
    x_ref[HALO:HALO + tr, :] = u_ref[...].astype(F32)
    t = i * tr + lax.broadcasted_iota(jnp.int32, (tr, POOL_GROUP), 0)
    for g, win in enumerate(POOL_WINDOWS):
        cols = slice(g * POOL_GROUP, (g + 1) * POOL_GROUP)
        x = x_ref[HALO:HALO + tr, cols]
        total = x
        for j in range(1, win):
            total = total + x_ref[HALO - j:HALO - j + tr, cols]
        count = jnp.minimum(t + 1, win).astype(F32)
        pooled = (total / count - x).astype(BF16)
        mixed = jnp.dot(pooled, w_ref[g].astype(BF16), preferred_element_type=F32)
        y_ref[:, cols] = (mixed * s_ref[:, cols]).astype(y_ref.dtype)


def _pool_branch(proj, pool_w, pool_scale, layer, tr):
    m = proj.shape[0]
    assert max(POOL_WINDOWS) <= HALO
    n_groups = len(POOL_WINDOWS)
    return pl.pallas_call(
        _pool_kernel,
        grid=(m // tr,),
        in_specs=[pl.BlockSpec((tr, BRANCH), lambda i: (i, 0)),
                  pl.BlockSpec((HALO, BRANCH), lambda i: (jnp.maximum(i * (tr // HALO) - 1, 0), 0)),
                  pl.BlockSpec((None, n_groups, POOL_GROUP, POOL_GROUP), lambda i: (layer, 0, 0, 0)),
                  pl.BlockSpec((None, 1, BRANCH), lambda i: (layer, 0, 0))],
        out_specs=pl.BlockSpec((tr, BRANCH), lambda i: (i, 0)),
        out_shape=jax.ShapeDtypeStruct((m, BRANCH), BF16),
        scratch_shapes=[pltpu.VMEM((HALO + tr, BRANCH), F32)],
        compiler_params=_cp("parallel"),
        name="pool_branch",
    )(proj, proj, pool_w, pool_scale)


def _store_value_t(vt_ref, v, heads):
    tr = v.shape[0]
    n_kt, _, tk = vt_ref.shape
    assert n_kt * tk == tr
    vt = v.T.astype(vt_ref.dtype)
    ones = jnp.ones((V_ROWS - LANES, tk), vt_ref.dtype)
    for c in range(n_kt):
        for h in range(heads):
            vt_ref[c, h * V_ROWS:h * V_ROWS + LANES, :] = vt[h * LANES:(h + 1) * LANES,
                                                             c * tk:(c + 1) * tk]
            vt_ref[c, h * V_ROWS + LANES:(h + 1) * V_ROWS, :] = ones


def _mla_prep_kernel(cq_ref, ckv_ref, sm_ref, gq_ref, gkv_ref, wuq_ref, wukv_ref,
                     cos_ref, sin_ref, qt_ref, kn_ref, kpe_ref, vt_ref):
    cos = cos_ref[...]
    sin = sin_ref[...]
    scale = (MLA_NOPE + MLA_ROPE) ** -0.5 * LOG2E
    cqn = _rms(cq_ref[...].astype(F32), gq_ref[...]).astype(BF16)
    q = jnp.dot(cqn, wuq_ref[...], preferred_element_type=F32)
    for h in range(MLA_HEADS):
        lo = 2 * h * LANES
        qt_ref[lo:lo + LANES, :] = (q[:, lo:lo + LANES] * scale).T.astype(qt_ref.dtype)
        pe = _rope_half64(q[:, lo + LANES:lo + 2 * LANES], cos, sin)
        qt_ref[lo + LANES:lo + 2 * LANES, :] = (pe * scale).T.astype(qt_ref.dtype)
    ckvn = _rms(ckv_ref[...].astype(F32), gkv_ref[...]).astype(BF16)
    kv = jnp.dot(ckvn, wukv_ref[...], preferred_element_type=F32)
    kn_ref[...] = kv[:, :BRANCH].astype(kn_ref.dtype)
    _store_value_t(vt_ref, kv[:, BRANCH:], MLA_HEADS)
    kpe_ref[...] = _rope_half64(sm_ref[:, 0:LANES], cos, sin).astype(kpe_ref.dtype)


def _mla_prep(proj, small, gq, gkv, wuq, wukv, cos64, sin64, layer, tr):
    m = proj.shape[0]
    qw = 2 * LANES * MLA_HEADS
    return pl.pallas_call(
        _mla_prep_kernel,
        grid=(m // tr,),
        in_specs=[pl.BlockSpec((tr, MLA_Q_RANK), lambda i: (i, 3 * BRANCH // MLA_Q_RANK)),
                  pl.BlockSpec((tr, MLA_KV_RANK), lambda i: (i, (3 * BRANCH + MLA_Q_RANK) // MLA_KV_RANK)),
                  pl.BlockSpec((tr, SMALL_COLS), lambda i: (i, 0)),
                  pl.BlockSpec((None, 1, MLA_Q_RANK), lambda i: (layer, 0, 0)),
                  pl.BlockSpec((None, 1, MLA_KV_RANK), lambda i: (layer, 0, 0)),
                  pl.BlockSpec((MLA_Q_RANK, qw), lambda i: (0, 0)),
                  pl.BlockSpec((MLA_KV_RANK, 2 * BRANCH), lambda i: (0, 0)),
                  pl.BlockSpec((tr, LANES), lambda i: (i, 0)),
                  pl.BlockSpec((tr, LANES), lambda i: (i, 0))],
        out_specs=[pl.BlockSpec((qw, tr), lambda i: (0, i)),
                   pl.BlockSpec((tr, BRANCH), lambda i: (i, 0)),
                   pl.BlockSpec((tr, LANES), lambda i: (i, 0)),
                   pl.BlockSpec((tr // K_TILE, MLA_HEADS * V_ROWS, K_TILE), lambda i: (i, 0, 0))],
        out_shape=[jax.ShapeDtypeStruct((qw, m), BF16),
                   jax.ShapeDtypeStruct((m, BRANCH), BF16),
                   jax.ShapeDtypeStruct((m, LANES), BF16),
                   jax.ShapeDtypeStruct((m // K_TILE, MLA_HEADS * V_ROWS, K_TILE), BF16)],
        compiler_params=_cp("parallel"),
        name="mla_prep",
    )(proj, proj, small, gq, gkv, wuq, wukv, cos64, sin64)


def _dsa_prep_kernel(dq_ref, dk_ref, dv_ref, iq_ref, sm_ref, c128_ref, s128_ref, c64_ref, s64_ref,
                     qt_ref, k_ref, vt_ref, iqt_ref, ik_ref, wt_ref):
    c128, s128 = c128_ref[...], s128_ref[...]
    c64, s64 = c64_ref[...], s64_ref[...]
    scale = DSA_DIM ** -0.5 * LOG2E
    for h in range(BRANCH // LANES):
        sl = slice(h * LANES, (h + 1) * LANES)
        qt_ref[sl, :] = (_rope_half128(dq_ref[:, sl].astype(F32), c128, s128) * scale).T.astype(qt_ref.dtype)
        k_ref[:, sl] = _rope_half128(dk_ref[:, sl].astype(F32), c128, s128).astype(k_ref.dtype)
        iqt_ref[sl, :] = _rope_half64(iq_ref[:, sl].astype(F32), c64, s64).T.astype(iqt_ref.dtype)
    _store_value_t(vt_ref, dv_ref[...].astype(F32), DSA_HEADS)
    tail = sm_ref[:, LANES:2 * LANES]
    lane = lax.broadcasted_iota(jnp.int32, tail.shape, 1)
    ik_lo = jnp.where(lane < IDX_DIM, _rope_half64(tail, c64, s64), 0.0)
    ik_ref[:, 0:LANES] = ik_lo.astype(ik_ref.dtype)
    ik_ref[:, LANES:2 * LANES] = pltpu.roll(ik_lo, IDX_DIM, axis=1).astype(ik_ref.dtype)
    idx_w_scale = (IDX_HEADS ** -0.5) * (IDX_DIM ** -0.5)
    wt_ref[...] = (tail * idx_w_scale).T


def _dsa_prep(proj_b, small, c128, s128, c64, s64, tr):
    m = proj_b.shape[0]
    col = lambda c: pl.BlockSpec((tr, BRANCH), lambda i: (i, c))
    tab = pl.BlockSpec((tr, LANES), lambda i: (i, 0))
    return pl.pallas_call(
        _dsa_prep_kernel,
        grid=(m // tr,),
        in_specs=[col(0), col(1), col(2), col(3),
                  pl.BlockSpec((tr, SMALL_COLS), lambda i: (i, 0)),
                  tab, tab, tab, tab],
        out_specs=[pl.BlockSpec((BRANCH, tr), lambda i: (0, i)),
                   pl.BlockSpec((tr, BRANCH), lambda i: (i, 0)),
                   pl.BlockSpec((tr // K_TILE, DSA_HEADS * V_ROWS, K_TILE), lambda i: (i, 0, 0)),
                   pl.BlockSpec((BRANCH, tr), lambda i: (0, i)),
                   pl.BlockSpec((tr, 2 * LANES), lambda i: (i, 0)),
                   pl.BlockSpec((LANES, tr), lambda i: (0, i))],
        out_shape=[jax.ShapeDtypeStruct((BRANCH, m), BF16),
                   jax.ShapeDtypeStruct((m, BRANCH), BF16),
                   jax.ShapeDtypeStruct((m // K_TILE, DSA_HEADS * V_ROWS, K_TILE), BF16),
                   jax.ShapeDtypeStruct((BRANCH, m), BF16),
                   jax.ShapeDtypeStruct((m, 2 * LANES), BF16),
                   jax.ShapeDtypeStruct((LANES, m), F32)],
        compiler_params=_cp("parallel"),
        name="dsa_prep",
    )(proj_b, proj_b, proj_b, proj_b, small, c128, s128, c64, s64)


def _float_to_ordered_int(s):
    b = lax.bitcast_convert_type(s, jnp.int32)
    return b ^ ((b >> 31) & jnp.int32(0x7FFFFFFF))


def _indexer_kernel(iq_ref, ik_ref, wt_ref, bias_ref, key_ref, cut_ref, high_ref, low_ref, *, topk):
    tq = iq_ref.shape[1]
    total_rows = ik_ref.shape[0]
    tk = IDX_K_TILE
    i = pl.program_id(0)
    n_tiles = (i * tq + tq) // tk
    n_rows = n_tiles * tk
    qpos = i * tq + lax.broadcasted_iota(jnp.int32, (1, tq), 1)
    int_min = jnp.int32(-2 ** 31)

    def score_tile(kt, carry):
        start = pl.multiple_of(kt * tk, tk)
        ik_lo = ik_ref[pl.ds(start, tk), 0:LANES]
        ik_hi = ik_ref[pl.ds(start, tk), LANES:2 * LANES]
        acc = jnp.zeros((tk, tq), F32)
        for j in range(IDX_HEADS // 2):
            qpair_t = iq_ref[j * LANES:(j + 1) * LANES, :]
            for half, ik in enumerate((ik_lo, ik_hi)):
                g = 2 * j + half
                dots = jnp.dot(ik, qpair_t, preferred_element_type=F32)
                acc = acc + jnp.maximum(dots, 0.0) * wt_ref[IDX_DIM + g:IDX_DIM + g + 1, :]
        acc = acc + 0.0
        kpos = start + lax.broadcasted_iota(jnp.int32, (tk, 1), 0)
        key = jnp.where(kpos <= qpos, _float_to_ordered_int(acc), int_min)
        key_ref[pl.ds(start, tk), :] = key
        high_ref[pl.ds(start, tk), :] = (key >> 16).astype(jnp.int16)
        return carry

    lax.fori_loop(0, n_tiles, score_tile, 0)

    n_chunks = n_rows // COUNT_CHUNK

    def count(pred, with_pos=False):
        def body(c, accs):
            start = pl.multiple_of(c * COUNT_CHUNK, COUNT_CHUNK)
            accs = list(accs)
            chunk = key_ref[pl.ds(start, COUNT_CHUNK), :]
            for r in range(COUNT_CHUNK // 8):
                blk = chunk[8 * r:8 * r + 8, :]
                if with_pos:
                    pos = start + 8 * r + lax.broadcasted_iota(jnp.int32, (8, 1), 0)
                    hit = pred(blk, pos)
                else:
                    hit = pred(blk)
                accs[r % COUNT_ACCS] = accs[r % COUNT_ACCS] + jnp.where(hit, 1, 0).astype(jnp.int32)
            return tuple(accs)
        zero = jnp.zeros((8, tq), jnp.int32)
        accs = lax.fori_loop(0, n_chunks, body, (zero,) * COUNT_ACCS)
        return jnp.sum(functools.reduce(lambda a, b: a + b, accs), axis=0, keepdims=True)

    int16_min = -(1 << 15)

    def count16(ref, cand, strict=False):
        cand16 = jnp.broadcast_to(cand, (16, tq)).astype(jnp.int16)
        one16 = jnp.ones((16, tq), jnp.int16)
        zero16 = jnp.zeros((16, tq), jnp.int16)

        def body(c, accs):
            start = pl.multiple_of(c * COUNT_CHUNK, COUNT_CHUNK)
            accs = list(accs)
            chunk = ref[pl.ds(start, COUNT_CHUNK), :]
            for r in range(COUNT_CHUNK // 16):
                blk = chunk[16 * r:16 * r + 16, :]
                hit = (blk > cand16) if strict else (blk >= cand16)
                accs[r % COUNT_ACCS] = accs[r % COUNT_ACCS] + jnp.where(hit, one16, zero16)
            return tuple(accs)
        zero = jnp.zeros((16, tq), jnp.int16)
        accs = lax.fori_loop(0, n_chunks, body, (zero,) * COUNT_ACCS)
        total = functools.reduce(lambda a, b: a + b, [a.astype(jnp.int32) for a in accs])
        return jnp.sum(total, axis=0, keepdims=True)

    def largest16(ref, want, n_all):
        def step(b, carry):
            t, n_t = carry
            cand = t + jnp.left_shift(jnp.int32(1), 15 - b)
            n_cand = count16(ref, cand)
            ok = n_cand >= want
            return jnp.where(ok, cand, t), jnp.where(ok, n_cand, n_t)
        return lax.fori_loop(0, 16, step, (jnp.full((1, tq), int16_min, jnp.int32), n_all))

    tau_high, n_high = largest16(high_ref, topk, jnp.full((1, tq), n_rows, jnp.int32))
    n_above = count16(high_ref, tau_high, strict=True)

    def park_low(c, carry):
        start = pl.multiple_of(c * COUNT_CHUNK, COUNT_CHUNK)
        key = key_ref[pl.ds(start, COUNT_CHUNK), :]
        low = (key & 0xFFFF) + int16_min
        low_ref[pl.ds(start, COUNT_CHUNK), :] = jnp.where(
            (key >> 16) == tau_high, low, int16_min).astype(jnp.int16)
        return carry

    lax.fori_loop(0, n_chunks, park_low, 0)
    tau_low, n_low = largest16(low_ref, topk - n_above, n_high - n_above)
    tau = jnp.left_shift(tau_high, 16) + (tau_low - int16_min)

    n_ge = n_above + n_low
    n_gt = n_above + count16(low_ref, tau_low, strict=True)
    need = topk - n_gt
    cut_bits = 14
    cut_ref[...] = jnp.full((1, tq), 1 << cut_bits, jnp.int32)

    @pl.when(jnp.max(n_ge) > topk)
    def _():
        def cut_step(b, cut):
            cand = cut + jnp.left_shift(jnp.int32(1), cut_bits - 1 - b)
            cnt = count(lambda blk, pos: (blk == tau) & (pos < cand), with_pos=True)
            return jnp.where(cnt <= need, cand, cut)
        cut_ref[...] = lax.fori_loop(0, cut_bits, cut_step, jnp.zeros((1, tq), jnp.int32))

    cut = cut_ref[...]

    def write_sel(c, carry):
        start = pl.multiple_of(c * WRITE_CHUNK, WRITE_CHUNK)
        blk = key_ref[pl.ds(start, WRITE_CHUNK), :]
        pos = start + lax.broadcasted_iota(jnp.int32, (WRITE_CHUNK, 1), 0)
        causal = jnp.where(pos <= qpos, 0.0, NEG)
        tie = jnp.where(blk == tau, jnp.where(pos < cut, causal, NEG), NEG)
        bias_ref[pl.ds(start, WRITE_CHUNK), :] = jnp.where(blk > tau, causal, tie).astype(bias_ref.dtype)
        return carry

    lax.fori_loop(0, n_rows // WRITE_CHUNK, write_sel, 0)

    def write_neg(c, carry):
        start = pl.multiple_of(c * WRITE_CHUNK, WRITE_CHUNK)
        bias_ref[pl.ds(start, WRITE_CHUNK), :] = jnp.full((WRITE_CHUNK, tq), NEG, bias_ref.dtype)
        return carry

    lax.fori_loop(n_rows // WRITE_CHUNK, total_rows // WRITE_CHUNK, write_neg, 0)


def _indexer(iq_t, ik_ab, wt, topk):
    m = ik_ab.shape[0]
    return pl.pallas_call(
        functools.partial(_indexer_kernel, topk=topk),
        grid=(m // Q_TILE,),
        in_specs=[pl.BlockSpec((BRANCH, Q_TILE), lambda i: (0, i)),
                  pl.BlockSpec((m, 2 * LANES), lambda i: (0, 0)),
                  pl.BlockSpec((LANES, Q_TILE), lambda i: (0, i))],
        out_specs=pl.BlockSpec((m, Q_TILE), lambda i: (0, i)),
        out_shape=jax.ShapeDtypeStruct((m, m), BF16),
        scratch_shapes=[pltpu.VMEM((m, Q_TILE), jnp.int32),
                        pltpu.VMEM((1, Q_TILE), jnp.int32),
                        pltpu.VMEM((m, Q_TILE), jnp.int16),
                        pltpu.VMEM((m, Q_TILE), jnp.int16)],
        compiler_params=_cp("parallel"),
        name="dsa_indexer",
    )(iq_t, ik_ab, wt)


def _flash_kernel(*refs, q_axis, has_kpe, has_bias):
    refs = list(refs)
    q_ref, k_ref = refs[0], refs[1]
    pos = 2
    kpe_ref = bias_ref = None
    if has_kpe:
        kpe_ref = refs[pos]
        pos += 1
    vt_ref = refs[pos]
    pos += 1
    if has_bias:
        bias_ref = refs[pos]
        pos += 1
    o_ref, sa_ref, sb_ref = refs[pos], refs[pos + 1], refs[pos + 2]

    tq = q_ref.shape[1]
    heads = vt_ref.shape[1] // V_ROWS
    tk = vt_ref.shape[2]
    dq = q_ref.shape[0] // heads
    assert tk % tq == 0
    i = pl.program_id(q_axis)
    n_tiles = (i * tq + tq + tk - 1) // tk
    qpos = i * tq + lax.broadcasted_iota(jnp.int32, (1, tq), 1)

    def compute_scores(kt, s_ref):
        start = pl.multiple_of(kt * tk, tk)
        kpe = kpe_ref[pl.ds(start, tk), :] if has_kpe else None
        for h in range(heads):
            k = k_ref[pl.ds(start, tk), h * LANES:(h + 1) * LANES]
            if has_kpe:
                k = jnp.concatenate([k, kpe], axis=1)
            s_ref[h] = jnp.dot(k, q_ref[h * dq:(h + 1) * dq, :], preferred_element_type=F32)

    def consume_scores(kt, s_ref, carry, causal_mask):
        start = pl.multiple_of(kt * tk, tk)
        bias = bias_ref[pl.ds(start, tk), :].astype(F32) if has_bias else None
        new = []
        for h in range(heads):
            m_run, acc = carry[h]
            s = s_ref[h]
            if has_bias:
                s = s + bias
            if causal_mask:
                kpos = start + lax.broadcasted_iota(jnp.int32, (tk, 1), 0)
                s = jnp.where(kpos <= qpos, s, NEG)
            m_new = jnp.maximum(m_run, jnp.max(s, axis=0, keepdims=True))
            alpha = jnp.exp2(m_run - m_new)
            p = jnp.exp2(s - m_new).astype(BF16)
            pv = jnp.dot(vt_ref[kt, h * V_ROWS:(h + 1) * V_ROWS, :], p, preferred_element_type=F32)
            new.append((m_new, alpha * acc + pv))
        return tuple(new)

    def double_step(u, carry):
        kt = 2 * u
        compute_scores(kt + 1, sb_ref)
        carry = consume_scores(kt, sa_ref, carry, False)
        compute_scores(kt + 2, sa_ref)
        return consume_scores(kt + 1, sb_ref, carry, False)

    mask_last = not has_bias
    last = n_tiles - 1

    def odd_tail(carry):
        compute_scores(last, sb_ref)
        carry = consume_scores(last - 1, sa_ref, carry, False)
        return consume_scores(last, sb_ref, carry, mask_last)

    def even_tail(carry):
        return consume_scores(last, sa_ref, carry, mask_last)

    init = tuple((jnp.full((1, tq), NEG, F32), jnp.zeros((V_ROWS, tq), F32)) for _ in range(heads))
    compute_scores(0, sa_ref)
    carry = lax.fori_loop(0, last // 2, double_step, init)
    carry = lax.cond(last % 2 == 1, odd_tail, even_tail, carry)
    for h in range(heads):
        acc = carry[h][1]
        out = acc[0:LANES, :] / acc[LANES:LANES + 1, :]
        o_ref[:, h * LANES:(h + 1) * LANES] = out.T.astype(o_ref.dtype)


def _score_scratch(heads, tk):
    return [pltpu.VMEM((heads, tk, Q_TILE), F32), pltpu.VMEM((heads, tk, Q_TILE), F32)]


def _mla_attention(q, kn, kpe, vt):
    m = kn.shape[0]
    n_kt, _, tk = vt.shape
    hp = HEADS_PER_STEP
    return pl.pallas_call(
        functools.partial(_flash_kernel, q_axis=1, has_kpe=True, has_bias=False),
        grid=(MLA_HEADS // hp, m // Q_TILE),
        in_specs=[pl.BlockSpec((hp * 2 * LANES, Q_TILE), lambda h, i: (h, i)),
                  pl.BlockSpec((m, hp * LANES), lambda h, i: (0, h)),
                  pl.BlockSpec((m, LANES), lambda h, i: (0, 0)),
                  pl.BlockSpec((n_kt, hp * V_ROWS, tk), lambda h, i: (0, h, 0))],
        out_specs=pl.BlockSpec((Q_TILE, hp * MLA_V), lambda h, i: (i, h)),
        out_shape=jax.ShapeDtypeStruct((m, BRANCH), BF16),
        scratch_shapes=_score_scratch(hp, tk),
        compiler_params=_cp("parallel", "parallel"),
        name="mla_attention",
    )(q, kn, kpe, vt)


def _dsa_attention(q, k, vt, bias):
    m = k.shape[0]
    n_kt, _, tk = vt.shape
    hp = HEADS_PER_STEP
    return pl.pallas_call(
        functools.partial(_flash_kernel, q_axis=1, has_kpe=False, has_bias=True),
        grid=(DSA_HEADS // hp, m // Q_TILE),
        in_specs=[pl.BlockSpec((hp * DSA_DIM, Q_TILE), lambda h, i: (h, i)),
                  pl.BlockSpec((m, hp * DSA_DIM), lambda h, i: (0, h)),
                  pl.BlockSpec((n_kt, hp * V_ROWS, tk), lambda h, i: (0, h, 0)),
                  pl.BlockSpec((m, Q_TILE), lambda h, i: (0, i))],
        out_specs=pl.BlockSpec((Q_TILE, hp * DSA_DIM), lambda h, i: (i, h)),
        out_shape=jax.ShapeDtypeStruct((m, BRANCH), BF16),
        scratch_shapes=_score_scratch(hp, tk),
        compiler_params=_cp("parallel", "parallel"),
        name="dsa_attention",
    )(q, k, vt, bias)


def _rope_tables(rows, dim):
    inv = 1.0 / jnp.power(ROPE_THETA, jnp.arange(0, dim, 2, dtype=F32) / dim)
    ang = jnp.arange(rows, dtype=F32)[:, None] * inv[None, :]
    cos, sin = jnp.cos(ang), jnp.sin(ang)
    reps = LANES // dim
    return (jnp.tile(jnp.concatenate([cos, cos], axis=1), (1, reps)),
            jnp.tile(jnp.concatenate([-sin, sin], axis=1), (1, reps)))


W_IN_KR = 3 * BRANCH + MLA_Q_RANK + MLA_KV_RANK
W_IN_DQ = W_IN_KR + MLA_ROPE
W_IN_IK = W_IN_DQ + 4 * BRANCH
W_IN_IW = W_IN_IK + IDX_DIM
W_IN_PU = W_IN_IW + IDX_HEADS


def _small_w_in(w_t):
    zeros = lambda n: jnp.zeros((w_t.shape[0], n, w_t.shape[2]), w_t.dtype)
    return jnp.concatenate([w_t[:, W_IN_KR:W_IN_DQ], zeros(LANES - MLA_ROPE),
                            w_t[:, W_IN_IK:W_IN_IW], w_t[:, W_IN_IW:W_IN_PU],
                            zeros(LANES - IDX_DIM - IDX_HEADS)], axis=1)


def _layout_w_uq(w):
    w3 = w.reshape(MLA_Q_RANK, MLA_HEADS, MLA_NOPE + MLA_ROPE)
    w3 = jnp.pad(w3, ((0, 0), (0, 0), (0, 2 * LANES - MLA_NOPE - MLA_ROPE)))
    return w3.reshape(MLA_Q_RANK, MLA_HEADS * 2 * LANES).astype(BF16)


def _layout_w_ukv(w):
    w4 = w.reshape(MLA_KV_RANK, MLA_HEADS, 2, MLA_NOPE)
    return w4.transpose(0, 2, 1, 3).reshape(MLA_KV_RANK, 2 * BRANCH).astype(BF16)


def _forward(x, meta_tokens, norm_mix_pre, norm_mix_post, norm_ffn_pre, norm_ffn_post,
             w_in, conv_w, mla_q_norm, mla_w_uq, mla_kv_norm, mla_w_ukv, pool_w, pool_scale,
             w_branch, w_gate, b_gate, w_out, ffn_w_gate, ffn_w_up, ffn_w_down):
    assert x.shape[0] == 1 and x.shape[2] == D_MODEL
    depth = w_in.shape[0]
    seq = x.shape[1]
    length = N_META + seq
    topk = min(IDX_TOPK_MAX, length // 4)
    rows = -(-length // ROW_TILE) * ROW_TILE
    assert rows % Q_TILE == 0 and rows % K_TILE == 0 and rows % NORM_TILE == 0
    assert rows < (1 << 14)

    c64, s64 = _rope_tables(rows, 64)
    c128, s128 = _rope_tables(rows, 128)
    assert W_IN_KR == HALF_COLS
    w_in_t = jnp.swapaxes(w_in, 1, 2)
    w_in_small = _small_w_in(w_in_t)
    tn = 512
    wide = 2 * tn
    gate_tiles = D_MODEL // wide

    h, xn = _assemble_prenorm(x[0], meta_tokens.astype(F32), norm_mix_pre[0][None], rows)
    for l in range(depth):
        at_layer = lambda j, l=l: (l,)
        tile = lambda j: j
        proj_a = _wide_proj(xn, w_in_t, at_layer, tile, HALF_COLS // wide, ROW_TILE, tn, BF16,
                            "in_proj_a", weights_nk=True)
        proj_b = _wide_proj(xn, w_in_t, at_layer, tile, 4 * BRANCH // wide, ROW_TILE, tn, BF16,
                            "in_proj_b", weights_nk=True,
                            first_half_block=W_IN_DQ // tn, row_shift=W_IN_DQ % tn)
        proj_p = _wide_proj(xn, w_in_t, at_layer, tile, BRANCH // wide, ROW_TILE, tn, BF16,
                            "in_proj_pool", weights_nk=True,
                            first_half_block=W_IN_PU // tn, row_shift=W_IN_PU % tn)
        small = _stacked_proj(xn, w_in_small, at_layer, tile, 1, ROW_TILE, SMALL_COLS, F32,
                              "in_proj_small", weights_nk=True)
        gates = _wide_proj(xn, w_gate,
                           lambda j, l=l: (l, j // gate_tiles), lambda j: j % gate_tiles,
                           N_BRANCH * gate_tiles, ROW_TILE, tn, BF16, "gates",
                           bias=b_gate[:, :, None, :])

        y_conv = _conv_branch(proj_a, conv_w, l, ROW_TILE)
        y_pool = _pool_branch(proj_p, pool_w, pool_scale[:, None, :], l, ROW_TILE)

        q_m, kn_m, kpe_m, vt_m = _mla_prep(
            proj_a, small, mla_q_norm[:, None, :], mla_kv_norm[:, None, :],
            _layout_w_uq(mla_w_uq[l]), _layout_w_ukv(mla_w_ukv[l]), c64, s64, l, ROW_TILE)
        y_mla = _mla_attention(q_m, kn_m, kpe_m, vt_m)

        q_d, k_d, vt_d, iq_t, ik_ab, wt = _dsa_prep(proj_b, small, c128, s128, c64, s64, ROW_TILE)
        bias = _indexer(iq_t, ik_ab, wt, topk)
        y_dsa = _dsa_attention(q_d, k_d, vt_d, bias)

        merged = _gated_merge((y_conv, y_mla, y_dsa, y_pool), gates, w_branch, l, HALF_ROW_TILE, tn)
        mix = _wide_proj(merged, w_out, at_layer, tile, D_MODEL // wide, ROW_TILE, tn, F32, "out_proj")
        h, xn = _resid_norm(h, mix, norm_mix_post[l][None], norm_ffn_pre[l][None])

        act = _swiglu(xn, ffn_w_gate, ffn_w_up, l, ROW_TILE, 256)
        f = _wide_proj(act, ffn_w_down, at_layer, tile, D_MODEL // tn, HALF_ROW_TILE, tn // 2, F32,
                       "ffn_down")
        if l + 1 < depth:
            h, xn = _resid_norm(h, f, norm_ffn_post[l][None], norm_mix_pre[l + 1][None])
        else:
            out = _resid_out(h, f, norm_ffn_post[l][None], seq)

    return out[None]


def kernel(x, meta_tokens, norm_mix_pre, norm_mix_post, norm_ffn_pre, norm_ffn_post, w_in, conv_w, mla_q_norm, mla_w_uq, mla_kv_norm, mla_w_ukv, pool_w, pool_scale, w_branch, w_gate, b_gate, w_out, ffn_w_gate, ffn_w_up, ffn_w_down):
    return _forward(x, meta_tokens, norm_mix_pre, norm_mix_post, norm_ffn_pre, norm_ffn_post,
                    w_in, conv_w, mla_q_norm, mla_w_uq, mla_kv_norm, mla_w_ukv, pool_w, pool_scale,
                    w_branch, w_gate, b_gate, w_out, ffn_w_gate, ffn_w_up, ffn_w_down)
```

```python
import functools
import math

import jax
import jax.numpy as jnp
from jax import lax
from jax.experimental import pallas as pl
from jax.experimental.pallas import tpu as pltpu

F32 = jnp.float32
BF16 = jnp.bfloat16

D_MODEL = 4096
N_META = 16
ROPE_THETA = 10000.0
EPS = 1e-6
N_BRANCH = 4
BRANCH = 1024
CONV_K = 3
MLA_NOPE, MLA_ROPE, MLA_V, MLA_HEADS = 128, 64, 128, 8
MLA_Q_RANK, MLA_KV_RANK = 1536, 512
DSA_DIM, DSA_HEADS = 128, 8
IDX_HEADS, IDX_DIM, IDX_TOPK_MAX = 16, 64, 256
POOL_WINDOWS = (2, 4, 8, 16)
POOL_GROUP = 256
D_FF = 11008

LANES = 128
HALO = 16
ROW_TILE = 768
HALF_ROW_TILE = ROW_TILE // 2
Q_TILE = 256
K_TILE = 768
V_ROWS = 144
HEADS_PER_STEP = 4
IDX_K_TILE = 256
NEXT_ROWS = 256
COUNT_CHUNK = 256
COUNT_ACCS = 4
WRITE_CHUNK = 64
NORM_TILE = 384
NEG = -1e30
LOG2E = math.log2(math.e)
VMEM_LIMIT = 58 * 1024 * 1024

HALF_COLS = 5120
SMALL_COLS = 256


def _cp(*sem):
    return pltpu.CompilerParams(dimension_semantics=sem, vmem_limit_bytes=VMEM_LIMIT)


def _sigmoid(x):
    return 0.5 * jnp.tanh(0.5 * x) + 0.5


def _rms(x, g):
    return x * lax.rsqrt(jnp.mean(x * x, axis=-1, keepdims=True) + EPS) * g


def _matmul_w(a, wbf, weights_nk):
    if weights_nk:
        return lax.dot_general(a, wbf, (((1,), (1,)), ((), ())), preferred_element_type=F32)
    return jnp.dot(a, wbf, preferred_element_type=F32)


def _proj_kernel(a_ref, w_ref, o_ref, wbf_ref, *, weights_nk):
    @pl.when(pl.program_id(1) == 0)
    def _():
        wbf_ref[...] = w_ref[...].astype(wbf_ref.dtype)

    o_ref[...] = _matmul_w(a_ref[...], wbf_ref[...], weights_nk).astype(o_ref.dtype)


def _stacked_proj(a, w, lead_of, col_of, n_tiles, tm, tn, out_dtype, name,
                  weights_nk=False):
    m, k = a.shape
    lead = (None,) * (w.ndim - 2)
    if weights_nk:
        assert w.shape[-1] == k
        w_block, w_index = (tn, k), lambda j, i: (*lead_of(j), col_of(j), 0)
    else:
        assert w.shape[-2] == k
        w_block, w_index = (k, tn), lambda j, i: (*lead_of(j), 0, col_of(j))
    return pl.pallas_call(
        functools.partial(_proj_kernel, weights_nk=weights_nk),
        grid=(n_tiles, m // tm),
        in_specs=[pl.BlockSpec((tm, k), lambda j, i: (i, 0)),
                  pl.BlockSpec(lead + w_block, w_index)],
        out_specs=pl.BlockSpec((tm, tn), lambda j, i: (i, j)),
        out_shape=jax.ShapeDtypeStruct((m, n_tiles * tn), out_dtype),
        scratch_shapes=[pltpu.VMEM(w_block, BF16)],
        compiler_params=_cp("parallel", "arbitrary"),
        name=name,
    )(a, w)


def _wide_proj_kernel(*refs, has_bias, weights_nk, row_shift):
    refs = list(refs)
    a_ref, w_ref = refs[0], refs[1]
    pos = 2
    next_ref = b_ref = None
    if row_shift:
        next_ref = refs[pos]
        pos += 1
    if has_bias:
        b_ref = refs[pos]
        pos += 1
    o_ref, wbf_ref = refs[pos], refs[pos + 1]
    half = o_ref.shape[1] // 2
    i = pl.program_id(1)

    def cast_half(lo):
        if not weights_nk:
            wbf_ref[:, lo:lo + half] = w_ref[...].astype(wbf_ref.dtype)
        elif not row_shift:
            wbf_ref[lo:lo + half, :] = w_ref[...].astype(wbf_ref.dtype)
        else:
            keep = half - row_shift
            wbf_ref[lo:lo + keep, :] = w_ref[row_shift:half, :].astype(wbf_ref.dtype)
            wbf_ref[lo + keep:lo + half, :] = next_ref[0:row_shift, :].astype(wbf_ref.dtype)

    def emit(lo, width):
        wbf = wbf_ref[lo:lo + width, :] if weights_nk else wbf_ref[:, lo:lo + width]
        z = _matmul_w(a_ref[...], wbf, weights_nk)
        if has_bias:
            z = _sigmoid(z + b_ref[:, lo:lo + width])
        o_ref[:, lo:lo + width] = z.astype(o_ref.dtype)

    @pl.when(i == 0)
    def _():
        cast_half(0)
        emit(0, half)

    @pl.when(i == 1)
    def _():
        cast_half(half)
        emit(half, half)

    @pl.when(i >= 2)
    def _():
        emit(0, 2 * half)


def _wide_proj(a, w, lead_of, col_of, n_tiles, tm, half, out_dtype, name, bias=None,
               weights_nk=False, first_half_block=0, row_shift=0):
    m, k = a.shape
    assert w.shape[-1 if weights_nk else -2] == k
    assert row_shift == 0 or (weights_nk and row_shift % 8 == 0 and row_shift <= NEXT_ROWS)
    lead = (None,) * (w.ndim - 2)
    row = lambda i: jnp.maximum(i - 1, 0)

    def weight_window(j, i):
        jw = jnp.where(i >= 2, jnp.minimum(j + 1, n_tiles - 1), j)
        return lead_of(jw), first_half_block + 2 * col_of(jw) + jnp.where(i == 1, 1, 0)

    def w_index(j, i):
        stacked, hb = weight_window(j, i)
        return (*stacked, hb, 0) if weights_nk else (*stacked, 0, hb)

    in_specs = [pl.BlockSpec((tm, k), lambda j, i: (row(i), 0)),
                pl.BlockSpec(lead + ((half, k) if weights_nk else (k, half)), w_index)]
    args = [a, w]
    if row_shift:
        def next_index(j, i):
            stacked, hb = weight_window(j, i)
            return (*stacked, (hb + 1) * (half // NEXT_ROWS), 0)

        in_specs.append(pl.BlockSpec(lead + (NEXT_ROWS, k), next_index))
        args.append(w)
    if bias is not None:
        in_specs.append(pl.BlockSpec(lead + (1, 2 * half), lambda j, i: (*lead_of(j), 0, col_of(j))))
        args.append(bias)
    return pl.pallas_call(
        functools.partial(_wide_proj_kernel, has_bias=bias is not None, weights_nk=weights_nk,
                          row_shift=row_shift),
        grid=(n_tiles, m // tm + 1),
        in_specs=in_specs,
        out_specs=pl.BlockSpec((tm, 2 * half), lambda j, i: (row(i), j)),
        out_shape=jax.ShapeDtypeStruct((m, n_tiles * 2 * half), out_dtype),
        scratch_shapes=[pltpu.VMEM((2 * half, k) if weights_nk else (k, 2 * half), BF16)],
        compiler_params=_cp("parallel", "arbitrary"),
        name=name,
    )(*args)


def _swiglu_kernel(a_ref, wg_ref, wu_ref, o_ref, wgbf_ref, wubf_ref, *, n_half_blocks):
    half = wg_ref.shape[1]
    j = pl.program_id(0)
    i = pl.program_id(1)

    def cast_half(lo):
        wgbf_ref[:, lo:lo + half] = wg_ref[...].astype(wgbf_ref.dtype)
        wubf_ref[:, lo:lo + half] = wu_ref[...].astype(wubf_ref.dtype)

    def emit(lo, width):
        a = a_ref[...]
        g = jnp.dot(a, wgbf_ref[:, lo:lo + width], preferred_element_type=F32)
        u = jnp.dot(a, wubf_ref[:, lo:lo + width], preferred_element_type=F32)
        o_ref[:, lo:lo + width] = (g * _sigmoid(g) * u).astype(o_ref.dtype)

    has_right = 2 * j + 1 < n_half_blocks

    @pl.when(i == 0)
    def _():
        cast_half(0)
        emit(0, half)

    @pl.when((i == 1) & has_right)
    def _():
        cast_half(half)
        emit(half, half)

    @pl.when((i >= 2) & has_right)
    def _():
        emit(0, 2 * half)

    @pl.when((i >= 2) & jnp.logical_not(has_right))
    def _():
        emit(0, half)


def _swiglu(a, wg, wu, layer, tm, half):
    m, k = a.shape
    n = wg.shape[2]
    n_half_blocks = n // half
    assert n_half_blocks * half == n
    n_tiles = -(-n_half_blocks // 2)
    row = lambda i: jnp.maximum(i - 1, 0)
    def w_index(j, i):
        jw = jnp.where(i >= 2, jnp.minimum(j + 1, n_tiles - 1), j)
        return (layer, 0, jnp.minimum(2 * jw + jnp.where(i == 1, 1, 0), n_half_blocks - 1))

    w_spec = pl.BlockSpec((None, k, half), w_index)
    return pl.pallas_call(
        functools.partial(_swiglu_kernel, n_half_blocks=n_half_blocks),
        grid=(n_tiles, m // tm + 1),
        in_specs=[pl.BlockSpec((tm, k), lambda j, i: (row(i), 0)), w_spec, w_spec],
        out_specs=pl.BlockSpec((tm, 2 * half), lambda j, i: (row(i), j)),
        out_shape=jax.ShapeDtypeStruct((m, n), BF16),
        scratch_shapes=[pltpu.VMEM((k, 2 * half), BF16), pltpu.VMEM((k, 2 * half), BF16)],
        compiler_params=_cp("parallel", "arbitrary"),
        name="ffn_swiglu",
    )(a, wg, wu)


def _merge_kernel(y0_ref, y1_ref, y2_ref, y3_ref, g0_ref, g1_ref, g2_ref, g3_ref, w_ref,
                  o_ref, wbf_ref):
    half = w_ref.shape[2]
    i = pl.program_id(1)
    branches = ((y0_ref, g0_ref), (y1_ref, g1_ref), (y2_ref, g2_ref), (y3_ref, g3_ref))

    def cast_half(lo):
        wbf_ref[:, :, lo:lo + half] = w_ref[...].astype(wbf_ref.dtype)

    def emit(lo, width):
        acc = None
        for b, (y_ref, g_ref) in enumerate(branches):
            val = g_ref[:, lo:lo + width].astype(F32) * jnp.dot(
                y_ref[...], wbf_ref[b, :, lo:lo + width], preferred_element_type=F32)
            acc = val if acc is None else acc + val
        o_ref[:, lo:lo + width] = acc.astype(o_ref.dtype)

    @pl.when(i == 0)
    def _():
        cast_half(0)
        emit(0, half)

    @pl.when(i == 1)
    def _():
        cast_half(half)
        emit(half, half)

    @pl.when(i >= 2)
    def _():
        emit(0, 2 * half)


def _gated_merge(ys, gates, w_branch, layer, tm, half):
    m = ys[0].shape[0]
    nj = D_MODEL // (2 * half)
    row = lambda i: jnp.maximum(i - 1, 0)

    def w_index(j, i):
        jw = jnp.where(i >= 2, jnp.minimum(j + 1, nj - 1), j)
        return (layer, 0, 0, 2 * jw + jnp.where(i == 1, 1, 0))

    y_spec = pl.BlockSpec((tm, BRANCH), lambda j, i: (row(i), 0))
    g_specs = [pl.BlockSpec((tm, 2 * half),
                            functools.partial(lambda j, i, b: (row(i), b * nj + j), b=b))
               for b in range(N_BRANCH)]
    return pl.pallas_call(
        _merge_kernel,
        grid=(nj, m // tm + 1),
        in_specs=[y_spec] * N_BRANCH + g_specs
                 + [pl.BlockSpec((None, N_BRANCH, BRANCH, half), w_index)],
        out_specs=pl.BlockSpec((tm, 2 * half), lambda j, i: (row(i), j)),
        out_shape=jax.ShapeDtypeStruct((m, D_MODEL), BF16),
        scratch_shapes=[pltpu.VMEM((N_BRANCH, BRANCH, 2 * half), BF16)],
        compiler_params=_cp("parallel", "arbitrary"),
        name="gated_merge",
    )(*ys, gates, gates, gates, gates, w_branch)


def _assemble_kernel(x_ref, prev_ref, meta_ref, g_ref, h_ref, xn_ref, *, length):
    tr = x_ref.shape[0]
    i = pl.program_id(0)
    head = jnp.where(i == 0, meta_ref[...], prev_ref[...])
    tile = jnp.concatenate([head, x_ref[0:tr - N_META, :]], axis=0)
    pos = i * tr + lax.broadcasted_iota(jnp.int32, (tr, 1), 0)
    h = jnp.where(pos < length, tile, 0.0)
    h_ref[...] = h
    xn_ref[...] = _rms(h, g_ref[...]).astype(xn_ref.dtype)


def _assemble_prenorm(x2d, meta, g, rows):
    seq = x2d.shape[0]
    assert N_META == HALO and NORM_TILE % N_META == 0
    last_x = (seq - 1) // NORM_TILE
    last_prev = (seq - 1) // N_META
    per_tile = NORM_TILE // N_META
    row = pl.BlockSpec((NORM_TILE, D_MODEL), lambda i: (i, 0))
    return pl.pallas_call(
        functools.partial(_assemble_kernel, length=N_META + seq),
        grid=(rows // NORM_TILE,),
        in_specs=[pl.BlockSpec((NORM_TILE, D_MODEL), lambda i: (jnp.minimum(i, last_x), 0)),
                  pl.BlockSpec((N_META, D_MODEL),
                               lambda i: (jnp.clip(i * per_tile - 1, 0, last_prev), 0)),
                  pl.BlockSpec((N_META, D_MODEL), lambda i: (0, 0)),
                  pl.BlockSpec((1, D_MODEL), lambda i: (0, 0))],
        out_specs=[row, row],
        out_shape=[jax.ShapeDtypeStruct((rows, D_MODEL), F32),
                   jax.ShapeDtypeStruct((rows, D_MODEL), BF16)],
        compiler_params=_cp("parallel"),
        name="assemble_prenorm",
    )(x2d, x2d, meta, g)


def _resid_norm_kernel(h_ref, o_ref, gpost_ref, gnext_ref, hn_ref, xn_ref):
    hn = h_ref[...] + _rms(o_ref[...], gpost_ref[...])
    hn_ref[...] = hn
    xn_ref[...] = _rms(hn, gnext_ref[...]).astype(xn_ref.dtype)


def _resid_out_kernel(h_ref, o_ref, hnext_ref, onext_ref, gpost_ref, out_ref):
    g = gpost_ref[...]
    cur = h_ref[N_META:, :] + _rms(o_ref[N_META:, :], g)
    nxt = hnext_ref[...] + _rms(onext_ref[...], g)
    out_ref[...] = jnp.concatenate([cur, nxt], axis=0)


def _resid_norm(h, o, g_post, g_next):
    m = h.shape[0]
    row = pl.BlockSpec((NORM_TILE, D_MODEL), lambda i: (i, 0))
    gain = pl.BlockSpec((1, D_MODEL), lambda i: (0, 0))
    return pl.pallas_call(
        _resid_norm_kernel,
        grid=(m // NORM_TILE,),
        in_specs=[row, row, gain, gain],
        out_specs=[row, row],
        out_shape=[jax.ShapeDtypeStruct((m, D_MODEL), F32),
                   jax.ShapeDtypeStruct((m, D_MODEL), BF16)],
        compiler_params=_cp("parallel"),
        name="resid_norm",
    )(h, o, g_post, g_next)


def _resid_out(h, o, g_post, seq):
    m = h.shape[0]
    per_tile = NORM_TILE // N_META
    last_head = m // N_META - 1
    row = pl.BlockSpec((NORM_TILE, D_MODEL), lambda i: (i, 0))
    head = pl.BlockSpec((N_META, D_MODEL), lambda i: (jnp.minimum((i + 1) * per_tile, last_head), 0))
    return pl.pallas_call(
        _resid_out_kernel,
        grid=(-(-seq // NORM_TILE),),
        in_specs=[row, row, head, head, pl.BlockSpec((1, D_MODEL), lambda i: (0, 0))],
        out_specs=row,
        out_shape=jax.ShapeDtypeStruct((seq, D_MODEL), F32),
        compiler_params=_cp("parallel"),
        name="resid_out",
    )(h, o, h, o, g_post)


def _rope_half128(x, cos, sin_signed):
    return x * cos + pltpu.roll(x, 64, axis=1) * sin_signed


def _rope_half64(x, cos, sin_signed):
    lane = lax.broadcasted_iota(jnp.int32, x.shape, 1)
    partner = jnp.where((lane & 32) == 0, pltpu.roll(x, 96, axis=1), pltpu.roll(x, 32, axis=1))
    return x * cos + partner * sin_signed


def _conv_kernel(cb_ref, cc_ref, cu_ref, pc_ref, pu_ref, w_ref, y_ref, z_ref):
    tr = cb_ref.shape[0]
    i = pl.program_id(0)
    z = cc_ref[...].astype(F32) * cu_ref[...].astype(F32)
    zp = pc_ref[...].astype(F32) * pu_ref[...].astype(F32)
    z_ref[0:HALO, :] = jnp.where(i > 0, zp, 0.0)
    z_ref[HALO:HALO + tr, :] = z
    w = w_ref[...]
    conv = (w[2:3, :] * z
            + w[1:2, :] * z_ref[HALO - 1:HALO - 1 + tr, :]
            + w[0:1, :] * z_ref[HALO - 2:HALO - 2 + tr, :])
    y_ref[...] = (cb_ref[...].astype(F32) * conv).astype(y_ref.dtype)


def _conv_branch(proj, conv_w, layer, tr):
    m = proj.shape[0]
    cw = 256
    nb = BRANCH // cw

    def halo_row(i):
        return jnp.maximum(i * (tr // HALO) - 1, 0)

    return pl.pallas_call(
        _conv_kernel,
        grid=(m // tr, nb),
        in_specs=[pl.BlockSpec((tr, cw), lambda i, c: (i, c)),
                  pl.BlockSpec((tr, cw), lambda i, c: (i, nb + c)),
                  pl.BlockSpec((tr, cw), lambda i, c: (i, 2 * nb + c)),
                  pl.BlockSpec((HALO, cw), lambda i, c: (halo_row(i), nb + c)),
                  pl.BlockSpec((HALO, cw), lambda i, c: (halo_row(i), 2 * nb + c)),
                  pl.BlockSpec((None, CONV_K, cw), lambda i, c: (layer, 0, c))],
        out_specs=pl.BlockSpec((tr, cw), lambda i, c: (i, c)),
        out_shape=jax.ShapeDtypeStruct((m, BRANCH), BF16),
        scratch_shapes=[pltpu.VMEM((HALO + tr, cw), F32)],
        compiler_params=_cp("parallel", "parallel"),
        name="conv_branch",
    )(proj, proj, proj, proj, proj, conv_w)


def _pool_kernel(u_ref, pu_ref, w_ref, s_ref, y_ref, x_ref):
    tr = u_ref.shape[0]
    i = pl.program_id(0)
    x_ref[0:HALO, :] = jnp.where(i > 0, pu_ref[...].astype(F32), 0.0)
    x_ref[HALO:HALO + tr, :] = u_ref[...].astype(F32)
    t = i * tr + lax.broadcasted_iota(jnp.int32, (tr, POOL_GROUP), 0)
    for g, win in enumerate(POOL_WINDOWS):
        cols = slice(g * POOL_GROUP, (g + 1) * POOL_GROUP)
        x = x_ref[HALO:HALO + tr, cols]
        total = x
        for j in range(1, win):
            total = total + x_ref[HALO - j:HALO - j + tr, cols]
        count = jnp.minimum(t + 1, win).astype(F32)
        pooled = (total / count - x).astype(BF16)
        mixed = jnp.dot(pooled, w_ref[g].astype(BF16), preferred_element_type=F32)
        y_ref[:, cols] = (mixed * s_ref[:, cols]).astype(y_ref.dtype)


def _pool_branch(proj, pool_w, pool_scale, layer, tr):
    m = proj.shape[0]
    assert max(POOL_WINDOWS) <= HALO
    n_groups = len(POOL_WINDOWS)
    return pl.pallas_call(
        _pool_kernel,
        grid=(m // tr,),
        in_specs=[pl.BlockSpec((tr, BRANCH), lambda i: (i, 0)),
                  pl.BlockSpec((HALO, BRANCH), lambda i: (jnp.maximum(i * (tr // HALO) - 1, 0), 0)),
                  pl.BlockSpec((None, n_groups, POOL_GROUP, POOL_GROUP), lambda i: (layer, 0, 0, 0)),
                  pl.BlockSpec((None, 1, BRANCH), lambda i: (layer, 0, 0))],
        out_specs=pl.BlockSpec((tr, BRANCH), lambda i: (i, 0)),
        out_shape=jax.ShapeDtypeStruct((m, BRANCH), BF16),
        scratch_shapes=[pltpu.VMEM((HALO + tr, BRANCH), F32)],
        compiler_params=_cp("parallel"),
        name="pool_branch",
    )(proj, proj, pool_w, pool_scale)


def _store_value_t(vt_ref, v, heads):
    tr = v.shape[0]
    n_kt, _, tk = vt_ref.shape
    assert n_kt * tk == tr
    vt = v.T.astype(vt_ref.dtype)
    ones = jnp.ones((V_ROWS - LANES, tk), vt_ref.dtype)
    for c in range(n_kt):
        for h in range(heads):
            vt_ref[c, h * V_ROWS:h * V_ROWS + LANES, :] = vt[h * LANES:(h + 1) * LANES,
                                                             c * tk:(c + 1) * tk]
            vt_ref[c, h * V_ROWS + LANES:(h + 1) * V_ROWS, :] = ones


def _mla_prep_kernel(cq_ref, ckv_ref, sm_ref, gq_ref, gkv_ref, wuq_ref, wukv_ref,
                     cos_ref, sin_ref, qt_ref, kn_ref, kpe_ref, vt_ref):
    cos = cos_ref[...]
    sin = sin_ref[...]
    scale = (MLA_NOPE + MLA_ROPE) ** -0.5 * LOG2E
    cqn = _rms(cq_ref[...].astype(F32), gq_ref[...]).astype(BF16)
    q = jnp.dot(cqn, wuq_ref[...], preferred_element_type=F32)
    for h in range(MLA_HEADS):
        lo = 2 * h * LANES
        qt_ref[lo:lo + LANES, :] = (q[:, lo:lo + LANES] * scale).T.astype(qt_ref.dtype)
        pe = _rope_half64(q[:, lo + LANES:lo + 2 * LANES], cos, sin)
        qt_ref[lo + LANES:lo + 2 * LANES, :] = (pe * scale).T.astype(qt_ref.dtype)
    ckvn = _rms(ckv_ref[...].astype(F32), gkv_ref[...]).astype(BF16)
    kv = jnp.dot(ckvn, wukv_ref[...], preferred_element_type=F32)
    kn_ref[...] = kv[:, :BRANCH].astype(kn_ref.dtype)
    _store_value_t(vt_ref, kv[:, BRANCH:], MLA_HEADS)
    kpe_ref[...] = _rope_half64(sm_ref[:, 0:LANES], cos, sin).astype(kpe_ref.dtype)


def _mla_prep(proj, small, gq, gkv, wuq, wukv, cos64, sin64, layer, tr):
    m = proj.shape[0]
    qw = 2 * LANES * MLA_HEADS
    return pl.pallas_call(
        _mla_prep_kernel,
        grid=(m // tr,),
        in_specs=[pl.BlockSpec((tr, MLA_Q_RANK), lambda i: (i, 3 * BRANCH // MLA_Q_RANK)),
                  pl.BlockSpec((tr, MLA_KV_RANK), lambda i: (i, (3 * BRANCH + MLA_Q_RANK) // MLA_KV_RANK)),
                  pl.BlockSpec((tr, SMALL_COLS), lambda i: (i, 0)),
                  pl.BlockSpec((None, 1, MLA_Q_RANK), lambda i: (layer, 0, 0)),
                  pl.BlockSpec((None, 1, MLA_KV_RANK), lambda i: (layer, 0, 0)),
                  pl.BlockSpec((MLA_Q_RANK, qw), lambda i: (0, 0)),
                  pl.BlockSpec((MLA_KV_RANK, 2 * BRANCH), lambda i: (0, 0)),
                  pl.BlockSpec((tr, LANES), lambda i: (i, 0)),
                  pl.BlockSpec((tr, LANES), lambda i: (i, 0))],
        out_specs=[pl.BlockSpec((qw, tr), lambda i: (0, i)),
                   pl.BlockSpec((tr, BRANCH), lambda i: (i, 0)),
                   pl.BlockSpec((tr, LANES), lambda i: (i, 0)),
                   pl.BlockSpec((tr // K_TILE, MLA_HEADS * V_ROWS, K_TILE), lambda i: (i, 0, 0))],
        out_shape=[jax.ShapeDtypeStruct((qw, m), BF16),
                   jax.ShapeDtypeStruct((m, BRANCH), BF16),
                   jax.ShapeDtypeStruct((m, LANES), BF16),
                   jax.ShapeDtypeStruct((m // K_TILE, MLA_HEADS * V_ROWS, K_TILE), BF16)],
        compiler_params=_cp("parallel"),
        name="mla_prep",
    )(proj, proj, small, gq, gkv, wuq, wukv, cos64, sin64)


def _dsa_prep_kernel(dq_ref, dk_ref, dv_ref, iq_ref, sm_ref, c128_ref, s128_ref, c64_ref, s64_ref,
                     qt_ref, k_ref, vt_ref, iqt_ref, ik_ref, wt_ref):
    c128, s128 = c128_ref[...], s128_ref[...]
    c64, s64 = c64_ref[...], s64_ref[...]
    scale = DSA_DIM ** -0.5 * LOG2E
    for h in range(BRANCH // LANES):
        sl = slice(h * LANES, (h + 1) * LANES)
        qt_ref[sl, :] = (_rope_half128(dq_ref[:, sl].astype(F32), c128, s128) * scale).T.astype(qt_ref.dtype)
        k_ref[:, sl] = _rope_half128(dk_ref[:, sl].astype(F32), c128, s128).astype(k_ref.dtype)
        iqt_ref[sl, :] = _rope_half64(iq_ref[:, sl].astype(F32), c64, s64).T.astype(iqt_ref.dtype)
    _store_value_t(vt_ref, dv_ref[...].astype(F32), DSA_HEADS)
    tail = sm_ref[:, LANES:2 * LANES]
    lane = lax.broadcasted_iota(jnp.int32, tail.shape, 1)
    ik_lo = jnp.where(lane < IDX_DIM, _rope_half64(tail, c64, s64), 0.0)
    ik_ref[:, 0:LANES] = ik_lo.astype(ik_ref.dtype)
    ik_ref[:, LANES:2 * LANES] = pltpu.roll(ik_lo, IDX_DIM, axis=1).astype(ik_ref.dtype)
    idx_w_scale = (IDX_HEADS ** -0.5) * (IDX_DIM ** -0.5)
    wt_ref[...] = (tail * idx_w_scale).T


def _dsa_prep(proj_b, small, c128, s128, c64, s64, tr):
    m = proj_b.shape[0]
    col = lambda c: pl.BlockSpec((tr, BRANCH), lambda i: (i, c))
    tab = pl.BlockSpec((tr, LANES), lambda i: (i, 0))
    return pl.pallas_call(
        _dsa_prep_kernel,
        grid=(m // tr,),
        in_specs=[col(0), col(1), col(2), col(3),
                  pl.BlockSpec((tr, SMALL_COLS), lambda i: (i, 0)),
                  tab, tab, tab, tab],
        out_specs=[pl.BlockSpec((BRANCH, tr), lambda i: (0, i)),
                   pl.BlockSpec((tr, BRANCH), lambda i: (i, 0)),
                   pl.BlockSpec((tr // K_TILE, DSA_HEADS * V_ROWS, K_TILE), lambda i: (i, 0, 0)),
                   pl.BlockSpec((BRANCH, tr), lambda i: (0, i)),
                   pl.BlockSpec((tr, 2 * LANES), lambda i: (i, 0)),
                   pl.BlockSpec((LANES, tr), lambda i: (0, i))],
        out_shape=[jax.ShapeDtypeStruct((BRANCH, m), BF16),
                   jax.ShapeDtypeStruct((m, BRANCH), BF16),
                   jax.ShapeDtypeStruct((m // K_TILE, DSA_HEADS * V_ROWS, K_TILE), BF16),
                   jax.ShapeDtypeStruct((BRANCH, m), BF16),
                   jax.ShapeDtypeStruct((m, 2 * LANES), BF16),
                   jax.ShapeDtypeStruct((LANES, m), F32)],
        compiler_params=_cp("parallel"),
        name="dsa_prep",
    )(proj_b, proj_b, proj_b, proj_b, small, c128, s128, c64, s64)


def _float_to_ordered_int(s):
    b = lax.bitcast_convert_type(s, jnp.int32)
    return b ^ ((b >> 31) & jnp.int32(0x7FFFFFFF))


def _indexer_kernel(iq_ref, ik_ref, wt_ref, bias_ref, key_ref, cut_ref, high_ref, low_ref, *, topk):
    tq = iq_ref.shape[1]
    total_rows = ik_ref.shape[0]
    tk = IDX_K_TILE
    i = pl.program_id(0)
    n_tiles = (i * tq + tq) // tk
    n_rows = n_tiles * tk
    qpos = i * tq + lax.broadcasted_iota(jnp.int32, (1, tq), 1)
    int_min = jnp.int32(-2 ** 31)

    def score_tile(kt, carry):
        start = pl.multiple_of(kt * tk, tk)
        ik_lo = ik_ref[pl.ds(start, tk), 0:LANES]
        ik_hi = ik_ref[pl.ds(start, tk), LANES:2 * LANES]
        acc = jnp.zeros((tk, tq), F32)
        for j in range(IDX_HEADS // 2):
            qpair_t = iq_ref[j * LANES:(j + 1) * LANES, :]
            for half, ik in enumerate((ik_lo, ik_hi)):
                g = 2 * j + half
                dots = jnp.dot(ik, qpair_t, preferred_element_type=F32)
                acc = acc + jnp.maximum(dots, 0.0) * wt_ref[IDX_DIM + g:IDX_DIM + g + 1, :]
        acc = acc + 0.0
        kpos = start + lax.broadcasted_iota(jnp.int32, (tk, 1), 0)
        key = jnp.where(kpos <= qpos, _float_to_ordered_int(acc), int_min)
        key_ref[pl.ds(start, tk), :] = key
        high_ref[pl.ds(start, tk), :] = (key >> 16).astype(jnp.int16)
        return carry

    lax.fori_loop(0, n_tiles, score_tile, 0)

    n_chunks = n_rows // COUNT_CHUNK

    def count(pred, with_pos=False):
        def body(c, accs):
            start = pl.multiple_of(c * COUNT_CHUNK, COUNT_CHUNK)
            accs = list(accs)
            chunk = key_ref[pl.ds(start, COUNT_CHUNK), :]
            for r in range(COUNT_CHUNK // 8):
                blk = chunk[8 * r:8 * r + 8, :]
                if with_pos:
                    pos = start + 8 * r + lax.broadcasted_iota(jnp.int32, (8, 1), 0)
                    hit = pred(blk, pos)
                else:
                    hit = pred(blk)
                accs[r % COUNT_ACCS] = accs[r % COUNT_ACCS] + jnp.where(hit, 1, 0).astype(jnp.int32)
            return tuple(accs)
        zero = jnp.zeros((8, tq), jnp.int32)
        accs = lax.fori_loop(0, n_chunks, body, (zero,) * COUNT_ACCS)
        return jnp.sum(functools.reduce(lambda a, b: a + b, accs), axis=0, keepdims=True)

    int16_min = -(1 << 15)

    def count16(ref, cand, strict=False):
        cand16 = jnp.broadcast_to(cand, (16, tq)).astype(jnp.int16)
        one16 = jnp.ones((16, tq), jnp.int16)
        zero16 = jnp.zeros((16, tq), jnp.int16)

        def body(c, accs):
            start = pl.multiple_of(c * COUNT_CHUNK, COUNT_CHUNK)
            accs = list(accs)
            chunk = ref[pl.ds(start, COUNT_CHUNK), :]
            for r in range(COUNT_CHUNK // 16):
                blk = chunk[16 * r:16 * r + 16, :]
                hit = (blk > cand16) if strict else (blk >= cand16)
                accs[r % COUNT_ACCS] = accs[r % COUNT_ACCS] + jnp.where(hit, one16, zero16)
            return tuple(accs)
        zero = jnp.zeros((16, tq), jnp.int16)
        accs = lax.fori_loop(0, n_chunks, body, (zero,) * COUNT_ACCS)
        total = functools.reduce(lambda a, b: a + b, [a.astype(jnp.int32) for a in accs])
        return jnp.sum(total, axis=0, keepdims=True)

    def largest16(ref, want, n_all):
        def step(b, carry):
            t, n_t = carry
            cand = t + jnp.left_shift(jnp.int32(1), 15 - b)
            n_cand = count16(ref, cand)
            ok = n_cand >= want
            return jnp.where(ok, cand, t), jnp.where(ok, n_cand, n_t)
        return lax.fori_loop(0, 16, step, (jnp.full((1, tq), int16_min, jnp.int32), n_all))

    tau_high, n_high = largest16(high_ref, topk, jnp.full((1, tq), n_rows, jnp.int32))
    n_above = count16(high_ref, tau_high, strict=True)

    def park_low(c, carry):
        start = pl.multiple_of(c * COUNT_CHUNK, COUNT_CHUNK)
        key = key_ref[pl.ds(start, COUNT_CHUNK), :]
        low = (key & 0xFFFF) + int16_min
        low_ref[pl.ds(start, COUNT_CHUNK), :] = jnp.where(
            (key >> 16) == tau_high, low, int16_min).astype(jnp.int16)
        return carry

    lax.fori_loop(0, n_chunks, park_low, 0)
    tau_low, n_low = largest16(low_ref, topk - n_above, n_high - n_above)
    tau = jnp.left_shift(tau_high, 16) + (tau_low - int16_min)

    n_ge = n_above + n_low
    n_gt = n_above + count16(low_ref, tau_low, strict=True)
    need = topk - n_gt
    cut_bits = 14
    cut_ref[...] = jnp.full((1, tq), 1 << cut_bits, jnp.int32)

    @pl.when(jnp.max(n_ge) > topk)
    def _():
        def cut_step(b, cut):
            cand = cut + jnp.left_shift(jnp.int32(1), cut_bits - 1 - b)
            cnt = count(lambda blk, pos: (blk == tau) & (pos < cand), with_pos=True)
            return jnp.where(cnt <= need, cand, cut)
        cut_ref[...] = lax.fori_loop(0, cut_bits, cut_step, jnp.zeros((1, tq), jnp.int32))

    cut = cut_ref[...]

    def write_sel(c, carry, on_diagonal):
        start = pl.multiple_of(c * WRITE_CHUNK, WRITE_CHUNK)
        blk = key_ref[pl.ds(start, WRITE_CHUNK), :]
        pos = start + lax.broadcasted_iota(jnp.int32, (WRITE_CHUNK, 1), 0)
        causal = jnp.where(pos <= qpos, 0.0, NEG) if on_diagonal else 0.0
        tie = jnp.where(blk == tau, jnp.where(pos < cut, causal, NEG), NEG)
        bias_ref[pl.ds(start, WRITE_CHUNK), :] = jnp.where(blk > tau, causal, tie).astype(bias_ref.dtype)
        return carry

    first_diagonal = (n_rows - tq) // WRITE_CHUNK
    lax.fori_loop(0, first_diagonal, functools.partial(write_sel, on_diagonal=False), 0)
    lax.fori_loop(first_diagonal, n_rows // WRITE_CHUNK,
                  functools.partial(write_sel, on_diagonal=True), 0)

    def write_neg(c, carry):
        start = pl.multiple_of(c * WRITE_CHUNK, WRITE_CHUNK)
        bias_ref[pl.ds(start, WRITE_CHUNK), :] = jnp.full((WRITE_CHUNK, tq), NEG, bias_ref.dtype)
        return carry

    lax.fori_loop(n_rows // WRITE_CHUNK, total_rows // WRITE_CHUNK, write_neg, 0)


def _indexer(iq_t, ik_ab, wt, topk):
    m = ik_ab.shape[0]
    return pl.pallas_call(
        functools.partial(_indexer_kernel, topk=topk),
        grid=(m // Q_TILE,),
        in_specs=[pl.BlockSpec((BRANCH, Q_TILE), lambda i: (0, i)),
                  pl.BlockSpec((m, 2 * LANES), lambda i: (0, 0)),
                  pl.BlockSpec((LANES, Q_TILE), lambda i: (0, i))],
        out_specs=pl.BlockSpec((m, Q_TILE), lambda i: (0, i)),
        out_shape=jax.ShapeDtypeStruct((m, m), BF16),
        scratch_shapes=[pltpu.VMEM((m, Q_TILE), jnp.int32),
                        pltpu.VMEM((1, Q_TILE), jnp.int32),
                        pltpu.VMEM((m, Q_TILE), jnp.int16),
                        pltpu.VMEM((m, Q_TILE), jnp.int16)],
        compiler_params=_cp("parallel"),
        name="dsa_indexer",
    )(iq_t, ik_ab, wt)


def _flash_kernel(*refs, q_axis, has_kpe, has_bias):
    refs = list(refs)
    q_ref, k_ref = refs[0], refs[1]
    pos = 2
    kpe_ref = bias_ref = None
    if has_kpe:
        kpe_ref = refs[pos]
        pos += 1
    vt_ref = refs[pos]
    pos += 1
    if has_bias:
        bias_ref = refs[pos]
        pos += 1
    o_ref, sa_ref, sb_ref = refs[pos], refs[pos + 1], refs[pos + 2]

    tq = q_ref.shape[1]
    heads = vt_ref.shape[1] // V_ROWS
    tk = vt_ref.shape[2]
    dq = q_ref.shape[0] // heads
    assert tk % tq == 0
    i = pl.program_id(q_axis)
    n_tiles = (i * tq + tq + tk - 1) // tk
    qpos = i * tq + lax.broadcasted_iota(jnp.int32, (1, tq), 1)

    def compute_scores(kt, s_ref):
        start = pl.multiple_of(kt * tk, tk)
        kpe = kpe_ref[pl.ds(start, tk), :] if has_kpe else None
        for h in range(heads):
            k = k_ref[pl.ds(start, tk), h * LANES:(h + 1) * LANES]
            if has_kpe:
                k = jnp.concatenate([k, kpe], axis=1)
            s_ref[h] = jnp.dot(k, q_ref[h * dq:(h + 1) * dq, :], preferred_element_type=F32)

    def consume_scores(kt, s_ref, carry, causal_mask):
        start = pl.multiple_of(kt * tk, tk)
        bias = bias_ref[pl.ds(start, tk), :].astype(F32) if has_bias else None
        new = []
        for h in range(heads):
            m_run, acc = carry[h]
            s = s_ref[h]
            if has_bias:
                s = s + bias
            if causal_mask:
                kpos = start + lax.broadcasted_iota(jnp.int32, (tk, 1), 0)
                s = jnp.where(kpos <= qpos, s, NEG)
            m_new = jnp.maximum(m_run, jnp.max(s, axis=0, keepdims=True))
            alpha = jnp.exp2(m_run - m_new)
            p = jnp.exp2(s - m_new).astype(BF16)
            pv = jnp.dot(vt_ref[kt, h * V_ROWS:(h + 1) * V_ROWS, :], p, preferred_element_type=F32)
            new.append((m_new, alpha * acc + pv))
        return tuple(new)

    def double_step(u, carry):
        kt = 2 * u
        compute_scores(kt + 1, sb_ref)
        carry = consume_scores(kt, sa_ref, carry, False)
        compute_scores(kt + 2, sa_ref)
        return consume_scores(kt + 1, sb_ref, carry, False)

    mask_last = not has_bias
    last = n_tiles - 1

    def odd_tail(carry):
        compute_scores(last, sb_ref)
        carry = consume_scores(last - 1, sa_ref, carry, False)
        return consume_scores(last, sb_ref, carry, mask_last)

    def even_tail(carry):
        return consume_scores(last, sa_ref, carry, mask_last)

    init = tuple((jnp.full((1, tq), NEG, F32), jnp.zeros((V_ROWS, tq), F32)) for _ in range(heads))
    compute_scores(0, sa_ref)
    carry = lax.fori_loop(0, last // 2, double_step, init)
    carry = lax.cond(last % 2 == 1, odd_tail, even_tail, carry)
    for h in range(heads):
        acc = carry[h][1]
        out = acc[0:LANES, :] / acc[LANES:LANES + 1, :]
        o_ref[:, h * LANES:(h + 1) * LANES] = out.T.astype(o_ref.dtype)


def _score_scratch(heads, tk):
    return [pltpu.VMEM((heads, tk, Q_TILE), F32), pltpu.VMEM((heads, tk, Q_TILE), F32)]


def _mla_attention(q, kn, kpe, vt):
    m = kn.shape[0]
    n_kt, _, tk = vt.shape
    hp = HEADS_PER_STEP
    return pl.pallas_call(
        functools.partial(_flash_kernel, q_axis=1, has_kpe=True, has_bias=False),
        grid=(MLA_HEADS // hp, m // Q_TILE),
        in_specs=[pl.BlockSpec((hp * 2 * LANES, Q_TILE), lambda h, i: (h, i)),
                  pl.BlockSpec((m, hp * LANES), lambda h, i: (0, h)),
                  pl.BlockSpec((m, LANES), lambda h, i: (0, 0)),
                  pl.BlockSpec((n_kt, hp * V_ROWS, tk), lambda h, i: (0, h, 0))],
        out_specs=pl.BlockSpec((Q_TILE, hp * MLA_V), lambda h, i: (i, h)),
        out_shape=jax.ShapeDtypeStruct((m, BRANCH), BF16),
        scratch_shapes=_score_scratch(hp, tk),
        compiler_params=_cp("parallel", "parallel"),
        name="mla_attention",
    )(q, kn, kpe, vt)


def _dsa_attention(q, k, vt, bias):
    m = k.shape[0]
    n_kt, _, tk = vt.shape
    hp = HEADS_PER_STEP
    return pl.pallas_call(
        functools.partial(_flash_kernel, q_axis=1, has_kpe=False, has_bias=True),
        grid=(DSA_HEADS // hp, m // Q_TILE),
        in_specs=[pl.BlockSpec((hp * DSA_DIM, Q_TILE), lambda h, i: (h, i)),
                  pl.BlockSpec((m, hp * DSA_DIM), lambda h, i: (0, h)),
                  pl.BlockSpec((n_kt, hp * V_ROWS, tk), lambda h, i: (0, h, 0)),
                  pl.BlockSpec((m, Q_TILE), lambda h, i: (0, i))],
        out_specs=pl.BlockSpec((Q_TILE, hp * DSA_DIM), lambda h, i: (i, h)),
        out_shape=jax.ShapeDtypeStruct((m, BRANCH), BF16),
        scratch_shapes=_score_scratch(hp, tk),
        compiler_params=_cp("parallel", "parallel"),
        name="dsa_attention",
    )(q, k, vt, bias)


def _rope_tables(rows, dim):
    inv = 1.0 / jnp.power(ROPE_THETA, jnp.arange(0, dim, 2, dtype=F32) / dim)
    ang = jnp.arange(rows, dtype=F32)[:, None] * inv[None, :]
    cos, sin = jnp.cos(ang), jnp.sin(ang)
    reps = LANES // dim
    return (jnp.tile(jnp.concatenate([cos, cos], axis=1), (1, reps)),
            jnp.tile(jnp.concatenate([-sin, sin], axis=1), (1, reps)))


W_IN_KR = 3 * BRANCH + MLA_Q_RANK + MLA_KV_RANK
W_IN_DQ = W_IN_KR + MLA_ROPE
W_IN_IK = W_IN_DQ + 4 * BRANCH
W_IN_IW = W_IN_IK + IDX_DIM
W_IN_PU = W_IN_IW + IDX_HEADS


def _small_w_in(w_t):
    zeros = lambda n: jnp.zeros((w_t.shape[0], n, w_t.shape[2]), w_t.dtype)
    return jnp.concatenate([w_t[:, W_IN_KR:W_IN_DQ], zeros(LANES - MLA_ROPE),
                            w_t[:, W_IN_IK:W_IN_IW], w_t[:, W_IN_IW:W_IN_PU],
                            zeros(LANES - IDX_DIM - IDX_HEADS)], axis=1)


def _layout_w_uq(w):
    w3 = w.reshape(MLA_Q_RANK, MLA_HEADS, MLA_NOPE + MLA_ROPE)
    w3 = jnp.pad(w3, ((0, 0), (0, 0), (0, 2 * LANES - MLA_NOPE - MLA_ROPE)))
    return w3.reshape(MLA_Q_RANK, MLA_HEADS * 2 * LANES).astype(BF16)


def _layout_w_ukv(w):
    w4 = w.reshape(MLA_KV_RANK, MLA_HEADS, 2, MLA_NOPE)
    return w4.transpose(0, 2, 1, 3).reshape(MLA_KV_RANK, 2 * BRANCH).astype(BF16)


def _forward(x, meta_tokens, norm_mix_pre, norm_mix_post, norm_ffn_pre, norm_ffn_post,
             w_in, conv_w, mla_q_norm, mla_w_uq, mla_kv_norm, mla_w_ukv, pool_w, pool_scale,
             w_branch, w_gate, b_gate, w_out, ffn_w_gate, ffn_w_up, ffn_w_down):
    assert x.shape[0] == 1 and x.shape[2] == D_MODEL
    depth = w_in.shape[0]
    seq = x.shape[1]
    length = N_META + seq
    topk = min(IDX_TOPK_MAX, length // 4)
    rows = -(-length // ROW_TILE) * ROW_TILE
    assert rows % Q_TILE == 0 and rows % K_TILE == 0 and rows % NORM_TILE == 0
    assert rows < (1 << 14)

    c64, s64 = _rope_tables(rows, 64)
    c128, s128 = _rope_tables(rows, 128)
    assert W_IN_KR == HALF_COLS
    w_in_t = jnp.swapaxes(w_in, 1, 2)
    w_in_small = _small_w_in(w_in_t)
    tn = 512
    wide = 2 * tn
    gate_tiles = D_MODEL // wide

    h, xn = _assemble_prenorm(x[0], meta_tokens.astype(F32), norm_mix_pre[0][None], rows)
    for l in range(depth):
        at_layer = lambda j, l=l: (l,)
        tile = lambda j: j
        proj_a = _wide_proj(xn, w_in_t, at_layer, tile, HALF_COLS // wide, ROW_TILE, tn, BF16,
                            "in_proj_a", weights_nk=True)
        proj_b = _wide_proj(xn, w_in_t, at_layer, tile, 4 * BRANCH // wide, ROW_TILE, tn, BF16,
                            "in_proj_b", weights_nk=True,
                            first_half_block=W_IN_DQ // tn, row_shift=W_IN_DQ % tn)
        proj_p = _wide_proj(xn, w_in_t, at_layer, tile, BRANCH // wide, ROW_TILE, tn, BF16,
                            "in_proj_pool", weights_nk=True,
                            first_half_block=W_IN_PU // tn, row_shift=W_IN_PU % tn)
        small = _stacked_proj(xn, w_in_small, at_layer, tile, 1, ROW_TILE, SMALL_COLS, F32,
                              "in_proj_small", weights_nk=True)
        gates = _wide_proj(xn, w_gate,
                           lambda j, l=l: (l, j // gate_tiles), lambda j: j % gate_tiles,
                           N_BRANCH * gate_tiles, ROW_TILE, tn, BF16, "gates",
                           bias=b_gate[:, :, None, :])

        y_conv = _conv_branch(proj_a, conv_w, l, ROW_TILE)
        y_pool = _pool_branch(proj_p, pool_w, pool_scale[:, None, :], l, ROW_TILE)

        q_m, kn_m, kpe_m, vt_m = _mla_prep(
            proj_a, small, mla_q_norm[:, None, :], mla_kv_norm[:, None, :],
            _layout_w_uq(mla_w_uq[l]), _layout_w_ukv(mla_w_ukv[l]), c64, s64, l, ROW_TILE)
        y_mla = _mla_attention(q_m, kn_m, kpe_m, vt_m)

        q_d, k_d, vt_d, iq_t, ik_ab, wt = _dsa_prep(proj_b, small, c128, s128, c64, s64, ROW_TILE)
        bias = _indexer(iq_t, ik_ab, wt, topk)
        y_dsa = _dsa_attention(q_d, k_d, vt_d, bias)

        merged = _gated_merge((y_conv, y_mla, y_dsa, y_pool), gates, w_branch, l, HALF_ROW_TILE, tn)
        mix = _wide_proj(merged, w_out, at_layer, tile, D_MODEL // wide, ROW_TILE, tn, F32, "out_proj")
        h, xn = _resid_norm(h, mix, norm_mix_post[l][None], norm_ffn_pre[l][None])

        act = _swiglu(xn, ffn_w_gate, ffn_w_up, l, ROW_TILE, 256)
        f = _wide_proj(act, ffn_w_down, at_layer, tile, D_MODEL // tn, HALF_ROW_TILE, tn // 2, F32,
                       "ffn_down")
        if l + 1 < depth:
            h, xn = _resid_norm(h, f, norm_ffn_post[l][None], norm_mix_pre[l + 1][None])
        else:
            out = _resid_out(h, f, norm_ffn_post[l][None], seq)

    return out[None]


def kernel(x, meta_tokens, norm_mix_pre, norm_mix_post, norm_ffn_pre, norm_ffn_post, w_in, conv_w, mla_q_norm, mla_w_uq, mla_kv_norm, mla_w_ukv, pool_w, pool_scale, w_branch, w_gate, b_gate, w_out, ffn_w_gate, ffn_w_up, ffn_w_down):
    return _forward(x, meta_tokens, norm_mix_pre, norm_mix_post, norm_ffn_pre, norm_ffn_post,
                    w_in, conv_w, mla_q_norm, mla_w_uq, mla_kv_norm, mla_w_ukv, pool_w, pool_scale,
                    w_branch, w_gate, b_gate, w_out, ffn_w_gate, ffn_w_up, ffn_w_down)
```

```python
import functools
import math

import jax
import jax.numpy as jnp
from jax import lax
from jax.experimental import pallas as pl
from jax.experimental.pallas import tpu as pltpu

F32 = jnp.float32
BF16 = jnp.bfloat16

D_MODEL = 4096
N_META = 16
ROPE_THETA = 10000.0
EPS = 1e-6
N_BRANCH = 4
BRANCH = 1024
CONV_K = 3
MLA_NOPE, MLA_ROPE, MLA_V, MLA_HEADS = 128, 64, 128, 8
MLA_Q_RANK, MLA_KV_RANK = 1536, 512
DSA_DIM, DSA_HEADS = 128, 8
IDX_HEADS, IDX_DIM, IDX_TOPK_MAX = 16, 64, 256
POOL_WINDOWS = (2, 4, 8, 16)
POOL_GROUP = 256
D_FF = 11008

LANES = 128
HALO = 16
ROW_TILE = 768
HALF_ROW_TILE = ROW_TILE // 2
BIG_ROW_TILE = 1056
Q_TILE = 256
K_TILE = 768
V_ROWS = 144
HEADS_PER_STEP = 4
IDX_K_TILE = 256
NEXT_ROWS = 256
COUNT_CHUNK = 256
COUNT_ACCS = 4
WRITE_CHUNK = 64
NORM_TILE = 384
NEG = -1e30
LOG2E = math.log2(math.e)
VMEM_LIMIT = 58 * 1024 * 1024

HALF_COLS = 5120
SMALL_COLS = 256


def _cp(*sem):
    return pltpu.CompilerParams(dimension_semantics=sem, vmem_limit_bytes=VMEM_LIMIT)


def _sigmoid(x):
    return 0.5 * jnp.tanh(0.5 * x) + 0.5


def _rms(x, g):
    return x * lax.rsqrt(jnp.mean(x * x, axis=-1, keepdims=True) + EPS) * g


def _matmul_w(a, wbf, weights_nk):
    if weights_nk:
        return lax.dot_general(a, wbf, (((1,), (1,)), ((), ())), preferred_element_type=F32)
    return jnp.dot(a, wbf, preferred_element_type=F32)


def _proj_kernel(a_ref, w_ref, o_ref, wbf_ref, *, weights_nk):
    @pl.when(pl.program_id(1) == 0)
    def _():
        wbf_ref[...] = w_ref[...].astype(wbf_ref.dtype)

    o_ref[...] = _matmul_w(a_ref[...], wbf_ref[...], weights_nk).astype(o_ref.dtype)


def _stacked_proj(a, w, lead_of, col_of, n_tiles, tm, tn, out_dtype, name,
                  weights_nk=False):
    m, k = a.shape
    lead = (None,) * (w.ndim - 2)
    if weights_nk:
        assert w.shape[-1] == k
        w_block, w_index = (tn, k), lambda j, i: (*lead_of(j), col_of(j), 0)
    else:
        assert w.shape[-2] == k
        w_block, w_index = (k, tn), lambda j, i: (*lead_of(j), 0, col_of(j))
    return pl.pallas_call(
        functools.partial(_proj_kernel, weights_nk=weights_nk),
        grid=(n_tiles, m // tm),
        in_specs=[pl.BlockSpec((tm, k), lambda j, i: (i, 0)),
                  pl.BlockSpec(lead + w_block, w_index)],
        out_specs=pl.BlockSpec((tm, tn), lambda j, i: (i, j)),
        out_shape=jax.ShapeDtypeStruct((m, n_tiles * tn), out_dtype),
        scratch_shapes=[pltpu.VMEM(w_block, BF16)],
        compiler_params=_cp("parallel", "arbitrary"),
        name=name,
    )(a, w)


def _wide_proj_kernel(*refs, has_bias, weights_nk, row_shift):
    refs = list(refs)
    a_ref, w_ref = refs[0], refs[1]
    pos = 2
    next_ref = b_ref = None
    if row_shift:
        next_ref = refs[pos]
        pos += 1
    if has_bias:
        b_ref = refs[pos]
        pos += 1
    o_ref, wbf_ref = refs[pos], refs[pos + 1]
    half = o_ref.shape[1] // 2
    i = pl.program_id(1)

    def cast_half(lo):
        if not weights_nk:
            wbf_ref[:, lo:lo + half] = w_ref[...].astype(wbf_ref.dtype)
        elif not row_shift:
            wbf_ref[lo:lo + half, :] = w_ref[...].astype(wbf_ref.dtype)
        else:
            keep = half - row_shift
            wbf_ref[lo:lo + keep, :] = w_ref[row_shift:half, :].astype(wbf_ref.dtype)
            wbf_ref[lo + keep:lo + half, :] = next_ref[0:row_shift, :].astype(wbf_ref.dtype)

    def emit(lo, width):
        wbf = wbf_ref[lo:lo + width, :] if weights_nk else wbf_ref[:, lo:lo + width]
        z = _matmul_w(a_ref[...], wbf, weights_nk)
        if has_bias:
            z = _sigmoid(z + b_ref[:, lo:lo + width])
        o_ref[:, lo:lo + width] = z.astype(o_ref.dtype)

    @pl.when(i == 0)
    def _():
        cast_half(0)
        emit(0, half)

    @pl.when(i == 1)
    def _():
        cast_half(half)
        emit(half, half)

    @pl.when(i >= 2)
    def _():
        emit(0, 2 * half)


def _wide_proj(a, w, lead_of, col_of, n_tiles, tm, half, out_dtype, name, bias=None,
               weights_nk=False, first_half_block=0, row_shift=0):
    m, k = a.shape
    assert w.shape[-1 if weights_nk else -2] == k
    assert row_shift == 0 or (weights_nk and row_shift % 8 == 0 and row_shift <= NEXT_ROWS)
    lead = (None,) * (w.ndim - 2)
    row = lambda i: jnp.maximum(i - 1, 0)

    def weight_window(j, i):
        jw = jnp.where(i >= 2, jnp.minimum(j + 1, n_tiles - 1), j)
        return lead_of(jw), first_half_block + 2 * col_of(jw) + jnp.where(i == 1, 1, 0)

    def w_index(j, i):
        stacked, hb = weight_window(j, i)
        return (*stacked, hb, 0) if weights_nk else (*stacked, 0, hb)

    in_specs = [pl.BlockSpec((tm, k), lambda j, i: (row(i), 0)),
                pl.BlockSpec(lead + ((half, k) if weights_nk else (k, half)), w_index)]
    args = [a, w]
    if row_shift:
        def next_index(j, i):
            stacked, hb = weight_window(j, i)
            return (*stacked, (hb + 1) * (half // NEXT_ROWS), 0)

        in_specs.append(pl.BlockSpec(lead + (NEXT_ROWS, k), next_index))
        args.append(w)
    if bias is not None:
        in_specs.append(pl.BlockSpec(lead + (1, 2 * half), lambda j, i: (*lead_of(j), 0, col_of(j))))
        args.append(bias)
    return pl.pallas_call(
        functools.partial(_wide_proj_kernel, has_bias=bias is not None, weights_nk=weights_nk,
                          row_shift=row_shift),
        grid=(n_tiles, m // tm + 1),
        in_specs=in_specs,
        out_specs=pl.BlockSpec((tm, 2 * half), lambda j, i: (row(i), j)),
        out_shape=jax.ShapeDtypeStruct((m, n_tiles * 2 * half), out_dtype),
        scratch_shapes=[pltpu.VMEM((2 * half, k) if weights_nk else (k, 2 * half), BF16)],
        compiler_params=_cp("parallel", "arbitrary"),
        name=name,
    )(*args)


def _swiglu_kernel(a_ref, wg_ref, wu_ref, o_ref, wgbf_ref, wubf_ref, *, n_half_blocks):
    half = wg_ref.shape[1]
    j = pl.program_id(0)
    i = pl.program_id(1)

    def cast_half(lo):
        wgbf_ref[:, lo:lo + half] = wg_ref[...].astype(wgbf_ref.dtype)
        wubf_ref[:, lo:lo + half] = wu_ref[...].astype(wubf_ref.dtype)

    def emit(lo, width):
        a = a_ref[...]
        g = jnp.dot(a, wgbf_ref[:, lo:lo + width], preferred_element_type=F32)
        u = jnp.dot(a, wubf_ref[:, lo:lo + width], preferred_element_type=F32)
        o_ref[:, lo:lo + width] = (g * _sigmoid(g) * u).astype(o_ref.dtype)

    has_right = 2 * j + 1 < n_half_blocks

    @pl.when(i == 0)
    def _():
        cast_half(0)
        emit(0, half)

    @pl.when((i == 1) & has_right)
    def _():
        cast_half(half)
        emit(half, half)

    @pl.when((i >= 2) & has_right)
    def _():
        emit(0, 2 * half)

    @pl.when((i >= 2) & jnp.logical_not(has_right))
    def _():
        emit(0, half)


def _swiglu(a, wg, wu, layer, tm, half):
    m, k = a.shape
    n = wg.shape[2]
    n_half_blocks = n // half
    assert n_half_blocks * half == n
    n_tiles = -(-n_half_blocks // 2)
    row = lambda i: jnp.maximum(i - 1, 0)
    def w_index(j, i):
        jw = jnp.where(i >= 2, jnp.minimum(j + 1, n_tiles - 1), j)
        return (layer, 0, jnp.minimum(2 * jw + jnp.where(i == 1, 1, 0), n_half_blocks - 1))

    w_spec = pl.BlockSpec((None, k, half), w_index)
    return pl.pallas_call(
        functools.partial(_swiglu_kernel, n_half_blocks=n_half_blocks),
        grid=(n_tiles, m // tm + 1),
        in_specs=[pl.BlockSpec((tm, k), lambda j, i: (row(i), 0)), w_spec, w_spec],
        out_specs=pl.BlockSpec((tm, 2 * half), lambda j, i: (row(i), j)),
        out_shape=jax.ShapeDtypeStruct((m, n), BF16),
        scratch_shapes=[pltpu.VMEM((k, 2 * half), BF16), pltpu.VMEM((k, 2 * half), BF16)],
        compiler_params=_cp("parallel", "arbitrary"),
        name="ffn_swiglu",
    )(a, wg, wu)


def _merge_kernel(y0_ref, y1_ref, y2_ref, y3_ref, g0_ref, g1_ref, g2_ref, g3_ref, w_ref,
                  o_ref, wbf_ref):
    half = w_ref.shape[2]
    i = pl.program_id(1)
    branches = ((y0_ref, g0_ref), (y1_ref, g1_ref), (y2_ref, g2_ref), (y3_ref, g3_ref))

    def cast_half(lo):
        wbf_ref[:, :, lo:lo + half] = w_ref[...].astype(wbf_ref.dtype)

    def emit(lo, width):
        acc = None
        for b, (y_ref, g_ref) in enumerate(branches):
            val = g_ref[:, lo:lo + width].astype(F32) * jnp.dot(
                y_ref[...], wbf_ref[b, :, lo:lo + width], preferred_element_type=F32)
            acc = val if acc is None else acc + val
        o_ref[:, lo:lo + width] = acc.astype(o_ref.dtype)

    @pl.when(i == 0)
    def _():
        cast_half(0)
        emit(0, half)

    @pl.when(i == 1)
    def _():
        cast_half(half)
        emit(half, half)

    @pl.when(i >= 2)
    def _():
        emit(0, 2 * half)


def _gated_merge(ys, gates, w_branch, layer, tm, half):
    m = ys[0].shape[0]
    nj = D_MODEL // (2 * half)
    row = lambda i: jnp.maximum(i - 1, 0)

    def w_index(j, i):
        jw = jnp.where(i >= 2, jnp.minimum(j + 1, nj - 1), j)
        return (layer, 0, 0, 2 * jw + jnp.where(i == 1, 1, 0))

    y_spec = pl.BlockSpec((tm, BRANCH), lambda j, i: (row(i), 0))
    g_specs = [pl.BlockSpec((tm, 2 * half),
                            functools.partial(lambda j, i, b: (row(i), b * nj + j), b=b))
               for b in range(N_BRANCH)]
    return pl.pallas_call(
        _merge_kernel,
        grid=(nj, m // tm + 1),
        in_specs=[y_spec] * N_BRANCH + g_specs
                 + [pl.BlockSpec((None, N_BRANCH, BRANCH, half), w_index)],
        out_specs=pl.BlockSpec((tm, 2 * half), lambda j, i: (row(i), j)),
        out_shape=jax.ShapeDtypeStruct((m, D_MODEL), BF16),
        scratch_shapes=[pltpu.VMEM((N_BRANCH, BRANCH, 2 * half), BF16)],
        compiler_params=_cp("parallel", "arbitrary"),
        name="gated_merge",
    )(*ys, gates, gates, gates, gates, w_branch)


def _assemble_kernel(x_ref, prev_ref, meta_ref, g_ref, h_ref, xn_ref, *, length):
    tr = x_ref.shape[0]
    i = pl.program_id(0)
    head = jnp.where(i == 0, meta_ref[...], prev_ref[...])
    tile = jnp.concatenate([head, x_ref[0:tr - N_META, :]], axis=0)
    pos = i * tr + lax.broadcasted_iota(jnp.int32, (tr, 1), 0)
    h = jnp.where(pos < length, tile, 0.0)
    h_ref[...] = h
    xn_ref[...] = _rms(h, g_ref[...]).astype(xn_ref.dtype)


def _assemble_prenorm(x2d, meta, g, rows):
    seq = x2d.shape[0]
    assert N_META == HALO and NORM_TILE % N_META == 0
    last_x = (seq - 1) // NORM_TILE
    last_prev = (seq - 1) // N_META
    per_tile = NORM_TILE // N_META
    row = pl.BlockSpec((NORM_TILE, D_MODEL), lambda i: (i, 0))
    return pl.pallas_call(
        functools.partial(_assemble_kernel, length=N_META + seq),
        grid=(rows // NORM_TILE,),
        in_specs=[pl.BlockSpec((NORM_TILE, D_MODEL), lambda i: (jnp.minimum(i, last_x), 0)),
                  pl.BlockSpec((N_META, D_MODEL),
                               lambda i: (jnp.clip(i * per_tile - 1, 0, last_prev), 0)),
                  pl.BlockSpec((N_META, D_MODEL), lambda i: (0, 0)),
                  pl.BlockSpec((1, D_MODEL), lambda i: (0, 0))],
        out_specs=[row, row],
        out_shape=[jax.ShapeDtypeStruct((rows, D_MODEL), F32),
                   jax.ShapeDtypeStruct((rows, D_MODEL), BF16)],
        compiler_params=_cp("parallel"),
        name="assemble_prenorm",
    )(x2d, x2d, meta, g)


def _resid_norm_kernel(h_ref, o_ref, gpost_ref, gnext_ref, hn_ref, xn_ref):
    hn = h_ref[...] + _rms(o_ref[...], gpost_ref[...])
    hn_ref[...] = hn
    xn_ref[...] = _rms(hn, gnext_ref[...]).astype(xn_ref.dtype)


def _resid_out_kernel(h_ref, o_ref, hnext_ref, onext_ref, gpost_ref, out_ref):
    g = gpost_ref[...]
    cur = h_ref[N_META:, :] + _rms(o_ref[N_META:, :], g)
    nxt = hnext_ref[...] + _rms(onext_ref[...], g)
    out_ref[...] = jnp.concatenate([cur, nxt], axis=0)


def _resid_norm(h, o, g_post, g_next):
    m = h.shape[0]
    row = pl.BlockSpec((NORM_TILE, D_MODEL), lambda i: (i, 0))
    gain = pl.BlockSpec((1, D_MODEL), lambda i: (0, 0))
    return pl.pallas_call(
        _resid_norm_kernel,
        grid=(m // NORM_TILE,),
        in_specs=[row, row, gain, gain],
        out_specs=[row, row],
        out_shape=[jax.ShapeDtypeStruct((m, D_MODEL), F32),
                   jax.ShapeDtypeStruct((m, D_MODEL), BF16)],
        compiler_params=_cp("parallel"),
        name="resid_norm",
    )(h, o, g_post, g_next)


def _resid_out(h, o, g_post, seq):
    m = h.shape[0]
    per_tile = NORM_TILE // N_META
    last_head = m // N_META - 1
    row = pl.BlockSpec((NORM_TILE, D_MODEL), lambda i: (i, 0))
    head = pl.BlockSpec((N_META, D_MODEL), lambda i: (jnp.minimum((i + 1) * per_tile, last_head), 0))
    return pl.pallas_call(
        _resid_out_kernel,
        grid=(-(-seq // NORM_TILE),),
        in_specs=[row, row, head, head, pl.BlockSpec((1, D_MODEL), lambda i: (0, 0))],
        out_specs=row,
        out_shape=jax.ShapeDtypeStruct((seq, D_MODEL), F32),
        compiler_params=_cp("parallel"),
        name="resid_out",
    )(h, o, h, o, g_post)


def _rope_half128(x, cos, sin_signed):
    return x * cos + pltpu.roll(x, 64, axis=1) * sin_signed


def _rope_half64(x, cos, sin_signed):
    lane = lax.broadcasted_iota(jnp.int32, x.shape, 1)
    partner = jnp.where((lane & 32) == 0, pltpu.roll(x, 96, axis=1), pltpu.roll(x, 32, axis=1))
    return x * cos + partner * sin_signed


def _conv_kernel(cb_ref, cc_ref, cu_ref, pc_ref, pu_ref, w_ref, y_ref, z_ref):
    tr = cb_ref.shape[0]
    i = pl.program_id(0)
    z = cc_ref[...].astype(F32) * cu_ref[...].astype(F32)
    zp = pc_ref[...].astype(F32) * pu_ref[...].astype(F32)
    z_ref[0:HALO, :] = jnp.where(i > 0, zp, 0.0)
    z_ref[HALO:HALO + tr, :] = z
    w = w_ref[...]
    conv = (w[2:3, :] * z
            + w[1:2, :] * z_ref[HALO - 1:HALO - 1 + tr, :]
            + w[0:1, :] * z_ref[HALO - 2:HALO - 2 + tr, :])
    y_ref[...] = (cb_ref[...].astype(F32) * conv).astype(y_ref.dtype)


def _conv_branch(proj, conv_w, layer, tr):
    m = proj.shape[0]
    cw = 256
    nb = BRANCH // cw

    def halo_row(i):
        return jnp.maximum(i * (tr // HALO) - 1, 0)

    return pl.pallas_call(
        _conv_kernel,
        grid=(m // tr, nb),
        in_specs=[pl.BlockSpec((tr, cw), lambda i, c: (i, c)),
                  pl.BlockSpec((tr, cw), lambda i, c: (i, nb + c)),
                  pl.BlockSpec((tr, cw), lambda i, c: (i, 2 * nb + c)),
                  pl.BlockSpec((HALO, cw), lambda i, c: (halo_row(i), nb + c)),
                  pl.BlockSpec((HALO, cw), lambda i, c: (halo_row(i), 2 * nb + c)),
                  pl.BlockSpec((None, CONV_K, cw), lambda i, c: (layer, 0, c))],
        out_specs=pl.BlockSpec((tr, cw), lambda i, c: (i, c)),
        out_shape=jax.ShapeDtypeStruct((m, BRANCH), BF16),
        scratch_shapes=[pltpu.VMEM((HALO + tr, cw), F32)],
        compiler_params=_cp("parallel", "parallel"),
        name="conv_branch",
    )(proj, proj, proj, proj, proj, conv_w)


def _pool_kernel(u_ref, pu_ref, w_ref, s_ref, y_ref, x_ref):
    tr = u_ref.shape[0]
    i = pl.program_id(0)
    x_ref[0:HALO, :] = jnp.where(i > 0, pu_ref[...].astype(F32), 0.0)
    x_ref[HALO:HALO + tr, :] = u_ref[...].astype(F32)
    t = i * tr + lax.broadcasted_iota(jnp.int32, (tr, POOL_GROUP), 0)
    for g, win in enumerate(POOL_WINDOWS):
        cols = slice(g * POOL_GROUP, (g + 1) * POOL_GROUP)
        x = x_ref[HALO:HALO + tr, cols]
        total = x
        for j in range(1, win):
            total = total + x_ref[HALO - j:HALO - j + tr, cols]
        count = jnp.minimum(t + 1, win).astype(F32)
        pooled = (total / count - x).astype(BF16)
        mixed = jnp.dot(pooled, w_ref[g].astype(BF16), preferred_element_type=F32)
        y_ref[:, cols] = (mixed * s_ref[:, cols]).astype(y_ref.dtype)


def _pool_branch(proj, pool_w, pool_scale, layer, tr):
    m = proj.shape[0]
    assert max(POOL_WINDOWS) <= HALO
    n_groups = len(POOL_WINDOWS)
    return pl.pallas_call(
        _pool_kernel,
        grid=(m // tr,),
        in_specs=[pl.BlockSpec((tr, BRANCH), lambda i: (i, 0)),
                  pl.BlockSpec((HALO, BRANCH), lambda i: (jnp.maximum(i * (tr // HALO) - 1, 0), 0)),
                  pl.BlockSpec((None, n_groups, POOL_GROUP, POOL_GROUP), lambda i: (layer, 0, 0, 0)),
                  pl.BlockSpec((None, 1, BRANCH), lambda i: (layer, 0, 0))],
        out_specs=pl.BlockSpec((tr, BRANCH), lambda i: (i, 0)),
        out_shape=jax.ShapeDtypeStruct((m, BRANCH), BF16),
        scratch_shapes=[pltpu.VMEM((HALO + tr, BRANCH), F32)],
        compiler_params=_cp("parallel"),
        name="pool_branch",
    )(proj, proj, pool_w, pool_scale)


def _store_value_t(vt_ref, v, heads):
    tr = v.shape[0]
    n_kt, _, tk = vt_ref.shape
    assert n_kt * tk == tr
    vt = v.T.astype(vt_ref.dtype)
    ones = jnp.ones((V_ROWS - LANES, tk), vt_ref.dtype)
    for c in range(n_kt):
        for h in range(heads):
            vt_ref[c, h * V_ROWS:h * V_ROWS + LANES, :] = vt[h * LANES:(h + 1) * LANES,
                                                             c * tk:(c + 1) * tk]
            vt_ref[c, h * V_ROWS + LANES:(h + 1) * V_ROWS, :] = ones


def _mla_prep_kernel(cq_ref, ckv_ref, sm_ref, gq_ref, gkv_ref, wuq_ref, wukv_ref,
                     cos_ref, sin_ref, qt_ref, kn_ref, kpe_ref, vt_ref):
    cos = cos_ref[...]
    sin = sin_ref[...]
    scale = (MLA_NOPE + MLA_ROPE) ** -0.5 * LOG2E
    cqn = _rms(cq_ref[...].astype(F32), gq_ref[...]).astype(BF16)
    q = jnp.dot(cqn, wuq_ref[...], preferred_element_type=F32)
    for h in range(MLA_HEADS):
        lo = 2 * h * LANES
        qt_ref[lo:lo + LANES, :] = (q[:, lo:lo + LANES] * scale).T.astype(qt_ref.dtype)
        pe = _rope_half64(q[:, lo + LANES:lo + 2 * LANES], cos, sin)
        qt_ref[lo + LANES:lo + 2 * LANES, :] = (pe * scale).T.astype(qt_ref.dtype)
    ckvn = _rms(ckv_ref[...].astype(F32), gkv_ref[...]).astype(BF16)
    kv = jnp.dot(ckvn, wukv_ref[...], preferred_element_type=F32)
    kn_ref[...] = kv[:, :BRANCH].astype(kn_ref.dtype)
    _store_value_t(vt_ref, kv[:, BRANCH:], MLA_HEADS)
    kpe_ref[...] = _rope_half64(sm_ref[:, 0:LANES], cos, sin).astype(kpe_ref.dtype)


def _mla_prep(proj, small, gq, gkv, wuq, wukv, cos64, sin64, layer, tr):
    m = proj.shape[0]
    qw = 2 * LANES * MLA_HEADS
    return pl.pallas_call(
        _mla_prep_kernel,
        grid=(m // tr,),
        in_specs=[pl.BlockSpec((tr, MLA_Q_RANK), lambda i: (i, 3 * BRANCH // MLA_Q_RANK)),
                  pl.BlockSpec((tr, MLA_KV_RANK), lambda i: (i, (3 * BRANCH + MLA_Q_RANK) // MLA_KV_RANK)),
                  pl.BlockSpec((tr, SMALL_COLS), lambda i: (i, 0)),
                  pl.BlockSpec((None, 1, MLA_Q_RANK), lambda i: (layer, 0, 0)),
                  pl.BlockSpec((None, 1, MLA_KV_RANK), lambda i: (layer, 0, 0)),
                  pl.BlockSpec((MLA_Q_RANK, qw), lambda i: (0, 0)),
                  pl.BlockSpec((MLA_KV_RANK, 2 * BRANCH), lambda i: (0, 0)),
                  pl.BlockSpec((tr, LANES), lambda i: (i, 0)),
                  pl.BlockSpec((tr, LANES), lambda i: (i, 0))],
        out_specs=[pl.BlockSpec((qw, tr), lambda i: (0, i)),
                   pl.BlockSpec((tr, BRANCH), lambda i: (i, 0)),
                   pl.BlockSpec((tr, LANES), lambda i: (i, 0)),
                   pl.BlockSpec((tr // K_TILE, MLA_HEADS * V_ROWS, K_TILE), lambda i: (i, 0, 0))],
        out_shape=[jax.ShapeDtypeStruct((qw, m), BF16),
                   jax.ShapeDtypeStruct((m, BRANCH), BF16),
                   jax.ShapeDtypeStruct((m, LANES), BF16),
                   jax.ShapeDtypeStruct((m // K_TILE, MLA_HEADS * V_ROWS, K_TILE), BF16)],
        compiler_params=_cp("parallel"),
        name="mla_prep",
    )(proj, proj, small, gq, gkv, wuq, wukv, cos64, sin64)


def _dsa_prep_kernel(dq_ref, dk_ref, dv_ref, iq_ref, sm_ref, c128_ref, s128_ref, c64_ref, s64_ref,
                     qt_ref, k_ref, vt_ref, iqt_ref, ik_ref, wt_ref):
    c128, s128 = c128_ref[...], s128_ref[...]
    c64, s64 = c64_ref[...], s64_ref[...]
    scale = DSA_DIM ** -0.5 * LOG2E
    for h in range(BRANCH // LANES):
        sl = slice(h * LANES, (h + 1) * LANES)
        qt_ref[sl, :] = (_rope_half128(dq_ref[:, sl].astype(F32), c128, s128) * scale).T.astype(qt_ref.dtype)
        k_ref[:, sl] = _rope_half128(dk_ref[:, sl].astype(F32), c128, s128).astype(k_ref.dtype)
        iqt_ref[sl, :] = _rope_half64(iq_ref[:, sl].astype(F32), c64, s64).T.astype(iqt_ref.dtype)
    _store_value_t(vt_ref, dv_ref[...].astype(F32), DSA_HEADS)
    tail = sm_ref[:, LANES:2 * LANES]
    lane = lax.broadcasted_iota(jnp.int32, tail.shape, 1)
    ik_lo = jnp.where(lane < IDX_DIM, _rope_half64(tail, c64, s64), 0.0)
    ik_ref[:, 0:LANES] = ik_lo.astype(ik_ref.dtype)
    ik_ref[:, LANES:2 * LANES] = pltpu.roll(ik_lo, IDX_DIM, axis=1).astype(ik_ref.dtype)
    idx_w_scale = (IDX_HEADS ** -0.5) * (IDX_DIM ** -0.5)
    wt_ref[...] = (tail * idx_w_scale).T


def _dsa_prep(proj_b, small, c128, s128, c64, s64, tr):
    m = proj_b.shape[0]
    col = lambda c: pl.BlockSpec((tr, BRANCH), lambda i: (i, c))
    tab = pl.BlockSpec((tr, LANES), lambda i: (i, 0))
    return pl.pallas_call(
        _dsa_prep_kernel,
        grid=(m // tr,),
        in_specs=[col(0), col(1), col(2), col(3),
                  pl.BlockSpec((tr, SMALL_COLS), lambda i: (i, 0)),
                  tab, tab, tab, tab],
        out_specs=[pl.BlockSpec((BRANCH, tr), lambda i: (0, i)),
                   pl.BlockSpec((tr, BRANCH), lambda i: (i, 0)),
                   pl.BlockSpec((tr // K_TILE, DSA_HEADS * V_ROWS, K_TILE), lambda i: (i, 0, 0)),
                   pl.BlockSpec((BRANCH, tr), lambda i: (0, i)),
                   pl.BlockSpec((tr, 2 * LANES), lambda i: (i, 0)),
                   pl.BlockSpec((LANES, tr), lambda i: (0, i))],
        out_shape=[jax.ShapeDtypeStruct((BRANCH, m), BF16),
                   jax.ShapeDtypeStruct((m, BRANCH), BF16),
                   jax.ShapeDtypeStruct((m // K_TILE, DSA_HEADS * V_ROWS, K_TILE), BF16),
                   jax.ShapeDtypeStruct((BRANCH, m), BF16),
                   jax.ShapeDtypeStruct((m, 2 * LANES), BF16),
                   jax.ShapeDtypeStruct((LANES, m), F32)],
        compiler_params=_cp("parallel"),
        name="dsa_prep",
    )(proj_b, proj_b, proj_b, proj_b, small, c128, s128, c64, s64)


def _float_to_ordered_int(s):
    b = lax.bitcast_convert_type(s, jnp.int32)
    return b ^ ((b >> 31) & jnp.int32(0x7FFFFFFF))


def _indexer_kernel(iq_ref, ik_ref, wt_ref, bias_ref, key_ref, cut_ref, high_ref, low_ref, *, topk):
    tq = iq_ref.shape[1]
    total_rows = ik_ref.shape[0]
    tk = IDX_K_TILE
    i = pl.program_id(0)
    n_tiles = (i * tq + tq) // tk
    n_rows = n_tiles * tk
    qpos = i * tq + lax.broadcasted_iota(jnp.int32, (1, tq), 1)
    int_min = jnp.int32(-2 ** 31)

    def score_tile(kt, carry):
        start = pl.multiple_of(kt * tk, tk)
        ik_lo = ik_ref[pl.ds(start, tk), 0:LANES]
        ik_hi = ik_ref[pl.ds(start, tk), LANES:2 * LANES]
        acc = jnp.zeros((tk, tq), F32)
        for j in range(IDX_HEADS // 2):
            qpair_t = iq_ref[j * LANES:(j + 1) * LANES, :]
            for half, ik in enumerate((ik_lo, ik_hi)):
                g = 2 * j + half
                dots = jnp.dot(ik, qpair_t, preferred_element_type=F32)
                acc = acc + jnp.maximum(dots, 0.0) * wt_ref[IDX_DIM + g:IDX_DIM + g + 1, :]
        acc = acc + 0.0
        kpos = start + lax.broadcasted_iota(jnp.int32, (tk, 1), 0)
        key = jnp.where(kpos <= qpos, _float_to_ordered_int(acc), int_min)
        key_ref[pl.ds(start, tk), :] = key
        high_ref[pl.ds(start, tk), :] = (key >> 16).astype(jnp.int16)
        return carry

    lax.fori_loop(0, n_tiles, score_tile, 0)

    n_chunks = n_rows // COUNT_CHUNK

    def count(pred, with_pos=False):
        def body(c, accs):
            start = pl.multiple_of(c * COUNT_CHUNK, COUNT_CHUNK)
            accs = list(accs)
            chunk = key_ref[pl.ds(start, COUNT_CHUNK), :]
            for r in range(COUNT_CHUNK // 8):
                blk = chunk[8 * r:8 * r + 8, :]
                if with_pos:
                    pos = start + 8 * r + lax.broadcasted_iota(jnp.int32, (8, 1), 0)
                    hit = pred(blk, pos)
                else:
                    hit = pred(blk)
                accs[r % COUNT_ACCS] = accs[r % COUNT_ACCS] + jnp.where(hit, 1, 0).astype(jnp.int32)
            return tuple(accs)
        zero = jnp.zeros((8, tq), jnp.int32)
        accs = lax.fori_loop(0, n_chunks, body, (zero,) * COUNT_ACCS)
        return jnp.sum(functools.reduce(lambda a, b: a + b, accs), axis=0, keepdims=True)

    int16_min = -(1 << 15)

    def count16(ref, cand, strict=False):
        cand16 = jnp.broadcast_to(cand, (16, tq)).astype(jnp.int16)
        one16 = jnp.ones((16, tq), jnp.int16)
        zero16 = jnp.zeros((16, tq), jnp.int16)

        def body(c, accs):
            start = pl.multiple_of(c * COUNT_CHUNK, COUNT_CHUNK)
            accs = list(accs)
            chunk = ref[pl.ds(start, COUNT_CHUNK), :]
            for r in range(COUNT_CHUNK // 16):
                blk = chunk[16 * r:16 * r + 16, :]
                hit = (blk > cand16) if strict else (blk >= cand16)
                accs[r % COUNT_ACCS] = accs[r % COUNT_ACCS] + jnp.where(hit, one16, zero16)
            return tuple(accs)
        zero = jnp.zeros((16, tq), jnp.int16)
        accs = lax.fori_loop(0, n_chunks, body, (zero,) * COUNT_ACCS)
        total = functools.reduce(lambda a, b: a + b, [a.astype(jnp.int32) for a in accs])
        return jnp.sum(total, axis=0, keepdims=True)

    def largest16(ref, want, n_all):
        def step(b, carry):
            t, n_t = carry
            cand = t + jnp.left_shift(jnp.int32(1), 15 - b)
            n_cand = count16(ref, cand)
            ok = n_cand >= want
            return jnp.where(ok, cand, t), jnp.where(ok, n_cand, n_t)
        return lax.fori_loop(0, 16, step, (jnp.full((1, tq), int16_min, jnp.int32), n_all))

    tau_high, n_high = largest16(high_ref, topk, jnp.full((1, tq), n_rows, jnp.int32))
    n_above = count16(high_ref, tau_high, strict=True)

    def park_low(c, carry):
        start = pl.multiple_of(c * COUNT_CHUNK, COUNT_CHUNK)
        key = key_ref[pl.ds(start, COUNT_CHUNK), :]
        low = (key & 0xFFFF) + int16_min
        low_ref[pl.ds(start, COUNT_CHUNK), :] = jnp.where(
            (key >> 16) == tau_high, low, int16_min).astype(jnp.int16)
        return carry

    lax.fori_loop(0, n_chunks, park_low, 0)
    tau_low, n_low = largest16(low_ref, topk - n_above, n_high - n_above)
    tau = jnp.left_shift(tau_high, 16) + (tau_low - int16_min)

    n_ge = n_above + n_low
    n_gt = n_above + count16(low_ref, tau_low, strict=True)
    need = topk - n_gt
    cut_bits = 14
    cut_ref[...] = jnp.full((1, tq), 1 << cut_bits, jnp.int32)

    @pl.when(jnp.max(n_ge) > topk)
    def _():
        def cut_step(b, cut):
            cand = cut + jnp.left_shift(jnp.int32(1), cut_bits - 1 - b)
            cnt = count(lambda blk, pos: (blk == tau) & (pos < cand), with_pos=True)
            return jnp.where(cnt <= need, cand, cut)
        cut_ref[...] = lax.fori_loop(0, cut_bits, cut_step, jnp.zeros((1, tq), jnp.int32))

    cut = cut_ref[...]

    def write_sel(c, carry, on_diagonal):
        start = pl.multiple_of(c * WRITE_CHUNK, WRITE_CHUNK)
        blk = key_ref[pl.ds(start, WRITE_CHUNK), :]
        pos = start + lax.broadcasted_iota(jnp.int32, (WRITE_CHUNK, 1), 0)
        causal = jnp.where(pos <= qpos, 0.0, NEG) if on_diagonal else 0.0
        tie = jnp.where(blk == tau, jnp.where(pos < cut, causal, NEG), NEG)
        bias_ref[pl.ds(start, WRITE_CHUNK), :] = jnp.where(blk > tau, causal, tie).astype(bias_ref.dtype)
        return carry

    first_diagonal = (n_rows - tq) // WRITE_CHUNK
    lax.fori_loop(0, first_diagonal, functools.partial(write_sel, on_diagonal=False), 0)
    lax.fori_loop(first_diagonal, n_rows // WRITE_CHUNK,
                  functools.partial(write_sel, on_diagonal=True), 0)

    def write_neg(c, carry):
        start = pl.multiple_of(c * WRITE_CHUNK, WRITE_CHUNK)
        bias_ref[pl.ds(start, WRITE_CHUNK), :] = jnp.full((WRITE_CHUNK, tq), NEG, bias_ref.dtype)
        return carry

    lax.fori_loop(n_rows // WRITE_CHUNK, total_rows // WRITE_CHUNK, write_neg, 0)


def _indexer(iq_t, ik_ab, wt, topk):
    m = ik_ab.shape[0]
    return pl.pallas_call(
        functools.partial(_indexer_kernel, topk=topk),
        grid=(m // Q_TILE,),
        in_specs=[pl.BlockSpec((BRANCH, Q_TILE), lambda i: (0, i)),
                  pl.BlockSpec((m, 2 * LANES), lambda i: (0, 0)),
                  pl.BlockSpec((LANES, Q_TILE), lambda i: (0, i))],
        out_specs=pl.BlockSpec((m, Q_TILE), lambda i: (0, i)),
        out_shape=jax.ShapeDtypeStruct((m, m), BF16),
        scratch_shapes=[pltpu.VMEM((m, Q_TILE), jnp.int32),
                        pltpu.VMEM((1, Q_TILE), jnp.int32),
                        pltpu.VMEM((m, Q_TILE), jnp.int16),
                        pltpu.VMEM((m, Q_TILE), jnp.int16)],
        compiler_params=_cp("parallel"),
        name="dsa_indexer",
    )(iq_t, ik_ab, wt)


def _flash_kernel(*refs, q_axis, has_kpe, has_bias):
    refs = list(refs)
    q_ref, k_ref = refs[0], refs[1]
    pos = 2
    kpe_ref = bias_ref = None
    if has_kpe:
        kpe_ref = refs[pos]
        pos += 1
    vt_ref = refs[pos]
    pos += 1
    if has_bias:
        bias_ref = refs[pos]
        pos += 1
    o_ref, sa_ref, sb_ref = refs[pos], refs[pos + 1], refs[pos + 2]

    tq = q_ref.shape[1]
    heads = vt_ref.shape[1] // V_ROWS
    tk = vt_ref.shape[2]
    dq = q_ref.shape[0] // heads
    assert tk % tq == 0
    i = pl.program_id(q_axis)
    n_tiles = (i * tq + tq + tk - 1) // tk
    qpos = i * tq + lax.broadcasted_iota(jnp.int32, (1, tq), 1)

    def compute_scores(kt, s_ref):
        start = pl.multiple_of(kt * tk, tk)
        kpe = kpe_ref[pl.ds(start, tk), :] if has_kpe else None
        for h in range(heads):
            k = k_ref[pl.ds(start, tk), h * LANES:(h + 1) * LANES]
            if has_kpe:
                k = jnp.concatenate([k, kpe], axis=1)
            s_ref[h] = jnp.dot(k, q_ref[h * dq:(h + 1) * dq, :], preferred_element_type=F32)

    def consume_scores(kt, s_ref, carry, causal_mask):
        start = pl.multiple_of(kt * tk, tk)
        bias = bias_ref[pl.ds(start, tk), :].astype(F32) if has_bias else None
        new = []
        for h in range(heads):
            m_run, acc = carry[h]
            s = s_ref[h]
            if has_bias:
                s = s + bias
            if causal_mask:
                kpos = start + lax.broadcasted_iota(jnp.int32, (tk, 1), 0)
                s = jnp.where(kpos <= qpos, s, NEG)
            m_new = jnp.maximum(m_run, jnp.max(s, axis=0, keepdims=True))
            alpha = jnp.exp2(m_run - m_new)
            p = jnp.exp2(s - m_new).astype(BF16)
            pv = jnp.dot(vt_ref[kt, h * V_ROWS:(h + 1) * V_ROWS, :], p, preferred_element_type=F32)
            new.append((m_new, alpha * acc + pv))
        return tuple(new)

    def double_step(u, carry):
        kt = 2 * u
        compute_scores(kt + 1, sb_ref)
        carry = consume_scores(kt, sa_ref, carry, False)
        compute_scores(kt + 2, sa_ref)
        return consume_scores(kt + 1, sb_ref, carry, False)

    mask_last = not has_bias
    last = n_tiles - 1

    def odd_tail(carry):
        compute_scores(last, sb_ref)
        carry = consume_scores(last - 1, sa_ref, carry, False)
        return consume_scores(last, sb_ref, carry, mask_last)

    def even_tail(carry):
        return consume_scores(last, sa_ref, carry, mask_last)

    init = tuple((jnp.full((1, tq), NEG, F32), jnp.zeros((V_ROWS, tq), F32)) for _ in range(heads))
    compute_scores(0, sa_ref)
    carry = lax.fori_loop(0, last // 2, double_step, init)
    carry = lax.cond(last % 2 == 1, odd_tail, even_tail, carry)
    for h in range(heads):
        acc = carry[h][1]
        out = acc[0:LANES, :] / acc[LANES:LANES + 1, :]
        o_ref[:, h * LANES:(h + 1) * LANES] = out.T.astype(o_ref.dtype)


def _score_scratch(heads, tk):
    return [pltpu.VMEM((heads, tk, Q_TILE), F32), pltpu.VMEM((heads, tk, Q_TILE), F32)]


def _mla_attention(q, kn, kpe, vt):
    m = kn.shape[0]
    n_kt, _, tk = vt.shape
    hp = HEADS_PER_STEP
    return pl.pallas_call(
        functools.partial(_flash_kernel, q_axis=1, has_kpe=True, has_bias=False),
        grid=(MLA_HEADS // hp, m // Q_TILE),
        in_specs=[pl.BlockSpec((hp * 2 * LANES, Q_TILE), lambda h, i: (h, i)),
                  pl.BlockSpec((m, hp * LANES), lambda h, i: (0, h)),
                  pl.BlockSpec((m, LANES), lambda h, i: (0, 0)),
                  pl.BlockSpec((n_kt, hp * V_ROWS, tk), lambda h, i: (0, h, 0))],
        out_specs=pl.BlockSpec((Q_TILE, hp * MLA_V), lambda h, i: (i, h)),
        out_shape=jax.ShapeDtypeStruct((m, BRANCH), BF16),
        scratch_shapes=_score_scratch(hp, tk),
        compiler_params=_cp("parallel", "parallel"),
        name="mla_attention",
    )(q, kn, kpe, vt)


def _dsa_attention(q, k, vt, bias):
    m = k.shape[0]
    n_kt, _, tk = vt.shape
    hp = HEADS_PER_STEP
    return pl.pallas_call(
        functools.partial(_flash_kernel, q_axis=1, has_kpe=False, has_bias=True),
        grid=(DSA_HEADS // hp, m // Q_TILE),
        in_specs=[pl.BlockSpec((hp * DSA_DIM, Q_TILE), lambda h, i: (h, i)),
                  pl.BlockSpec((m, hp * DSA_DIM), lambda h, i: (0, h)),
                  pl.BlockSpec((n_kt, hp * V_ROWS, tk), lambda h, i: (0, h, 0)),
                  pl.BlockSpec((m, Q_TILE), lambda h, i: (0, i))],
        out_specs=pl.BlockSpec((Q_TILE, hp * DSA_DIM), lambda h, i: (i, h)),
        out_shape=jax.ShapeDtypeStruct((m, BRANCH), BF16),
        scratch_shapes=_score_scratch(hp, tk),
        compiler_params=_cp("parallel", "parallel"),
        name="dsa_attention",
    )(q, k, vt, bias)


def _rope_tables(rows, dim):
    inv = 1.0 / jnp.power(ROPE_THETA, jnp.arange(0, dim, 2, dtype=F32) / dim)
    ang = jnp.arange(rows, dtype=F32)[:, None] * inv[None, :]
    cos, sin = jnp.cos(ang), jnp.sin(ang)
    reps = LANES // dim
    return (jnp.tile(jnp.concatenate([cos, cos], axis=1), (1, reps)),
            jnp.tile(jnp.concatenate([-sin, sin], axis=1), (1, reps)))


W_IN_KR = 3 * BRANCH + MLA_Q_RANK + MLA_KV_RANK
W_IN_DQ = W_IN_KR + MLA_ROPE
W_IN_IK = W_IN_DQ + 4 * BRANCH
W_IN_IW = W_IN_IK + IDX_DIM
W_IN_PU = W_IN_IW + IDX_HEADS


def _small_w_in(w_t):
    zeros = lambda n: jnp.zeros((w_t.shape[0], n, w_t.shape[2]), w_t.dtype)
    return jnp.concatenate([w_t[:, W_IN_KR:W_IN_DQ], zeros(LANES - MLA_ROPE),
                            w_t[:, W_IN_IK:W_IN_IW], w_t[:, W_IN_IW:W_IN_PU],
                            zeros(LANES - IDX_DIM - IDX_HEADS)], axis=1)


def _layout_w_uq(w):
    w3 = w.reshape(MLA_Q_RANK, MLA_HEADS, MLA_NOPE + MLA_ROPE)
    w3 = jnp.pad(w3, ((0, 0), (0, 0), (0, 2 * LANES - MLA_NOPE - MLA_ROPE)))
    return w3.reshape(MLA_Q_RANK, MLA_HEADS * 2 * LANES).astype(BF16)


def _layout_w_ukv(w):
    w4 = w.reshape(MLA_KV_RANK, MLA_HEADS, 2, MLA_NOPE)
    return w4.transpose(0, 2, 1, 3).reshape(MLA_KV_RANK, 2 * BRANCH).astype(BF16)


def _forward(x, meta_tokens, norm_mix_pre, norm_mix_post, norm_ffn_pre, norm_ffn_post,
             w_in, conv_w, mla_q_norm, mla_w_uq, mla_kv_norm, mla_w_ukv, pool_w, pool_scale,
             w_branch, w_gate, b_gate, w_out, ffn_w_gate, ffn_w_up, ffn_w_down):
    assert x.shape[0] == 1 and x.shape[2] == D_MODEL
    depth = w_in.shape[0]
    seq = x.shape[1]
    length = N_META + seq
    topk = min(IDX_TOPK_MAX, length // 4)
    rows = -(-length // ROW_TILE) * ROW_TILE
    assert rows % Q_TILE == 0 and rows % K_TILE == 0 and rows % NORM_TILE == 0
    assert rows < (1 << 14)

    c64, s64 = _rope_tables(rows, 64)
    c128, s128 = _rope_tables(rows, 128)
    assert W_IN_KR == HALF_COLS
    w_in_t = jnp.swapaxes(w_in, 1, 2)
    w_in_small = _small_w_in(w_in_t)
    tn = 512
    wide = 2 * tn
    gate_tiles = D_MODEL // wide
    big_tm = BIG_ROW_TILE if rows % BIG_ROW_TILE == 0 else ROW_TILE

    h, xn = _assemble_prenorm(x[0], meta_tokens.astype(F32), norm_mix_pre[0][None], rows)
    for l in range(depth):
        at_layer = lambda j, l=l: (l,)
        tile = lambda j: j
        proj_a = _wide_proj(xn, w_in_t, at_layer, tile, HALF_COLS // wide, ROW_TILE, tn, BF16,
                            "in_proj_a", weights_nk=True)
        proj_b = _wide_proj(xn, w_in_t, at_layer, tile, 4 * BRANCH // wide, ROW_TILE, tn, BF16,
                            "in_proj_b", weights_nk=True,
                            first_half_block=W_IN_DQ // tn, row_shift=W_IN_DQ % tn)
        proj_p = _wide_proj(xn, w_in_t, at_layer, tile, BRANCH // wide, ROW_TILE, tn, BF16,
                            "in_proj_pool", weights_nk=True,
                            first_half_block=W_IN_PU // tn, row_shift=W_IN_PU % tn)
        small = _stacked_proj(xn, w_in_small, at_layer, tile, 1, ROW_TILE, SMALL_COLS, F32,
                              "in_proj_small", weights_nk=True)
        gates = _wide_proj(xn, w_gate,
                           lambda j, l=l: (l, j // gate_tiles), lambda j: j % gate_tiles,
                           N_BRANCH * gate_tiles, big_tm, tn, BF16, "gates",
                           bias=b_gate[:, :, None, :])

        y_conv = _conv_branch(proj_a, conv_w, l, ROW_TILE)
        y_pool = _pool_branch(proj_p, pool_w, pool_scale[:, None, :], l, ROW_TILE)

        q_m, kn_m, kpe_m, vt_m = _mla_prep(
            proj_a, small, mla_q_norm[:, None, :], mla_kv_norm[:, None, :],
            _layout_w_uq(mla_w_uq[l]), _layout_w_ukv(mla_w_ukv[l]), c64, s64, l, ROW_TILE)
        y_mla = _mla_attention(q_m, kn_m, kpe_m, vt_m)

        q_d, k_d, vt_d, iq_t, ik_ab, wt = _dsa_prep(proj_b, small, c128, s128, c64, s64, ROW_TILE)
        bias = _indexer(iq_t, ik_ab, wt, topk)
        y_dsa = _dsa_attention(q_d, k_d, vt_d, bias)

        merged = _gated_merge((y_conv, y_mla, y_dsa, y_pool), gates, w_branch, l, HALF_ROW_TILE, tn)
        mix = _wide_proj(merged, w_out, at_layer, tile, D_MODEL // wide, ROW_TILE, tn, F32, "out_proj")
        h, xn = _resid_norm(h, mix, norm_mix_post[l][None], norm_ffn_pre[l][None])

        act = _swiglu(xn, ffn_w_gate, ffn_w_up, l, big_tm, 256)
        f = _wide_proj(act, ffn_w_down, at_layer, tile, D_MODEL // tn, HALF_ROW_TILE, tn // 2, F32,
                       "ffn_down")
        if l + 1 < depth:
            h, xn = _resid_norm(h, f, norm_ffn_post[l][None], norm_mix_pre[l + 1][None])
        else:
            out = _resid_out(h, f, norm_ffn_post[l][None], seq)

    return out[None]


def kernel(x, meta_tokens, norm_mix_pre, norm_mix_post, norm_ffn_pre, norm_ffn_post, w_in, conv_w, mla_q_norm, mla_w_uq, mla_kv_norm, mla_w_ukv, pool_w, pool_scale, w_branch, w_gate, b_gate, w_out, ffn_w_gate, ffn_w_up, ffn_w_down):
    return _forward(x, meta_tokens, norm_mix_pre, norm_mix_post, norm_ffn_pre, norm_ffn_post,
                    w_in, conv_w, mla_q_norm, mla_w_uq, mla_kv_norm, mla_w_ukv, pool_w, pool_scale,
                    w_branch, w_gate, b_gate, w_out, ffn_w_gate, ffn_w_up, ffn_w_down)
```
